```python
import math
import jax
import jax.numpy as jnp
from jax import lax
import numpy as np


D_MODEL = 1024
BATCH = 2
SEQ = 8192
DEPTH = 1

GRID_W = 64
CTX_LEN = 256
EPS = 1e-6
N_MOD = 6

D_HY = 512
HY_ORDER = 2
HY_BANDS = 16
HY_EMB = 1 + 2 * HY_BANDS
HY_FFN = 64
HY_DECAY_TARGET = 1e-2
HY_FAST_PCT = 0.3
HY_SLOW_PCT = 1.5

NA_HEADS = 8
HEAD_DIM = 64
D_NA = NA_HEADS * HEAD_DIM
WIN_ROWS = 8
WIN_COLS = 16
ROPE_THETA = 10000.0
NEG_INF = -1e30

PEER_HEADS = 8
PEER_N_KEYS = 128
PEER_N_EXPERTS = PEER_N_KEYS * PEER_N_KEYS
PEER_TOPK = 16
PEER_D_KEY = 256
PEER_D_HALF = PEER_D_KEY // 2
PEER_BLOCK = 128

COL_HY = 0
COL_Q = COL_HY + 3 * D_HY
COL_K = COL_Q + D_NA
COL_V = COL_K + D_NA
COL_G_HY = COL_V + D_NA
COL_G_NA = COL_G_HY + D_MODEL
N_PROJ = COL_G_NA + D_MODEL

kernel_name = 'hybrid_hyena_natten_peer_block'


def rmsnorm(x, g):
    x32 = x.astype(jnp.float32)
    y = x32 * lax.rsqrt(jnp.mean(x32 * x32, axis=-1, keepdims=True) + EPS)
    return (y * g.astype(jnp.float32)).astype(x.dtype)


def modulate(x, g, shift, scale):
    return rmsnorm(x, g) * (1.0 + scale) + shift


def adaln(cond, p):
    m = jax.nn.silu(cond) @ p['w_ada'] + p['b_ada']
    return m.reshape(cond.shape[:-1] + (N_MOD, D_MODEL))


def hyena_filters(L, p):
    f32 = jnp.float32
    t = jnp.linspace(0.0, 1.0, L, dtype=f32)[:, None]
    w = (2.0 * math.pi / L) * jnp.arange(L, dtype=f32)[:, None]
    bands = jnp.linspace(1e-4, HY_BANDS - 1.0, HY_BANDS, dtype=f32)[None, :]
    feats = jnp.concatenate([t, jnp.cos(bands * w), -jnp.sin(bands * w)], axis=-1)
    freq = p['hy_sin_freq'].astype(f32)
    h = jnp.sin(freq * (feats @ p['hy_f1_w'].astype(f32) + p['hy_f1_b'].astype(f32)))
    h = jnp.sin(freq * (h @ p['hy_f2_w'].astype(f32) + p['hy_f2_b'].astype(f32)))
    h = jnp.sin(freq * (h @ p['hy_f3_w'].astype(f32) + p['hy_f3_b'].astype(f32)))
    h = (h @ p['hy_f4_w'].astype(f32)).reshape(L, HY_ORDER, 2, D_HY)
    deltas = jnp.abs(jnp.linspace(math.log(HY_DECAY_TARGET) / HY_SLOW_PCT,
                                  math.log(HY_DECAY_TARGET) / HY_FAST_PCT, D_HY, dtype=f32))
    h = h * jnp.exp(-t * deltas)[:, None, None, :]
    h = h / jnp.sum(jnp.abs(h), axis=(0, 2), keepdims=True)
    fwd, bwd = h[:, :, 0], h[:, :, 1]
    return jnp.concatenate([fwd, jnp.zeros_like(fwd[:1]), bwd[:0:-1]], axis=0)


def hyena(zh, p):
    B, L, _ = zh.shape
    cw = p['hy_conv_w']
    zp = jnp.pad(zh, ((0, 0), (1, 1), (0, 0)))
    zc = cw[0] * zp[:, :-2] + cw[1] * zp[:, 1:-1] + cw[2] * zp[:, 2:] + p['hy_conv_b']
    v, x1, x2 = jnp.split(zc.astype(jnp.float32), 3, axis=-1)
    kf = jnp.fft.rfft(hyena_filters(L, p), axis=0)
    skip = p['hy_skip'].astype(jnp.float32)
    y = v
    for o, gate in enumerate((x1, x2)):
        yf = jnp.fft.rfft(y, n=2 * L, axis=1)
        conv = jnp.fft.irfft(yf * kf[:, o][None], n=2 * L, axis=1)[:, :L]
        y = gate * (conv + skip[o] * y)
    return y.astype(zh.dtype)


def axial_rope(x):
    f32 = jnp.float32
    S = x.shape[1]
    pos = jnp.arange(S, dtype=jnp.int32)
    rows = (pos // GRID_W).astype(f32)
    cols = (pos % GRID_W).astype(f32)
    half = HEAD_DIM // 2
    nf = half // 2
    inv = ROPE_THETA ** (-jnp.arange(nf, dtype=f32) / nf)

    def rot(xp, ps):
        ang = ps[:, None] * inv[None, :]
        cos = jnp.cos(ang)[None, :, None, :]
        sin = jnp.sin(ang)[None, :, None, :]
        a, b = xp[..., :nf], xp[..., nf:]
        return jnp.concatenate([a * cos - b * sin, a * sin + b * cos], axis=-1)

    x32 = x.astype(f32)
    return jnp.concatenate([rot(x32[..., :half], rows), rot(x32[..., half:], cols)], axis=-1).astype(x.dtype)


def neighbourhood_attention(q, k, v, k_ctx, v_ctx, rpb):
    B, S, H, Dh = q.shape
    rows = S // GRID_W
    wr = min(WIN_ROWS, rows)
    r = np.arange(rows)
    row_start = np.clip(r - wr // 2, 0, rows - wr)
    row_idx = row_start[:, None] + np.arange(wr)[None, :]
    col = np.arange(GRID_W)
    col_start = np.clip(col - WIN_COLS // 2, 0, GRID_W - WIN_COLS)
    col_mask = (col[None, :] >= col_start[:, None]) & (col[None, :] < col_start[:, None] + WIN_COLS)
    dr_idx = row_idx - r[:, None] + (WIN_ROWS - 1)
    dc_idx = np.clip(col[None, :] - col[:, None], -(WIN_COLS - 1), WIN_COLS - 1) + (WIN_COLS - 1)
    bias = rpb[:, dr_idx[:, None, :, None], dc_idx[None, :, None, :]]

    qg = q.reshape(B, rows, GRID_W, H, Dh)
    kg = k.reshape(B, rows, GRID_W, H, Dh)[:, row_idx]
    vg = v.reshape(B, rows, GRID_W, H, Dh)[:, row_idx]
    scale = HEAD_DIM ** -0.5
    s_loc = jnp.einsum('brqhd,brikhd->bhrqik', qg, kg).astype(jnp.float32) * scale + bias.astype(jnp.float32)
    s_loc = jnp.where(col_mask[:, None, :], s_loc, NEG_INF).reshape(B, H, rows, GRID_W, wr * GRID_W)
    s_ctx = jnp.einsum('brqhd,bchd->bhrqc', qg, k_ctx).astype(jnp.float32) * scale
    p = jax.nn.softmax(jnp.concatenate([s_loc, s_ctx], axis=-1), axis=-1).astype(v.dtype)
    n_loc = wr * GRID_W
    p_loc = p[..., :n_loc].reshape(B, H, rows, GRID_W, wr, GRID_W)
    out = (jnp.einsum('bhrqik,brikhd->brqhd', p_loc, vg)
           + jnp.einsum('bhrqc,bchd->brqhd', p[..., n_loc:], v_ctx))
    return out.reshape(B, S, H, Dh)


def context_attention(q, k, v):
    s = jnp.einsum('bqhd,bkhd->bhqk', q, k).astype(jnp.float32) * (HEAD_DIM ** -0.5)
    p = jax.nn.softmax(s, axis=-1).astype(v.dtype)
    return jnp.einsum('bhqk,bkhd->bqhd', p, v)


def context_kv(h_ctx, p):
    B, C, _ = h_ctx.shape
    k = (h_ctx @ p['w_in'][:, COL_K:COL_V] + p['b_in'][COL_K:COL_V]).reshape(B, C, NA_HEADS, HEAD_DIM)
    v = (h_ctx @ p['w_in'][:, COL_V:COL_G_HY] + p['b_in'][COL_V:COL_G_HY]).reshape(B, C, NA_HEADS, HEAD_DIM)
    return rmsnorm(k, p['k_norm_g']), v


def token_mixer(h, p, attend):
    B, L, _ = h.shape
    z = h @ p['w_in'] + p['b_in']
    q = rmsnorm(z[..., COL_Q:COL_K].reshape(B, L, NA_HEADS, HEAD_DIM), p['q_norm_g'])
    k = rmsnorm(z[..., COL_K:COL_V].reshape(B, L, NA_HEADS, HEAD_DIM), p['k_norm_g'])
    v = z[..., COL_V:COL_G_HY].reshape(B, L, NA_HEADS, HEAD_DIM)
    y_hy = hyena(z[..., COL_HY:COL_Q], p) @ p['w_hy_out']
    y_na = attend(q, k, v).reshape(B, L, D_NA) @ p['w_na_out']
    gate = jax.nn.sigmoid(z[..., COL_G_HY:N_PROJ].astype(jnp.float32)).astype(h.dtype)
    merged = gate[..., :D_MODEL] * y_hy + gate[..., D_MODEL:] * y_na
    return merged @ p['w_out']


def peer(h, p):
    B, L, D = h.shape
    T = B * L
    hf = h.reshape(T, D)
    q = (hf @ p['peer_w_q']).reshape(T, PEER_HEADS, 2, PEER_D_HALF)
    s = jnp.einsum('thpd,hpnd->thpn', q, p['peer_keys']).astype(jnp.float32)
    top_s, top_i = lax.top_k(s, PEER_TOPK)
    cand_s = (top_s[:, :, 0, :, None] + top_s[:, :, 1, None, :]).reshape(T, PEER_HEADS, PEER_TOPK * PEER_TOPK)
    cand_i = (top_i[:, :, 0, :, None] * PEER_N_KEYS + top_i[:, :, 1, None, :]).reshape(T, PEER_HEADS, PEER_TOPK * PEER_TOPK)
    best_s, best_pos = lax.top_k(cand_s, PEER_TOPK)
    idx = jnp.take_along_axis(cand_i, best_pos, axis=-1)
    g = jax.nn.softmax(best_s, axis=-1).astype(h.dtype)
    nb = T // PEER_BLOCK
    u_tab, v_tab = p['peer_u'], p['peer_v']

    def block(args):
        xb, ib, gb = args
        act = jax.nn.gelu(jnp.einsum('td,thkd->thk', xb, u_tab[ib]), approximate=False)
        return jnp.einsum('thk,thkd->td', gb * act, v_tab[ib])

    out = lax.map(block, (hf.reshape(nb, PEER_BLOCK, D),
                          idx.reshape(nb, PEER_BLOCK, PEER_HEADS, PEER_TOPK),
                          g.reshape(nb, PEER_BLOCK, PEER_HEADS, PEER_TOPK)))
    return out.reshape(B, L, D)


def setup_inputs(seed: int = 0) -> dict:
    key = jax.random.key(seed)
    ks = jax.random.split(key, 32)
    f32 = jnp.float32

    def nrm(k, shape, scale):
        return jax.random.normal(k, shape, f32) * scale

    Ld = DEPTH
    return {
        'x': nrm(ks[0], (BATCH, SEQ, D_MODEL), 1.0),
        'c': nrm(ks[1], (BATCH, D_MODEL), 1.0),
        'ctx': nrm(ks[2], (BATCH, CTX_LEN, D_MODEL), 1.0),
        'c_ctx': nrm(ks[3], (D_MODEL,), 1.0),
        'norm1_g': 1.0 + nrm(ks[4], (Ld, D_MODEL), 0.02),
        'norm2_g': 1.0 + nrm(ks[5], (Ld, D_MODEL), 0.02),
        'w_ada': nrm(ks[6], (Ld, D_MODEL, N_MOD * D_MODEL), D_MODEL ** -0.5),
        'b_ada': nrm(ks[7], (Ld, N_MOD * D_MODEL), 0.02),
        'w_in': nrm(ks[8], (Ld, D_MODEL, N_PROJ), D_MODEL ** -0.5),
        'b_in': nrm(ks[9], (Ld, N_PROJ), 0.02),
        'hy_conv_w': nrm(ks[10], (Ld, 3, 3 * D_HY), 3 ** -0.5),
        'hy_conv_b': nrm(ks[11], (Ld, 3 * D_HY), 0.02),
        'hy_f1_w': nrm(ks[12], (Ld, HY_EMB, HY_FFN), HY_EMB ** -0.5),
        'hy_f1_b': nrm(ks[13], (Ld, HY_FFN), 0.02),
        'hy_f2_w': nrm(ks[14], (Ld, HY_FFN, HY_FFN), HY_FFN ** -0.5),
        'hy_f2_b': nrm(ks[15], (Ld, HY_FFN), 0.02),
        'hy_f3_w': nrm(ks[16], (Ld, HY_FFN, HY_FFN), HY_FFN ** -0.5),
        'hy_f3_b': nrm(ks[17], (Ld, HY_FFN), 0.02),
        'hy_f4_w': nrm(ks[18], (Ld, HY_FFN, HY_ORDER * 2 * D_HY), HY_FFN ** -0.5),
        'hy_sin_freq': 1.0 + nrm(ks[19], (Ld, HY_FFN), 0.02),
        'hy_skip': nrm(ks[20], (Ld, HY_ORDER, D_HY), 0.5),
        'q_norm_g': 1.0 + nrm(ks[21], (Ld, HEAD_DIM), 0.02),
        'k_norm_g': 1.0 + nrm(ks[22], (Ld, HEAD_DIM), 0.02),
        'na_rpb': nrm(ks[23], (Ld, NA_HEADS, 2 * WIN_ROWS - 1, 2 * WIN_COLS - 1), 0.02),
        'w_hy_out': nrm(ks[24], (Ld, D_HY, D_MODEL), D_HY ** -0.5),
        'w_na_out': nrm(ks[25], (Ld, D_NA, D_MODEL), D_NA ** -0.5),
        'w_out': nrm(ks[26], (Ld, D_MODEL, D_MODEL), D_MODEL ** -0.5),
        'peer_w_q': nrm(ks[27], (Ld, D_MODEL, PEER_HEADS * PEER_D_KEY), D_MODEL ** -0.5),
        'peer_keys': nrm(ks[28], (Ld, PEER_HEADS, 2, PEER_N_KEYS, PEER_D_HALF), PEER_D_HALF ** -0.5),
        'peer_u': nrm(ks[29], (Ld, PEER_N_EXPERTS, D_MODEL), D_MODEL ** -0.5),
        'peer_v': nrm(ks[30], (Ld, PEER_N_EXPERTS, D_MODEL), 0.5),
    }


def reference(x, c, ctx, c_ctx, norm1_g, norm2_g, w_ada, b_ada, w_in, b_in, hy_conv_w, hy_conv_b,
              hy_f1_w, hy_f1_b, hy_f2_w, hy_f2_b, hy_f3_w, hy_f3_b, hy_f4_w, hy_sin_freq, hy_skip,
              q_norm_g, k_norm_g, na_rpb, w_hy_out, w_na_out, w_out, peer_w_q, peer_keys, peer_u, peer_v):
    for layer in range(DEPTH):
        p = {
            'norm1_g': norm1_g[layer], 'norm2_g': norm2_g[layer],
            'w_ada': w_ada[layer], 'b_ada': b_ada[layer],
            'w_in': w_in[layer], 'b_in': b_in[layer],
            'hy_conv_w': hy_conv_w[layer], 'hy_conv_b': hy_conv_b[layer],
            'hy_f1_w': hy_f1_w[layer], 'hy_f1_b': hy_f1_b[layer],
            'hy_f2_w': hy_f2_w[layer], 'hy_f2_b': hy_f2_b[layer],
            'hy_f3_w': hy_f3_w[layer], 'hy_f3_b': hy_f3_b[layer],
            'hy_f4_w': hy_f4_w[layer], 'hy_sin_freq': hy_sin_freq[layer], 'hy_skip': hy_skip[layer],
            'q_norm_g': q_norm_g[layer], 'k_norm_g': k_norm_g[layer], 'na_rpb': na_rpb[layer],
            'w_hy_out': w_hy_out[layer], 'w_na_out': w_na_out[layer], 'w_out': w_out[layer],
            'peer_w_q': peer_w_q[layer], 'peer_keys': peer_keys[layer],
            'peer_u': peer_u[layer], 'peer_v': peer_v[layer],
        }
        m_lat = adaln(c, p)
        m_ctx = adaln(c_ctx, p)
        sh1, sc1, g1, sh2, sc2, g2 = [m_lat[:, i:i + 1] for i in range(N_MOD)]
        ch1, cs1, cg1, ch2, cs2, cg2 = [m_ctx[i:i + 1] for i in range(N_MOD)]

        h_ctx = modulate(ctx, p['norm1_g'], ch1, cs1)
        k_ctx, v_ctx = context_kv(h_ctx, p)

        def attend_latent(q, k, v, k_ctx=k_ctx, v_ctx=v_ctx, rpb=p['na_rpb']):
            return neighbourhood_attention(axial_rope(q), axial_rope(k), v, k_ctx, v_ctx, rpb)

        h = modulate(x, p['norm1_g'], sh1, sc1)
        x_next = x + g1 * token_mixer(h, p, attend_latent)
        x_next = x_next + g2 * peer(modulate(x_next, p['norm2_g'], sh2, sc2), p)

        if layer + 1 < DEPTH:
            ctx_next = ctx + cg1 * token_mixer(h_ctx, p, context_attention)
            ctx = ctx_next + cg2 * peer(modulate(ctx_next, p['norm2_g'], ch2, cs2), p)
        x = x_next
    return x
```

```python
import functools
import math

import numpy as np
import jax
import jax.numpy as jnp
from jax import lax
from jax.experimental import pallas as pl
from jax.experimental.pallas import tpu as pltpu

F32 = jnp.float32
BF16 = jnp.bfloat16

D_MODEL = 1024
GRID_W = 64
EPS = 1e-6
N_MOD = 6

D_HY = 512
HY_ORDER = 2
HY_BANDS = 16
HY_DECAY_TARGET = 1e-2
HY_FAST_PCT = 0.3
HY_SLOW_PCT = 1.5

NA_HEADS = 8
HEAD_DIM = 64
D_NA = NA_HEADS * HEAD_DIM
WIN_ROWS = 8
WIN_COLS = 16
ROPE_THETA = 10000.0
NEG_INF = -1e30

PEER_HEADS = 8
PEER_N_KEYS = 128
PEER_N_EXPERTS = PEER_N_KEYS * PEER_N_KEYS
PEER_TOPK = 16
PEER_D_KEY = 256
PEER_D_HALF = PEER_D_KEY // 2

COL_HY = 0
COL_Q = COL_HY + 3 * D_HY
COL_K = COL_Q + D_NA
COL_V = COL_K + D_NA
COL_G_HY = COL_V + D_NA
N_PROJ = COL_G_HY + 2 * D_MODEL

VMEM_LIMIT = 56 * 1024 * 1024
TOK_TILE = 256
PEER_TOK_TILE = 512
PEER_EXP_TILE = 1024


def _cparams(sem):
    return pltpu.CompilerParams(dimension_semantics=sem, vmem_limit_bytes=VMEM_LIMIT)


def _dot(a, b):
    return jnp.dot(a, b, preferred_element_type=F32)


def _dot_nt(a, b):
    return lax.dot_general(a, b, (((1,), (1,)), ((), ())), preferred_element_type=F32)


def _modulated_norm(x, gain, shift, scale):
    ms = jnp.mean(x * x, axis=-1, keepdims=True)
    return (x * lax.rsqrt(ms + EPS) * gain) * (1.0 + scale) + shift


def _head_rms(z, seg, gain):
    zz = z * z
    hi = zz.astype(BF16)
    lo = (zz - hi.astype(F32)).astype(BF16)
    ms = _dot(hi, seg) + _dot(lo, seg)
    return z * lax.rsqrt(ms + EPS) * gain


def _rope(z, cos, sin_signed):
    n = z.shape[-1]
    lane = lax.broadcasted_iota(jnp.int32, z.shape, 1)
    first = (lane // (HEAD_DIM // 4)) % 2 == 0
    partner = jnp.where(first, pltpu.roll(z, n - HEAD_DIM // 4, 1), pltpu.roll(z, HEAD_DIM // 4, 1))
    return z * cos + partner * sin_signed


def _inproj_kernel(x_ref, shift_ref, scale_ref, g_ref, w_ref, b_ref, cos_ref, sin_ref, qg_ref, kg_ref,
                   seg_ref, zh_ref, q_ref, k_ref, v_ref, gate_ref):
    h = _modulated_norm(x_ref[...], g_ref[...], shift_ref[...], scale_ref[...]).astype(BF16)

    def proj(lo, hi):
        return _dot(h, w_ref[:, lo:hi]) + b_ref[:, lo:hi]

    zh_ref[...] = proj(COL_HY, COL_Q)
    cos, sin = cos_ref[...], sin_ref[...]
    seg = seg_ref[...]
    q = _rope(_head_rms(proj(COL_Q, COL_K), seg, qg_ref[...]), cos, sin)
    q_ref[...] = q.astype(BF16)
    k = _rope(_head_rms(proj(COL_K, COL_V), seg, kg_ref[...]), cos, sin)
    k_ref[...] = k.astype(BF16)
    v_ref[...] = proj(COL_V, COL_G_HY).astype(BF16)
    gate_ref[...] = jax.nn.sigmoid(proj(COL_G_HY, N_PROJ)).astype(BF16)


def _inproj(x2, shift, scale, gain, w_bf, b_in, cos, sin, qg, kg, seg, seq):
    T = x2.shape[0]
    tiles_per_batch = seq // TOK_TILE
    row = lambda i: (i, 0)
    per_batch = lambda i: (i // tiles_per_batch, 0, 0)
    const = lambda i: (0, 0)
    pos = lambda i: (i % tiles_per_batch, 0)
    return pl.pallas_call(
        _inproj_kernel,
        grid=(T // TOK_TILE,),
        in_specs=[
            pl.BlockSpec((TOK_TILE, D_MODEL), row),
            pl.BlockSpec((None, 1, D_MODEL), per_batch),
            pl.BlockSpec((None, 1, D_MODEL), per_batch),
            pl.BlockSpec((1, D_MODEL), const),
            pl.BlockSpec((D_MODEL, N_PROJ), const),
            pl.BlockSpec((1, N_PROJ), const),
            pl.BlockSpec((TOK_TILE, D_NA), pos),
            pl.BlockSpec((TOK_TILE, D_NA), pos),
            pl.BlockSpec((1, D_NA), const),
            pl.BlockSpec((1, D_NA), const),
            pl.BlockSpec((D_NA, D_NA), const),
        ],
        out_specs=[
            pl.BlockSpec((TOK_TILE, 3 * D_HY), row),
            pl.BlockSpec((TOK_TILE, D_NA), row),
            pl.BlockSpec((TOK_TILE, D_NA), row),
            pl.BlockSpec((TOK_TILE, D_NA), row),
            pl.BlockSpec((TOK_TILE, 2 * D_MODEL), row),
        ],
        out_shape=[
            jax.ShapeDtypeStruct((T, 3 * D_HY), F32),
            jax.ShapeDtypeStruct((T, D_NA), BF16),
            jax.ShapeDtypeStruct((T, D_NA), BF16),
            jax.ShapeDtypeStruct((T, D_NA), BF16),
            jax.ShapeDtypeStruct((T, 2 * D_MODEL), BF16),
        ],
        compiler_params=_cparams(("parallel",)),
        name="inproj",
    )(x2, shift, scale, gain, w_bf, b_in, cos, sin, qg, kg, seg)


def _ctxkv_kernel(x_ref, shift_ref, scale_ref, g_ref, wk_ref, bk_ref, wv_ref, bv_ref, kg_ref, seg_ref,
                  k_ref, v_ref):
    h = _modulated_norm(x_ref[...], g_ref[...], shift_ref[...], scale_ref[...]).astype(BF16)
    k = _head_rms(_dot(h, wk_ref[...]) + bk_ref[...], seg_ref[...], kg_ref[...])
    k_ref[...] = k.astype(BF16)
    v_ref[...] = (_dot(h, wv_ref[...]) + bv_ref[...]).astype(BF16)


def _ctxkv(ctx2, shift, scale, gain, wk, bk, wv, bv, kg, seg):
    T = ctx2.shape[0]
    row = lambda i: (i, 0)
    const = lambda i: (0, 0)
    return pl.pallas_call(
        _ctxkv_kernel,
        grid=(T // TOK_TILE,),
        in_specs=[
            pl.BlockSpec((TOK_TILE, D_MODEL), row),
            pl.BlockSpec((1, D_MODEL), const),
            pl.BlockSpec((1, D_MODEL), const),
            pl.BlockSpec((1, D_MODEL), const),
            pl.BlockSpec((D_MODEL, D_NA), const),
            pl.BlockSpec((1, D_NA), const),
            pl.BlockSpec((D_MODEL, D_NA), const),
            pl.BlockSpec((1, D_NA), const),
            pl.BlockSpec((1, D_NA), const),
            pl.BlockSpec((D_NA, D_NA), const),
        ],
        out_specs=[pl.BlockSpec((TOK_TILE, D_NA), row), pl.BlockSpec((TOK_TILE, D_NA), row)],
        out_shape=[jax.ShapeDtypeStruct((T, D_NA), BF16), jax.ShapeDtypeStruct((T, D_NA), BF16)],
        compiler_params=_cparams(("parallel",)),
        name="ctxkv",
    )(ctx2, shift, scale, gain, wk, bk, wv, bv, kg, seg)


def _natten_kernel(q_ref, k_ref, v_ref, kc_ref, vc_ref, bias_ref, o_ref, *, rows):
    r = pl.program_id(1)
    row_start = jnp.clip(r - WIN_ROWS // 2, 0, rows - WIN_ROWS)
    off = row_start - r + (WIN_ROWS - 1)
    start = pl.multiple_of(row_start * GRID_W, GRID_W)
    n_loc = WIN_ROWS * GRID_W
    scale = HEAD_DIM ** -0.5
    pair = 2 * HEAD_DIM
    lane = lax.broadcasted_iota(jnp.int32, (GRID_W, pair), 1)
    for p in range(NA_HEADS // 2):
        cols = slice(p * pair, (p + 1) * pair)
        qp = q_ref[:, cols]
        kp = k_ref[pl.ds(start, n_loc), cols]
        vp = v_ref[pl.ds(start, n_loc), cols]
        kcp = kc_ref[:, cols]
        vcp = vc_ref[:, cols]
        outs = []
        for hh in range(2):
            in_head = (lane // HEAD_DIM) == hh
            qm = jnp.where(in_head, qp, jnp.zeros_like(qp))
            s_loc = _dot_nt(qm, kp) * scale + bias_ref[off, 2 * p + hh]
            s_ctx = _dot_nt(qm, kcp) * scale
            m = jnp.maximum(jnp.max(s_loc, axis=-1, keepdims=True), jnp.max(s_ctx, axis=-1, keepdims=True))
            e_loc = jnp.exp(s_loc - m)
            e_ctx = jnp.exp(s_ctx - m)
            denom = jnp.sum(e_loc, axis=-1, keepdims=True) + jnp.sum(e_ctx, axis=-1, keepdims=True)
            o = _dot(e_loc.astype(BF16), vp) + _dot(e_ctx.astype(BF16), vcp)
            outs.append(o / denom)
        o_ref[:, cols] = jnp.where((lane // HEAD_DIM) == 0, outs[0], outs[1]).astype(o_ref.dtype)


def _natten(q, k, v, kc, vc, bias):
    B, S, _ = q.shape
    rows = S // GRID_W
    C = kc.shape[1]
    return pl.pallas_call(
        functools.partial(_natten_kernel, rows=rows),
        grid=(B, rows),
        in_specs=[
            pl.BlockSpec((None, GRID_W, D_NA), lambda b, r: (b, r, 0)),
            pl.BlockSpec((None, S, D_NA), lambda b, r: (b, 0, 0)),
            pl.BlockSpec((None, S, D_NA), lambda b, r: (b, 0, 0)),
            pl.BlockSpec((None, C, D_NA), lambda b, r: (b, 0, 0)),
            pl.BlockSpec((None, C, D_NA), lambda b, r: (b, 0, 0)),
            pl.BlockSpec(bias.shape, lambda b, r: (0, 0, 0, 0)),
        ],
        out_specs=pl.BlockSpec((None, GRID_W, D_NA), lambda b, r: (b, r, 0)),
        out_shape=jax.ShapeDtypeStruct((B, S, D_NA), BF16),
        compiler_params=_cparams(("parallel", "arbitrary")),
        name="natten",
    )(q, k, v, kc, vc, bias)


def _natten_bias(rpb):
    off = np.arange(WIN_ROWS)
    i = np.arange(WIN_ROWS)
    col = np.arange(GRID_W)
    dr = off[:, None] + i[None, :]
    dc = np.clip(col[None, :] - col[:, None], -(WIN_COLS - 1), WIN_COLS - 1) + (WIN_COLS - 1)
    col_start = np.clip(col - WIN_COLS // 2, 0, GRID_W - WIN_COLS)
    col_mask = (col[None, :] >= col_start[:, None]) & (col[None, :] < col_start[:, None] + WIN_COLS)
    b = rpb.astype(F32)[:, dr[:, None, :, None], dc[None, :, None, :]]
    b = jnp.where(col_mask[None, None, :, None, :], b, NEG_INF)
    return jnp.transpose(b, (1, 0, 2, 3, 4)).reshape(WIN_ROWS, NA_HEADS, GRID_W, WIN_ROWS * GRID_W)


def _rope_tables(seq):
    pos = jnp.arange(seq, dtype=jnp.int32)
    rows = (pos // GRID_W).astype(F32)
    cols = (pos % GRID_W).astype(F32)
    nf = HEAD_DIM // 4
    inv = ROPE_THETA ** (-jnp.arange(nf, dtype=F32) / nf)
    ar = rows[:, None] * inv[None, :]
    ac = cols[:, None] * inv[None, :]
    cos = jnp.concatenate([jnp.cos(ar), jnp.cos(ar), jnp.cos(ac), jnp.cos(ac)], axis=-1)
    sin = jnp.concatenate([-jnp.sin(ar), jnp.sin(ar), -jnp.sin(ac), jnp.sin(ac)], axis=-1)
    return jnp.tile(cos, (1, NA_HEADS)), jnp.tile(sin, (1, NA_HEADS))


def _postmix_kernel(yh_ref, yn_ref, gate_ref, x_ref, g1_ref, sh2_ref, sc2_ref, n2_ref, whyo_ref, wnao_ref,
                    wout_ref, wq_ref, x1_ref, h2_ref, pq_ref):
    a = _dot(yh_ref[...].astype(BF16), whyo_ref[...])
    b = _dot(yn_ref[...], wnao_ref[...])
    gate = gate_ref[...].astype(F32)
    merged = gate[:, :D_MODEL] * a + gate[:, D_MODEL:] * b
    x1 = x_ref[...] + g1_ref[...] * _dot(merged.astype(BF16), wout_ref[...])
    x1_ref[...] = x1
    h2 = _modulated_norm(x1, n2_ref[...], sh2_ref[...], sc2_ref[...]).astype(BF16)
    h2_ref[...] = h2
    pq_ref[...] = _dot(h2, wq_ref[...])


def _postmix(yh, yn, gate, x2, g1, sh2, sc2, n2g, whyo, wnao, wout, wq, seq):
    T = x2.shape[0]
    tiles_per_batch = seq // TOK_TILE
    row = lambda i: (i, 0)
    per_batch = lambda i: (i // tiles_per_batch, 0, 0)
    const = lambda i: (0, 0)
    nq = wq.shape[1]
    return pl.pallas_call(
        _postmix_kernel,
        grid=(T // TOK_TILE,),
        in_specs=[
            pl.BlockSpec((TOK_TILE, D_HY), row),
            pl.BlockSpec((TOK_TILE, D_NA), row),
            pl.BlockSpec((TOK_TILE, 2 * D_MODEL), row),
            pl.BlockSpec((TOK_TILE, D_MODEL), row),
            pl.BlockSpec((None, 1, D_MODEL), per_batch),
            pl.BlockSpec((None, 1, D_MODEL), per_batch),
            pl.BlockSpec((None, 1, D_MODEL), per_batch),
            pl.BlockSpec((1, D_MODEL), const),
            pl.BlockSpec((D_HY, D_MODEL), const),
            pl.BlockSpec((D_NA, D_MODEL), const),
            pl.BlockSpec((D_MODEL, D_MODEL), const),
            pl.BlockSpec((D_MODEL, nq), const),
        ],
        out_specs=[
            pl.BlockSpec((TOK_TILE, D_MODEL), row),
            pl.BlockSpec((TOK_TILE, D_MODEL), row),
            pl.BlockSpec((TOK_TILE, nq), row),
        ],
        out_shape=[
            jax.ShapeDtypeStruct((T, D_MODEL), F32),
            jax.ShapeDtypeStruct((T, D_MODEL), BF16),
            jax.ShapeDtypeStruct((T, nq), F32),
        ],
        compiler_params=_cparams(("parallel",)),
        name="postmix",
    )(yh, yn, gate, x2, g1, sh2, sc2, n2g, whyo, wnao, wout, wq)


def _peer_dense_kernel(h2_ref, w_ref, u_ref, v_ref, x1_ref, g2_ref, o_ref, acc_ref):
    e = pl.program_id(1)

    @pl.when(e == 0)
    def _():
        acc_ref[...] = jnp.zeros_like(acc_ref)

    a = _dot_nt(h2_ref[...], u_ref[...])
    act = 0.5 * a * (1.0 + lax.erf(a * (2.0 ** -0.5)))
    acc_ref[...] += _dot((w_ref[...].astype(F32) * act).astype(BF16), v_ref[...])

    @pl.when(e == pl.num_programs(1) - 1)
    def _():
        o_ref[...] = x1_ref[...] + g2_ref[...] * acc_ref[...]


def _peer_dense(h2, w, u_bf, v_bf, x1, g2, seq):
    T = h2.shape[0]
    tiles_per_batch = seq // PEER_TOK_TILE
    n_exp = u_bf.shape[0]
    return pl.pallas_call(
        _peer_dense_kernel,
        grid=(T // PEER_TOK_TILE, n_exp // PEER_EXP_TILE),
        in_specs=[
            pl.BlockSpec((PEER_TOK_TILE, D_MODEL), lambda i, e: (i, 0)),
            pl.BlockSpec((PEER_TOK_TILE, PEER_EXP_TILE), lambda i, e: (i, e)),
            pl.BlockSpec((PEER_EXP_TILE, D_MODEL), lambda i, e: (e, 0)),
            pl.BlockSpec((PEER_EXP_TILE, D_MODEL), lambda i, e: (e, 0)),
            pl.BlockSpec((PEER_TOK_TILE, D_MODEL), lambda i, e: (i, 0)),
            pl.BlockSpec((None, 1, D_MODEL), lambda i, e: (i // tiles_per_batch, 0, 0)),
        ],
        out_specs=pl.BlockSpec((PEER_TOK_TILE, D_MODEL), lambda i, e: (i, 0)),
        out_shape=jax.ShapeDtypeStruct((T, D_MODEL), F32),
        scratch_shapes=[pltpu.VMEM((PEER_TOK_TILE, D_MODEL), F32)],
        compiler_params=_cparams(("parallel", "arbitrary")),
        name="peer_dense",
    )(h2, w, u_bf, v_bf, x1, g2)


def _hyena_filters(L, p):
    t = jnp.linspace(0.0, 1.0, L, dtype=F32)[:, None]
    w = (2.0 * math.pi / L) * jnp.arange(L, dtype=F32)[:, None]
    bands = jnp.linspace(1e-4, HY_BANDS - 1.0, HY_BANDS, dtype=F32)[None, :]
    feats = jnp.concatenate([t, jnp.cos(bands * w), -jnp.sin(bands * w)], axis=-1)
    freq = p['hy_sin_freq']
    h = jnp.sin(freq * (feats @ p['hy_f1_w'] + p['hy_f1_b']))
    h = jnp.sin(freq * (h @ p['hy_f2_w'] + p['hy_f2_b']))
    h = jnp.sin(freq * (h @ p['hy_f3_w'] + p['hy_f3_b']))
    h = (h @ p['hy_f4_w']).reshape(L, HY_ORDER, 2, D_HY)
    deltas = jnp.abs(jnp.linspace(math.log(HY_DECAY_TARGET) / HY_SLOW_PCT,
                                  math.log(HY_DECAY_TARGET) / HY_FAST_PCT, D_HY, dtype=F32))
    h = h * jnp.exp(-t * deltas)[:, None, None, :]
    h = h / jnp.sum(jnp.abs(h), axis=(0, 2), keepdims=True)
    fwd, bwd = h[:, :, 0], h[:, :, 1]
    return jnp.concatenate([fwd, jnp.zeros_like(fwd[:1]), bwd[:0:-1]], axis=0)


def _hyena(zh, p):
    B, L, _ = zh.shape
    cw = p['hy_conv_w']
    zp = jnp.pad(zh, ((0, 0), (1, 1), (0, 0)))
    zc = cw[0] * zp[:, :-2] + cw[1] * zp[:, 1:-1] + cw[2] * zp[:, 2:] + p['hy_conv_b']
    v, x1, x2 = jnp.split(zc, 3, axis=-1)
    kf = jnp.fft.rfft(_hyena_filters(L, p), axis=0)
    skip = p['hy_skip']
    y = v
    for o, gate in enumerate((x1, x2)):
        yf = jnp.fft.rfft(y, n=2 * L, axis=1)
        conv = jnp.fft.irfft(yf * kf[:, o][None], n=2 * L, axis=1)[:, :L]
        y = gate * (conv + skip[o] * y)
    return y


def _peer_route(pq, keys):
    T = pq.shape[0]
    q = pq.reshape(T, PEER_HEADS, 2, PEER_D_HALF)
    s = jnp.einsum('thpd,hpnd->thpn', q, keys)
    top_s, top_i = lax.top_k(s, PEER_TOPK)
    cand_s = (top_s[:, :, 0, :, None] + top_s[:, :, 1, None, :]).reshape(T, PEER_HEADS, PEER_TOPK * PEER_TOPK)
    best_s, best_pos = lax.top_k(cand_s, PEER_TOPK)
    a = jnp.take_along_axis(top_i[:, :, 0], best_pos // PEER_TOPK, axis=-1)
    b = jnp.take_along_axis(top_i[:, :, 1], best_pos % PEER_TOPK, axis=-1)
    g = jax.nn.softmax(best_s, axis=-1)
    n = PEER_HEADS * PEER_TOPK
    ga = (g.reshape(T, n, 1) * jax.nn.one_hot(a.reshape(T, n), PEER_N_KEYS, dtype=F32)).astype(BF16)
    ob = jax.nn.one_hot(b.reshape(T, n), PEER_N_KEYS, dtype=BF16)
    w = jnp.einsum('thi,thj->tij', ga, ob, preferred_element_type=F32)
    return w.astype(BF16).reshape(T, PEER_N_EXPERTS)


def kernel(x, c, ctx, c_ctx, norm1_g, norm2_g, w_ada, b_ada, w_in, b_in, hy_conv_w, hy_conv_b, hy_f1_w, hy_f1_b, hy_f2_w, hy_f2_b, hy_f3_w, hy_f3_b, hy_f4_w, hy_sin_freq, hy_skip, q_norm_g, k_norm_g, na_rpb, w_hy_out, w_na_out, w_out, peer_w_q, peer_keys, peer_u, peer_v):
    assert w_in.shape[0] == 1, "single-layer block"
    B, S, D = x.shape
    C = ctx.shape[1]
    T = B * S

    m_lat = (jax.nn.silu(c) @ w_ada[0] + b_ada[0]).reshape(B, N_MOD, 1, D)
    m_ctx = (jax.nn.silu(c_ctx) @ w_ada[0] + b_ada[0]).reshape(N_MOD, 1, D)
    sh1, sc1, g1, sh2, sc2, g2 = [m_lat[:, i] for i in range(N_MOD)]

    w_in_bf = w_in[0].astype(BF16)
    b_in2 = b_in[0][None, :]
    gain1 = norm1_g[0][None, :]
    qg = jnp.tile(q_norm_g[0], NA_HEADS)[None, :]
    kg = jnp.tile(k_norm_g[0], NA_HEADS)[None, :]
    seg = jnp.asarray(np.kron(np.eye(NA_HEADS), np.full((HEAD_DIM, HEAD_DIM), 1.0 / HEAD_DIM)), dtype=BF16)
    cos, sin = _rope_tables(S)

    x2 = x.reshape(T, D)
    zh, q, k, v, gate = _inproj(x2, sh1, sc1, gain1, w_in_bf, b_in2, cos, sin, qg, kg, seg, S)

    k_ctx, v_ctx = _ctxkv(ctx.reshape(B * C, D), m_ctx[0], m_ctx[1], gain1,
                          w_in_bf[:, COL_K:COL_V], b_in2[:, COL_K:COL_V],
                          w_in_bf[:, COL_V:COL_G_HY], b_in2[:, COL_V:COL_G_HY], kg, seg)

    p = {'hy_conv_w': hy_conv_w[0], 'hy_conv_b': hy_conv_b[0], 'hy_f1_w': hy_f1_w[0], 'hy_f1_b': hy_f1_b[0],
         'hy_f2_w': hy_f2_w[0], 'hy_f2_b': hy_f2_b[0], 'hy_f3_w': hy_f3_w[0], 'hy_f3_b': hy_f3_b[0],
         'hy_f4_w': hy_f4_w[0], 'hy_sin_freq': hy_sin_freq[0], 'hy_skip': hy_skip[0]}
    y_hy = _hyena(zh.reshape(B, S, 3 * D_HY), p).reshape(T, D_HY)

    y_na = _natten(q.reshape(B, S, D_NA), k.reshape(B, S, D_NA), v.reshape(B, S, D_NA),
                   k_ctx.reshape(B, C, D_NA), v_ctx.reshape(B, C, D_NA), _natten_bias(na_rpb[0])).reshape(T, D_NA)

    x1, h2, pq = _postmix(y_hy, y_na, gate, x2, g1, sh2, sc2, norm2_g[0][None, :],
                          w_hy_out[0].astype(BF16), w_na_out[0].astype(BF16), w_out[0].astype(BF16),
                          peer_w_q[0].astype(BF16), S)

    w = _peer_route(pq, peer_keys[0])
    out = _peer_dense(h2, w, peer_u[0].astype(BF16), peer_v[0].astype(BF16), x1, g2, S)
    return out.reshape(B, S, D)
```

```python
import functools
import math

import numpy as np
import jax
import jax.numpy as jnp
from jax import lax
from jax.experimental import pallas as pl
from jax.experimental.pallas import tpu as pltpu

F32 = jnp.float32
BF16 = jnp.bfloat16

D_MODEL = 1024
GRID_W = 64
EPS = 1e-6
N_MOD = 6

D_HY = 512
HY_ORDER = 2
HY_BANDS = 16
HY_DECAY_TARGET = 1e-2
HY_FAST_PCT = 0.3
HY_SLOW_PCT = 1.5

NA_HEADS = 8
HEAD_DIM = 64
D_NA = NA_HEADS * HEAD_DIM
WIN_ROWS = 8
WIN_COLS = 16
ROPE_THETA = 10000.0
NEG_INF = -1e30

PEER_HEADS = 8
PEER_N_KEYS = 128
PEER_N_EXPERTS = PEER_N_KEYS * PEER_N_KEYS
PEER_TOPK = 16
PEER_D_KEY = 256
PEER_D_HALF = PEER_D_KEY // 2

COL_HY = 0
COL_Q = COL_HY + 3 * D_HY
COL_K = COL_Q + D_NA
COL_V = COL_K + D_NA
COL_G_HY = COL_V + D_NA
N_PROJ = COL_G_HY + 2 * D_MODEL

VMEM_LIMIT = 56 * 1024 * 1024
TOK_TILE = 256
PEER_TOK_TILE = 512
PEER_EXP_TILE = 1024


def _cparams(sem):
    return pltpu.CompilerParams(dimension_semantics=sem, vmem_limit_bytes=VMEM_LIMIT)


def _dot(a, b):
    return jnp.dot(a, b, preferred_element_type=F32)


def _dot_nt(a, b):
    return lax.dot_general(a, b, (((1,), (1,)), ((), ())), preferred_element_type=F32)


def _modulated_norm(x, gain, shift, scale):
    ms = jnp.mean(x * x, axis=-1, keepdims=True)
    return (x * lax.rsqrt(ms + EPS) * gain) * (1.0 + scale) + shift


def _head_rms(z, seg, gain):
    zz = z * z
    hi = zz.astype(BF16)
    lo = (zz - hi.astype(F32)).astype(BF16)
    ms = _dot(hi, seg) + _dot(lo, seg)
    return z * lax.rsqrt(ms + EPS) * gain


def _rope(z, cos, sin_signed):
    n = z.shape[-1]
    lane = lax.broadcasted_iota(jnp.int32, z.shape, 1)
    first = (lane // (HEAD_DIM // 4)) % 2 == 0
    partner = jnp.where(first, pltpu.roll(z, n - HEAD_DIM // 4, 1), pltpu.roll(z, HEAD_DIM // 4, 1))
    return z * cos + partner * sin_signed


def _inproj_kernel(x_ref, shift_ref, scale_ref, g_ref, w_ref, b_ref, cos_ref, sin_ref, qg_ref, kg_ref,
                   seg_ref, zh_ref, q_ref, k_ref, v_ref, gate_ref):
    h = _modulated_norm(x_ref[...], g_ref[...], shift_ref[...], scale_ref[...]).astype(BF16)

    def proj(lo, hi):
        return _dot(h, w_ref[:, lo:hi]) + b_ref[:, lo:hi]

    zh_ref[...] = proj(COL_HY, COL_Q)
    cos, sin = cos_ref[...], sin_ref[...]
    seg = seg_ref[...]
    q = _rope(_head_rms(proj(COL_Q, COL_K), seg, qg_ref[...]), cos, sin)
    q_ref[...] = q.astype(BF16)
    k = _rope(_head_rms(proj(COL_K, COL_V), seg, kg_ref[...]), cos, sin)
    k_ref[...] = k.astype(BF16)
    v_ref[...] = proj(COL_V, COL_G_HY).astype(BF16)
    gate_ref[...] = jax.nn.sigmoid(proj(COL_G_HY, N_PROJ)).astype(BF16)


def _inproj(x2, shift, scale, gain, w_bf, b_in, cos, sin, qg, kg, seg, seq):
    T = x2.shape[0]
    tiles_per_batch = seq // TOK_TILE
    row = lambda i: (i, 0)
    per_batch = lambda i: (i // tiles_per_batch, 0, 0)
    const = lambda i: (0, 0)
    pos = lambda i: (i % tiles_per_batch, 0)
    return pl.pallas_call(
        _inproj_kernel,
        grid=(T // TOK_TILE,),
        in_specs=[
            pl.BlockSpec((TOK_TILE, D_MODEL), row),
            pl.BlockSpec((None, 1, D_MODEL), per_batch),
            pl.BlockSpec((None, 1, D_MODEL), per_batch),
            pl.BlockSpec((1, D_MODEL), const),
            pl.BlockSpec((D_MODEL, N_PROJ), const),
            pl.BlockSpec((1, N_PROJ), const),
            pl.BlockSpec((TOK_TILE, D_NA), pos),
            pl.BlockSpec((TOK_TILE, D_NA), pos),
            pl.BlockSpec((1, D_NA), const),
            pl.BlockSpec((1, D_NA), const),
            pl.BlockSpec((D_NA, D_NA), const),
        ],
        out_specs=[
            pl.BlockSpec((TOK_TILE, 3 * D_HY), row),
            pl.BlockSpec((TOK_TILE, D_NA), row),
            pl.BlockSpec((TOK_TILE, D_NA), row),
            pl.BlockSpec((TOK_TILE, D_NA), row),
            pl.BlockSpec((TOK_TILE, 2 * D_MODEL), row),
        ],
        out_shape=[
            jax.ShapeDtypeStruct((T, 3 * D_HY), F32),
            jax.ShapeDtypeStruct((T, D_NA), BF16),
            jax.ShapeDtypeStruct((T, D_NA), BF16),
            jax.ShapeDtypeStruct((T, D_NA), BF16),
            jax.ShapeDtypeStruct((T, 2 * D_MODEL), BF16),
        ],
        compiler_params=_cparams(("parallel",)),
        name="inproj",
    )(x2, shift, scale, gain, w_bf, b_in, cos, sin, qg, kg, seg)


def _ctxkv_kernel(x_ref, shift_ref, scale_ref, g_ref, wk_ref, bk_ref, wv_ref, bv_ref, kg_ref, seg_ref,
                  k_ref, v_ref):
    h = _modulated_norm(x_ref[...], g_ref[...], shift_ref[...], scale_ref[...]).astype(BF16)
    k = _head_rms(_dot(h, wk_ref[...]) + bk_ref[...], seg_ref[...], kg_ref[...])
    k_ref[...] = k.astype(BF16)
    v_ref[...] = (_dot(h, wv_ref[...]) + bv_ref[...]).astype(BF16)


def _ctxkv(ctx2, shift, scale, gain, wk, bk, wv, bv, kg, seg):
    T = ctx2.shape[0]
    row = lambda i: (i, 0)
    const = lambda i: (0, 0)
    return pl.pallas_call(
        _ctxkv_kernel,
        grid=(T // TOK_TILE,),
        in_specs=[
            pl.BlockSpec((TOK_TILE, D_MODEL), row),
            pl.BlockSpec((1, D_MODEL), const),
            pl.BlockSpec((1, D_MODEL), const),
            pl.BlockSpec((1, D_MODEL), const),
            pl.BlockSpec((D_MODEL, D_NA), const),
            pl.BlockSpec((1, D_NA), const),
            pl.BlockSpec((D_MODEL, D_NA), const),
            pl.BlockSpec((1, D_NA), const),
            pl.BlockSpec((1, D_NA), const),
            pl.BlockSpec((D_NA, D_NA), const),
        ],
        out_specs=[pl.BlockSpec((TOK_TILE, D_NA), row), pl.BlockSpec((TOK_TILE, D_NA), row)],
        out_shape=[jax.ShapeDtypeStruct((T, D_NA), BF16), jax.ShapeDtypeStruct((T, D_NA), BF16)],
        compiler_params=_cparams(("parallel",)),
        name="ctxkv",
    )(ctx2, shift, scale, gain, wk, bk, wv, bv, kg, seg)


def _natten_kernel(q_ref, k_ref, v_ref, kc_ref, vc_ref, bias_ref, o_ref, *, rows):
    r = pl.program_id(1)
    row_start = jnp.clip(r - WIN_ROWS // 2, 0, rows - WIN_ROWS)
    off = row_start - r + (WIN_ROWS - 1)
    start = pl.multiple_of(row_start * GRID_W, GRID_W)
    n_loc = WIN_ROWS * GRID_W
    scale = HEAD_DIM ** -0.5
    pair = 2 * HEAD_DIM
    lane = lax.broadcasted_iota(jnp.int32, (GRID_W, pair), 1)
    for p in range(NA_HEADS // 2):
        cols = slice(p * pair, (p + 1) * pair)
        qp = q_ref[:, cols]
        kp = k_ref[pl.ds(start, n_loc), cols]
        vp = v_ref[pl.ds(start, n_loc), cols]
        kcp = kc_ref[:, cols]
        vcp = vc_ref[:, cols]
        outs = []
        for hh in range(2):
            in_head = (lane // HEAD_DIM) == hh
            qm = jnp.where(in_head, qp, jnp.zeros_like(qp))
            s_loc = _dot_nt(qm, kp) * scale + bias_ref[off, 2 * p + hh]
            s_ctx = _dot_nt(qm, kcp) * scale
            m = jnp.maximum(jnp.max(s_loc, axis=-1, keepdims=True), jnp.max(s_ctx, axis=-1, keepdims=True))
            e_loc = jnp.exp(s_loc - m)
            e_ctx = jnp.exp(s_ctx - m)
            denom = jnp.sum(e_loc, axis=-1, keepdims=True) + jnp.sum(e_ctx, axis=-1, keepdims=True)
            o = _dot(e_loc.astype(BF16), vp) + _dot(e_ctx.astype(BF16), vcp)
            outs.append(o / denom)
        o_ref[:, cols] = jnp.where((lane // HEAD_DIM) == 0, outs[0], outs[1]).astype(o_ref.dtype)


def _natten(q, k, v, kc, vc, bias):
    B, S, _ = q.shape
    rows = S // GRID_W
    C = kc.shape[1]
    return pl.pallas_call(
        functools.partial(_natten_kernel, rows=rows),
        grid=(B, rows),
        in_specs=[
            pl.BlockSpec((None, GRID_W, D_NA), lambda b, r: (b, r, 0)),
            pl.BlockSpec((None, S, D_NA), lambda b, r: (b, 0, 0)),
            pl.BlockSpec((None, S, D_NA), lambda b, r: (b, 0, 0)),
            pl.BlockSpec((None, C, D_NA), lambda b, r: (b, 0, 0)),
            pl.BlockSpec((None, C, D_NA), lambda b, r: (b, 0, 0)),
            pl.BlockSpec(bias.shape, lambda b, r: (0, 0, 0, 0)),
        ],
        out_specs=pl.BlockSpec((None, GRID_W, D_NA), lambda b, r: (b, r, 0)),
        out_shape=jax.ShapeDtypeStruct((B, S, D_NA), BF16),
        compiler_params=_cparams(("parallel", "arbitrary")),
        name="natten",
    )(q, k, v, kc, vc, bias)


def _natten_bias(rpb):
    off = np.arange(WIN_ROWS)
    i = np.arange(WIN_ROWS)
    col = np.arange(GRID_W)
    dr = off[:, None] + i[None, :]
    dc = np.clip(col[None, :] - col[:, None], -(WIN_COLS - 1), WIN_COLS - 1) + (WIN_COLS - 1)
    col_start = np.clip(col - WIN_COLS // 2, 0, GRID_W - WIN_COLS)
    col_mask = (col[None, :] >= col_start[:, None]) & (col[None, :] < col_start[:, None] + WIN_COLS)
    b = rpb.astype(F32)[:, dr[:, None, :, None], dc[None, :, None, :]]
    b = jnp.where(col_mask[None, None, :, None, :], b, NEG_INF)
    return jnp.transpose(b, (1, 0, 2, 3, 4)).reshape(WIN_ROWS, NA_HEADS, GRID_W, WIN_ROWS * GRID_W)


def _rope_tables(seq):
    pos = jnp.arange(seq, dtype=jnp.int32)
    rows = (pos // GRID_W).astype(F32)
    cols = (pos % GRID_W).astype(F32)
    nf = HEAD_DIM // 4
    inv = ROPE_THETA ** (-jnp.arange(nf, dtype=F32) / nf)
    ar = rows[:, None] * inv[None, :]
    ac = cols[:, None] * inv[None, :]
    cos = jnp.concatenate([jnp.cos(ar), jnp.cos(ar), jnp.cos(ac), jnp.cos(ac)], axis=-1)
    sin = jnp.concatenate([-jnp.sin(ar), jnp.sin(ar), -jnp.sin(ac), jnp.sin(ac)], axis=-1)
    return jnp.tile(cos, (1, NA_HEADS)), jnp.tile(sin, (1, NA_HEADS))


def _postmix_kernel(yh_ref, yn_ref, gate_ref, x_ref, g1_ref, sh2_ref, sc2_ref, n2_ref, whyo_ref, wnao_ref,
                    wout_ref, wq_ref, x1_ref, h2_ref, pq_ref):
    a = _dot(yh_ref[...].astype(BF16), whyo_ref[...])
    b = _dot(yn_ref[...], wnao_ref[...])
    gate = gate_ref[...].astype(F32)
    merged = gate[:, :D_MODEL] * a + gate[:, D_MODEL:] * b
    x1 = x_ref[...] + g1_ref[...] * _dot(merged.astype(BF16), wout_ref[...])
    x1_ref[...] = x1
    h2 = _modulated_norm(x1, n2_ref[...], sh2_ref[...], sc2_ref[...]).astype(BF16)
    h2_ref[...] = h2
    for hp in range(2 * PEER_HEADS):
        pq_ref[hp] = _dot(h2, wq_ref[:, hp * PEER_D_HALF:(hp + 1) * PEER_D_HALF]).astype(BF16)


def _postmix(yh, yn, gate, x2, g1, sh2, sc2, n2g, whyo, wnao, wout, wq, seq):
    T = x2.shape[0]
    tiles_per_batch = seq // TOK_TILE
    row = lambda i: (i, 0)
    per_batch = lambda i: (i // tiles_per_batch, 0, 0)
    const = lambda i: (0, 0)
    nq = wq.shape[1]
    return pl.pallas_call(
        _postmix_kernel,
        grid=(T // TOK_TILE,),
        in_specs=[
            pl.BlockSpec((TOK_TILE, D_HY), row),
            pl.BlockSpec((TOK_TILE, D_NA), row),
            pl.BlockSpec((TOK_TILE, 2 * D_MODEL), row),
            pl.BlockSpec((TOK_TILE, D_MODEL), row),
            pl.BlockSpec((None, 1, D_MODEL), per_batch),
            pl.BlockSpec((None, 1, D_MODEL), per_batch),
            pl.BlockSpec((None, 1, D_MODEL), per_batch),
            pl.BlockSpec((1, D_MODEL), const),
            pl.BlockSpec((D_HY, D_MODEL), const),
            pl.BlockSpec((D_NA, D_MODEL), const),
            pl.BlockSpec((D_MODEL, D_MODEL), const),
            pl.BlockSpec((D_MODEL, nq), const),
        ],
        out_specs=[
            pl.BlockSpec((TOK_TILE, D_MODEL), row),
            pl.BlockSpec((TOK_TILE, D_MODEL), row),
            pl.BlockSpec((nq // PEER_D_HALF, TOK_TILE, PEER_D_HALF), lambda i: (0, i, 0)),
        ],
        out_shape=[
            jax.ShapeDtypeStruct((T, D_MODEL), F32),
            jax.ShapeDtypeStruct((T, D_MODEL), BF16),
            jax.ShapeDtypeStruct((nq // PEER_D_HALF, T, PEER_D_HALF), BF16),
        ],
        compiler_params=_cparams(("parallel",)),
        name="postmix",
    )(yh, yn, gate, x2, g1, sh2, sc2, n2g, whyo, wnao, wout, wq)


SUBLANES = 8
ROUTE_TILE = 256


def _argmax_tree(vals, idxs):
    while len(vals) > 1:
        nv, ni = [], []
        for j in range(0, len(vals) - 1, 2):
            right = vals[j + 1] > vals[j]
            nv.append(jnp.where(right, vals[j + 1], vals[j]))
            ni.append(jnp.where(right, idxs[j + 1], idxs[j]))
        if len(vals) % 2:
            nv.append(vals[-1])
            ni.append(idxs[-1])
        vals, idxs = nv, ni
    return vals[0], idxs[0]


def _route_kernel(pq_ref, keys_ref, a_ref, b_ref, g_ref, tv_ref, ti_ref):
    rt = pq_ref.shape[1]
    n_grp = PEER_N_KEYS // SUBLANES
    sub = lax.broadcasted_iota(jnp.int32, (SUBLANES, rt), 0)
    key_idx = [sub + SUBLANES * g for g in range(n_grp)]

    def first_level(hp, carry):
        s = _dot_nt(keys_ref[hp], pq_ref[hp]).reshape(n_grp, SUBLANES, rt)
        slabs = [s[g] for g in range(n_grp)]
        h, p = hp // 2, hp % 2
        for j in range(PEER_TOPK):
            v, i = _argmax_tree(slabs, key_idx)
            for shift in (4, 2, 1):
                v2, i2 = pltpu.roll(v, shift, 0), pltpu.roll(i, shift, 0)
                take = (v2 > v) | ((v2 == v) & (i2 < i))
                v, i = jnp.where(take, v2, v), jnp.where(take, i2, i)
            tv_ref[p, j, pl.ds(h, 1), :] = v[0:1]
            ti_ref[p, j, pl.ds(h, 1), :] = i[0:1]
            slabs = [jnp.where(key_idx[g] == i, -jnp.inf, slabs[g]) for g in range(n_grp)]
        return carry

    lax.fori_loop(0, 2 * PEER_HEADS, first_level, 0)

    pairs = [(a, b) for a in range(PEER_TOPK) for b in range(PEER_TOPK) if (a + 1) * (b + 1) <= PEER_TOPK]
    cand = [tv_ref[0, a] + tv_ref[1, b] for a, b in pairs]
    pos = [jnp.full((PEER_HEADS, rt), a * PEER_TOPK + b, jnp.int32) for a, b in pairs]
    best, sel_a, sel_b = [], [], []
    for j in range(PEER_TOPK):
        v, w = _argmax_tree(cand, pos)
        best.append(v)
        ia = jnp.zeros((PEER_HEADS, rt), jnp.int32)
        ib = jnp.zeros((PEER_HEADS, rt), jnp.int32)
        for n, (a, b) in enumerate(pairs):
            hit = w == (a * PEER_TOPK + b)
            cand[n] = jnp.where(hit, -jnp.inf, cand[n])
            ia = jnp.where(hit, ti_ref[0, a], ia)
            ib = jnp.where(hit, ti_ref[1, b], ib)
        sel_a.append(ia)
        sel_b.append(ib)
    ex = [jnp.exp(v - best[0]) for v in best]
    denom = functools.reduce(lambda x, y: x + y, ex)
    gates = [e / denom for e in ex]
    a_ref[...] = jnp.concatenate(sel_a, axis=0).T
    b_ref[...] = jnp.concatenate(sel_b, axis=0).T
    g_ref[...] = jnp.concatenate(gates, axis=0).T


def _route(pq, keys_bf):
    n_hp, T, _ = pq.shape
    n_sel = PEER_HEADS * PEER_TOPK
    row = lambda i: (i, 0)
    return pl.pallas_call(
        _route_kernel,
        grid=(T // ROUTE_TILE,),
        in_specs=[
            pl.BlockSpec((n_hp, ROUTE_TILE, PEER_D_HALF), lambda i: (0, i, 0)),
            pl.BlockSpec((n_hp, PEER_N_KEYS, PEER_D_HALF), lambda i: (0, 0, 0)),
        ],
        out_specs=[pl.BlockSpec((ROUTE_TILE, n_sel), row)] * 3,
        out_shape=[
            jax.ShapeDtypeStruct((T, n_sel), jnp.int32),
            jax.ShapeDtypeStruct((T, n_sel), jnp.int32),
            jax.ShapeDtypeStruct((T, n_sel), F32),
        ],
        scratch_shapes=[
            pltpu.VMEM((2, PEER_TOPK, PEER_HEADS, ROUTE_TILE), F32),
            pltpu.VMEM((2, PEER_TOPK, PEER_HEADS, ROUTE_TILE), jnp.int32),
        ],
        compiler_params=_cparams(("parallel",)),
        name="route",
    )(pq, keys_bf)


WBUILD_TILE = 128


def _wbuild_kernel(a_ref, b_ref, g_ref, w_ref):
    n = PEER_N_KEYS
    key = lax.broadcasted_iota(jnp.int32, (n, a_ref.shape[1]), 0)

    def body(t, carry):
        a = a_ref[pl.ds(t, 1), :]
        b = b_ref[pl.ds(t, 1), :]
        g = g_ref[pl.ds(t, 1), :]
        ga = jnp.where(a == key, g, 0.0).astype(BF16)
        ob = jnp.where(b == key, 1.0, 0.0).astype(BF16)
        w_ref[t] = _dot_nt(ga, ob).astype(BF16)
        return carry

    lax.fori_loop(0, a_ref.shape[0], body, 0)


def _wbuild(a, b, g):
    T, n_sel = a.shape
    row = lambda i: (i, 0)
    return pl.pallas_call(
        _wbuild_kernel,
        grid=(T // WBUILD_TILE,),
        in_specs=[pl.BlockSpec((WBUILD_TILE, n_sel), row)] * 3,
        out_specs=pl.BlockSpec((WBUILD_TILE, PEER_N_KEYS, PEER_N_KEYS), lambda i: (i, 0, 0)),
        out_shape=jax.ShapeDtypeStruct((T, PEER_N_KEYS, PEER_N_KEYS), BF16),
        compiler_params=_cparams(("parallel",)),
        name="wbuild",
    )(a, b, g)


def _peer_dense_kernel(h2_ref, w_ref, u_ref, v_ref, x1_ref, g2_ref, o_ref, acc_ref):
    e = pl.program_id(1)

    @pl.when(e == 0)
    def _():
        acc_ref[...] = jnp.zeros_like(acc_ref)

    a = _dot_nt(h2_ref[...], u_ref[...])
    act = 0.5 * a * (1.0 + lax.erf(a * (2.0 ** -0.5)))
    acc_ref[...] += _dot((w_ref[...].astype(F32) * act).astype(BF16), v_ref[...])

    @pl.when(e == pl.num_programs(1) - 1)
    def _():
        o_ref[...] = x1_ref[...] + g2_ref[...] * acc_ref[...]


def _peer_dense(h2, w, u_bf, v_bf, x1, g2, seq):
    T = h2.shape[0]
    tiles_per_batch = seq // PEER_TOK_TILE
    n_exp = u_bf.shape[0]
    return pl.pallas_call(
        _peer_dense_kernel,
        grid=(T // PEER_TOK_TILE, n_exp // PEER_EXP_TILE),
        in_specs=[
            pl.BlockSpec((PEER_TOK_TILE, D_MODEL), lambda i, e: (i, 0)),
            pl.BlockSpec((PEER_TOK_TILE, PEER_EXP_TILE), lambda i, e: (i, e)),
            pl.BlockSpec((PEER_EXP_TILE, D_MODEL), lambda i, e: (e, 0)),
            pl.BlockSpec((PEER_EXP_TILE, D_MODEL), lambda i, e: (e, 0)),
            pl.BlockSpec((PEER_TOK_TILE, D_MODEL), lambda i, e: (i, 0)),
            pl.BlockSpec((None, 1, D_MODEL), lambda i, e: (i // tiles_per_batch, 0, 0)),
        ],
        out_specs=pl.BlockSpec((PEER_TOK_TILE, D_MODEL), lambda i, e: (i, 0)),
        out_shape=jax.ShapeDtypeStruct((T, D_MODEL), F32),
        scratch_shapes=[pltpu.VMEM((PEER_TOK_TILE, D_MODEL), F32)],
        compiler_params=_cparams(("parallel", "arbitrary")),
        name="peer_dense",
    )(h2, w, u_bf, v_bf, x1, g2)


def _hyena_filters(L, p):
    t = jnp.linspace(0.0, 1.0, L, dtype=F32)[:, None]
    w = (2.0 * math.pi / L) * jnp.arange(L, dtype=F32)[:, None]
    bands = jnp.linspace(1e-4, HY_BANDS - 1.0, HY_BANDS, dtype=F32)[None, :]
    feats = jnp.concatenate([t, jnp.cos(bands * w), -jnp.sin(bands * w)], axis=-1)
    freq = p['hy_sin_freq']
    h = jnp.sin(freq * (feats @ p['hy_f1_w'] + p['hy_f1_b']))
    h = jnp.sin(freq * (h @ p['hy_f2_w'] + p['hy_f2_b']))
    h = jnp.sin(freq * (h @ p['hy_f3_w'] + p['hy_f3_b']))
    h = (h @ p['hy_f4_w']).reshape(L, HY_ORDER, 2, D_HY)
    deltas = jnp.abs(jnp.linspace(math.log(HY_DECAY_TARGET) / HY_SLOW_PCT,
                                  math.log(HY_DECAY_TARGET) / HY_FAST_PCT, D_HY, dtype=F32))
    h = h * jnp.exp(-t * deltas)[:, None, None, :]
    h = h / jnp.sum(jnp.abs(h), axis=(0, 2), keepdims=True)
    fwd, bwd = h[:, :, 0], h[:, :, 1]
    return jnp.concatenate([fwd, jnp.zeros_like(fwd[:1]), bwd[:0:-1]], axis=0)


def _hyena(zh, p):
    B, L, _ = zh.shape
    cw = p['hy_conv_w']
    zp = jnp.pad(zh, ((0, 0), (1, 1), (0, 0)))
    zc = cw[0] * zp[:, :-2] + cw[1] * zp[:, 1:-1] + cw[2] * zp[:, 2:] + p['hy_conv_b']
    v, x1, x2 = jnp.split(zc, 3, axis=-1)
    kf = jnp.fft.rfft(_hyena_filters(L, p), axis=0)
    skip = p['hy_skip']
    y = v
    for o, gate in enumerate((x1, x2)):
        yf = jnp.fft.rfft(y, n=2 * L, axis=1)
        conv = jnp.fft.irfft(yf * kf[:, o][None], n=2 * L, axis=1)[:, :L]
        y = gate * (conv + skip[o] * y)
    return y


def kernel(x, c, ctx, c_ctx, norm1_g, norm2_g, w_ada, b_ada, w_in, b_in, hy_conv_w, hy_conv_b, hy_f1_w, hy_f1_b, hy_f2_w, hy_f2_b, hy_f3_w, hy_f3_b, hy_f4_w, hy_sin_freq, hy_skip, q_norm_g, k_norm_g, na_rpb, w_hy_out, w_na_out, w_out, peer_w_q, peer_keys, peer_u, peer_v):
    assert w_in.shape[0] == 1, "single-layer block"
    B, S, D = x.shape
    C = ctx.shape[1]
    T = B * S

    m_lat = (jax.nn.silu(c) @ w_ada[0] + b_ada[0]).reshape(B, N_MOD, 1, D)
    m_ctx = (jax.nn.silu(c_ctx) @ w_ada[0] + b_ada[0]).reshape(N_MOD, 1, D)
    sh1, sc1, g1, sh2, sc2, g2 = [m_lat[:, i] for i in range(N_MOD)]

    w_in_bf = w_in[0].astype(BF16)
    b_in2 = b_in[0][None, :]
    gain1 = norm1_g[0][None, :]
    qg = jnp.tile(q_norm_g[0], NA_HEADS)[None, :]
    kg = jnp.tile(k_norm_g[0], NA_HEADS)[None, :]
    seg = jnp.asarray(np.kron(np.eye(NA_HEADS), np.full((HEAD_DIM, HEAD_DIM), 1.0 / HEAD_DIM)), dtype=BF16)
    cos, sin = _rope_tables(S)

    x2 = x.reshape(T, D)
    zh, q, k, v, gate = _inproj(x2, sh1, sc1, gain1, w_in_bf, b_in2, cos, sin, qg, kg, seg, S)

    k_ctx, v_ctx = _ctxkv(ctx.reshape(B * C, D), m_ctx[0], m_ctx[1], gain1,
                          w_in_bf[:, COL_K:COL_V], b_in2[:, COL_K:COL_V],
                          w_in_bf[:, COL_V:COL_G_HY], b_in2[:, COL_V:COL_G_HY], kg, seg)

    p = {'hy_conv_w': hy_conv_w[0], 'hy_conv_b': hy_conv_b[0], 'hy_f1_w': hy_f1_w[0], 'hy_f1_b': hy_f1_b[0],
         'hy_f2_w': hy_f2_w[0], 'hy_f2_b': hy_f2_b[0], 'hy_f3_w': hy_f3_w[0], 'hy_f3_b': hy_f3_b[0],
         'hy_f4_w': hy_f4_w[0], 'hy_sin_freq': hy_sin_freq[0], 'hy_skip': hy_skip[0]}
    y_hy = _hyena(zh.reshape(B, S, 3 * D_HY), p).reshape(T, D_HY)

    y_na = _natten(q.reshape(B, S, D_NA), k.reshape(B, S, D_NA), v.reshape(B, S, D_NA),
                   k_ctx.reshape(B, C, D_NA), v_ctx.reshape(B, C, D_NA), _natten_bias(na_rpb[0])).reshape(T, D_NA)

    x1, h2, pq = _postmix(y_hy, y_na, gate, x2, g1, sh2, sc2, norm2_g[0][None, :],
                          w_hy_out[0].astype(BF16), w_na_out[0].astype(BF16), w_out[0].astype(BF16),
                          peer_w_q[0].astype(BF16), S)

    keys_bf = peer_keys[0].astype(BF16).reshape(2 * PEER_HEADS, PEER_N_KEYS, PEER_D_HALF)
    sel_a, sel_b, sel_g = _route(pq, keys_bf)
    w = _wbuild(sel_a, sel_b, sel_g).reshape(T, PEER_N_EXPERTS)
    out = _peer_dense(h2, w, peer_u[0].astype(BF16), peer_v[0].astype(BF16), x1, g2, S)
    return out.reshape(B, S, D)
```

```python
import functools
import math

import numpy as np
import jax
import jax.numpy as jnp
from jax import lax
from jax.experimental import pallas as pl
from jax.experimental.pallas import tpu as pltpu

F32 = jnp.float32
BF16 = jnp.bfloat16

D_MODEL = 1024
GRID_W = 64
EPS = 1e-6
N_MOD = 6

D_HY = 512
HY_ORDER = 2
HY_BANDS = 16
HY_DECAY_TARGET = 1e-2
HY_FAST_PCT = 0.3
HY_SLOW_PCT = 1.5

NA_HEADS = 8
HEAD_DIM = 64
D_NA = NA_HEADS * HEAD_DIM
WIN_ROWS = 8
WIN_COLS = 16
ROPE_THETA = 10000.0
NEG_INF = -1e30

PEER_HEADS = 8
PEER_N_KEYS = 128
PEER_N_EXPERTS = PEER_N_KEYS * PEER_N_KEYS
PEER_TOPK = 16
PEER_D_KEY = 256
PEER_D_HALF = PEER_D_KEY // 2

COL_HY = 0
COL_Q = COL_HY + 3 * D_HY
COL_K = COL_Q + D_NA
COL_V = COL_K + D_NA
COL_G_HY = COL_V + D_NA
N_PROJ = COL_G_HY + 2 * D_MODEL

VMEM_LIMIT = 56 * 1024 * 1024
TOK_TILE = 256
PEER_TOK_TILE = 512
PEER_EXP_TILE = 1024


def _cparams(sem):
    return pltpu.CompilerParams(dimension_semantics=sem, vmem_limit_bytes=VMEM_LIMIT)


def _dot(a, b):
    return jnp.dot(a, b, preferred_element_type=F32)


def _dot_nt(a, b):
    return lax.dot_general(a, b, (((1,), (1,)), ((), ())), preferred_element_type=F32)


def _modulated_norm(x, gain, shift, scale):
    ms = jnp.mean(x * x, axis=-1, keepdims=True)
    return (x * lax.rsqrt(ms + EPS) * gain) * (1.0 + scale) + shift


def _head_rms(z, seg, gain):
    zz = z * z
    hi = zz.astype(BF16)
    lo = (zz - hi.astype(F32)).astype(BF16)
    ms = _dot(hi, seg) + _dot(lo, seg)
    return z * lax.rsqrt(ms + EPS) * gain


def _rope(z, cos, sin_signed):
    n = z.shape[-1]
    lane = lax.broadcasted_iota(jnp.int32, z.shape, 1)
    first = (lane // (HEAD_DIM // 4)) % 2 == 0
    partner = jnp.where(first, pltpu.roll(z, n - HEAD_DIM // 4, 1), pltpu.roll(z, HEAD_DIM // 4, 1))
    return z * cos + partner * sin_signed


def _inproj_kernel(x_ref, shift_ref, scale_ref, g_ref, w_ref, b_ref, cos_ref, sin_ref, qg_ref, kg_ref,
                   seg_ref, zh_ref, q_ref, k_ref, v_ref, gate_ref):
    h = _modulated_norm(x_ref[...], g_ref[...], shift_ref[...], scale_ref[...]).astype(BF16)

    def proj(lo, hi):
        return _dot(h, w_ref[:, lo:hi]) + b_ref[:, lo:hi]

    zh_ref[...] = proj(COL_HY, COL_Q)
    cos, sin = cos_ref[...], sin_ref[...]
    seg = seg_ref[...]
    q = _rope(_head_rms(proj(COL_Q, COL_K), seg, qg_ref[...]), cos, sin)
    q_ref[...] = q.astype(BF16)
    k = _rope(_head_rms(proj(COL_K, COL_V), seg, kg_ref[...]), cos, sin)
    k_ref[...] = k.astype(BF16)
    v_ref[...] = proj(COL_V, COL_G_HY).astype(BF16)
    gate_ref[...] = jax.nn.sigmoid(proj(COL_G_HY, N_PROJ)).astype(BF16)


def _inproj(x2, shift, scale, gain, w_bf, b_in, cos, sin, qg, kg, seg, seq):
    T = x2.shape[0]
    tiles_per_batch = seq // TOK_TILE
    row = lambda i: (i, 0)
    per_batch = lambda i: (i // tiles_per_batch, 0, 0)
    const = lambda i: (0, 0)
    pos = lambda i: (i % tiles_per_batch, 0)
    return pl.pallas_call(
        _inproj_kernel,
        grid=(T // TOK_TILE,),
        in_specs=[
            pl.BlockSpec((TOK_TILE, D_MODEL), row),
            pl.BlockSpec((None, 1, D_MODEL), per_batch),
            pl.BlockSpec((None, 1, D_MODEL), per_batch),
            pl.BlockSpec((1, D_MODEL), const),
            pl.BlockSpec((D_MODEL, N_PROJ), const),
            pl.BlockSpec((1, N_PROJ), const),
            pl.BlockSpec((TOK_TILE, D_NA), pos),
            pl.BlockSpec((TOK_TILE, D_NA), pos),
            pl.BlockSpec((1, D_NA), const),
            pl.BlockSpec((1, D_NA), const),
            pl.BlockSpec((D_NA, D_NA), const),
        ],
        out_specs=[
            pl.BlockSpec((TOK_TILE, 3 * D_HY), row),
            pl.BlockSpec((TOK_TILE, D_NA), row),
            pl.BlockSpec((TOK_TILE, D_NA), row),
            pl.BlockSpec((TOK_TILE, D_NA), row),
            pl.BlockSpec((TOK_TILE, 2 * D_MODEL), row),
        ],
        out_shape=[
            jax.ShapeDtypeStruct((T, 3 * D_HY), F32),
            jax.ShapeDtypeStruct((T, D_NA), BF16),
            jax.ShapeDtypeStruct((T, D_NA), BF16),
            jax.ShapeDtypeStruct((T, D_NA), BF16),
            jax.ShapeDtypeStruct((T, 2 * D_MODEL), BF16),
        ],
        compiler_params=_cparams(("parallel",)),
        name="inproj",
    )(x2, shift, scale, gain, w_bf, b_in, cos, sin, qg, kg, seg)


def _ctxkv_kernel(x_ref, shift_ref, scale_ref, g_ref, wk_ref, bk_ref, wv_ref, bv_ref, kg_ref, seg_ref,
                  k_ref, v_ref):
    h = _modulated_norm(x_ref[...], g_ref[...], shift_ref[...], scale_ref[...]).astype(BF16)
    k = _head_rms(_dot(h, wk_ref[...]) + bk_ref[...], seg_ref[...], kg_ref[...])
    k_ref[...] = k.astype(BF16)
    v_ref[...] = (_dot(h, wv_ref[...]) + bv_ref[...]).astype(BF16)


def _ctxkv(ctx2, shift, scale, gain, wk, bk, wv, bv, kg, seg):
    T = ctx2.shape[0]
    row = lambda i: (i, 0)
    const = lambda i: (0, 0)
    return pl.pallas_call(
        _ctxkv_kernel,
        grid=(T // TOK_TILE,),
        in_specs=[
            pl.BlockSpec((TOK_TILE, D_MODEL), row),
            pl.BlockSpec((1, D_MODEL), const),
            pl.BlockSpec((1, D_MODEL), const),
            pl.BlockSpec((1, D_MODEL), const),
            pl.BlockSpec((D_MODEL, D_NA), const),
            pl.BlockSpec((1, D_NA), const),
            pl.BlockSpec((D_MODEL, D_NA), const),
            pl.BlockSpec((1, D_NA), const),
            pl.BlockSpec((1, D_NA), const),
            pl.BlockSpec((D_NA, D_NA), const),
        ],
        out_specs=[pl.BlockSpec((TOK_TILE, D_NA), row), pl.BlockSpec((TOK_TILE, D_NA), row)],
        out_shape=[jax.ShapeDtypeStruct((T, D_NA), BF16), jax.ShapeDtypeStruct((T, D_NA), BF16)],
        compiler_params=_cparams(("parallel",)),
        name="ctxkv",
    )(ctx2, shift, scale, gain, wk, bk, wv, bv, kg, seg)


def _natten_kernel(q_ref, k_ref, v_ref, kc_ref, vc_ref, bias_ref, o_ref, *, rows):
    r = pl.program_id(1)
    row_start = jnp.clip(r - WIN_ROWS // 2, 0, rows - WIN_ROWS)
    off = row_start - r + (WIN_ROWS - 1)
    start = pl.multiple_of(row_start * GRID_W, GRID_W)
    n_loc = WIN_ROWS * GRID_W
    scale = HEAD_DIM ** -0.5
    pair = 2 * HEAD_DIM
    lane = lax.broadcasted_iota(jnp.int32, (GRID_W, pair), 1)
    for p in range(NA_HEADS // 2):
        cols = slice(p * pair, (p + 1) * pair)
        qp = q_ref[:, cols]
        kp = k_ref[pl.ds(start, n_loc), cols]
        vp = v_ref[pl.ds(start, n_loc), cols]
        kcp = kc_ref[:, cols]
        vcp = vc_ref[:, cols]
        outs = []
        for hh in range(2):
            in_head = (lane // HEAD_DIM) == hh
            qm = jnp.where(in_head, qp, jnp.zeros_like(qp))
            s_loc = _dot_nt(qm, kp) * scale + bias_ref[off, 2 * p + hh]
            s_ctx = _dot_nt(qm, kcp) * scale
            m = jnp.maximum(jnp.max(s_loc, axis=-1, keepdims=True), jnp.max(s_ctx, axis=-1, keepdims=True))
            e_loc = jnp.exp(s_loc - m)
            e_ctx = jnp.exp(s_ctx - m)
            denom = jnp.sum(e_loc, axis=-1, keepdims=True) + jnp.sum(e_ctx, axis=-1, keepdims=True)
            o = _dot(e_loc.astype(BF16), vp) + _dot(e_ctx.astype(BF16), vcp)
            outs.append(o / denom)
        o_ref[:, cols] = jnp.where((lane // HEAD_DIM) == 0, outs[0], outs[1]).astype(o_ref.dtype)


def _natten(q, k, v, kc, vc, bias):
    B, S, _ = q.shape
    rows = S // GRID_W
    C = kc.shape[1]
    return pl.pallas_call(
        functools.partial(_natten_kernel, rows=rows),
        grid=(B, rows),
        in_specs=[
            pl.BlockSpec((None, GRID_W, D_NA), lambda b, r: (b, r, 0)),
            pl.BlockSpec((None, S, D_NA), lambda b, r: (b, 0, 0)),
            pl.BlockSpec((None, S, D_NA), lambda b, r: (b, 0, 0)),
            pl.BlockSpec((None, C, D_NA), lambda b, r: (b, 0, 0)),
            pl.BlockSpec((None, C, D_NA), lambda b, r: (b, 0, 0)),
            pl.BlockSpec(bias.shape, lambda b, r: (0, 0, 0, 0)),
        ],
        out_specs=pl.BlockSpec((None, GRID_W, D_NA), lambda b, r: (b, r, 0)),
        out_shape=jax.ShapeDtypeStruct((B, S, D_NA), BF16),
        compiler_params=_cparams(("parallel", "arbitrary")),
        name="natten",
    )(q, k, v, kc, vc, bias)


def _natten_bias(rpb):
    off = np.arange(WIN_ROWS)
    i = np.arange(WIN_ROWS)
    col = np.arange(GRID_W)
    dr = off[:, None] + i[None, :]
    dc = np.clip(col[None, :] - col[:, None], -(WIN_COLS - 1), WIN_COLS - 1) + (WIN_COLS - 1)
    col_start = np.clip(col - WIN_COLS // 2, 0, GRID_W - WIN_COLS)
    col_mask = (col[None, :] >= col_start[:, None]) & (col[None, :] < col_start[:, None] + WIN_COLS)
    b = rpb.astype(F32)[:, dr[:, None, :, None], dc[None, :, None, :]]
    b = jnp.where(col_mask[None, None, :, None, :], b, NEG_INF)
    return jnp.transpose(b, (1, 0, 2, 3, 4)).reshape(WIN_ROWS, NA_HEADS, GRID_W, WIN_ROWS * GRID_W)


def _rope_tables(seq):
    pos = jnp.arange(seq, dtype=jnp.int32)
    rows = (pos // GRID_W).astype(F32)
    cols = (pos % GRID_W).astype(F32)
    nf = HEAD_DIM // 4
    inv = ROPE_THETA ** (-jnp.arange(nf, dtype=F32) / nf)
    ar = rows[:, None] * inv[None, :]
    ac = cols[:, None] * inv[None, :]
    cos = jnp.concatenate([jnp.cos(ar), jnp.cos(ar), jnp.cos(ac), jnp.cos(ac)], axis=-1)
    sin = jnp.concatenate([-jnp.sin(ar), jnp.sin(ar), -jnp.sin(ac), jnp.sin(ac)], axis=-1)
    return jnp.tile(cos, (1, NA_HEADS)), jnp.tile(sin, (1, NA_HEADS))


def _postmix_kernel(yh_ref, yn_ref, gate_ref, x_ref, g1_ref, sh2_ref, sc2_ref, n2_ref, whyo_ref, wnao_ref,
                    wout_ref, wq_ref, x1_ref, h2_ref, pq_ref):
    a = _dot(yh_ref[...].astype(BF16), whyo_ref[...])
    b = _dot(yn_ref[...], wnao_ref[...])
    gate = gate_ref[...].astype(F32)
    merged = gate[:, :D_MODEL] * a + gate[:, D_MODEL:] * b
    x1 = x_ref[...] + g1_ref[...] * _dot(merged.astype(BF16), wout_ref[...])
    x1_ref[...] = x1
    h2 = _modulated_norm(x1, n2_ref[...], sh2_ref[...], sc2_ref[...]).astype(BF16)
    h2_ref[...] = h2
    for hp in range(2 * PEER_HEADS):
        pq_ref[hp] = _dot(h2, wq_ref[:, hp * PEER_D_HALF:(hp + 1) * PEER_D_HALF]).astype(BF16)


def _postmix(yh, yn, gate, x2, g1, sh2, sc2, n2g, whyo, wnao, wout, wq, seq):
    T = x2.shape[0]
    tiles_per_batch = seq // TOK_TILE
    row = lambda i: (i, 0)
    per_batch = lambda i: (i // tiles_per_batch, 0, 0)
    const = lambda i: (0, 0)
    nq = wq.shape[1]
    return pl.pallas_call(
        _postmix_kernel,
        grid=(T // TOK_TILE,),
        in_specs=[
            pl.BlockSpec((TOK_TILE, D_HY), row),
            pl.BlockSpec((TOK_TILE, D_NA), row),
            pl.BlockSpec((TOK_TILE, 2 * D_MODEL), row),
            pl.BlockSpec((TOK_TILE, D_MODEL), row),
            pl.BlockSpec((None, 1, D_MODEL), per_batch),
            pl.BlockSpec((None, 1, D_MODEL), per_batch),
            pl.BlockSpec((None, 1, D_MODEL), per_batch),
            pl.BlockSpec((1, D_MODEL), const),
            pl.BlockSpec((D_HY, D_MODEL), const),
            pl.BlockSpec((D_NA, D_MODEL), const),
            pl.BlockSpec((D_MODEL, D_MODEL), const),
            pl.BlockSpec((D_MODEL, nq), const),
        ],
        out_specs=[
            pl.BlockSpec((TOK_TILE, D_MODEL), row),
            pl.BlockSpec((TOK_TILE, D_MODEL), row),
            pl.BlockSpec((nq // PEER_D_HALF, TOK_TILE, PEER_D_HALF), lambda i: (0, i, 0)),
        ],
        out_shape=[
            jax.ShapeDtypeStruct((T, D_MODEL), F32),
            jax.ShapeDtypeStruct((T, D_MODEL), BF16),
            jax.ShapeDtypeStruct((nq // PEER_D_HALF, T, PEER_D_HALF), BF16),
        ],
        compiler_params=_cparams(("parallel",)),
        name="postmix",
    )(yh, yn, gate, x2, g1, sh2, sc2, n2g, whyo, wnao, wout, wq)


SUBLANES = 8
ROUTE_TILE = 256


def _argmax_tree(vals, idxs):
    while len(vals) > 1:
        nv, ni = [], []
        for j in range(0, len(vals) - 1, 2):
            right = vals[j + 1] > vals[j]
            nv.append(jnp.where(right, vals[j + 1], vals[j]))
            ni.append(jnp.where(right, idxs[j + 1], idxs[j]))
        if len(vals) % 2:
            nv.append(vals[-1])
            ni.append(idxs[-1])
        vals, idxs = nv, ni
    return vals[0], idxs[0]


def _route_kernel(pq_ref, keys_ref, a_ref, b_ref, g_ref, tv_ref, ti_ref):
    rt = pq_ref.shape[1]
    n_grp = PEER_N_KEYS // SUBLANES
    sub = lax.broadcasted_iota(jnp.int32, (SUBLANES, rt), 0)
    key_idx = [sub + SUBLANES * g for g in range(n_grp)]

    def first_level(hp, carry):
        s = _dot_nt(keys_ref[hp], pq_ref[hp]).reshape(n_grp, SUBLANES, rt)
        slabs = [s[g] for g in range(n_grp)]
        h, p = hp // 2, hp % 2
        for j in range(PEER_TOPK):
            v, i = _argmax_tree(slabs, key_idx)
            for shift in (4, 2, 1):
                v2, i2 = pltpu.roll(v, shift, 0), pltpu.roll(i, shift, 0)
                take = (v2 > v) | ((v2 == v) & (i2 < i))
                v, i = jnp.where(take, v2, v), jnp.where(take, i2, i)
            tv_ref[p, j, pl.ds(h, 1), :] = v[0:1]
            ti_ref[p, j, pl.ds(h, 1), :] = i[0:1]
            slabs = [jnp.where(key_idx[g] == i, -jnp.inf, slabs[g]) for g in range(n_grp)]
        return carry

    lax.fori_loop(0, 2 * PEER_HEADS, first_level, 0)

    pairs = [(a, b) for a in range(PEER_TOPK) for b in range(PEER_TOPK) if (a + 1) * (b + 1) <= PEER_TOPK]
    cand = [tv_ref[0, a] + tv_ref[1, b] for a, b in pairs]
    pos = [jnp.full((PEER_HEADS, rt), a * PEER_TOPK + b, jnp.int32) for a, b in pairs]
    best, sel_a, sel_b = [], [], []
    for j in range(PEER_TOPK):
        v, w = _argmax_tree(cand, pos)
        best.append(v)
        ia = jnp.zeros((PEER_HEADS, rt), jnp.int32)
        ib = jnp.zeros((PEER_HEADS, rt), jnp.int32)
        for n, (a, b) in enumerate(pairs):
            hit = w == (a * PEER_TOPK + b)
            cand[n] = jnp.where(hit, -jnp.inf, cand[n])
            ia = jnp.where(hit, ti_ref[0, a], ia)
            ib = jnp.where(hit, ti_ref[1, b], ib)
        sel_a.append(ia)
        sel_b.append(ib)
    ex = [jnp.exp(v - best[0]) for v in best]
    denom = functools.reduce(lambda x, y: x + y, ex)
    gates = [e / denom for e in ex]
    a_ref[...] = jnp.concatenate(sel_a, axis=0).T
    b_ref[...] = jnp.concatenate(sel_b, axis=0).T
    g_ref[...] = jnp.concatenate(gates, axis=0).T


def _route(pq, keys_bf):
    n_hp, T, _ = pq.shape
    n_sel = PEER_HEADS * PEER_TOPK
    row = lambda i: (i, 0)
    return pl.pallas_call(
        _route_kernel,
        grid=(T // ROUTE_TILE,),
        in_specs=[
            pl.BlockSpec((n_hp, ROUTE_TILE, PEER_D_HALF), lambda i: (0, i, 0)),
            pl.BlockSpec((n_hp, PEER_N_KEYS, PEER_D_HALF), lambda i: (0, 0, 0)),
        ],
        out_specs=[pl.BlockSpec((ROUTE_TILE, n_sel), row)] * 3,
        out_shape=[
            jax.ShapeDtypeStruct((T, n_sel), jnp.int32),
            jax.ShapeDtypeStruct((T, n_sel), jnp.int32),
            jax.ShapeDtypeStruct((T, n_sel), F32),
        ],
        scratch_shapes=[
            pltpu.VMEM((2, PEER_TOPK, PEER_HEADS, ROUTE_TILE), F32),
            pltpu.VMEM((2, PEER_TOPK, PEER_HEADS, ROUTE_TILE), jnp.int32),
        ],
        compiler_params=_cparams(("parallel",)),
        name="route",
    )(pq, keys_bf)


WBUILD_TILE = 128


def _wbuild_kernel(a_ref, b_ref, g_ref, w_ref):
    n = PEER_N_KEYS
    key = lax.broadcasted_iota(jnp.int32, (n, a_ref.shape[1]), 0)

    def body(t, carry):
        a = a_ref[pl.ds(t, 1), :]
        b = b_ref[pl.ds(t, 1), :]
        g = g_ref[pl.ds(t, 1), :]
        ga = jnp.where(a == key, g, 0.0).astype(BF16)
        ob = jnp.where(b == key, 1.0, 0.0).astype(BF16)
        w_ref[t] = _dot_nt(ga, ob).astype(BF16)
        return carry

    lax.fori_loop(0, a_ref.shape[0], body, 0)


def _wbuild(a, b, g):
    T, n_sel = a.shape
    row = lambda i: (i, 0)
    return pl.pallas_call(
        _wbuild_kernel,
        grid=(T // WBUILD_TILE,),
        in_specs=[pl.BlockSpec((WBUILD_TILE, n_sel), row)] * 3,
        out_specs=pl.BlockSpec((WBUILD_TILE, PEER_N_KEYS, PEER_N_KEYS), lambda i: (i, 0, 0)),
        out_shape=jax.ShapeDtypeStruct((T, PEER_N_KEYS, PEER_N_KEYS), BF16),
        compiler_params=_cparams(("parallel",)),
        name="wbuild",
    )(a, b, g)


def _peer_dense_kernel(h2_ref, w_ref, u_ref, v_ref, x1_ref, g2_ref, o_ref, acc_ref):
    e = pl.program_id(1)

    @pl.when(e == 0)
    def _():
        acc_ref[...] = jnp.zeros_like(acc_ref)

    a = _dot_nt(h2_ref[...], u_ref[...])
    act = 0.5 * a * (1.0 + lax.erf(a * (2.0 ** -0.5)))
    acc_ref[...] += _dot((w_ref[...].astype(F32) * act).astype(BF16), v_ref[...])

    @pl.when(e == pl.num_programs(1) - 1)
    def _():
        o_ref[...] = x1_ref[...] + g2_ref[...] * acc_ref[...]


def _peer_dense(h2, w, u_bf, v_bf, x1, g2, seq):
    T = h2.shape[0]
    tiles_per_batch = seq // PEER_TOK_TILE
    n_exp = u_bf.shape[0]
    return pl.pallas_call(
        _peer_dense_kernel,
        grid=(T // PEER_TOK_TILE, n_exp // PEER_EXP_TILE),
        in_specs=[
            pl.BlockSpec((PEER_TOK_TILE, D_MODEL), lambda i, e: (i, 0)),
            pl.BlockSpec((PEER_TOK_TILE, PEER_EXP_TILE), lambda i, e: (i, e)),
            pl.BlockSpec((PEER_EXP_TILE, D_MODEL), lambda i, e: (e, 0)),
            pl.BlockSpec((PEER_EXP_TILE, D_MODEL), lambda i, e: (e, 0)),
            pl.BlockSpec((PEER_TOK_TILE, D_MODEL), lambda i, e: (i, 0)),
            pl.BlockSpec((None, 1, D_MODEL), lambda i, e: (i // tiles_per_batch, 0, 0)),
        ],
        out_specs=pl.BlockSpec((PEER_TOK_TILE, D_MODEL), lambda i, e: (i, 0)),
        out_shape=jax.ShapeDtypeStruct((T, D_MODEL), F32),
        scratch_shapes=[pltpu.VMEM((PEER_TOK_TILE, D_MODEL), F32)],
        compiler_params=_cparams(("parallel", "arbitrary")),
        name="peer_dense",
    )(h2, w, u_bf, v_bf, x1, g2)


HI = lax.Precision.HIGHEST


HY_LANES = 128


def _shortconv_kernel(z_ref, w_ref, b_ref, o_ref):
    z = z_ref[...]
    n = z.shape[0]
    row = lax.broadcasted_iota(jnp.int32, z.shape, 0)
    prev = jnp.where(row == 0, 0.0, pltpu.roll(z, 1, 0))
    nxt = jnp.where(row == n - 1, 0.0, pltpu.roll(z, n - 1, 0))
    o_ref[...] = w_ref[0:1] * prev + w_ref[1:2] * z + w_ref[2:3] * nxt + b_ref[...]


def _shortconv(zh, w, b):
    B, L, C = zh.shape
    return pl.pallas_call(
        _shortconv_kernel,
        grid=(B, C // HY_LANES),
        in_specs=[
            pl.BlockSpec((None, L, HY_LANES), lambda i, j: (i, 0, j)),
            pl.BlockSpec((3, HY_LANES), lambda i, j: (0, j)),
            pl.BlockSpec((1, HY_LANES), lambda i, j: (0, j)),
        ],
        out_specs=pl.BlockSpec((None, L, HY_LANES), lambda i, j: (i, 0, j)),
        out_shape=jax.ShapeDtypeStruct((B, L, C), F32),
        compiler_params=_cparams(("parallel", "parallel")),
        name="shortconv",
    )(zh, w, b)


FFT_R = 128
FFT_N = FFT_R * FFT_R
FFT_PITCH = 136
FFT_K1_CHUNK = 16


def _fft_tables():
    r = jnp.arange(FFT_R, dtype=jnp.int32)
    n2, k1, n1 = r[:, None, None], r[None, :, None], r[None, None, :]
    ang = (2.0 * math.pi / FFT_N) * ((k1 * (FFT_R * n1 + n2)) % FFT_N).astype(F32)
    c, s = jnp.cos(ang), jnp.sin(ang)
    g_full = jnp.concatenate([c, -s], axis=1)
    h_half = jnp.concatenate([jnp.swapaxes(c, 1, 2), -jnp.swapaxes(s, 1, 2)], axis=2)[:, :FFT_R // 2] / FFT_N
    ang2 = (2.0 * math.pi / FFT_R) * ((r[:, None] * r[None, :]) % FFT_R).astype(F32)
    c2, s2 = jnp.cos(ang2), jnp.sin(ang2)
    f_fwd = jnp.concatenate([jnp.concatenate([c2, s2], 1), jnp.concatenate([-s2, c2], 1)], 0)
    f_inv = jnp.concatenate([jnp.concatenate([c2, -s2], 1), jnp.concatenate([s2, c2], 1)], 0)
    return g_full.astype(BF16), h_half.astype(BF16), f_fwd.astype(BF16), f_inv.astype(BF16)


def _fft_stage1(load_slab, g_ref, bre_ref, bim_ref):
    def body(n2, carry):
        out = _dot(g_ref[n2], load_slab(n2).astype(BF16))
        bre_ref[pl.ds(n2, FFT_R, stride=FFT_PITCH), :] = out[:FFT_R]
        bim_ref[pl.ds(n2, FFT_R, stride=FFT_PITCH), :] = out[FFT_R:]
        return carry

    lax.fori_loop(0, FFT_R, body, 0)


def _load_k1_slab(bre_ref, bim_ref, k1):
    base = pl.multiple_of(k1 * FFT_PITCH, SUBLANES)
    return jnp.concatenate([bre_ref[pl.ds(base, FFT_R), :], bim_ref[pl.ds(base, FFT_R), :]], axis=0)


FILT_ROWS = 512
FILT_FEAT = 128


def _filt_kernel(feat_ref, delta_ref, w1_ref, b1_ref, w2_ref, b2_ref, w3_ref, b3_ref, w4_ref, freq_ref,
                 h_ref, sum_ref):
    @pl.when(pl.program_id(0) == 0)
    def _():
        sum_ref[...] = jnp.zeros_like(sum_ref)

    feat = feat_ref[...]
    freq = freq_ref[...]
    dot = lambda a, b: jnp.dot(a, b, precision=HI, preferred_element_type=F32)
    h = jnp.sin(freq * (dot(feat, w1_ref[...]) + b1_ref[...]))
    h = jnp.sin(freq * (dot(h, w2_ref[...]) + b2_ref[...]))
    h = jnp.sin(freq * (dot(h, w3_ref[...]) + b3_ref[...]))
    h = dot(h, w4_ref[...])
    n_feat = 1 + 2 * HY_BANDS
    t = feat[:, 0:1]
    forward = feat[:, n_feat:n_feat + 1] > 0.5
    keep = feat[:, n_feat + 1:n_feat + 2]
    decay = jnp.exp(-t * delta_ref[...])
    outs = []
    for o in range(HY_ORDER):
        base = o * 2 * D_HY
        outs.append(jnp.where(forward, h[:, base:base + D_HY], h[:, base + D_HY:base + 2 * D_HY]) * decay)
    out = jnp.concatenate(outs, axis=1)
    sum_ref[...] += jnp.sum(jnp.abs(out), axis=0, keepdims=True)
    h_ref[...] = out * keep


def _filt_features(L):
    r = np.arange(FFT_N)
    n = FFT_R * (r % FFT_R) + r // FFT_R
    j = np.where(n < L, n, np.where(n > L, 2 * L - n, 0))
    jj = jnp.asarray(j, dtype=F32)[:, None]
    t = jnp.linspace(0.0, 1.0, L, dtype=F32)[j][:, None]
    w = (2.0 * math.pi / L) * jj
    bands = jnp.linspace(1e-4, HY_BANDS - 1.0, HY_BANDS, dtype=F32)[None, :]
    flags = jnp.asarray(np.stack([n < L, n != L], axis=1), dtype=F32)
    feats = jnp.concatenate([t, jnp.cos(bands * w), -jnp.sin(bands * w), flags], axis=-1)
    return jnp.pad(feats, ((0, 0), (0, FILT_FEAT - feats.shape[1])))


def _filters(p, L):
    feats = _filt_features(L)
    deltas = jnp.abs(jnp.linspace(math.log(HY_DECAY_TARGET) / HY_SLOW_PCT,
                                  math.log(HY_DECAY_TARGET) / HY_FAST_PCT, D_HY, dtype=F32))[None, :]
    w1 = jnp.pad(p['hy_f1_w'], ((0, FILT_FEAT - p['hy_f1_w'].shape[0]), (0, 0)))
    n_out = HY_ORDER * D_HY
    const = lambda i: (0, 0)
    full = lambda a: pl.BlockSpec(a.shape, const)
    args = [deltas, w1, p['hy_f1_b'][None], p['hy_f2_w'], p['hy_f2_b'][None], p['hy_f3_w'], p['hy_f3_b'][None],
            p['hy_f4_w'], p['hy_sin_freq'][None]]
    return pl.pallas_call(
        _filt_kernel,
        grid=(FFT_N // FILT_ROWS,),
        in_specs=[pl.BlockSpec((FILT_ROWS, FILT_FEAT), lambda i: (i, 0))] + [full(a) for a in args],
        out_specs=[pl.BlockSpec((FILT_ROWS, n_out), lambda i: (i, 0)), pl.BlockSpec((1, n_out), const)],
        out_shape=[jax.ShapeDtypeStruct((FFT_N, n_out), F32), jax.ShapeDtypeStruct((1, n_out), F32)],
        compiler_params=_cparams(("arbitrary",)),
        name="hyena_filter",
    )(feats, *args)


def _fspec_kernel(h_ref, sum_ref, g_ref, f_ref, o_ref, bre_ref, bim_ref):
    c = pl.program_id(1)

    @pl.when(c == 0)
    def _():
        _fft_stage1(lambda n2: h_ref[pl.ds(pl.multiple_of(n2 * FFT_R, FFT_R), FFT_R), :], g_ref, bre_ref, bim_ref)

    inv = 1.0 / sum_ref[...]

    def slab(j, carry):
        x = _dot(f_ref[...], _load_k1_slab(bre_ref, bim_ref, c * FFT_K1_CHUNK + j).astype(BF16))
        rows = pl.ds(pl.multiple_of(j * FFT_R, FFT_R), FFT_R)
        o_ref[0, rows, :] = x[:FFT_R] * inv
        o_ref[1, rows, :] = x[FFT_R:] * inv
        return carry

    lax.fori_loop(0, FFT_K1_CHUNK, slab, 0)


def _filter_spectrum(h, hsum, g_full, f_fwd):
    n_ch = h.shape[1]
    single = pl.Buffered(1)
    return pl.pallas_call(
        _fspec_kernel,
        grid=(n_ch // HY_LANES, FFT_R // FFT_K1_CHUNK),
        in_specs=[
            pl.BlockSpec((FFT_N, HY_LANES), lambda j, c: (0, j), pipeline_mode=single),
            pl.BlockSpec((1, HY_LANES), lambda j, c: (0, j)),
            pl.BlockSpec(g_full.shape, lambda j, c: (0, 0, 0), pipeline_mode=single),
            pl.BlockSpec(f_fwd.shape, lambda j, c: (0, 0)),
        ],
        out_specs=pl.BlockSpec((2, FFT_K1_CHUNK * FFT_R, HY_LANES), lambda j, c: (0, c, j)),
        out_shape=jax.ShapeDtypeStruct((2, FFT_N, n_ch), F32),
        scratch_shapes=[pltpu.VMEM((FFT_R * FFT_PITCH, HY_LANES), F32)] * 2,
        compiler_params=_cparams(("parallel", "arbitrary")),
        name="hyena_filter_fft",
    )(h, hsum, g_full, f_fwd)


def _hyconv_kernel(u_ref, gate_ref, kf_ref, skip_ref, g_ref, h_ref, f_ref, fi_ref, o_ref, bre_ref, bim_ref):
    c = pl.program_id(2)
    half = FFT_R // 2

    def rows_of(n2):
        return pl.ds(pl.multiple_of(n2 * half, half), half)

    @pl.when(c == 0)
    def _():
        _fft_stage1(lambda n2: u_ref[rows_of(n2), :], g_ref, bre_ref, bim_ref)

    def slab(j, carry):
        k1 = c * FFT_K1_CHUNK + j
        x = _dot(f_ref[...], _load_k1_slab(bre_ref, bim_ref, k1).astype(BF16))
        xr, xi = x[:FFT_R], x[FFT_R:]
        rows = pl.ds(pl.multiple_of(j * FFT_R, FFT_R), FFT_R)
        kr, ki = kf_ref[0, rows, :], kf_ref[1, rows, :]
        y = jnp.concatenate([xr * kr - xi * ki, xr * ki + xi * kr], axis=0)
        z = _dot(fi_ref[...], y.astype(BF16))
        base = pl.multiple_of(k1 * FFT_PITCH, SUBLANES)
        bre_ref[pl.ds(base, FFT_R), :] = z[:FFT_R]
        bim_ref[pl.ds(base, FFT_R), :] = z[FFT_R:]
        return carry

    lax.fori_loop(0, FFT_K1_CHUNK, slab, 0)

    @pl.when(c == pl.num_programs(2) - 1)
    def _():
        def body(n2, carry):
            z = jnp.concatenate([bre_ref[pl.ds(n2, FFT_R, stride=FFT_PITCH), :],
                                 bim_ref[pl.ds(n2, FFT_R, stride=FFT_PITCH), :]], axis=0)
            conv = _dot(h_ref[n2], z.astype(BF16))
            rows = rows_of(n2)
            o_ref[rows, :] = gate_ref[rows, :] * (conv + skip_ref[...] * u_ref[rows, :])
            return carry

        lax.fori_loop(0, FFT_R, body, 0)


def _hyconv(u, u_col, gate, gate_col, kf, kf_col, skip, tables):
    g_half, h_half, f_fwd, f_inv = tables
    B, L, _ = u.shape
    single = pl.Buffered(1)
    return pl.pallas_call(
        _hyconv_kernel,
        grid=(B, D_HY // HY_LANES, FFT_R // FFT_K1_CHUNK),
        in_specs=[
            pl.BlockSpec((None, L, HY_LANES), lambda b, j, c: (b, 0, u_col + j), pipeline_mode=single),
            pl.BlockSpec((None, L, HY_LANES), lambda b, j, c: (b, 0, gate_col + j), pipeline_mode=single),
            pl.BlockSpec((2, FFT_K1_CHUNK * FFT_R, HY_LANES), lambda b, j, c: (0, c, kf_col + j)),
            pl.BlockSpec((1, HY_LANES), lambda b, j, c: (0, j)),
            pl.BlockSpec(g_half.shape, lambda b, j, c: (0, 0, 0), pipeline_mode=single),
            pl.BlockSpec(h_half.shape, lambda b, j, c: (0, 0, 0), pipeline_mode=single),
            pl.BlockSpec(f_fwd.shape, lambda b, j, c: (0, 0)),
            pl.BlockSpec(f_inv.shape, lambda b, j, c: (0, 0)),
        ],
        out_specs=pl.BlockSpec((None, L, HY_LANES), lambda b, j, c: (b, 0, j)),
        out_shape=jax.ShapeDtypeStruct((B, L, D_HY), F32),
        scratch_shapes=[pltpu.VMEM((FFT_R * FFT_PITCH, HY_LANES), F32)] * 2,
        compiler_params=_cparams(("parallel", "parallel", "arbitrary")),
        name="hyena_conv",
    )(u, gate, kf, skip, g_half, h_half, f_fwd, f_inv)


def _slab_major(a, n1):
    B, L, C = a.shape
    return a.reshape(B, n1, L // n1, C).transpose(0, 2, 1, 3).reshape(B, L, C)


def _hyena(zh, p):
    B, L, _ = zh.shape
    assert 2 * L == FFT_N
    g_full, h_half, f_fwd, f_inv = _fft_tables()
    tables = (g_full[:, :, :FFT_R // 2], h_half, f_fwd, f_inv)
    filt, filt_sum = _filters(p, L)
    both_dirs = filt_sum[:, :]
    kf = _filter_spectrum(filt, both_dirs, g_full, f_fwd)
    zc = _slab_major(_shortconv(zh, p['hy_conv_w'], p['hy_conv_b'][None]), FFT_R // 2)
    lanes = D_HY // HY_LANES
    y = _hyconv(zc, 0, zc, lanes, kf, 0, p['hy_skip'][0:1], tables)
    y = _hyconv(y, 0, zc, 2 * lanes, kf, lanes, p['hy_skip'][1:2], tables)
    return _slab_major(y, L // (FFT_R // 2))


def kernel(x, c, ctx, c_ctx, norm1_g, norm2_g, w_ada, b_ada, w_in, b_in, hy_conv_w, hy_conv_b, hy_f1_w, hy_f1_b, hy_f2_w, hy_f2_b, hy_f3_w, hy_f3_b, hy_f4_w, hy_sin_freq, hy_skip, q_norm_g, k_norm_g, na_rpb, w_hy_out, w_na_out, w_out, peer_w_q, peer_keys, peer_u, peer_v):
    assert w_in.shape[0] == 1, "single-layer block"
    B, S, D = x.shape
    C = ctx.shape[1]
    T = B * S

    m_lat = (jax.nn.silu(c) @ w_ada[0] + b_ada[0]).reshape(B, N_MOD, 1, D)
    m_ctx = (jax.nn.silu(c_ctx) @ w_ada[0] + b_ada[0]).reshape(N_MOD, 1, D)
    sh1, sc1, g1, sh2, sc2, g2 = [m_lat[:, i] for i in range(N_MOD)]

    w_in_bf = w_in[0].astype(BF16)
    b_in2 = b_in[0][None, :]
    gain1 = norm1_g[0][None, :]
    qg = jnp.tile(q_norm_g[0], NA_HEADS)[None, :]
    kg = jnp.tile(k_norm_g[0], NA_HEADS)[None, :]
    seg = jnp.asarray(np.kron(np.eye(NA_HEADS), np.full((HEAD_DIM, HEAD_DIM), 1.0 / HEAD_DIM)), dtype=BF16)
    cos, sin = _rope_tables(S)

    x2 = x.reshape(T, D)
    zh, q, k, v, gate = _inproj(x2, sh1, sc1, gain1, w_in_bf, b_in2, cos, sin, qg, kg, seg, S)

    k_ctx, v_ctx = _ctxkv(ctx.reshape(B * C, D), m_ctx[0], m_ctx[1], gain1,
                          w_in_bf[:, COL_K:COL_V], b_in2[:, COL_K:COL_V],
                          w_in_bf[:, COL_V:COL_G_HY], b_in2[:, COL_V:COL_G_HY], kg, seg)

    p = {'hy_conv_w': hy_conv_w[0], 'hy_conv_b': hy_conv_b[0], 'hy_f1_w': hy_f1_w[0], 'hy_f1_b': hy_f1_b[0],
         'hy_f2_w': hy_f2_w[0], 'hy_f2_b': hy_f2_b[0], 'hy_f3_w': hy_f3_w[0], 'hy_f3_b': hy_f3_b[0],
         'hy_f4_w': hy_f4_w[0], 'hy_sin_freq': hy_sin_freq[0], 'hy_skip': hy_skip[0]}
    y_hy = _hyena(zh.reshape(B, S, 3 * D_HY), p).reshape(T, D_HY)

    y_na = _natten(q.reshape(B, S, D_NA), k.reshape(B, S, D_NA), v.reshape(B, S, D_NA),
                   k_ctx.reshape(B, C, D_NA), v_ctx.reshape(B, C, D_NA), _natten_bias(na_rpb[0])).reshape(T, D_NA)

    x1, h2, pq = _postmix(y_hy, y_na, gate, x2, g1, sh2, sc2, norm2_g[0][None, :],
                          w_hy_out[0].astype(BF16), w_na_out[0].astype(BF16), w_out[0].astype(BF16),
                          peer_w_q[0].astype(BF16), S)

    keys_bf = peer_keys[0].astype(BF16).reshape(2 * PEER_HEADS, PEER_N_KEYS, PEER_D_HALF)
    sel_a, sel_b, sel_g = _route(pq, keys_bf)
    w = _wbuild(sel_a, sel_b, sel_g).reshape(T, PEER_N_EXPERTS)
    out = _peer_dense(h2, w, peer_u[0].astype(BF16), peer_v[0].astype(BF16), x1, g2, S)
    return out.reshape(B, S, D)
```

```python
import functools
import math

import numpy as np
import jax
import jax.numpy as jnp
from jax import lax
from jax.experimental import pallas as pl
from jax.experimental.pallas import tpu as pltpu

F32 = jnp.float32
BF16 = jnp.bfloat16

D_MODEL = 1024
GRID_W = 64
EPS = 1e-6
N_MOD = 6

D_HY = 512
HY_ORDER = 2
HY_BANDS = 16
HY_DECAY_TARGET = 1e-2
HY_FAST_PCT = 0.3
HY_SLOW_PCT = 1.5

NA_HEADS = 8
HEAD_DIM = 64
D_NA = NA_HEADS * HEAD_DIM
WIN_ROWS = 8
WIN_COLS = 16
ROPE_THETA = 10000.0
NEG_INF = -1e30

PEER_HEADS = 8
PEER_N_KEYS = 128
PEER_N_EXPERTS = PEER_N_KEYS * PEER_N_KEYS
PEER_TOPK = 16
PEER_D_KEY = 256
PEER_D_HALF = PEER_D_KEY // 2

COL_HY = 0
COL_Q = COL_HY + 3 * D_HY
COL_K = COL_Q + D_NA
COL_V = COL_K + D_NA
COL_G_HY = COL_V + D_NA
N_PROJ = COL_G_HY + 2 * D_MODEL

VMEM_LIMIT = 56 * 1024 * 1024
TOK_TILE = 256
PEER_TOK_TILE = 512
PEER_EXP_TILE = 1024


def _cparams(sem):
    return pltpu.CompilerParams(dimension_semantics=sem, vmem_limit_bytes=VMEM_LIMIT)


def _dot(a, b):
    return jnp.dot(a, b, preferred_element_type=F32)


def _dot_nt(a, b):
    return lax.dot_general(a, b, (((1,), (1,)), ((), ())), preferred_element_type=F32)


def _modulated_norm(x, gain, shift, scale):
    ms = jnp.mean(x * x, axis=-1, keepdims=True)
    return (x * lax.rsqrt(ms + EPS) * gain) * (1.0 + scale) + shift


def _head_rms(z, seg, gain):
    zz = z * z
    hi = zz.astype(BF16)
    lo = (zz - hi.astype(F32)).astype(BF16)
    ms = _dot(hi, seg) + _dot(lo, seg)
    return z * lax.rsqrt(ms + EPS) * gain


def _rope(z, cos, sin_signed):
    n = z.shape[-1]
    lane = lax.broadcasted_iota(jnp.int32, z.shape, 1)
    first = (lane // (HEAD_DIM // 4)) % 2 == 0
    partner = jnp.where(first, pltpu.roll(z, n - HEAD_DIM // 4, 1), pltpu.roll(z, HEAD_DIM // 4, 1))
    return z * cos + partner * sin_signed


def _inproj_kernel(x_ref, shift_ref, scale_ref, g_ref, w_ref, b_ref, cos_ref, sin_ref, qg_ref, kg_ref,
                   seg_ref, zh_ref, q_ref, k_ref, v_ref, gate_ref):
    h = _modulated_norm(x_ref[...], g_ref[...], shift_ref[...], scale_ref[...]).astype(BF16)

    def proj(lo, hi):
        return _dot(h, w_ref[:, lo:hi]) + b_ref[:, lo:hi]

    zh_ref[...] = proj(COL_HY, COL_Q)
    cos, sin = cos_ref[...], sin_ref[...]
    seg = seg_ref[...]
    q = _rope(_head_rms(proj(COL_Q, COL_K), seg, qg_ref[...]), cos, sin)
    q_ref[...] = q.astype(BF16)
    k = _rope(_head_rms(proj(COL_K, COL_V), seg, kg_ref[...]), cos, sin)
    k_ref[...] = k.astype(BF16)
    v_ref[...] = proj(COL_V, COL_G_HY).astype(BF16)
    gate_ref[...] = jax.nn.sigmoid(proj(COL_G_HY, N_PROJ)).astype(BF16)


def _inproj(x2, shift, scale, gain, w_bf, b_in, cos, sin, qg, kg, seg, seq):
    T = x2.shape[0]
    tiles_per_batch = seq // TOK_TILE
    row = lambda i: (i, 0)
    per_batch = lambda i: (i // tiles_per_batch, 0, 0)
    const = lambda i: (0, 0)
    pos = lambda i: (i % tiles_per_batch, 0)
    return pl.pallas_call(
        _inproj_kernel,
        grid=(T // TOK_TILE,),
        in_specs=[
            pl.BlockSpec((TOK_TILE, D_MODEL), row),
            pl.BlockSpec((None, 1, D_MODEL), per_batch),
            pl.BlockSpec((None, 1, D_MODEL), per_batch),
            pl.BlockSpec((1, D_MODEL), const),
            pl.BlockSpec((D_MODEL, N_PROJ), const),
            pl.BlockSpec((1, N_PROJ), const),
            pl.BlockSpec((TOK_TILE, D_NA), pos),
            pl.BlockSpec((TOK_TILE, D_NA), pos),
            pl.BlockSpec((1, D_NA), const),
            pl.BlockSpec((1, D_NA), const),
            pl.BlockSpec((D_NA, D_NA), const),
        ],
        out_specs=[
            pl.BlockSpec((TOK_TILE, 3 * D_HY), row),
            pl.BlockSpec((TOK_TILE, D_NA), row),
            pl.BlockSpec((TOK_TILE, D_NA), row),
            pl.BlockSpec((TOK_TILE, D_NA), row),
            pl.BlockSpec((TOK_TILE, 2 * D_MODEL), row),
        ],
        out_shape=[
            jax.ShapeDtypeStruct((T, 3 * D_HY), F32),
            jax.ShapeDtypeStruct((T, D_NA), BF16),
            jax.ShapeDtypeStruct((T, D_NA), BF16),
            jax.ShapeDtypeStruct((T, D_NA), BF16),
            jax.ShapeDtypeStruct((T, 2 * D_MODEL), BF16),
        ],
        compiler_params=_cparams(("parallel",)),
        name="inproj",
    )(x2, shift, scale, gain, w_bf, b_in, cos, sin, qg, kg, seg)


def _ctxkv_kernel(x_ref, shift_ref, scale_ref, g_ref, wk_ref, bk_ref, wv_ref, bv_ref, kg_ref, seg_ref,
                  k_ref, v_ref):
    h = _modulated_norm(x_ref[...], g_ref[...], shift_ref[...], scale_ref[...]).astype(BF16)
    k = _head_rms(_dot(h, wk_ref[...]) + bk_ref[...], seg_ref[...], kg_ref[...])
    k_ref[...] = k.astype(BF16)
    v_ref[...] = (_dot(h, wv_ref[...]) + bv_ref[...]).astype(BF16)


def _ctxkv(ctx2, shift, scale, gain, wk, bk, wv, bv, kg, seg):
    T = ctx2.shape[0]
    row = lambda i: (i, 0)
    const = lambda i: (0, 0)
    return pl.pallas_call(
        _ctxkv_kernel,
        grid=(T // TOK_TILE,),
        in_specs=[
            pl.BlockSpec((TOK_TILE, D_MODEL), row),
            pl.BlockSpec((1, D_MODEL), const),
            pl.BlockSpec((1, D_MODEL), const),
            pl.BlockSpec((1, D_MODEL), const),
            pl.BlockSpec((D_MODEL, D_NA), const),
            pl.BlockSpec((1, D_NA), const),
            pl.BlockSpec((D_MODEL, D_NA), const),
            pl.BlockSpec((1, D_NA), const),
            pl.BlockSpec((1, D_NA), const),
            pl.BlockSpec((D_NA, D_NA), const),
        ],
        out_specs=[pl.BlockSpec((TOK_TILE, D_NA), row), pl.BlockSpec((TOK_TILE, D_NA), row)],
        out_shape=[jax.ShapeDtypeStruct((T, D_NA), BF16), jax.ShapeDtypeStruct((T, D_NA), BF16)],
        compiler_params=_cparams(("parallel",)),
        name="ctxkv",
    )(ctx2, shift, scale, gain, wk, bk, wv, bv, kg, seg)


def _natten_kernel(q_ref, k_ref, v_ref, kc_ref, vc_ref, bias_ref, o_ref, *, rows):
    r = pl.program_id(1)
    row_start = jnp.clip(r - WIN_ROWS // 2, 0, rows - WIN_ROWS)
    off = row_start - r + (WIN_ROWS - 1)
    start = pl.multiple_of(row_start * GRID_W, GRID_W)
    n_loc = WIN_ROWS * GRID_W
    scale = HEAD_DIM ** -0.5
    pair = 2 * HEAD_DIM
    lane = lax.broadcasted_iota(jnp.int32, (GRID_W, pair), 1)
    for p in range(NA_HEADS // 2):
        cols = slice(p * pair, (p + 1) * pair)
        qp = q_ref[:, cols]
        kp = k_ref[pl.ds(start, n_loc), cols]
        vp = v_ref[pl.ds(start, n_loc), cols]
        kcp = kc_ref[:, cols]
        vcp = vc_ref[:, cols]
        outs = []
        for hh in range(2):
            in_head = (lane // HEAD_DIM) == hh
            qm = jnp.where(in_head, qp, jnp.zeros_like(qp))
            s_loc = _dot_nt(qm, kp) * scale + bias_ref[off, 2 * p + hh]
            s_ctx = _dot_nt(qm, kcp) * scale
            m = jnp.maximum(jnp.max(s_loc, axis=-1, keepdims=True), jnp.max(s_ctx, axis=-1, keepdims=True))
            e_loc = jnp.exp(s_loc - m)
            e_ctx = jnp.exp(s_ctx - m)
            denom = jnp.sum(e_loc, axis=-1, keepdims=True) + jnp.sum(e_ctx, axis=-1, keepdims=True)
            o = _dot(e_loc.astype(BF16), vp) + _dot(e_ctx.astype(BF16), vcp)
            outs.append(o / denom)
        o_ref[:, cols] = jnp.where((lane // HEAD_DIM) == 0, outs[0], outs[1]).astype(o_ref.dtype)


def _natten(q, k, v, kc, vc, bias):
    B, S, _ = q.shape
    rows = S // GRID_W
    C = kc.shape[1]
    return pl.pallas_call(
        functools.partial(_natten_kernel, rows=rows),
        grid=(B, rows),
        in_specs=[
            pl.BlockSpec((None, GRID_W, D_NA), lambda b, r: (b, r, 0)),
            pl.BlockSpec((None, S, D_NA), lambda b, r: (b, 0, 0)),
            pl.BlockSpec((None, S, D_NA), lambda b, r: (b, 0, 0)),
            pl.BlockSpec((None, C, D_NA), lambda b, r: (b, 0, 0)),
            pl.BlockSpec((None, C, D_NA), lambda b, r: (b, 0, 0)),
            pl.BlockSpec(bias.shape, lambda b, r: (0, 0, 0, 0)),
        ],
        out_specs=pl.BlockSpec((None, GRID_W, D_NA), lambda b, r: (b, r, 0)),
        out_shape=jax.ShapeDtypeStruct((B, S, D_NA), BF16),
        compiler_params=_cparams(("parallel", "arbitrary")),
        name="natten",
    )(q, k, v, kc, vc, bias)


def _natten_bias(rpb):
    col = np.arange(GRID_W)
    dc = np.clip(col[None, :] - col[:, None], -(WIN_COLS - 1), WIN_COLS - 1) + (WIN_COLS - 1)
    col_start = np.clip(col - WIN_COLS // 2, 0, GRID_W - WIN_COLS)
    col_mask = (col[None, :] >= col_start[:, None]) & (col[None, :] < col_start[:, None] + WIN_COLS)
    pick = jnp.asarray(dc[None] == np.arange(2 * WIN_COLS - 1)[:, None, None], dtype=F32)
    toep = jnp.einsum('hrd,dqk->hrqk', rpb.astype(F32), pick, precision=HI)
    toep = jnp.where(col_mask[None, None], toep, NEG_INF)
    wins = jnp.stack([toep[:, off:off + WIN_ROWS] for off in range(WIN_ROWS)], axis=0)
    return jnp.transpose(wins, (0, 1, 3, 2, 4)).reshape(WIN_ROWS, NA_HEADS, GRID_W, WIN_ROWS * GRID_W)


def _rope_tables(seq):
    pos = jnp.arange(seq, dtype=jnp.int32)
    rows = (pos // GRID_W).astype(F32)
    cols = (pos % GRID_W).astype(F32)
    nf = HEAD_DIM // 4
    inv = ROPE_THETA ** (-jnp.arange(nf, dtype=F32) / nf)
    ar = rows[:, None] * inv[None, :]
    ac = cols[:, None] * inv[None, :]
    cos = jnp.concatenate([jnp.cos(ar), jnp.cos(ar), jnp.cos(ac), jnp.cos(ac)], axis=-1)
    sin = jnp.concatenate([-jnp.sin(ar), jnp.sin(ar), -jnp.sin(ac), jnp.sin(ac)], axis=-1)
    return jnp.tile(cos, (1, NA_HEADS)), jnp.tile(sin, (1, NA_HEADS))


def _postmix_kernel(yh_ref, yn_ref, gate_ref, x_ref, g1_ref, sh2_ref, sc2_ref, n2_ref, whyo_ref, wnao_ref,
                    wout_ref, wq_ref, x1_ref, h2_ref, pq_ref):
    a = _dot(yh_ref[...].astype(BF16), whyo_ref[...])
    b = _dot(yn_ref[...], wnao_ref[...])
    gate = gate_ref[...].astype(F32)
    merged = gate[:, :D_MODEL] * a + gate[:, D_MODEL:] * b
    x1 = x_ref[...] + g1_ref[...] * _dot(merged.astype(BF16), wout_ref[...])
    x1_ref[...] = x1
    h2 = _modulated_norm(x1, n2_ref[...], sh2_ref[...], sc2_ref[...]).astype(BF16)
    h2_ref[...] = h2
    for hp in range(2 * PEER_HEADS):
        pq_ref[hp] = _dot(h2, wq_ref[:, hp * PEER_D_HALF:(hp + 1) * PEER_D_HALF]).astype(BF16)


def _postmix(yh, yn, gate, x2, g1, sh2, sc2, n2g, whyo, wnao, wout, wq, seq):
    T = x2.shape[0]
    tiles_per_batch = seq // TOK_TILE
    row = lambda i: (i, 0)
    per_batch = lambda i: (i // tiles_per_batch, 0, 0)
    const = lambda i: (0, 0)
    nq = wq.shape[1]
    return pl.pallas_call(
        _postmix_kernel,
        grid=(T // TOK_TILE,),
        in_specs=[
            pl.BlockSpec((TOK_TILE, D_HY), row),
            pl.BlockSpec((TOK_TILE, D_NA), row),
            pl.BlockSpec((TOK_TILE, 2 * D_MODEL), row),
            pl.BlockSpec((TOK_TILE, D_MODEL), row),
            pl.BlockSpec((None, 1, D_MODEL), per_batch),
            pl.BlockSpec((None, 1, D_MODEL), per_batch),
            pl.BlockSpec((None, 1, D_MODEL), per_batch),
            pl.BlockSpec((1, D_MODEL), const),
            pl.BlockSpec((D_HY, D_MODEL), const),
            pl.BlockSpec((D_NA, D_MODEL), const),
            pl.BlockSpec((D_MODEL, D_MODEL), const),
            pl.BlockSpec((D_MODEL, nq), const),
        ],
        out_specs=[
            pl.BlockSpec((TOK_TILE, D_MODEL), row),
            pl.BlockSpec((TOK_TILE, D_MODEL), row),
            pl.BlockSpec((nq // PEER_D_HALF, TOK_TILE, PEER_D_HALF), lambda i: (0, i, 0)),
        ],
        out_shape=[
            jax.ShapeDtypeStruct((T, D_MODEL), F32),
            jax.ShapeDtypeStruct((T, D_MODEL), BF16),
            jax.ShapeDtypeStruct((nq // PEER_D_HALF, T, PEER_D_HALF), BF16),
        ],
        compiler_params=_cparams(("parallel",)),
        name="postmix",
    )(yh, yn, gate, x2, g1, sh2, sc2, n2g, whyo, wnao, wout, wq)


SUBLANES = 8
ROUTE_TILE = 256


def _argmax_tree(vals, idxs):
    while len(vals) > 1:
        nv, ni = [], []
        for j in range(0, len(vals) - 1, 2):
            right = vals[j + 1] > vals[j]
            nv.append(jnp.where(right, vals[j + 1], vals[j]))
            ni.append(jnp.where(right, idxs[j + 1], idxs[j]))
        if len(vals) % 2:
            nv.append(vals[-1])
            ni.append(idxs[-1])
        vals, idxs = nv, ni
    return vals[0], idxs[0]


def _route_kernel(pq_ref, keys_ref, a_ref, b_ref, g_ref, tv_ref, ti_ref):
    rt = pq_ref.shape[1]
    n_grp = PEER_N_KEYS // SUBLANES
    sub = lax.broadcasted_iota(jnp.int32, (SUBLANES, rt), 0)
    key_idx = [sub + SUBLANES * g for g in range(n_grp)]

    def first_level(hp, carry):
        s = _dot_nt(keys_ref[hp], pq_ref[hp]).reshape(n_grp, SUBLANES, rt)
        slabs = [s[g] for g in range(n_grp)]
        h, p = hp // 2, hp % 2
        for j in range(PEER_TOPK):
            v, i = _argmax_tree(slabs, key_idx)
            for shift in (4, 2, 1):
                v2, i2 = pltpu.roll(v, shift, 0), pltpu.roll(i, shift, 0)
                take = (v2 > v) | ((v2 == v) & (i2 < i))
                v, i = jnp.where(take, v2, v), jnp.where(take, i2, i)
            tv_ref[p, j, pl.ds(h, 1), :] = v[0:1]
            ti_ref[p, j, pl.ds(h, 1), :] = i[0:1]
            slabs = [jnp.where(key_idx[g] == i, -jnp.inf, slabs[g]) for g in range(n_grp)]
        return carry

    lax.fori_loop(0, 2 * PEER_HEADS, first_level, 0)

    pairs = [(a, b) for a in range(PEER_TOPK) for b in range(PEER_TOPK) if (a + 1) * (b + 1) <= PEER_TOPK]
    cand = [tv_ref[0, a] + tv_ref[1, b] for a, b in pairs]
    pos = [jnp.full((PEER_HEADS, rt), a * PEER_TOPK + b, jnp.int32) for a, b in pairs]
    best, sel_a, sel_b = [], [], []
    for j in range(PEER_TOPK):
        v, w = _argmax_tree(cand, pos)
        best.append(v)
        ia = jnp.zeros((PEER_HEADS, rt), jnp.int32)
        ib = jnp.zeros((PEER_HEADS, rt), jnp.int32)
        for n, (a, b) in enumerate(pairs):
            hit = w == (a * PEER_TOPK + b)
            cand[n] = jnp.where(hit, -jnp.inf, cand[n])
            ia = jnp.where(hit, ti_ref[0, a], ia)
            ib = jnp.where(hit, ti_ref[1, b], ib)
        sel_a.append(ia)
        sel_b.append(ib)
    ex = [jnp.exp(v - best[0]) for v in best]
    denom = functools.reduce(lambda x, y: x + y, ex)
    gates = [e / denom for e in ex]
    a_ref[...] = jnp.concatenate(sel_a, axis=0).T
    b_ref[...] = jnp.concatenate(sel_b, axis=0).T
    g_ref[...] = jnp.concatenate(gates, axis=0).T


def _route(pq, keys_bf):
    n_hp, T, _ = pq.shape
    n_sel = PEER_HEADS * PEER_TOPK
    row = lambda i: (i, 0)
    return pl.pallas_call(
        _route_kernel,
        grid=(T // ROUTE_TILE,),
        in_specs=[
            pl.BlockSpec((n_hp, ROUTE_TILE, PEER_D_HALF), lambda i: (0, i, 0)),
            pl.BlockSpec((n_hp, PEER_N_KEYS, PEER_D_HALF), lambda i: (0, 0, 0)),
        ],
        out_specs=[pl.BlockSpec((ROUTE_TILE, n_sel), row)] * 3,
        out_shape=[
            jax.ShapeDtypeStruct((T, n_sel), jnp.int32),
            jax.ShapeDtypeStruct((T, n_sel), jnp.int32),
            jax.ShapeDtypeStruct((T, n_sel), F32),
        ],
        scratch_shapes=[
            pltpu.VMEM((2, PEER_TOPK, PEER_HEADS, ROUTE_TILE), F32),
            pltpu.VMEM((2, PEER_TOPK, PEER_HEADS, ROUTE_TILE), jnp.int32),
        ],
        compiler_params=_cparams(("parallel",)),
        name="route",
    )(pq, keys_bf)


WBUILD_TILE = 128


WBUILD_PITCH = 136
WBUILD_UNROLL = 8


def _wbuild_kernel(a_ref, b_ref, g_ref, w_ref, s_ref):
    n = PEER_N_KEYS
    tb = a_ref.shape[0]
    key = lax.broadcasted_iota(jnp.int32, (n, a_ref.shape[1]), 0)

    def body(t, carry):
        a = a_ref[pl.ds(t, 1), :]
        b = b_ref[pl.ds(t, 1), :]
        g = g_ref[pl.ds(t, 1), :]
        ga = jnp.where(a == key, g, 0.0).astype(BF16)
        ob = jnp.where(b == key, 1.0, 0.0).astype(BF16)
        s_ref[pl.ds(pl.multiple_of(t * WBUILD_PITCH, SUBLANES), n), :] = _dot_nt(ga, ob)
        return carry

    lax.fori_loop(0, tb, body, 0, unroll=WBUILD_UNROLL)

    def emit(i1, carry):
        w_ref[i1] = s_ref[pl.ds(i1, tb, stride=WBUILD_PITCH), :].astype(BF16)
        return carry

    lax.fori_loop(0, n, emit, 0, unroll=WBUILD_UNROLL)


def _wbuild(a, b, g):
    T, n_sel = a.shape
    row = lambda i: (i, 0)
    return pl.pallas_call(
        _wbuild_kernel,
        grid=(T // WBUILD_TILE,),
        in_specs=[pl.BlockSpec((WBUILD_TILE, n_sel), row)] * 3,
        out_specs=pl.BlockSpec((None, PEER_N_KEYS, WBUILD_TILE, PEER_N_KEYS), lambda i: (i, 0, 0, 0)),
        out_shape=jax.ShapeDtypeStruct((T // WBUILD_TILE, PEER_N_KEYS, WBUILD_TILE, PEER_N_KEYS), BF16),
        scratch_shapes=[pltpu.VMEM((WBUILD_TILE * WBUILD_PITCH, PEER_N_KEYS), F32)],
        compiler_params=_cparams(("parallel",)),
        name="wbuild",
    )(a, b, g)


def _peer_dense_kernel(h2_ref, w_ref, u_ref, v_ref, x1_ref, g2_ref, o_ref, acc_ref):
    e = pl.program_id(1)

    @pl.when(e == 0)
    def _():
        acc_ref[...] = jnp.zeros_like(acc_ref)

    a = _dot_nt(h2_ref[...], u_ref[...])
    act = 0.5 * a * (1.0 + lax.erf(a * (2.0 ** -0.5)))
    w = jnp.concatenate(
        [jnp.concatenate([w_ref[tb, i] for i in range(w_ref.shape[1])], axis=1) for tb in range(w_ref.shape[0])],
        axis=0)
    acc_ref[...] += _dot((w.astype(F32) * act).astype(BF16), v_ref[...])

    @pl.when(e == pl.num_programs(1) - 1)
    def _():
        o_ref[...] = x1_ref[...] + g2_ref[...] * acc_ref[...]


def _peer_dense(h2, w, u_bf, v_bf, x1, g2, seq):
    T = h2.shape[0]
    tiles_per_batch = seq // PEER_TOK_TILE
    n_exp = u_bf.shape[0]
    return pl.pallas_call(
        _peer_dense_kernel,
        grid=(T // PEER_TOK_TILE, n_exp // PEER_EXP_TILE),
        in_specs=[
            pl.BlockSpec((PEER_TOK_TILE, D_MODEL), lambda i, e: (i, 0)),
            pl.BlockSpec((PEER_TOK_TILE // WBUILD_TILE, PEER_EXP_TILE // PEER_N_KEYS, WBUILD_TILE, PEER_N_KEYS),
                         lambda i, e: (i, e, 0, 0)),
            pl.BlockSpec((PEER_EXP_TILE, D_MODEL), lambda i, e: (e, 0)),
            pl.BlockSpec((PEER_EXP_TILE, D_MODEL), lambda i, e: (e, 0)),
            pl.BlockSpec((PEER_TOK_TILE, D_MODEL), lambda i, e: (i, 0)),
            pl.BlockSpec((None, 1, D_MODEL), lambda i, e: (i // tiles_per_batch, 0, 0)),
        ],
        out_specs=pl.BlockSpec((PEER_TOK_TILE, D_MODEL), lambda i, e: (i, 0)),
        out_shape=jax.ShapeDtypeStruct((T, D_MODEL), F32),
        scratch_shapes=[pltpu.VMEM((PEER_TOK_TILE, D_MODEL), F32)],
        compiler_params=_cparams(("parallel", "arbitrary")),
        name="peer_dense",
    )(h2, w, u_bf, v_bf, x1, g2)


HI = lax.Precision.HIGHEST


HY_LANES = 128


def _shortconv_kernel(z_ref, w_ref, b_ref, o_ref):
    z = z_ref[...]
    n = z.shape[0]
    row = lax.broadcasted_iota(jnp.int32, z.shape, 0)
    prev = jnp.where(row == 0, 0.0, pltpu.roll(z, 1, 0))
    nxt = jnp.where(row == n - 1, 0.0, pltpu.roll(z, n - 1, 0))
    o_ref[...] = w_ref[0:1] * prev + w_ref[1:2] * z + w_ref[2:3] * nxt + b_ref[...]


def _shortconv(zh, w, b):
    B, L, C = zh.shape
    return pl.pallas_call(
        _shortconv_kernel,
        grid=(B, C // HY_LANES),
        in_specs=[
            pl.BlockSpec((None, L, HY_LANES), lambda i, j: (i, 0, j)),
            pl.BlockSpec((3, HY_LANES), lambda i, j: (0, j)),
            pl.BlockSpec((1, HY_LANES), lambda i, j: (0, j)),
        ],
        out_specs=pl.BlockSpec((None, L, HY_LANES), lambda i, j: (i, 0, j)),
        out_shape=jax.ShapeDtypeStruct((B, L, C), F32),
        compiler_params=_cparams(("parallel", "parallel")),
        name="shortconv",
    )(zh, w, b)


FFT_R = 128
FFT_N = FFT_R * FFT_R
FFT_PITCH = 136
FFT_K1_CHUNK = 16


def _fft_tables():
    r = jnp.arange(FFT_R, dtype=jnp.int32)
    n2, k1, n1 = r[:, None, None], r[None, :, None], r[None, None, :]
    ang = (2.0 * math.pi / FFT_N) * ((k1 * (FFT_R * n1 + n2)) % FFT_N).astype(F32)
    c, s = jnp.cos(ang), jnp.sin(ang)
    g_full = jnp.concatenate([c, -s], axis=1)
    h_half = jnp.concatenate([jnp.swapaxes(c, 1, 2), -jnp.swapaxes(s, 1, 2)], axis=2)[:, :FFT_R // 2] / FFT_N
    ang2 = (2.0 * math.pi / FFT_R) * ((r[:, None] * r[None, :]) % FFT_R).astype(F32)
    c2, s2 = jnp.cos(ang2), jnp.sin(ang2)
    f_fwd = jnp.concatenate([jnp.concatenate([c2, s2], 1), jnp.concatenate([-s2, c2], 1)], 0)
    f_inv = jnp.concatenate([jnp.concatenate([c2, -s2], 1), jnp.concatenate([s2, c2], 1)], 0)
    return g_full.astype(BF16), h_half.astype(BF16), f_fwd.astype(BF16), f_inv.astype(BF16)


def _fft_stage1(load_slab, g_ref, bre_ref, bim_ref):
    def body(n2, carry):
        out = _dot(g_ref[n2], load_slab(n2).astype(BF16))
        bre_ref[pl.ds(n2, FFT_R, stride=FFT_PITCH), :] = out[:FFT_R]
        bim_ref[pl.ds(n2, FFT_R, stride=FFT_PITCH), :] = out[FFT_R:]
        return carry

    lax.fori_loop(0, FFT_R, body, 0)


def _load_k1_slab(bre_ref, bim_ref, k1):
    base = pl.multiple_of(k1 * FFT_PITCH, SUBLANES)
    return jnp.concatenate([bre_ref[pl.ds(base, FFT_R), :], bim_ref[pl.ds(base, FFT_R), :]], axis=0)


FILT_ROWS = 512
FILT_FEAT = 128


def _filt_kernel(feat_ref, delta_ref, w1_ref, b1_ref, w2_ref, b2_ref, w3_ref, b3_ref, w4_ref, freq_ref,
                 h_ref, sum_ref):
    @pl.when(pl.program_id(0) == 0)
    def _():
        sum_ref[...] = jnp.zeros_like(sum_ref)

    feat = feat_ref[...]
    freq = freq_ref[...]
    dot = lambda a, b: jnp.dot(a, b, precision=HI, preferred_element_type=F32)
    h = jnp.sin(freq * (dot(feat, w1_ref[...]) + b1_ref[...]))
    h = jnp.sin(freq * (dot(h, w2_ref[...]) + b2_ref[...]))
    h = jnp.sin(freq * (dot(h, w3_ref[...]) + b3_ref[...]))
    h = dot(h, w4_ref[...])
    n_feat = 1 + 2 * HY_BANDS
    t = feat[:, 0:1]
    forward = feat[:, n_feat:n_feat + 1] > 0.5
    keep = feat[:, n_feat + 1:n_feat + 2]
    decay = jnp.exp(-t * delta_ref[...])
    outs = []
    for o in range(HY_ORDER):
        base = o * 2 * D_HY
        outs.append(jnp.where(forward, h[:, base:base + D_HY], h[:, base + D_HY:base + 2 * D_HY]) * decay)
    out = jnp.concatenate(outs, axis=1)
    sum_ref[...] += jnp.sum(jnp.abs(out), axis=0, keepdims=True)
    h_ref[...] = out * keep


def _filt_features(L):
    r = np.arange(FFT_N)
    n = FFT_R * (r % FFT_R) + r // FFT_R
    j = np.where(n < L, n, np.where(n > L, 2 * L - n, 0))
    jj = jnp.asarray(j, dtype=F32)[:, None]
    t = jnp.linspace(0.0, 1.0, L, dtype=F32)[j][:, None]
    w = (2.0 * math.pi / L) * jj
    bands = jnp.linspace(1e-4, HY_BANDS - 1.0, HY_BANDS, dtype=F32)[None, :]
    flags = jnp.asarray(np.stack([n < L, n != L], axis=1), dtype=F32)
    feats = jnp.concatenate([t, jnp.cos(bands * w), -jnp.sin(bands * w), flags], axis=-1)
    return jnp.pad(feats, ((0, 0), (0, FILT_FEAT - feats.shape[1])))


def _filters(p, L):
    feats = _filt_features(L)
    deltas = jnp.abs(jnp.linspace(math.log(HY_DECAY_TARGET) / HY_SLOW_PCT,
                                  math.log(HY_DECAY_TARGET) / HY_FAST_PCT, D_HY, dtype=F32))[None, :]
    w1 = jnp.pad(p['hy_f1_w'], ((0, FILT_FEAT - p['hy_f1_w'].shape[0]), (0, 0)))
    n_out = HY_ORDER * D_HY
    const = lambda i: (0, 0)
    full = lambda a: pl.BlockSpec(a.shape, const)
    args = [deltas, w1, p['hy_f1_b'][None], p['hy_f2_w'], p['hy_f2_b'][None], p['hy_f3_w'], p['hy_f3_b'][None],
            p['hy_f4_w'], p['hy_sin_freq'][None]]
    return pl.pallas_call(
        _filt_kernel,
        grid=(FFT_N // FILT_ROWS,),
        in_specs=[pl.BlockSpec((FILT_ROWS, FILT_FEAT), lambda i: (i, 0))] + [full(a) for a in args],
        out_specs=[pl.BlockSpec((FILT_ROWS, n_out), lambda i: (i, 0)), pl.BlockSpec((1, n_out), const)],
        out_shape=[jax.ShapeDtypeStruct((FFT_N, n_out), F32), jax.ShapeDtypeStruct((1, n_out), F32)],
        compiler_params=_cparams(("arbitrary",)),
        name="hyena_filter",
    )(feats, *args)


def _fspec_kernel(h_ref, sum_ref, g_ref, f_ref, o_ref, bre_ref, bim_ref):
    c = pl.program_id(1)

    @pl.when(c == 0)
    def _():
        _fft_stage1(lambda n2: h_ref[pl.ds(pl.multiple_of(n2 * FFT_R, FFT_R), FFT_R), :], g_ref, bre_ref, bim_ref)

    inv = 1.0 / sum_ref[...]

    def slab(j, carry):
        x = _dot(f_ref[...], _load_k1_slab(bre_ref, bim_ref, c * FFT_K1_CHUNK + j).astype(BF16))
        rows = pl.ds(pl.multiple_of(j * FFT_R, FFT_R), FFT_R)
        o_ref[0, rows, :] = x[:FFT_R] * inv
        o_ref[1, rows, :] = x[FFT_R:] * inv
        return carry

    lax.fori_loop(0, FFT_K1_CHUNK, slab, 0)


def _filter_spectrum(h, hsum, g_full, f_fwd):
    n_ch = h.shape[1]
    single = pl.Buffered(1)
    return pl.pallas_call(
        _fspec_kernel,
        grid=(n_ch // HY_LANES, FFT_R // FFT_K1_CHUNK),
        in_specs=[
            pl.BlockSpec((FFT_N, HY_LANES), lambda j, c: (0, j), pipeline_mode=single),
            pl.BlockSpec((1, HY_LANES), lambda j, c: (0, j)),
            pl.BlockSpec(g_full.shape, lambda j, c: (0, 0, 0), pipeline_mode=single),
            pl.BlockSpec(f_fwd.shape, lambda j, c: (0, 0)),
        ],
        out_specs=pl.BlockSpec((2, FFT_K1_CHUNK * FFT_R, HY_LANES), lambda j, c: (0, c, j)),
        out_shape=jax.ShapeDtypeStruct((2, FFT_N, n_ch), F32),
        scratch_shapes=[pltpu.VMEM((FFT_R * FFT_PITCH, HY_LANES), F32)] * 2,
        compiler_params=_cparams(("parallel", "arbitrary")),
        name="hyena_filter_fft",
    )(h, hsum, g_full, f_fwd)


def _hyconv_kernel(u_ref, gate_ref, kf_ref, skip_ref, g_ref, h_ref, f_ref, fi_ref, o_ref, bre_ref, bim_ref):
    c = pl.program_id(2)
    half = FFT_R // 2

    def rows_of(n2):
        return pl.ds(pl.multiple_of(n2 * half, half), half)

    @pl.when(c == 0)
    def _():
        _fft_stage1(lambda n2: u_ref[rows_of(n2), :], g_ref, bre_ref, bim_ref)

    def slab(j, carry):
        k1 = c * FFT_K1_CHUNK + j
        x = _dot(f_ref[...], _load_k1_slab(bre_ref, bim_ref, k1).astype(BF16))
        xr, xi = x[:FFT_R], x[FFT_R:]
        rows = pl.ds(pl.multiple_of(j * FFT_R, FFT_R), FFT_R)
        kr, ki = kf_ref[0, rows, :], kf_ref[1, rows, :]
        y = jnp.concatenate([xr * kr - xi * ki, xr * ki + xi * kr], axis=0)
        z = _dot(fi_ref[...], y.astype(BF16))
        base = pl.multiple_of(k1 * FFT_PITCH, SUBLANES)
        bre_ref[pl.ds(base, FFT_R), :] = z[:FFT_R]
        bim_ref[pl.ds(base, FFT_R), :] = z[FFT_R:]
        return carry

    lax.fori_loop(0, FFT_K1_CHUNK, slab, 0)

    @pl.when(c == pl.num_programs(2) - 1)
    def _():
        def body(n2, carry):
            z = jnp.concatenate([bre_ref[pl.ds(n2, FFT_R, stride=FFT_PITCH), :],
                                 bim_ref[pl.ds(n2, FFT_R, stride=FFT_PITCH), :]], axis=0)
            conv = _dot(h_ref[n2], z.astype(BF16))
            rows = rows_of(n2)
            o_ref[rows, :] = gate_ref[rows, :] * (conv + skip_ref[...] * u_ref[rows, :])
            return carry

        lax.fori_loop(0, FFT_R, body, 0)


def _hyconv(u, u_col, gate, gate_col, kf, kf_col, skip, tables):
    g_half, h_half, f_fwd, f_inv = tables
    B, L, _ = u.shape
    single = pl.Buffered(1)
    return pl.pallas_call(
        _hyconv_kernel,
        grid=(B, D_HY // HY_LANES, FFT_R // FFT_K1_CHUNK),
        in_specs=[
            pl.BlockSpec((None, L, HY_LANES), lambda b, j, c: (b, 0, u_col + j), pipeline_mode=single),
            pl.BlockSpec((None, L, HY_LANES), lambda b, j, c: (b, 0, gate_col + j), pipeline_mode=single),
            pl.BlockSpec((2, FFT_K1_CHUNK * FFT_R, HY_LANES), lambda b, j, c: (0, c, kf_col + j)),
            pl.BlockSpec((1, HY_LANES), lambda b, j, c: (0, j)),
            pl.BlockSpec(g_half.shape, lambda b, j, c: (0, 0, 0), pipeline_mode=single),
            pl.BlockSpec(h_half.shape, lambda b, j, c: (0, 0, 0), pipeline_mode=single),
            pl.BlockSpec(f_fwd.shape, lambda b, j, c: (0, 0)),
            pl.BlockSpec(f_inv.shape, lambda b, j, c: (0, 0)),
        ],
        out_specs=pl.BlockSpec((None, L, HY_LANES), lambda b, j, c: (b, 0, j)),
        out_shape=jax.ShapeDtypeStruct((B, L, D_HY), F32),
        scratch_shapes=[pltpu.VMEM((FFT_R * FFT_PITCH, HY_LANES), F32)] * 2,
        compiler_params=_cparams(("parallel", "parallel", "arbitrary")),
        name="hyena_conv",
    )(u, gate, kf, skip, g_half, h_half, f_fwd, f_inv)


def _slab_major(a, n1):
    B, L, C = a.shape
    return a.reshape(B, n1, L // n1, C).transpose(0, 2, 1, 3).reshape(B, L, C)


def _hyena(zh, p):
    B, L, _ = zh.shape
    assert 2 * L == FFT_N
    g_full, h_half, f_fwd, f_inv = _fft_tables()
    tables = (g_full[:, :, :FFT_R // 2], h_half, f_fwd, f_inv)
    filt, filt_sum = _filters(p, L)
    both_dirs = filt_sum[:, :]
    kf = _filter_spectrum(filt, both_dirs, g_full, f_fwd)
    zc = _slab_major(_shortconv(zh, p['hy_conv_w'], p['hy_conv_b'][None]), FFT_R // 2)
    lanes = D_HY // HY_LANES
    y = _hyconv(zc, 0, zc, lanes, kf, 0, p['hy_skip'][0:1], tables)
    y = _hyconv(y, 0, zc, 2 * lanes, kf, lanes, p['hy_skip'][1:2], tables)
    return _slab_major(y, L // (FFT_R // 2))


def kernel(x, c, ctx, c_ctx, norm1_g, norm2_g, w_ada, b_ada, w_in, b_in, hy_conv_w, hy_conv_b, hy_f1_w, hy_f1_b, hy_f2_w, hy_f2_b, hy_f3_w, hy_f3_b, hy_f4_w, hy_sin_freq, hy_skip, q_norm_g, k_norm_g, na_rpb, w_hy_out, w_na_out, w_out, peer_w_q, peer_keys, peer_u, peer_v):
    assert w_in.shape[0] == 1, "single-layer block"
    B, S, D = x.shape
    C = ctx.shape[1]
    T = B * S

    m_lat = (jax.nn.silu(c) @ w_ada[0] + b_ada[0]).reshape(B, N_MOD, 1, D)
    m_ctx = (jax.nn.silu(c_ctx) @ w_ada[0] + b_ada[0]).reshape(N_MOD, 1, D)
    sh1, sc1, g1, sh2, sc2, g2 = [m_lat[:, i] for i in range(N_MOD)]

    w_in_bf = w_in[0].astype(BF16)
    b_in2 = b_in[0][None, :]
    gain1 = norm1_g[0][None, :]
    qg = jnp.tile(q_norm_g[0], NA_HEADS)[None, :]
    kg = jnp.tile(k_norm_g[0], NA_HEADS)[None, :]
    seg = jnp.asarray(np.kron(np.eye(NA_HEADS), np.full((HEAD_DIM, HEAD_DIM), 1.0 / HEAD_DIM)), dtype=BF16)
    cos, sin = _rope_tables(S)

    x2 = x.reshape(T, D)
    zh, q, k, v, gate = _inproj(x2, sh1, sc1, gain1, w_in_bf, b_in2, cos, sin, qg, kg, seg, S)

    k_ctx, v_ctx = _ctxkv(ctx.reshape(B * C, D), m_ctx[0], m_ctx[1], gain1,
                          w_in_bf[:, COL_K:COL_V], b_in2[:, COL_K:COL_V],
                          w_in_bf[:, COL_V:COL_G_HY], b_in2[:, COL_V:COL_G_HY], kg, seg)

    p = {'hy_conv_w': hy_conv_w[0], 'hy_conv_b': hy_conv_b[0], 'hy_f1_w': hy_f1_w[0], 'hy_f1_b': hy_f1_b[0],
         'hy_f2_w': hy_f2_w[0], 'hy_f2_b': hy_f2_b[0], 'hy_f3_w': hy_f3_w[0], 'hy_f3_b': hy_f3_b[0],
         'hy_f4_w': hy_f4_w[0], 'hy_sin_freq': hy_sin_freq[0], 'hy_skip': hy_skip[0]}
    y_hy = _hyena(zh.reshape(B, S, 3 * D_HY), p).reshape(T, D_HY)

    y_na = _natten(q.reshape(B, S, D_NA), k.reshape(B, S, D_NA), v.reshape(B, S, D_NA),
                   k_ctx.reshape(B, C, D_NA), v_ctx.reshape(B, C, D_NA), _natten_bias(na_rpb[0])).reshape(T, D_NA)

    x1, h2, pq = _postmix(y_hy, y_na, gate, x2, g1, sh2, sc2, norm2_g[0][None, :],
                          w_hy_out[0].astype(BF16), w_na_out[0].astype(BF16), w_out[0].astype(BF16),
                          peer_w_q[0].astype(BF16), S)

    keys_bf = peer_keys[0].astype(BF16).reshape(2 * PEER_HEADS, PEER_N_KEYS, PEER_D_HALF)
    sel_a, sel_b, sel_g = _route(pq, keys_bf)
    w = _wbuild(sel_a, sel_b, sel_g)
    out = _peer_dense(h2, w, peer_u[0].astype(BF16), peer_v[0].astype(BF16), x1, g2, S)
    return out.reshape(B, S, D)
```

```python
import functools
import math

import numpy as np
import jax
import jax.numpy as jnp
from jax import lax
from jax.experimental import pallas as pl
from jax.experimental.pallas import tpu as pltpu

F32 = jnp.float32
BF16 = jnp.bfloat16

D_MODEL = 1024
GRID_W = 64
EPS = 1e-6
N_MOD = 6

D_HY = 512
HY_ORDER = 2
HY_BANDS = 16
HY_DECAY_TARGET = 1e-2
HY_FAST_PCT = 0.3
HY_SLOW_PCT = 1.5

NA_HEADS = 8
HEAD_DIM = 64
D_NA = NA_HEADS * HEAD_DIM
WIN_ROWS = 8
WIN_COLS = 16
ROPE_THETA = 10000.0
NEG_INF = -1e30

PEER_HEADS = 8
PEER_N_KEYS = 128
PEER_N_EXPERTS = PEER_N_KEYS * PEER_N_KEYS
PEER_TOPK = 16
PEER_D_KEY = 256
PEER_D_HALF = PEER_D_KEY // 2

COL_HY = 0
COL_Q = COL_HY + 3 * D_HY
COL_K = COL_Q + D_NA
COL_V = COL_K + D_NA
COL_G_HY = COL_V + D_NA
N_PROJ = COL_G_HY + 2 * D_MODEL

VMEM_LIMIT = 56 * 1024 * 1024
TOK_TILE = 256
PEER_TOK_TILE = 512
PEER_EXP_TILE = 1024


def _cparams(sem):
    return pltpu.CompilerParams(dimension_semantics=sem, vmem_limit_bytes=VMEM_LIMIT)


def _dot(a, b):
    return jnp.dot(a, b, preferred_element_type=F32)


def _dot_nt(a, b):
    return lax.dot_general(a, b, (((1,), (1,)), ((), ())), preferred_element_type=F32)


def _modulated_norm(x, gain, shift, scale):
    ms = jnp.mean(x * x, axis=-1, keepdims=True)
    return (x * lax.rsqrt(ms + EPS) * gain) * (1.0 + scale) + shift


def _head_rms(z, seg, gain):
    zz = z * z
    hi = zz.astype(BF16)
    lo = (zz - hi.astype(F32)).astype(BF16)
    ms = _dot(hi, seg) + _dot(lo, seg)
    return z * lax.rsqrt(ms + EPS) * gain


def _rope(z, cos, sin_signed):
    n = z.shape[-1]
    lane = lax.broadcasted_iota(jnp.int32, z.shape, 1)
    first = (lane // (HEAD_DIM // 4)) % 2 == 0
    partner = jnp.where(first, pltpu.roll(z, n - HEAD_DIM // 4, 1), pltpu.roll(z, HEAD_DIM // 4, 1))
    return z * cos + partner * sin_signed


def _inproj_kernel(x_ref, shift_ref, scale_ref, g_ref, w_ref, b_ref, cos_ref, sin_ref, qg_ref, kg_ref,
                   seg_ref, zh_ref, q_ref, k_ref, v_ref, gate_ref):
    h = _modulated_norm(x_ref[...], g_ref[...], shift_ref[...], scale_ref[...]).astype(BF16)

    def proj(lo, hi):
        return _dot(h, w_ref[:, lo:hi]) + b_ref[:, lo:hi]

    zh_ref[...] = proj(COL_HY, COL_Q)
    cos, sin = cos_ref[...], sin_ref[...]
    seg = seg_ref[...]
    q = _rope(_head_rms(proj(COL_Q, COL_K), seg, qg_ref[...]), cos, sin)
    q_ref[...] = q.astype(BF16)
    k = _rope(_head_rms(proj(COL_K, COL_V), seg, kg_ref[...]), cos, sin)
    k_ref[...] = k.astype(BF16)
    v_ref[...] = proj(COL_V, COL_G_HY).astype(BF16)
    gate_ref[...] = jax.nn.sigmoid(proj(COL_G_HY, N_PROJ)).astype(BF16)


def _inproj(x2, shift, scale, gain, w_bf, b_in, cos, sin, qg, kg, seg, seq):
    T = x2.shape[0]
    tiles_per_batch = seq // TOK_TILE
    row = lambda i: (i, 0)
    per_batch = lambda i: (i // tiles_per_batch, 0, 0)
    const = lambda i: (0, 0)
    pos = lambda i: (i % tiles_per_batch, 0)
    return pl.pallas_call(
        _inproj_kernel,
        grid=(T // TOK_TILE,),
        in_specs=[
            pl.BlockSpec((TOK_TILE, D_MODEL), row),
            pl.BlockSpec((None, 1, D_MODEL), per_batch),
            pl.BlockSpec((None, 1, D_MODEL), per_batch),
            pl.BlockSpec((1, D_MODEL), const),
            pl.BlockSpec((D_MODEL, N_PROJ), const),
            pl.BlockSpec((1, N_PROJ), const),
            pl.BlockSpec((TOK_TILE, D_NA), pos),
            pl.BlockSpec((TOK_TILE, D_NA), pos),
            pl.BlockSpec((1, D_NA), const),
            pl.BlockSpec((1, D_NA), const),
            pl.BlockSpec((D_NA, D_NA), const),
        ],
        out_specs=[
            pl.BlockSpec((TOK_TILE, 3 * D_HY), row),
            pl.BlockSpec((TOK_TILE, D_NA), row),
            pl.BlockSpec((TOK_TILE, D_NA), row),
            pl.BlockSpec((TOK_TILE, D_NA), row),
            pl.BlockSpec((TOK_TILE, 2 * D_MODEL), row),
        ],
        out_shape=[
            jax.ShapeDtypeStruct((T, 3 * D_HY), F32),
            jax.ShapeDtypeStruct((T, D_NA), BF16),
            jax.ShapeDtypeStruct((T, D_NA), BF16),
            jax.ShapeDtypeStruct((T, D_NA), BF16),
            jax.ShapeDtypeStruct((T, 2 * D_MODEL), BF16),
        ],
        compiler_params=_cparams(("parallel",)),
        name="inproj",
    )(x2, shift, scale, gain, w_bf, b_in, cos, sin, qg, kg, seg)


def _ctxkv_kernel(x_ref, shift_ref, scale_ref, g_ref, wk_ref, bk_ref, wv_ref, bv_ref, kg_ref, seg_ref,
                  k_ref, v_ref):
    h = _modulated_norm(x_ref[...], g_ref[...], shift_ref[...], scale_ref[...]).astype(BF16)
    k = _head_rms(_dot(h, wk_ref[...]) + bk_ref[...], seg_ref[...], kg_ref[...])
    k_ref[...] = k.astype(BF16)
    v_ref[...] = (_dot(h, wv_ref[...]) + bv_ref[...]).astype(BF16)


def _ctxkv(ctx2, shift, scale, gain, wk, bk, wv, bv, kg, seg):
    T = ctx2.shape[0]
    row = lambda i: (i, 0)
    const = lambda i: (0, 0)
    return pl.pallas_call(
        _ctxkv_kernel,
        grid=(T // TOK_TILE,),
        in_specs=[
            pl.BlockSpec((TOK_TILE, D_MODEL), row),
            pl.BlockSpec((1, D_MODEL), const),
            pl.BlockSpec((1, D_MODEL), const),
            pl.BlockSpec((1, D_MODEL), const),
            pl.BlockSpec((D_MODEL, D_NA), const),
            pl.BlockSpec((1, D_NA), const),
            pl.BlockSpec((D_MODEL, D_NA), const),
            pl.BlockSpec((1, D_NA), const),
            pl.BlockSpec((1, D_NA), const),
            pl.BlockSpec((D_NA, D_NA), const),
        ],
        out_specs=[pl.BlockSpec((TOK_TILE, D_NA), row), pl.BlockSpec((TOK_TILE, D_NA), row)],
        out_shape=[jax.ShapeDtypeStruct((T, D_NA), BF16), jax.ShapeDtypeStruct((T, D_NA), BF16)],
        compiler_params=_cparams(("parallel",)),
        name="ctxkv",
    )(ctx2, shift, scale, gain, wk, bk, wv, bv, kg, seg)


def _natten_kernel(q_ref, k_ref, v_ref, kc_ref, vc_ref, bias_ref, o_ref, *, rows):
    r = pl.program_id(1)
    row_start = jnp.clip(r - WIN_ROWS // 2, 0, rows - WIN_ROWS)
    off = row_start - r + (WIN_ROWS - 1)
    start = pl.multiple_of(row_start * GRID_W, GRID_W)
    n_loc = WIN_ROWS * GRID_W
    scale = HEAD_DIM ** -0.5
    pair = 2 * HEAD_DIM
    lane = lax.broadcasted_iota(jnp.int32, (GRID_W, pair), 1)
    for p in range(NA_HEADS // 2):
        cols = slice(p * pair, (p + 1) * pair)
        qp = q_ref[:, cols]
        kp = k_ref[pl.ds(start, n_loc), cols]
        vp = v_ref[pl.ds(start, n_loc), cols]
        kcp = kc_ref[:, cols]
        vcp = vc_ref[:, cols]
        outs = []
        for hh in range(2):
            in_head = (lane // HEAD_DIM) == hh
            qm = jnp.where(in_head, qp, jnp.zeros_like(qp))
            s_loc = _dot_nt(qm, kp) * scale + bias_ref[off, 2 * p + hh]
            s_ctx = _dot_nt(qm, kcp) * scale
            m = jnp.maximum(jnp.max(s_loc, axis=-1, keepdims=True), jnp.max(s_ctx, axis=-1, keepdims=True))
            e_loc = jnp.exp(s_loc - m)
            e_ctx = jnp.exp(s_ctx - m)
            denom = jnp.sum(e_loc, axis=-1, keepdims=True) + jnp.sum(e_ctx, axis=-1, keepdims=True)
            o = _dot(e_loc.astype(BF16), vp) + _dot(e_ctx.astype(BF16), vcp)
            outs.append(o / denom)
        o_ref[:, cols] = jnp.where((lane // HEAD_DIM) == 0, outs[0], outs[1]).astype(o_ref.dtype)


def _natten(q, k, v, kc, vc, bias):
    B, S, _ = q.shape
    rows = S // GRID_W
    C = kc.shape[1]
    return pl.pallas_call(
        functools.partial(_natten_kernel, rows=rows),
        grid=(B, rows),
        in_specs=[
            pl.BlockSpec((None, GRID_W, D_NA), lambda b, r: (b, r, 0)),
            pl.BlockSpec((None, S, D_NA), lambda b, r: (b, 0, 0)),
            pl.BlockSpec((None, S, D_NA), lambda b, r: (b, 0, 0)),
            pl.BlockSpec((None, C, D_NA), lambda b, r: (b, 0, 0)),
            pl.BlockSpec((None, C, D_NA), lambda b, r: (b, 0, 0)),
            pl.BlockSpec(bias.shape, lambda b, r: (0, 0, 0, 0)),
        ],
        out_specs=pl.BlockSpec((None, GRID_W, D_NA), lambda b, r: (b, r, 0)),
        out_shape=jax.ShapeDtypeStruct((B, S, D_NA), BF16),
        compiler_params=_cparams(("parallel", "arbitrary")),
        name="natten",
    )(q, k, v, kc, vc, bias)


def _natten_bias(rpb):
    col = np.arange(GRID_W)
    dc = np.clip(col[None, :] - col[:, None], -(WIN_COLS - 1), WIN_COLS - 1) + (WIN_COLS - 1)
    col_start = np.clip(col - WIN_COLS // 2, 0, GRID_W - WIN_COLS)
    col_mask = (col[None, :] >= col_start[:, None]) & (col[None, :] < col_start[:, None] + WIN_COLS)
    pick = jnp.asarray(dc[None] == np.arange(2 * WIN_COLS - 1)[:, None, None], dtype=F32)
    toep = jnp.einsum('hrd,dqk->hrqk', rpb.astype(F32), pick, precision=HI)
    toep = jnp.where(col_mask[None, None], toep, NEG_INF)
    wins = jnp.stack([toep[:, off:off + WIN_ROWS] for off in range(WIN_ROWS)], axis=0)
    return jnp.transpose(wins, (0, 1, 3, 2, 4)).reshape(WIN_ROWS, NA_HEADS, GRID_W, WIN_ROWS * GRID_W)


def _rope_tables(seq):
    pos = jnp.arange(seq, dtype=jnp.int32)
    rows = (pos // GRID_W).astype(F32)
    cols = (pos % GRID_W).astype(F32)
    nf = HEAD_DIM // 4
    inv = ROPE_THETA ** (-jnp.arange(nf, dtype=F32) / nf)
    ar = rows[:, None] * inv[None, :]
    ac = cols[:, None] * inv[None, :]
    cos = jnp.concatenate([jnp.cos(ar), jnp.cos(ar), jnp.cos(ac), jnp.cos(ac)], axis=-1)
    sin = jnp.concatenate([-jnp.sin(ar), jnp.sin(ar), -jnp.sin(ac), jnp.sin(ac)], axis=-1)
    return jnp.tile(cos, (1, NA_HEADS)), jnp.tile(sin, (1, NA_HEADS))


def _postmix_kernel(yh_ref, yn_ref, gate_ref, x_ref, g1_ref, sh2_ref, sc2_ref, n2_ref, whyo_ref, wnao_ref,
                    wout_ref, wq_ref, x1_ref, h2_ref, pq_ref):
    a = _dot(yh_ref[...].astype(BF16), whyo_ref[...])
    b = _dot(yn_ref[...], wnao_ref[...])
    gate = gate_ref[...].astype(F32)
    merged = gate[:, :D_MODEL] * a + gate[:, D_MODEL:] * b
    x1 = x_ref[...] + g1_ref[...] * _dot(merged.astype(BF16), wout_ref[...])
    x1_ref[...] = x1
    h2 = _modulated_norm(x1, n2_ref[...], sh2_ref[...], sc2_ref[...]).astype(BF16)
    h2_ref[...] = h2
    for hp in range(2 * PEER_HEADS):
        pq_ref[hp] = _dot(h2, wq_ref[:, hp * PEER_D_HALF:(hp + 1) * PEER_D_HALF]).astype(BF16)


def _postmix(yh, yn, gate, x2, g1, sh2, sc2, n2g, whyo, wnao, wout, wq, seq):
    T = x2.shape[0]
    tiles_per_batch = seq // TOK_TILE
    row = lambda i: (i, 0)
    per_batch = lambda i: (i // tiles_per_batch, 0, 0)
    const = lambda i: (0, 0)
    nq = wq.shape[1]
    return pl.pallas_call(
        _postmix_kernel,
        grid=(T // TOK_TILE,),
        in_specs=[
            pl.BlockSpec((TOK_TILE, D_HY), row),
            pl.BlockSpec((TOK_TILE, D_NA), row),
            pl.BlockSpec((TOK_TILE, 2 * D_MODEL), row),
            pl.BlockSpec((TOK_TILE, D_MODEL), row),
            pl.BlockSpec((None, 1, D_MODEL), per_batch),
            pl.BlockSpec((None, 1, D_MODEL), per_batch),
            pl.BlockSpec((None, 1, D_MODEL), per_batch),
            pl.BlockSpec((1, D_MODEL), const),
            pl.BlockSpec((D_HY, D_MODEL), const),
            pl.BlockSpec((D_NA, D_MODEL), const),
            pl.BlockSpec((D_MODEL, D_MODEL), const),
            pl.BlockSpec((D_MODEL, nq), const),
        ],
        out_specs=[
            pl.BlockSpec((TOK_TILE, D_MODEL), row),
            pl.BlockSpec((TOK_TILE, D_MODEL), row),
            pl.BlockSpec((nq // PEER_D_HALF, TOK_TILE, PEER_D_HALF), lambda i: (0, i, 0)),
        ],
        out_shape=[
            jax.ShapeDtypeStruct((T, D_MODEL), F32),
            jax.ShapeDtypeStruct((T, D_MODEL), BF16),
            jax.ShapeDtypeStruct((nq // PEER_D_HALF, T, PEER_D_HALF), BF16),
        ],
        compiler_params=_cparams(("parallel",)),
        name="postmix",
    )(yh, yn, gate, x2, g1, sh2, sc2, n2g, whyo, wnao, wout, wq)


SUBLANES = 8
ROUTE_TILE = 256


def _argmax_tree(vals, idxs):
    while len(vals) > 1:
        nv, ni = [], []
        for j in range(0, len(vals) - 1, 2):
            right = vals[j + 1] > vals[j]
            nv.append(jnp.where(right, vals[j + 1], vals[j]))
            ni.append(jnp.where(right, idxs[j + 1], idxs[j]))
        if len(vals) % 2:
            nv.append(vals[-1])
            ni.append(idxs[-1])
        vals, idxs = nv, ni
    return vals[0], idxs[0]


def _route_kernel(pq_ref, keys_ref, a_ref, b_ref, g_ref, tv_ref, ti_ref):
    rt = pq_ref.shape[1]
    n_grp = PEER_N_KEYS // SUBLANES
    sub = lax.broadcasted_iota(jnp.int32, (SUBLANES, rt), 0)
    key_idx = [sub + SUBLANES * g for g in range(n_grp)]

    def first_level(hp, carry):
        s = _dot_nt(keys_ref[hp], pq_ref[hp]).reshape(n_grp, SUBLANES, rt)
        slabs = [s[g] for g in range(n_grp)]
        h, p = hp // 2, hp % 2
        for j in range(PEER_TOPK):
            v, i = _argmax_tree(slabs, key_idx)
            for shift in (4, 2, 1):
                v2, i2 = pltpu.roll(v, shift, 0), pltpu.roll(i, shift, 0)
                take = (v2 > v) | ((v2 == v) & (i2 < i))
                v, i = jnp.where(take, v2, v), jnp.where(take, i2, i)
            tv_ref[p, j, pl.ds(h, 1), :] = v[0:1]
            ti_ref[p, j, pl.ds(h, 1), :] = i[0:1]
            slabs = [jnp.where(key_idx[g] == i, -jnp.inf, slabs[g]) for g in range(n_grp)]
        return carry

    lax.fori_loop(0, 2 * PEER_HEADS, first_level, 0, unroll=2)

    pairs = [(a, b) for a in range(PEER_TOPK) for b in range(PEER_TOPK) if (a + 1) * (b + 1) <= PEER_TOPK]
    cand = [tv_ref[0, a] + tv_ref[1, b] for a, b in pairs]
    pos = [jnp.full((PEER_HEADS, rt), a * PEER_TOPK + b, jnp.int32) for a, b in pairs]
    best, sel_a, sel_b = [], [], []
    for j in range(PEER_TOPK):
        v, w = _argmax_tree(cand, pos)
        best.append(v)
        ia = jnp.zeros((PEER_HEADS, rt), jnp.int32)
        ib = jnp.zeros((PEER_HEADS, rt), jnp.int32)
        for n, (a, b) in enumerate(pairs):
            hit = w == (a * PEER_TOPK + b)
            cand[n] = jnp.where(hit, -jnp.inf, cand[n])
            ia = jnp.where(hit, ti_ref[0, a], ia)
            ib = jnp.where(hit, ti_ref[1, b], ib)
        sel_a.append(ia)
        sel_b.append(ib)
    ex = [jnp.exp(v - best[0]) for v in best]
    denom = functools.reduce(lambda x, y: x + y, ex)
    gates = [e / denom for e in ex]
    a_ref[...] = jnp.concatenate(sel_a, axis=0).T
    b_ref[...] = jnp.concatenate(sel_b, axis=0).T
    g_ref[...] = jnp.concatenate(gates, axis=0).T


def _route(pq, keys_bf):
    n_hp, T, _ = pq.shape
    n_sel = PEER_HEADS * PEER_TOPK
    row = lambda i: (i, 0)
    return pl.pallas_call(
        _route_kernel,
        grid=(T // ROUTE_TILE,),
        in_specs=[
            pl.BlockSpec((n_hp, ROUTE_TILE, PEER_D_HALF), lambda i: (0, i, 0)),
            pl.BlockSpec((n_hp, PEER_N_KEYS, PEER_D_HALF), lambda i: (0, 0, 0)),
        ],
        out_specs=[pl.BlockSpec((ROUTE_TILE, n_sel), row)] * 3,
        out_shape=[
            jax.ShapeDtypeStruct((T, n_sel), jnp.int32),
            jax.ShapeDtypeStruct((T, n_sel), jnp.int32),
            jax.ShapeDtypeStruct((T, n_sel), F32),
        ],
        scratch_shapes=[
            pltpu.VMEM((2, PEER_TOPK, PEER_HEADS, ROUTE_TILE), F32),
            pltpu.VMEM((2, PEER_TOPK, PEER_HEADS, ROUTE_TILE), jnp.int32),
        ],
        compiler_params=_cparams(("parallel",)),
        name="route",
    )(pq, keys_bf)


WBUILD_TILE = 128


WBUILD_PITCH = 136
WBUILD_UNROLL = 8


def _wbuild_kernel(a_ref, b_ref, g_ref, w_ref, s_ref):
    n = PEER_N_KEYS
    tb = a_ref.shape[0]
    key = lax.broadcasted_iota(jnp.int32, (n, a_ref.shape[1]), 0)

    def body(t, carry):
        a = a_ref[pl.ds(t, 1), :]
        b = b_ref[pl.ds(t, 1), :]
        g = g_ref[pl.ds(t, 1), :]
        ga = jnp.where(a == key, g, 0.0).astype(BF16)
        ob = jnp.where(b == key, 1.0, 0.0).astype(BF16)
        s_ref[pl.ds(pl.multiple_of(t * WBUILD_PITCH, SUBLANES), n), :] = _dot_nt(ga, ob)
        return carry

    lax.fori_loop(0, tb, body, 0, unroll=WBUILD_UNROLL)

    def emit(i1, carry):
        w_ref[i1] = s_ref[pl.ds(i1, tb, stride=WBUILD_PITCH), :].astype(BF16)
        return carry

    lax.fori_loop(0, n, emit, 0, unroll=WBUILD_UNROLL)


def _wbuild(a, b, g):
    T, n_sel = a.shape
    row = lambda i: (i, 0)
    return pl.pallas_call(
        _wbuild_kernel,
        grid=(T // WBUILD_TILE,),
        in_specs=[pl.BlockSpec((WBUILD_TILE, n_sel), row)] * 3,
        out_specs=pl.BlockSpec((None, PEER_N_KEYS, WBUILD_TILE, PEER_N_KEYS), lambda i: (i, 0, 0, 0)),
        out_shape=jax.ShapeDtypeStruct((T // WBUILD_TILE, PEER_N_KEYS, WBUILD_TILE, PEER_N_KEYS), BF16),
        scratch_shapes=[pltpu.VMEM((WBUILD_TILE * WBUILD_PITCH, PEER_N_KEYS), F32)],
        compiler_params=_cparams(("parallel",)),
        name="wbuild",
    )(a, b, g)


def _peer_dense_kernel(h2_ref, w_ref, u_ref, v_ref, x1_ref, g2_ref, o_ref, acc_ref):
    e = pl.program_id(1)

    @pl.when(e == 0)
    def _():
        acc_ref[...] = jnp.zeros_like(acc_ref)

    a = _dot_nt(h2_ref[...], u_ref[...])
    act = 0.5 * a * (1.0 + lax.erf(a * (2.0 ** -0.5)))
    w = jnp.concatenate(
        [jnp.concatenate([w_ref[tb, i] for i in range(w_ref.shape[1])], axis=1) for tb in range(w_ref.shape[0])],
        axis=0)
    acc_ref[...] += _dot((w.astype(F32) * act).astype(BF16), v_ref[...])

    @pl.when(e == pl.num_programs(1) - 1)
    def _():
        o_ref[...] = x1_ref[...] + g2_ref[...] * acc_ref[...]


def _peer_dense(h2, w, u_bf, v_bf, x1, g2, seq):
    T = h2.shape[0]
    tiles_per_batch = seq // PEER_TOK_TILE
    n_exp = u_bf.shape[0]
    return pl.pallas_call(
        _peer_dense_kernel,
        grid=(T // PEER_TOK_TILE, n_exp // PEER_EXP_TILE),
        in_specs=[
            pl.BlockSpec((PEER_TOK_TILE, D_MODEL), lambda i, e: (i, 0)),
            pl.BlockSpec((PEER_TOK_TILE // WBUILD_TILE, PEER_EXP_TILE // PEER_N_KEYS, WBUILD_TILE, PEER_N_KEYS),
                         lambda i, e: (i, e, 0, 0)),
            pl.BlockSpec((PEER_EXP_TILE, D_MODEL), lambda i, e: (e, 0)),
            pl.BlockSpec((PEER_EXP_TILE, D_MODEL), lambda i, e: (e, 0)),
            pl.BlockSpec((PEER_TOK_TILE, D_MODEL), lambda i, e: (i, 0)),
            pl.BlockSpec((None, 1, D_MODEL), lambda i, e: (i // tiles_per_batch, 0, 0)),
        ],
        out_specs=pl.BlockSpec((PEER_TOK_TILE, D_MODEL), lambda i, e: (i, 0)),
        out_shape=jax.ShapeDtypeStruct((T, D_MODEL), F32),
        scratch_shapes=[pltpu.VMEM((PEER_TOK_TILE, D_MODEL), F32)],
        compiler_params=_cparams(("parallel", "arbitrary")),
        name="peer_dense",
    )(h2, w, u_bf, v_bf, x1, g2)


HI = lax.Precision.HIGHEST


HY_LANES = 128


def _shortconv_kernel(z_ref, w_ref, b_ref, o_ref):
    z = z_ref[...]
    n = z.shape[0]
    row = lax.broadcasted_iota(jnp.int32, z.shape, 0)
    prev = jnp.where(row == 0, 0.0, pltpu.roll(z, 1, 0))
    nxt = jnp.where(row == n - 1, 0.0, pltpu.roll(z, n - 1, 0))
    o_ref[...] = w_ref[0:1] * prev + w_ref[1:2] * z + w_ref[2:3] * nxt + b_ref[...]


def _shortconv(zh, w, b):
    B, L, C = zh.shape
    return pl.pallas_call(
        _shortconv_kernel,
        grid=(B, C // HY_LANES),
        in_specs=[
            pl.BlockSpec((None, L, HY_LANES), lambda i, j: (i, 0, j)),
            pl.BlockSpec((3, HY_LANES), lambda i, j: (0, j)),
            pl.BlockSpec((1, HY_LANES), lambda i, j: (0, j)),
        ],
        out_specs=pl.BlockSpec((None, L, HY_LANES), lambda i, j: (i, 0, j)),
        out_shape=jax.ShapeDtypeStruct((B, L, C), F32),
        compiler_params=_cparams(("parallel", "parallel")),
        name="shortconv",
    )(zh, w, b)


FFT_R = 128
FFT_N = FFT_R * FFT_R
FFT_PITCH = 136
FFT_K1_CHUNK = 16
FFT_UNROLL = 4


def _fft_tables():
    r = jnp.arange(FFT_R, dtype=jnp.int32)
    n2, k1, n1 = r[:, None, None], r[None, :, None], r[None, None, :]
    ang = (2.0 * math.pi / FFT_N) * ((k1 * (FFT_R * n1 + n2)) % FFT_N).astype(F32)
    c, s = jnp.cos(ang), jnp.sin(ang)
    g_full = jnp.concatenate([c, -s], axis=1)
    h_half = jnp.concatenate([jnp.swapaxes(c, 1, 2), -jnp.swapaxes(s, 1, 2)], axis=2)[:, :FFT_R // 2] / FFT_N
    ang2 = (2.0 * math.pi / FFT_R) * ((r[:, None] * r[None, :]) % FFT_R).astype(F32)
    c2, s2 = jnp.cos(ang2), jnp.sin(ang2)
    f_fwd = jnp.concatenate([jnp.concatenate([c2, s2], 1), jnp.concatenate([-s2, c2], 1)], 0)
    f_inv = jnp.concatenate([jnp.concatenate([c2, -s2], 1), jnp.concatenate([s2, c2], 1)], 0)
    return g_full.astype(BF16), h_half.astype(BF16), f_fwd.astype(BF16), f_inv.astype(BF16)


def _fft_stage1(load_slab, g_ref, bre_ref, bim_ref):
    def body(n2, carry):
        out = _dot(g_ref[n2], load_slab(n2).astype(BF16))
        bre_ref[pl.ds(n2, FFT_R, stride=FFT_PITCH), :] = out[:FFT_R]
        bim_ref[pl.ds(n2, FFT_R, stride=FFT_PITCH), :] = out[FFT_R:]
        return carry

    lax.fori_loop(0, FFT_R, body, 0, unroll=FFT_UNROLL)


def _k1_rows(k1):
    return pl.ds(pl.multiple_of(k1 * FFT_PITCH, SUBLANES), FFT_R)


def _load_k1_pair(bre_ref, bim_ref, k1):
    return jnp.concatenate(
        [jnp.concatenate([bre_ref[_k1_rows(k1 + d), :], bim_ref[_k1_rows(k1 + d), :]], axis=0) for d in range(2)],
        axis=1)


FILT_ROWS = 512
FILT_FEAT = 128


def _filt_kernel(feat_ref, delta_ref, w1_ref, b1_ref, w2_ref, b2_ref, w3_ref, b3_ref, w4_ref, freq_ref,
                 h_ref, sum_ref):
    @pl.when(pl.program_id(0) == 0)
    def _():
        sum_ref[...] = jnp.zeros_like(sum_ref)

    feat = feat_ref[...]
    freq = freq_ref[...]
    dot = lambda a, b: jnp.dot(a, b, precision=HI, preferred_element_type=F32)
    h = jnp.sin(freq * (dot(feat, w1_ref[...]) + b1_ref[...]))
    h = jnp.sin(freq * (dot(h, w2_ref[...]) + b2_ref[...]))
    h = jnp.sin(freq * (dot(h, w3_ref[...]) + b3_ref[...]))
    h = dot(h, w4_ref[...])
    n_feat = 1 + 2 * HY_BANDS
    t = feat[:, 0:1]
    forward = feat[:, n_feat:n_feat + 1] > 0.5
    keep = feat[:, n_feat + 1:n_feat + 2]
    decay = jnp.exp(-t * delta_ref[...])
    outs = []
    for o in range(HY_ORDER):
        base = o * 2 * D_HY
        outs.append(jnp.where(forward, h[:, base:base + D_HY], h[:, base + D_HY:base + 2 * D_HY]) * decay)
    out = jnp.concatenate(outs, axis=1)
    sum_ref[...] += jnp.sum(jnp.abs(out), axis=0, keepdims=True)
    h_ref[...] = out * keep


def _filt_features(L):
    r = np.arange(FFT_N)
    n = FFT_R * (r % FFT_R) + r // FFT_R
    j = np.where(n < L, n, np.where(n > L, 2 * L - n, 0))
    jj = jnp.asarray(j, dtype=F32)[:, None]
    t = jj / (L - 1.0)
    w = (2.0 * math.pi / L) * jj
    bands = jnp.linspace(1e-4, HY_BANDS - 1.0, HY_BANDS, dtype=F32)[None, :]
    flags = jnp.asarray(np.stack([n < L, n != L], axis=1), dtype=F32)
    feats = jnp.concatenate([t, jnp.cos(bands * w), -jnp.sin(bands * w), flags], axis=-1)
    return jnp.pad(feats, ((0, 0), (0, FILT_FEAT - feats.shape[1])))


def _filters(p, L):
    feats = _filt_features(L)
    deltas = jnp.abs(jnp.linspace(math.log(HY_DECAY_TARGET) / HY_SLOW_PCT,
                                  math.log(HY_DECAY_TARGET) / HY_FAST_PCT, D_HY, dtype=F32))[None, :]
    w1 = jnp.pad(p['hy_f1_w'], ((0, FILT_FEAT - p['hy_f1_w'].shape[0]), (0, 0)))
    n_out = HY_ORDER * D_HY
    const = lambda i: (0, 0)
    full = lambda a: pl.BlockSpec(a.shape, const)
    args = [deltas, w1, p['hy_f1_b'][None], p['hy_f2_w'], p['hy_f2_b'][None], p['hy_f3_w'], p['hy_f3_b'][None],
            p['hy_f4_w'], p['hy_sin_freq'][None]]
    return pl.pallas_call(
        _filt_kernel,
        grid=(FFT_N // FILT_ROWS,),
        in_specs=[pl.BlockSpec((FILT_ROWS, FILT_FEAT), lambda i: (i, 0))] + [full(a) for a in args],
        out_specs=[pl.BlockSpec((FILT_ROWS, n_out), lambda i: (i, 0)), pl.BlockSpec((1, n_out), const)],
        out_shape=[jax.ShapeDtypeStruct((FFT_N, n_out), F32), jax.ShapeDtypeStruct((1, n_out), F32)],
        compiler_params=_cparams(("arbitrary",)),
        name="hyena_filter",
    )(feats, *args)


def _fspec_kernel(h_ref, sum_ref, g_ref, f_ref, o_ref, bre_ref, bim_ref):
    c = pl.program_id(1)

    @pl.when(c == 0)
    def _():
        _fft_stage1(lambda n2: h_ref[pl.ds(pl.multiple_of(n2 * FFT_R, FFT_R), FFT_R), :], g_ref, bre_ref, bim_ref)

    inv = 1.0 / sum_ref[...]

    lanes = h_ref.shape[1]

    def slab_pair(j, carry):
        x = _dot(f_ref[...], _load_k1_pair(bre_ref, bim_ref, c * FFT_K1_CHUNK + 2 * j).astype(BF16))
        for d in range(2):
            rows = pl.ds(pl.multiple_of((2 * j + d) * FFT_R, FFT_R), FFT_R)
            o_ref[0, rows, :] = x[:FFT_R, d * lanes:(d + 1) * lanes] * inv
            o_ref[1, rows, :] = x[FFT_R:, d * lanes:(d + 1) * lanes] * inv
        return carry

    lax.fori_loop(0, FFT_K1_CHUNK // 2, slab_pair, 0, unroll=FFT_UNROLL // 2)


def _filter_spectrum(h, hsum, g_full, f_fwd):
    n_ch = h.shape[1]
    single = pl.Buffered(1)
    return pl.pallas_call(
        _fspec_kernel,
        grid=(n_ch // HY_LANES, FFT_R // FFT_K1_CHUNK),
        in_specs=[
            pl.BlockSpec((FFT_N, HY_LANES), lambda j, c: (0, j), pipeline_mode=single),
            pl.BlockSpec((1, HY_LANES), lambda j, c: (0, j)),
            pl.BlockSpec(g_full.shape, lambda j, c: (0, 0, 0), pipeline_mode=single),
            pl.BlockSpec(f_fwd.shape, lambda j, c: (0, 0)),
        ],
        out_specs=pl.BlockSpec((2, FFT_K1_CHUNK * FFT_R, HY_LANES), lambda j, c: (0, c, j)),
        out_shape=jax.ShapeDtypeStruct((2, FFT_N, n_ch), F32),
        scratch_shapes=[pltpu.VMEM((FFT_R * FFT_PITCH, HY_LANES), F32)] * 2,
        compiler_params=_cparams(("parallel", "arbitrary")),
        name="hyena_filter_fft",
    )(h, hsum, g_full, f_fwd)


def _hyconv_kernel(u_ref, gate_ref, kf_ref, skip_ref, g_ref, h_ref, f_ref, fi_ref, o_ref, bre_ref, bim_ref):
    c = pl.program_id(2)
    half = FFT_R // 2

    def rows_of(n2):
        return pl.ds(pl.multiple_of(n2 * half, half), half)

    @pl.when(c == 0)
    def _():
        _fft_stage1(lambda n2: u_ref[rows_of(n2), :], g_ref, bre_ref, bim_ref)

    lanes = u_ref.shape[1]

    def slab_pair(j, carry):
        k1 = c * FFT_K1_CHUNK + 2 * j
        x = _dot(f_ref[...], _load_k1_pair(bre_ref, bim_ref, k1).astype(BF16))
        xr, xi = x[:FFT_R], x[FFT_R:]
        kr, ki = [jnp.concatenate([kf_ref[part, pl.ds(pl.multiple_of((2 * j + d) * FFT_R, FFT_R), FFT_R), :]
                                   for d in range(2)], axis=1) for part in range(2)]
        y = jnp.concatenate([xr * kr - xi * ki, xr * ki + xi * kr], axis=0)
        z = _dot(fi_ref[...], y.astype(BF16))
        for d in range(2):
            bre_ref[_k1_rows(k1 + d), :] = z[:FFT_R, d * lanes:(d + 1) * lanes]
            bim_ref[_k1_rows(k1 + d), :] = z[FFT_R:, d * lanes:(d + 1) * lanes]
        return carry

    lax.fori_loop(0, FFT_K1_CHUNK // 2, slab_pair, 0, unroll=FFT_UNROLL // 2)

    @pl.when(c == pl.num_programs(2) - 1)
    def _():
        def body(n2, carry):
            z = jnp.concatenate([bre_ref[pl.ds(n2, FFT_R, stride=FFT_PITCH), :],
                                 bim_ref[pl.ds(n2, FFT_R, stride=FFT_PITCH), :]], axis=0)
            conv = _dot(h_ref[n2], z.astype(BF16))
            rows = rows_of(n2)
            o_ref[rows, :] = gate_ref[rows, :] * (conv + skip_ref[...] * u_ref[rows, :])
            return carry

        lax.fori_loop(0, FFT_R, body, 0, unroll=FFT_UNROLL)


def _hyconv(u, u_col, gate, gate_col, kf, kf_col, skip, tables):
    g_half, h_half, f_fwd, f_inv = tables
    B, L, _ = u.shape
    single = pl.Buffered(1)
    return pl.pallas_call(
        _hyconv_kernel,
        grid=(B, D_HY // HY_LANES, FFT_R // FFT_K1_CHUNK),
        in_specs=[
            pl.BlockSpec((None, L, HY_LANES), lambda b, j, c: (b, 0, u_col + j), pipeline_mode=single),
            pl.BlockSpec((None, L, HY_LANES), lambda b, j, c: (b, 0, gate_col + j), pipeline_mode=single),
            pl.BlockSpec((2, FFT_K1_CHUNK * FFT_R, HY_LANES), lambda b, j, c: (0, c, kf_col + j)),
            pl.BlockSpec((1, HY_LANES), lambda b, j, c: (0, j)),
            pl.BlockSpec(g_half.shape, lambda b, j, c: (0, 0, 0), pipeline_mode=single),
            pl.BlockSpec(h_half.shape, lambda b, j, c: (0, 0, 0), pipeline_mode=single),
            pl.BlockSpec(f_fwd.shape, lambda b, j, c: (0, 0)),
            pl.BlockSpec(f_inv.shape, lambda b, j, c: (0, 0)),
        ],
        out_specs=pl.BlockSpec((None, L, HY_LANES), lambda b, j, c: (b, 0, j)),
        out_shape=jax.ShapeDtypeStruct((B, L, D_HY), F32),
        scratch_shapes=[pltpu.VMEM((FFT_R * FFT_PITCH, HY_LANES), F32)] * 2,
        compiler_params=_cparams(("parallel", "parallel", "arbitrary")),
        name="hyena_conv",
    )(u, gate, kf, skip, g_half, h_half, f_fwd, f_inv)


def _slab_major(a, n1):
    B, L, C = a.shape
    return a.reshape(B, n1, L // n1, C).transpose(0, 2, 1, 3).reshape(B, L, C)


def _hyena(zh, p):
    B, L, _ = zh.shape
    assert 2 * L == FFT_N
    g_full, h_half, f_fwd, f_inv = _fft_tables()
    tables = (g_full[:, :, :FFT_R // 2], h_half, f_fwd, f_inv)
    filt, filt_sum = _filters(p, L)
    both_dirs = filt_sum[:, :]
    kf = _filter_spectrum(filt, both_dirs, g_full, f_fwd)
    zc = _slab_major(_shortconv(zh, p['hy_conv_w'], p['hy_conv_b'][None]), FFT_R // 2)
    lanes = D_HY // HY_LANES
    y = _hyconv(zc, 0, zc, lanes, kf, 0, p['hy_skip'][0:1], tables)
    y = _hyconv(y, 0, zc, 2 * lanes, kf, lanes, p['hy_skip'][1:2], tables)
    return _slab_major(y, L // (FFT_R // 2))


def kernel(x, c, ctx, c_ctx, norm1_g, norm2_g, w_ada, b_ada, w_in, b_in, hy_conv_w, hy_conv_b, hy_f1_w, hy_f1_b, hy_f2_w, hy_f2_b, hy_f3_w, hy_f3_b, hy_f4_w, hy_sin_freq, hy_skip, q_norm_g, k_norm_g, na_rpb, w_hy_out, w_na_out, w_out, peer_w_q, peer_keys, peer_u, peer_v):
    assert w_in.shape[0] == 1, "single-layer block"
    B, S, D = x.shape
    C = ctx.shape[1]
    T = B * S

    m_lat = (jax.nn.silu(c) @ w_ada[0] + b_ada[0]).reshape(B, N_MOD, 1, D)
    m_ctx = (jax.nn.silu(c_ctx) @ w_ada[0] + b_ada[0]).reshape(N_MOD, 1, D)
    sh1, sc1, g1, sh2, sc2, g2 = [m_lat[:, i] for i in range(N_MOD)]

    w_in_bf = w_in[0].astype(BF16)
    b_in2 = b_in[0][None, :]
    gain1 = norm1_g[0][None, :]
    qg = jnp.tile(q_norm_g[0], NA_HEADS)[None, :]
    kg = jnp.tile(k_norm_g[0], NA_HEADS)[None, :]
    seg = jnp.asarray(np.kron(np.eye(NA_HEADS), np.full((HEAD_DIM, HEAD_DIM), 1.0 / HEAD_DIM)), dtype=BF16)
    cos, sin = _rope_tables(S)

    x2 = x.reshape(T, D)
    zh, q, k, v, gate = _inproj(x2, sh1, sc1, gain1, w_in_bf, b_in2, cos, sin, qg, kg, seg, S)

    k_ctx, v_ctx = _ctxkv(ctx.reshape(B * C, D), m_ctx[0], m_ctx[1], gain1,
                          w_in_bf[:, COL_K:COL_V], b_in2[:, COL_K:COL_V],
                          w_in_bf[:, COL_V:COL_G_HY], b_in2[:, COL_V:COL_G_HY], kg, seg)

    p = {'hy_conv_w': hy_conv_w[0], 'hy_conv_b': hy_conv_b[0], 'hy_f1_w': hy_f1_w[0], 'hy_f1_b': hy_f1_b[0],
         'hy_f2_w': hy_f2_w[0], 'hy_f2_b': hy_f2_b[0], 'hy_f3_w': hy_f3_w[0], 'hy_f3_b': hy_f3_b[0],
         'hy_f4_w': hy_f4_w[0], 'hy_sin_freq': hy_sin_freq[0], 'hy_skip': hy_skip[0]}
    y_hy = _hyena(zh.reshape(B, S, 3 * D_HY), p).reshape(T, D_HY)

    y_na = _natten(q.reshape(B, S, D_NA), k.reshape(B, S, D_NA), v.reshape(B, S, D_NA),
                   k_ctx.reshape(B, C, D_NA), v_ctx.reshape(B, C, D_NA), _natten_bias(na_rpb[0])).reshape(T, D_NA)

    x1, h2, pq = _postmix(y_hy, y_na, gate, x2, g1, sh2, sc2, norm2_g[0][None, :],
                          w_hy_out[0].astype(BF16), w_na_out[0].astype(BF16), w_out[0].astype(BF16),
                          peer_w_q[0].astype(BF16), S)

    keys_bf = peer_keys[0].astype(BF16).reshape(2 * PEER_HEADS, PEER_N_KEYS, PEER_D_HALF)
    sel_a, sel_b, sel_g = _route(pq, keys_bf)
    w = _wbuild(sel_a, sel_b, sel_g)
    out = _peer_dense(h2, w, peer_u[0].astype(BF16), peer_v[0].astype(BF16), x1, g2, S)
    return out.reshape(B, S, D)
```

```python
import functools
import math

import numpy as np
import jax
import jax.numpy as jnp
from jax import lax
from jax.experimental import pallas as pl
from jax.experimental.pallas import tpu as pltpu

F32 = jnp.float32
BF16 = jnp.bfloat16

D_MODEL = 1024
GRID_W = 64
EPS = 1e-6
N_MOD = 6

D_HY = 512
HY_ORDER = 2
HY_BANDS = 16
HY_DECAY_TARGET = 1e-2
HY_FAST_PCT = 0.3
HY_SLOW_PCT = 1.5

NA_HEADS = 8
HEAD_DIM = 64
D_NA = NA_HEADS * HEAD_DIM
WIN_ROWS = 8
WIN_COLS = 16
ROPE_THETA = 10000.0
NEG_INF = -1e30

PEER_HEADS = 8
PEER_N_KEYS = 128
PEER_N_EXPERTS = PEER_N_KEYS * PEER_N_KEYS
PEER_TOPK = 16
PEER_D_KEY = 256
PEER_D_HALF = PEER_D_KEY // 2

COL_HY = 0
COL_Q = COL_HY + 3 * D_HY
COL_K = COL_Q + D_NA
COL_V = COL_K + D_NA
COL_G_HY = COL_V + D_NA
N_PROJ = COL_G_HY + 2 * D_MODEL

VMEM_LIMIT = 56 * 1024 * 1024
TOK_TILE = 256
PEER_TOK_TILE = 512
PEER_EXP_TILE = 2048


def _cparams(sem):
    return pltpu.CompilerParams(dimension_semantics=sem, vmem_limit_bytes=VMEM_LIMIT)


def _dot(a, b):
    return jnp.dot(a, b, preferred_element_type=F32)


def _dot_nt(a, b):
    return lax.dot_general(a, b, (((1,), (1,)), ((), ())), preferred_element_type=F32)


def _modulated_norm(x, gain, shift, scale):
    ms = jnp.mean(x * x, axis=-1, keepdims=True)
    return (x * lax.rsqrt(ms + EPS) * gain) * (1.0 + scale) + shift


def _head_rms(z, seg, gain):
    zz = z * z
    hi = zz.astype(BF16)
    lo = (zz - hi.astype(F32)).astype(BF16)
    ms = _dot(hi, seg) + _dot(lo, seg)
    return z * lax.rsqrt(ms + EPS) * gain


def _rope(z, cos, sin_signed):
    n = z.shape[-1]
    lane = lax.broadcasted_iota(jnp.int32, z.shape, 1)
    first = (lane // (HEAD_DIM // 4)) % 2 == 0
    partner = jnp.where(first, pltpu.roll(z, n - HEAD_DIM // 4, 1), pltpu.roll(z, HEAD_DIM // 4, 1))
    return z * cos + partner * sin_signed


def _inproj_kernel(x_ref, shift_ref, scale_ref, g_ref, w_ref, b_ref, cos_ref, sin_ref, qg_ref, kg_ref,
                   seg_ref, zh_ref, q_ref, k_ref, v_ref, gate_ref):
    h = _modulated_norm(x_ref[...], g_ref[...], shift_ref[...], scale_ref[...]).astype(BF16)

    def proj(lo, hi):
        return _dot(h, w_ref[:, lo:hi]) + b_ref[:, lo:hi]

    zh_ref[...] = proj(COL_HY, COL_Q)
    cos, sin = cos_ref[...], sin_ref[...]
    seg = seg_ref[...]
    q = _rope(_head_rms(proj(COL_Q, COL_K), seg, qg_ref[...]), cos, sin)
    q_ref[...] = q.astype(BF16)
    k = _rope(_head_rms(proj(COL_K, COL_V), seg, kg_ref[...]), cos, sin)
    k_ref[...] = k.astype(BF16)
    v_ref[...] = proj(COL_V, COL_G_HY).astype(BF16)
    gate_ref[...] = jax.nn.sigmoid(proj(COL_G_HY, N_PROJ)).astype(BF16)


def _inproj(x2, shift, scale, gain, w_bf, b_in, cos, sin, qg, kg, seg, seq):
    T = x2.shape[0]
    tiles_per_batch = seq // TOK_TILE
    row = lambda i: (i, 0)
    per_batch = lambda i: (i // tiles_per_batch, 0, 0)
    const = lambda i: (0, 0)
    pos = lambda i: (i % tiles_per_batch, 0)
    return pl.pallas_call(
        _inproj_kernel,
        grid=(T // TOK_TILE,),
        in_specs=[
            pl.BlockSpec((TOK_TILE, D_MODEL), row),
            pl.BlockSpec((None, 1, D_MODEL), per_batch),
            pl.BlockSpec((None, 1, D_MODEL), per_batch),
            pl.BlockSpec((1, D_MODEL), const),
            pl.BlockSpec((D_MODEL, N_PROJ), const),
            pl.BlockSpec((1, N_PROJ), const),
            pl.BlockSpec((TOK_TILE, D_NA), pos),
            pl.BlockSpec((TOK_TILE, D_NA), pos),
            pl.BlockSpec((1, D_NA), const),
            pl.BlockSpec((1, D_NA), const),
            pl.BlockSpec((D_NA, D_NA), const),
        ],
        out_specs=[
            pl.BlockSpec((TOK_TILE, 3 * D_HY), row),
            pl.BlockSpec((TOK_TILE, D_NA), row),
            pl.BlockSpec((TOK_TILE, D_NA), row),
            pl.BlockSpec((TOK_TILE, D_NA), row),
            pl.BlockSpec((TOK_TILE, 2 * D_MODEL), row),
        ],
        out_shape=[
            jax.ShapeDtypeStruct((T, 3 * D_HY), F32),
            jax.ShapeDtypeStruct((T, D_NA), BF16),
            jax.ShapeDtypeStruct((T, D_NA), BF16),
            jax.ShapeDtypeStruct((T, D_NA), BF16),
            jax.ShapeDtypeStruct((T, 2 * D_MODEL), BF16),
        ],
        compiler_params=_cparams(("parallel",)),
        name="inproj",
    )(x2, shift, scale, gain, w_bf, b_in, cos, sin, qg, kg, seg)


def _ctxkv_kernel(x_ref, shift_ref, scale_ref, g_ref, wk_ref, bk_ref, wv_ref, bv_ref, kg_ref, seg_ref,
                  k_ref, v_ref):
    h = _modulated_norm(x_ref[...], g_ref[...], shift_ref[...], scale_ref[...]).astype(BF16)
    k = _head_rms(_dot(h, wk_ref[...]) + bk_ref[...], seg_ref[...], kg_ref[...])
    k_ref[...] = k.astype(BF16)
    v_ref[...] = (_dot(h, wv_ref[...]) + bv_ref[...]).astype(BF16)


def _ctxkv(ctx2, shift, scale, gain, wk, bk, wv, bv, kg, seg):
    T = ctx2.shape[0]
    row = lambda i: (i, 0)
    const = lambda i: (0, 0)
    return pl.pallas_call(
        _ctxkv_kernel,
        grid=(T // TOK_TILE,),
        in_specs=[
            pl.BlockSpec((TOK_TILE, D_MODEL), row),
            pl.BlockSpec((1, D_MODEL), const),
            pl.BlockSpec((1, D_MODEL), const),
            pl.BlockSpec((1, D_MODEL), const),
            pl.BlockSpec((D_MODEL, D_NA), const),
            pl.BlockSpec((1, D_NA), const),
            pl.BlockSpec((D_MODEL, D_NA), const),
            pl.BlockSpec((1, D_NA), const),
            pl.BlockSpec((1, D_NA), const),
            pl.BlockSpec((D_NA, D_NA), const),
        ],
        out_specs=[pl.BlockSpec((TOK_TILE, D_NA), row), pl.BlockSpec((TOK_TILE, D_NA), row)],
        out_shape=[jax.ShapeDtypeStruct((T, D_NA), BF16), jax.ShapeDtypeStruct((T, D_NA), BF16)],
        compiler_params=_cparams(("parallel",)),
        name="ctxkv",
    )(ctx2, shift, scale, gain, wk, bk, wv, bv, kg, seg)


NA_BLOCK_ROWS = 4
NA_KEY_ROWS = WIN_ROWS + NA_BLOCK_ROWS
NA_CLASSES = 3


def _natten_kernel(q_ref, k_ref, v_ref, kc_ref, vc_ref, bias_ref, o_ref, *, rows):
    r0 = pl.program_id(1) * NA_BLOCK_ROWS
    key_start = jnp.clip(r0 - WIN_ROWS // 2, 0, rows - NA_KEY_ROWS)
    start = pl.multiple_of(key_start * GRID_W, GRID_W * NA_BLOCK_ROWS)
    n_q = NA_BLOCK_ROWS * GRID_W
    n_loc = NA_KEY_ROWS * GRID_W
    scale = jnp.asarray(HEAD_DIM ** -0.5, BF16)
    pair = 2 * HEAD_DIM
    lane = lax.broadcasted_iota(jnp.int32, (n_q, pair), 1)
    for p in range(NA_HEADS // 2):
        cols = slice(p * pair, (p + 1) * pair)
        qp = q_ref[:, cols] * scale
        kp = k_ref[pl.ds(start, n_loc), cols]
        vp = v_ref[pl.ds(start, n_loc), cols]
        kcp = kc_ref[:, cols]
        vcp = vc_ref[:, cols]
        outs = []
        for hh in range(2):
            qm = jnp.where((lane // HEAD_DIM) == hh, qp, jnp.zeros_like(qp))
            s_loc = _dot_nt(qm, kp) + bias_ref[2 * p + hh]
            s_ctx = _dot_nt(qm, kcp)
            m = jnp.maximum(jnp.max(s_loc, axis=-1, keepdims=True), jnp.max(s_ctx, axis=-1, keepdims=True))
            e_loc = jnp.exp(s_loc - m)
            e_ctx = jnp.exp(s_ctx - m)
            denom = jnp.sum(e_loc, axis=-1, keepdims=True) + jnp.sum(e_ctx, axis=-1, keepdims=True)
            o = _dot(e_loc.astype(BF16), vp) + _dot(e_ctx.astype(BF16), vcp)
            outs.append(o / denom)
        o_ref[:, cols] = jnp.where((lane // HEAD_DIM) == 0, outs[0], outs[1]).astype(o_ref.dtype)


def _natten(q, k, v, kc, vc, bias):
    B, S, _ = q.shape
    rows = S // GRID_W
    n_blk = rows // NA_BLOCK_ROWS
    C = kc.shape[1]
    n_q = NA_BLOCK_ROWS * GRID_W
    single = pl.Buffered(1)
    block_class = lambda b, i: (jnp.where(i == 0, 0, jnp.where(i == n_blk - 1, 2, 1)), 0, 0, 0)
    return pl.pallas_call(
        functools.partial(_natten_kernel, rows=rows),
        grid=(B, n_blk),
        in_specs=[
            pl.BlockSpec((None, n_q, D_NA), lambda b, i: (b, i, 0)),
            pl.BlockSpec((None, S, D_NA), lambda b, i: (b, 0, 0), pipeline_mode=single),
            pl.BlockSpec((None, S, D_NA), lambda b, i: (b, 0, 0), pipeline_mode=single),
            pl.BlockSpec((None, C, D_NA), lambda b, i: (b, 0, 0)),
            pl.BlockSpec((None, C, D_NA), lambda b, i: (b, 0, 0)),
            pl.BlockSpec((None,) + bias.shape[1:], block_class),
        ],
        out_specs=pl.BlockSpec((None, n_q, D_NA), lambda b, i: (b, i, 0)),
        out_shape=jax.ShapeDtypeStruct((B, S, D_NA), BF16),
        compiler_params=_cparams(("parallel", "arbitrary")),
        name="natten",
    )(q, k, v, kc, vc, bias)


def _natten_bias(rpb, rows):
    col = np.arange(GRID_W)
    dc = np.clip(col[None, :] - col[:, None], -(WIN_COLS - 1), WIN_COLS - 1) + (WIN_COLS - 1)
    col_start = np.clip(col - WIN_COLS // 2, 0, GRID_W - WIN_COLS)
    col_mask = (col[None, :] >= col_start[:, None]) & (col[None, :] < col_start[:, None] + WIN_COLS)
    r0 = np.array([0, 2 * NA_BLOCK_ROWS, rows - NA_BLOCK_ROWS])
    r = r0[:, None] + np.arange(NA_BLOCK_ROWS)[None, :]
    key_row = np.clip(r0 - WIN_ROWS // 2, 0, rows - NA_KEY_ROWS)[:, None, None] + np.arange(NA_KEY_ROWS)
    win_start = np.clip(r - WIN_ROWS // 2, 0, rows - WIN_ROWS)[:, :, None]
    row_mask = (key_row >= win_start) & (key_row < win_start + WIN_ROWS)
    dr = np.clip(key_row - r[:, :, None] + (WIN_ROWS - 1), 0, 2 * WIN_ROWS - 2)
    pick_c = jnp.asarray(dc[None] == np.arange(2 * WIN_COLS - 1)[:, None, None], dtype=F32)
    pick_r = jnp.asarray(dr[..., None] == np.arange(2 * WIN_ROWS - 1), dtype=F32)
    b = jnp.einsum('crkd,hde,eqx->chrqkx', pick_r, rpb.astype(F32), pick_c, precision=HI)
    mask = row_mask[:, None, :, None, :, None] & col_mask[None, None, None, :, None, :]
    b = jnp.where(mask, b, NEG_INF)
    return b.reshape(NA_CLASSES, NA_HEADS, NA_BLOCK_ROWS * GRID_W, NA_KEY_ROWS * GRID_W)


def _rope_tables(seq):
    pos = jnp.arange(seq, dtype=jnp.int32)
    rows = (pos // GRID_W).astype(F32)
    cols = (pos % GRID_W).astype(F32)
    nf = HEAD_DIM // 4
    inv = ROPE_THETA ** (-jnp.arange(nf, dtype=F32) / nf)
    ar = rows[:, None] * inv[None, :]
    ac = cols[:, None] * inv[None, :]
    cos = jnp.concatenate([jnp.cos(ar), jnp.cos(ar), jnp.cos(ac), jnp.cos(ac)], axis=-1)
    sin = jnp.concatenate([-jnp.sin(ar), jnp.sin(ar), -jnp.sin(ac), jnp.sin(ac)], axis=-1)
    return jnp.tile(cos, (1, NA_HEADS)), jnp.tile(sin, (1, NA_HEADS))


def _postmix_kernel(yh_ref, yn_ref, gate_ref, x_ref, g1_ref, sh2_ref, sc2_ref, n2_ref, whyo_ref, wnao_ref,
                    wout_ref, wq_ref, x1_ref, h2_ref, pq_ref):
    a = _dot(yh_ref[...].astype(BF16), whyo_ref[...])
    b = _dot(yn_ref[...], wnao_ref[...])
    gate = gate_ref[...].astype(F32)
    merged = gate[:, :D_MODEL] * a + gate[:, D_MODEL:] * b
    x1 = x_ref[...] + g1_ref[...] * _dot(merged.astype(BF16), wout_ref[...])
    x1_ref[...] = x1
    h2 = _modulated_norm(x1, n2_ref[...], sh2_ref[...], sc2_ref[...]).astype(BF16)
    h2_ref[...] = h2
    for hp in range(2 * PEER_HEADS):
        pq_ref[hp] = _dot(h2, wq_ref[:, hp * PEER_D_HALF:(hp + 1) * PEER_D_HALF]).astype(BF16)


def _postmix(yh, yn, gate, x2, g1, sh2, sc2, n2g, whyo, wnao, wout, wq, seq):
    T = x2.shape[0]
    tiles_per_batch = seq // TOK_TILE
    row = lambda i: (i, 0)
    per_batch = lambda i: (i // tiles_per_batch, 0, 0)
    const = lambda i: (0, 0)
    nq = wq.shape[1]
    return pl.pallas_call(
        _postmix_kernel,
        grid=(T // TOK_TILE,),
        in_specs=[
            pl.BlockSpec((TOK_TILE, D_HY), row),
            pl.BlockSpec((TOK_TILE, D_NA), row),
            pl.BlockSpec((TOK_TILE, 2 * D_MODEL), row),
            pl.BlockSpec((TOK_TILE, D_MODEL), row),
            pl.BlockSpec((None, 1, D_MODEL), per_batch),
            pl.BlockSpec((None, 1, D_MODEL), per_batch),
            pl.BlockSpec((None, 1, D_MODEL), per_batch),
            pl.BlockSpec((1, D_MODEL), const),
            pl.BlockSpec((D_HY, D_MODEL), const),
            pl.BlockSpec((D_NA, D_MODEL), const),
            pl.BlockSpec((D_MODEL, D_MODEL), const),
            pl.BlockSpec((D_MODEL, nq), const),
        ],
        out_specs=[
            pl.BlockSpec((TOK_TILE, D_MODEL), row),
            pl.BlockSpec((TOK_TILE, D_MODEL), row),
            pl.BlockSpec((nq // PEER_D_HALF, TOK_TILE, PEER_D_HALF), lambda i: (0, i, 0)),
        ],
        out_shape=[
            jax.ShapeDtypeStruct((T, D_MODEL), F32),
            jax.ShapeDtypeStruct((T, D_MODEL), BF16),
            jax.ShapeDtypeStruct((nq // PEER_D_HALF, T, PEER_D_HALF), BF16),
        ],
        compiler_params=_cparams(("parallel",)),
        name="postmix",
    )(yh, yn, gate, x2, g1, sh2, sc2, n2g, whyo, wnao, wout, wq)


SUBLANES = 8
ROUTE_TILE = 256


def _argmax_tree(vals, idxs):
    while len(vals) > 1:
        nv, ni = [], []
        for j in range(0, len(vals) - 1, 2):
            right = vals[j + 1] > vals[j]
            nv.append(jnp.where(right, vals[j + 1], vals[j]))
            ni.append(jnp.where(right, idxs[j + 1], idxs[j]))
        if len(vals) % 2:
            nv.append(vals[-1])
            ni.append(idxs[-1])
        vals, idxs = nv, ni
    return vals[0], idxs[0]


def _route_kernel(pq_ref, keys_ref, a_ref, b_ref, g_ref, tv_ref, ti_ref):
    rt = pq_ref.shape[1]
    n_grp = PEER_N_KEYS // SUBLANES
    sub = lax.broadcasted_iota(jnp.int32, (SUBLANES, rt), 0)
    key_idx = [sub + SUBLANES * g for g in range(n_grp)]

    def first_level(hp, carry):
        s = _dot_nt(keys_ref[hp], pq_ref[hp]).reshape(n_grp, SUBLANES, rt)
        slabs = [s[g] for g in range(n_grp)]
        h, p = hp // 2, hp % 2
        for j in range(PEER_TOPK):
            v, i = _argmax_tree(slabs, key_idx)
            for shift in (4, 2, 1):
                v2, i2 = pltpu.roll(v, shift, 0), pltpu.roll(i, shift, 0)
                take = (v2 > v) | ((v2 == v) & (i2 < i))
                v, i = jnp.where(take, v2, v), jnp.where(take, i2, i)
            tv_ref[p, j, pl.ds(h, 1), :] = v[0:1]
            ti_ref[p, j, pl.ds(h, 1), :] = i[0:1]
            slabs = [jnp.where(key_idx[g] == i, -jnp.inf, slabs[g]) for g in range(n_grp)]
        return carry

    lax.fori_loop(0, 2 * PEER_HEADS, first_level, 0, unroll=2)

    pairs = [(a, b) for a in range(PEER_TOPK) for b in range(PEER_TOPK) if (a + 1) * (b + 1) <= PEER_TOPK]
    cand = [tv_ref[0, a] + tv_ref[1, b] for a, b in pairs]
    pos = [jnp.full((PEER_HEADS, rt), a * PEER_TOPK + b, jnp.int32) for a, b in pairs]
    best, sel_a, sel_b = [], [], []
    for j in range(PEER_TOPK):
        v, w = _argmax_tree(cand, pos)
        best.append(v)
        ia = jnp.zeros((PEER_HEADS, rt), jnp.int32)
        ib = jnp.zeros((PEER_HEADS, rt), jnp.int32)
        for n, (a, b) in enumerate(pairs):
            hit = w == (a * PEER_TOPK + b)
            cand[n] = jnp.where(hit, -jnp.inf, cand[n])
            ia = jnp.where(hit, ti_ref[0, a], ia)
            ib = jnp.where(hit, ti_ref[1, b], ib)
        sel_a.append(ia)
        sel_b.append(ib)
    ex = [jnp.exp(v - best[0]) for v in best]
    denom = functools.reduce(lambda x, y: x + y, ex)
    gates = [e / denom for e in ex]
    a_ref[...] = jnp.concatenate(sel_a, axis=0).T
    b_ref[...] = jnp.concatenate(sel_b, axis=0).T
    g_ref[...] = jnp.concatenate(gates, axis=0).T


def _route(pq, keys_bf):
    n_hp, T, _ = pq.shape
    n_sel = PEER_HEADS * PEER_TOPK
    row = lambda i: (i, 0)
    return pl.pallas_call(
        _route_kernel,
        grid=(T // ROUTE_TILE,),
        in_specs=[
            pl.BlockSpec((n_hp, ROUTE_TILE, PEER_D_HALF), lambda i: (0, i, 0)),
            pl.BlockSpec((n_hp, PEER_N_KEYS, PEER_D_HALF), lambda i: (0, 0, 0)),
        ],
        out_specs=[pl.BlockSpec((ROUTE_TILE, n_sel), row)] * 3,
        out_shape=[
            jax.ShapeDtypeStruct((T, n_sel), jnp.int32),
            jax.ShapeDtypeStruct((T, n_sel), jnp.int32),
            jax.ShapeDtypeStruct((T, n_sel), F32),
        ],
        scratch_shapes=[
            pltpu.VMEM((2, PEER_TOPK, PEER_HEADS, ROUTE_TILE), F32),
            pltpu.VMEM((2, PEER_TOPK, PEER_HEADS, ROUTE_TILE), jnp.int32),
        ],
        compiler_params=_cparams(("parallel",)),
        name="route",
    )(pq, keys_bf)


WBUILD_TILE = 128


WBUILD_PITCH = 136
WBUILD_UNROLL = 8


def _wbuild_kernel(a_ref, b_ref, g_ref, w_ref, s_ref):
    n = PEER_N_KEYS
    tb = a_ref.shape[0]
    key = lax.broadcasted_iota(jnp.int32, (n, a_ref.shape[1]), 0)

    def body(t, carry):
        a = a_ref[pl.ds(t, 1), :]
        b = b_ref[pl.ds(t, 1), :]
        g = g_ref[pl.ds(t, 1), :]
        ga = jnp.where(a == key, g, 0.0).astype(BF16)
        ob = jnp.where(b == key, 1.0, 0.0).astype(BF16)
        s_ref[pl.ds(pl.multiple_of(t * WBUILD_PITCH, SUBLANES), n), :] = _dot_nt(ga, ob)
        return carry

    lax.fori_loop(0, tb, body, 0, unroll=WBUILD_UNROLL)

    def emit(i1, carry):
        w_ref[i1] = s_ref[pl.ds(i1, tb, stride=WBUILD_PITCH), :].astype(BF16)
        return carry

    lax.fori_loop(0, n, emit, 0, unroll=WBUILD_UNROLL)


def _wbuild(a, b, g):
    T, n_sel = a.shape
    row = lambda i: (i, 0)
    return pl.pallas_call(
        _wbuild_kernel,
        grid=(T // WBUILD_TILE,),
        in_specs=[pl.BlockSpec((WBUILD_TILE, n_sel), row)] * 3,
        out_specs=pl.BlockSpec((None, PEER_N_KEYS, WBUILD_TILE, PEER_N_KEYS), lambda i: (i, 0, 0, 0)),
        out_shape=jax.ShapeDtypeStruct((T // WBUILD_TILE, PEER_N_KEYS, WBUILD_TILE, PEER_N_KEYS), BF16),
        scratch_shapes=[pltpu.VMEM((WBUILD_TILE * WBUILD_PITCH, PEER_N_KEYS), F32)],
        compiler_params=_cparams(("parallel",)),
        name="wbuild",
    )(a, b, g)


def _peer_dense_kernel(h2_ref, w_ref, u_ref, v_ref, x1_ref, g2_ref, o_ref, acc_ref):
    e = pl.program_id(1)

    @pl.when(e == 0)
    def _():
        acc_ref[...] = jnp.zeros_like(acc_ref)

    a = _dot_nt(h2_ref[...], u_ref[...])
    act = 0.5 * a * (1.0 + lax.erf(a * (2.0 ** -0.5)))
    w = jnp.concatenate(
        [jnp.concatenate([w_ref[tb, i] for i in range(w_ref.shape[1])], axis=1) for tb in range(w_ref.shape[0])],
        axis=0)
    acc_ref[...] += _dot((w.astype(F32) * act).astype(BF16), v_ref[...])

    @pl.when(e == pl.num_programs(1) - 1)
    def _():
        o_ref[...] = x1_ref[...] + g2_ref[...] * acc_ref[...]


def _peer_dense(h2, w, u_bf, v_bf, x1, g2, seq):
    T = h2.shape[0]
    tiles_per_batch = seq // PEER_TOK_TILE
    n_exp = u_bf.shape[0]
    return pl.pallas_call(
        _peer_dense_kernel,
        grid=(T // PEER_TOK_TILE, n_exp // PEER_EXP_TILE),
        in_specs=[
            pl.BlockSpec((PEER_TOK_TILE, D_MODEL), lambda i, e: (i, 0)),
            pl.BlockSpec((PEER_TOK_TILE // WBUILD_TILE, PEER_EXP_TILE // PEER_N_KEYS, WBUILD_TILE, PEER_N_KEYS),
                         lambda i, e: (i, e, 0, 0)),
            pl.BlockSpec((PEER_EXP_TILE, D_MODEL), lambda i, e: (e, 0)),
            pl.BlockSpec((PEER_EXP_TILE, D_MODEL), lambda i, e: (e, 0)),
            pl.BlockSpec((PEER_TOK_TILE, D_MODEL), lambda i, e: (i, 0)),
            pl.BlockSpec((None, 1, D_MODEL), lambda i, e: (i // tiles_per_batch, 0, 0)),
        ],
        out_specs=pl.BlockSpec((PEER_TOK_TILE, D_MODEL), lambda i, e: (i, 0)),
        out_shape=jax.ShapeDtypeStruct((T, D_MODEL), F32),
        scratch_shapes=[pltpu.VMEM((PEER_TOK_TILE, D_MODEL), F32)],
        compiler_params=_cparams(("parallel", "arbitrary")),
        name="peer_dense",
    )(h2, w, u_bf, v_bf, x1, g2)


HI = lax.Precision.HIGHEST


HY_LANES = 128


def _shortconv_kernel(z_ref, w_ref, b_ref, o_ref):
    z = z_ref[...]
    n = z.shape[0]
    row = lax.broadcasted_iota(jnp.int32, z.shape, 0)
    prev = jnp.where(row == 0, 0.0, pltpu.roll(z, 1, 0))
    nxt = jnp.where(row == n - 1, 0.0, pltpu.roll(z, n - 1, 0))
    o_ref[...] = w_ref[0:1] * prev + w_ref[1:2] * z + w_ref[2:3] * nxt + b_ref[...]


def _shortconv(zh, w, b):
    B, L, C = zh.shape
    return pl.pallas_call(
        _shortconv_kernel,
        grid=(B, C // HY_LANES),
        in_specs=[
            pl.BlockSpec((None, L, HY_LANES), lambda i, j: (i, 0, j)),
            pl.BlockSpec((3, HY_LANES), lambda i, j: (0, j)),
            pl.BlockSpec((1, HY_LANES), lambda i, j: (0, j)),
        ],
        out_specs=pl.BlockSpec((None, L, HY_LANES), lambda i, j: (i, 0, j)),
        out_shape=jax.ShapeDtypeStruct((B, L, C), F32),
        compiler_params=_cparams(("parallel", "parallel")),
        name="shortconv",
    )(zh, w, b)


FFT_R = 128
FFT_N = FFT_R * FFT_R
FFT_PITCH = 136
FFT_K1_CHUNK = 16
FFT_UNROLL = 4


def _fft_tables():
    r = jnp.arange(FFT_R, dtype=jnp.int32)
    n2, k1, n1 = r[:, None, None], r[None, :, None], r[None, None, :]
    ang = (2.0 * math.pi / FFT_N) * ((k1 * (FFT_R * n1 + n2)) % FFT_N).astype(F32)
    c, s = jnp.cos(ang), jnp.sin(ang)
    g_full = jnp.concatenate([c, -s], axis=1)
    h_half = jnp.concatenate([jnp.swapaxes(c, 1, 2), -jnp.swapaxes(s, 1, 2)], axis=2)[:, :FFT_R // 2] / FFT_N
    ang2 = (2.0 * math.pi / FFT_R) * ((r[:, None] * r[None, :]) % FFT_R).astype(F32)
    c2, s2 = jnp.cos(ang2), jnp.sin(ang2)
    f_fwd = jnp.concatenate([jnp.concatenate([c2, s2], 1), jnp.concatenate([-s2, c2], 1)], 0)
    f_inv = jnp.concatenate([jnp.concatenate([c2, -s2], 1), jnp.concatenate([s2, c2], 1)], 0)
    return g_full.astype(BF16), h_half.astype(BF16), f_fwd.astype(BF16), f_inv.astype(BF16)


def _fft_stage1(load_slab, g_ref, bre_ref, bim_ref):
    def body(n2, carry):
        out = _dot(g_ref[n2], load_slab(n2).astype(BF16))
        bre_ref[pl.ds(n2, FFT_R, stride=FFT_PITCH), :] = out[:FFT_R]
        bim_ref[pl.ds(n2, FFT_R, stride=FFT_PITCH), :] = out[FFT_R:]
        return carry

    lax.fori_loop(0, FFT_R, body, 0, unroll=FFT_UNROLL)


def _k1_rows(k1):
    return pl.ds(pl.multiple_of(k1 * FFT_PITCH, SUBLANES), FFT_R)


def _load_k1_pair(bre_ref, bim_ref, k1):
    return jnp.concatenate(
        [jnp.concatenate([bre_ref[_k1_rows(k1 + d), :], bim_ref[_k1_rows(k1 + d), :]], axis=0) for d in range(2)],
        axis=1)


FILT_ROWS = 512
FILT_FEAT = 128


def _filt_kernel(feat_ref, delta_ref, w1_ref, b1_ref, w2_ref, b2_ref, w3_ref, b3_ref, w4_ref, freq_ref,
                 h_ref, sum_ref):
    @pl.when(pl.program_id(0) == 0)
    def _():
        sum_ref[...] = jnp.zeros_like(sum_ref)

    feat = feat_ref[...]
    freq = freq_ref[...]
    dot = lambda a, b: jnp.dot(a, b, precision=HI, preferred_element_type=F32)
    h = jnp.sin(freq * (dot(feat, w1_ref[...]) + b1_ref[...]))
    h = jnp.sin(freq * (dot(h, w2_ref[...]) + b2_ref[...]))
    h = jnp.sin(freq * (dot(h, w3_ref[...]) + b3_ref[...]))
    h = dot(h, w4_ref[...])
    n_feat = 1 + 2 * HY_BANDS
    t = feat[:, 0:1]
    forward = feat[:, n_feat:n_feat + 1] > 0.5
    keep = feat[:, n_feat + 1:n_feat + 2]
    decay = jnp.exp(-t * delta_ref[...])
    outs = []
    for o in range(HY_ORDER):
        base = o * 2 * D_HY
        outs.append(jnp.where(forward, h[:, base:base + D_HY], h[:, base + D_HY:base + 2 * D_HY]) * decay)
    out = jnp.concatenate(outs, axis=1)
    sum_ref[...] += jnp.sum(jnp.abs(out), axis=0, keepdims=True)
    h_ref[...] = out * keep


def _filt_features(L):
    r = np.arange(FFT_N)
    n = FFT_R * (r % FFT_R) + r // FFT_R
    j = np.where(n < L, n, np.where(n > L, 2 * L - n, 0))
    jj = jnp.asarray(j, dtype=F32)[:, None]
    t = jj / (L - 1.0)
    w = (2.0 * math.pi / L) * jj
    bands = jnp.linspace(1e-4, HY_BANDS - 1.0, HY_BANDS, dtype=F32)[None, :]
    flags = jnp.asarray(np.stack([n < L, n != L], axis=1), dtype=F32)
    feats = jnp.concatenate([t, jnp.cos(bands * w), -jnp.sin(bands * w), flags], axis=-1)
    return jnp.pad(feats, ((0, 0), (0, FILT_FEAT - feats.shape[1])))


def _filters(p, L):
    feats = _filt_features(L)
    deltas = jnp.abs(jnp.linspace(math.log(HY_DECAY_TARGET) / HY_SLOW_PCT,
                                  math.log(HY_DECAY_TARGET) / HY_FAST_PCT, D_HY, dtype=F32))[None, :]
    w1 = jnp.pad(p['hy_f1_w'], ((0, FILT_FEAT - p['hy_f1_w'].shape[0]), (0, 0)))
    n_out = HY_ORDER * D_HY
    const = lambda i: (0, 0)
    full = lambda a: pl.BlockSpec(a.shape, const)
    args = [deltas, w1, p['hy_f1_b'][None], p['hy_f2_w'], p['hy_f2_b'][None], p['hy_f3_w'], p['hy_f3_b'][None],
            p['hy_f4_w'], p['hy_sin_freq'][None]]
    return pl.pallas_call(
        _filt_kernel,
        grid=(FFT_N // FILT_ROWS,),
        in_specs=[pl.BlockSpec((FILT_ROWS, FILT_FEAT), lambda i: (i, 0))] + [full(a) for a in args],
        out_specs=[pl.BlockSpec((FILT_ROWS, n_out), lambda i: (i, 0)), pl.BlockSpec((1, n_out), const)],
        out_shape=[jax.ShapeDtypeStruct((FFT_N, n_out), F32), jax.ShapeDtypeStruct((1, n_out), F32)],
        compiler_params=_cparams(("arbitrary",)),
        name="hyena_filter",
    )(feats, *args)


def _fspec_kernel(h_ref, sum_ref, g_ref, f_ref, o_ref, bre_ref, bim_ref):
    c = pl.program_id(1)

    @pl.when(c == 0)
    def _():
        _fft_stage1(lambda n2: h_ref[pl.ds(pl.multiple_of(n2 * FFT_R, FFT_R), FFT_R), :], g_ref, bre_ref, bim_ref)

    inv = 1.0 / sum_ref[...]

    lanes = h_ref.shape[1]

    def slab_pair(j, carry):
        x = _dot(f_ref[...], _load_k1_pair(bre_ref, bim_ref, c * FFT_K1_CHUNK + 2 * j).astype(BF16))
        for d in range(2):
            rows = pl.ds(pl.multiple_of((2 * j + d) * FFT_R, FFT_R), FFT_R)
            o_ref[0, rows, :] = x[:FFT_R, d * lanes:(d + 1) * lanes] * inv
            o_ref[1, rows, :] = x[FFT_R:, d * lanes:(d + 1) * lanes] * inv
        return carry

    lax.fori_loop(0, FFT_K1_CHUNK // 2, slab_pair, 0, unroll=FFT_UNROLL // 2)


def _filter_spectrum(h, hsum, g_full, f_fwd):
    n_ch = h.shape[1]
    single = pl.Buffered(1)
    return pl.pallas_call(
        _fspec_kernel,
        grid=(n_ch // HY_LANES, FFT_R // FFT_K1_CHUNK),
        in_specs=[
            pl.BlockSpec((FFT_N, HY_LANES), lambda j, c: (0, j), pipeline_mode=single),
            pl.BlockSpec((1, HY_LANES), lambda j, c: (0, j)),
            pl.BlockSpec(g_full.shape, lambda j, c: (0, 0, 0), pipeline_mode=single),
            pl.BlockSpec(f_fwd.shape, lambda j, c: (0, 0)),
        ],
        out_specs=pl.BlockSpec((2, FFT_K1_CHUNK * FFT_R, HY_LANES), lambda j, c: (0, c, j)),
        out_shape=jax.ShapeDtypeStruct((2, FFT_N, n_ch), F32),
        scratch_shapes=[pltpu.VMEM((FFT_R * FFT_PITCH, HY_LANES), F32)] * 2,
        compiler_params=_cparams(("parallel", "arbitrary")),
        name="hyena_filter_fft",
    )(h, hsum, g_full, f_fwd)


def _hyconv_kernel(u_ref, gate_ref, kf_ref, skip_ref, g_ref, h_ref, f_ref, fi_ref, o_ref, bre_ref, bim_ref):
    c = pl.program_id(2)
    half = FFT_R // 2

    def rows_of(n2):
        return pl.ds(pl.multiple_of(n2 * half, half), half)

    @pl.when(c == 0)
    def _():
        _fft_stage1(lambda n2: u_ref[rows_of(n2), :], g_ref, bre_ref, bim_ref)

    lanes = u_ref.shape[1]

    def slab_pair(j, carry):
        k1 = c * FFT_K1_CHUNK + 2 * j
        x = _dot(f_ref[...], _load_k1_pair(bre_ref, bim_ref, k1).astype(BF16))
        xr, xi = x[:FFT_R], x[FFT_R:]
        kr, ki = [jnp.concatenate([kf_ref[part, pl.ds(pl.multiple_of((2 * j + d) * FFT_R, FFT_R), FFT_R), :]
                                   for d in range(2)], axis=1) for part in range(2)]
        y = jnp.concatenate([xr * kr - xi * ki, xr * ki + xi * kr], axis=0)
        z = _dot(fi_ref[...], y.astype(BF16))
        for d in range(2):
            bre_ref[_k1_rows(k1 + d), :] = z[:FFT_R, d * lanes:(d + 1) * lanes]
            bim_ref[_k1_rows(k1 + d), :] = z[FFT_R:, d * lanes:(d + 1) * lanes]
        return carry

    lax.fori_loop(0, FFT_K1_CHUNK // 2, slab_pair, 0, unroll=FFT_UNROLL // 2)

    @pl.when(c == pl.num_programs(2) - 1)
    def _():
        def body(n2, carry):
            z = jnp.concatenate([bre_ref[pl.ds(n2, FFT_R, stride=FFT_PITCH), :],
                                 bim_ref[pl.ds(n2, FFT_R, stride=FFT_PITCH), :]], axis=0)
            conv = _dot(h_ref[n2], z.astype(BF16))
            rows = rows_of(n2)
            o_ref[rows, :] = gate_ref[rows, :] * (conv + skip_ref[...] * u_ref[rows, :])
            return carry

        lax.fori_loop(0, FFT_R, body, 0, unroll=FFT_UNROLL)


def _hyconv(u, u_col, gate, gate_col, kf, kf_col, skip, tables):
    g_half, h_half, f_fwd, f_inv = tables
    B, L, _ = u.shape
    single = pl.Buffered(1)
    return pl.pallas_call(
        _hyconv_kernel,
        grid=(B, D_HY // HY_LANES, FFT_R // FFT_K1_CHUNK),
        in_specs=[
            pl.BlockSpec((None, L, HY_LANES), lambda b, j, c: (b, 0, u_col + j), pipeline_mode=single),
            pl.BlockSpec((None, L, HY_LANES), lambda b, j, c: (b, 0, gate_col + j), pipeline_mode=single),
            pl.BlockSpec((2, FFT_K1_CHUNK * FFT_R, HY_LANES), lambda b, j, c: (0, c, kf_col + j)),
            pl.BlockSpec((1, HY_LANES), lambda b, j, c: (0, j)),
            pl.BlockSpec(g_half.shape, lambda b, j, c: (0, 0, 0), pipeline_mode=single),
            pl.BlockSpec(h_half.shape, lambda b, j, c: (0, 0, 0), pipeline_mode=single),
            pl.BlockSpec(f_fwd.shape, lambda b, j, c: (0, 0)),
            pl.BlockSpec(f_inv.shape, lambda b, j, c: (0, 0)),
        ],
        out_specs=pl.BlockSpec((None, L, HY_LANES), lambda b, j, c: (b, 0, j)),
        out_shape=jax.ShapeDtypeStruct((B, L, D_HY), F32),
        scratch_shapes=[pltpu.VMEM((FFT_R * FFT_PITCH, HY_LANES), F32)] * 2,
        compiler_params=_cparams(("parallel", "parallel", "arbitrary")),
        name="hyena_conv",
    )(u, gate, kf, skip, g_half, h_half, f_fwd, f_inv)


def _slab_major(a, n1):
    B, L, C = a.shape
    return a.reshape(B, n1, L // n1, C).transpose(0, 2, 1, 3).reshape(B, L, C)


def _hyena(zh, p):
    B, L, _ = zh.shape
    assert 2 * L == FFT_N
    g_full, h_half, f_fwd, f_inv = _fft_tables()
    tables = (g_full[:, :, :FFT_R // 2], h_half, f_fwd, f_inv)
    filt, filt_sum = _filters(p, L)
    both_dirs = filt_sum[:, :]
    kf = _filter_spectrum(filt, both_dirs, g_full, f_fwd)
    zc = _slab_major(_shortconv(zh, p['hy_conv_w'], p['hy_conv_b'][None]), FFT_R // 2)
    lanes = D_HY // HY_LANES
    y = _hyconv(zc, 0, zc, lanes, kf, 0, p['hy_skip'][0:1], tables)
    y = _hyconv(y, 0, zc, 2 * lanes, kf, lanes, p['hy_skip'][1:2], tables)
    return _slab_major(y, L // (FFT_R // 2))


def kernel(x, c, ctx, c_ctx, norm1_g, norm2_g, w_ada, b_ada, w_in, b_in, hy_conv_w, hy_conv_b, hy_f1_w, hy_f1_b, hy_f2_w, hy_f2_b, hy_f3_w, hy_f3_b, hy_f4_w, hy_sin_freq, hy_skip, q_norm_g, k_norm_g, na_rpb, w_hy_out, w_na_out, w_out, peer_w_q, peer_keys, peer_u, peer_v):
    assert w_in.shape[0] == 1, "single-layer block"
    B, S, D = x.shape
    C = ctx.shape[1]
    T = B * S

    m_lat = (jax.nn.silu(c) @ w_ada[0] + b_ada[0]).reshape(B, N_MOD, 1, D)
    m_ctx = (jax.nn.silu(c_ctx) @ w_ada[0] + b_ada[0]).reshape(N_MOD, 1, D)
    sh1, sc1, g1, sh2, sc2, g2 = [m_lat[:, i] for i in range(N_MOD)]

    w_in_bf = w_in[0].astype(BF16)
    b_in2 = b_in[0][None, :]
    gain1 = norm1_g[0][None, :]
    qg = jnp.tile(q_norm_g[0], NA_HEADS)[None, :]
    kg = jnp.tile(k_norm_g[0], NA_HEADS)[None, :]
    seg = jnp.asarray(np.kron(np.eye(NA_HEADS), np.full((HEAD_DIM, HEAD_DIM), 1.0 / HEAD_DIM)), dtype=BF16)
    cos, sin = _rope_tables(S)

    x2 = x.reshape(T, D)
    zh, q, k, v, gate = _inproj(x2, sh1, sc1, gain1, w_in_bf, b_in2, cos, sin, qg, kg, seg, S)

    k_ctx, v_ctx = _ctxkv(ctx.reshape(B * C, D), m_ctx[0], m_ctx[1], gain1,
                          w_in_bf[:, COL_K:COL_V], b_in2[:, COL_K:COL_V],
                          w_in_bf[:, COL_V:COL_G_HY], b_in2[:, COL_V:COL_G_HY], kg, seg)

    p = {'hy_conv_w': hy_conv_w[0], 'hy_conv_b': hy_conv_b[0], 'hy_f1_w': hy_f1_w[0], 'hy_f1_b': hy_f1_b[0],
         'hy_f2_w': hy_f2_w[0], 'hy_f2_b': hy_f2_b[0], 'hy_f3_w': hy_f3_w[0], 'hy_f3_b': hy_f3_b[0],
         'hy_f4_w': hy_f4_w[0], 'hy_sin_freq': hy_sin_freq[0], 'hy_skip': hy_skip[0]}
    y_hy = _hyena(zh.reshape(B, S, 3 * D_HY), p).reshape(T, D_HY)

    y_na = _natten(q.reshape(B, S, D_NA), k.reshape(B, S, D_NA), v.reshape(B, S, D_NA),
                   k_ctx.reshape(B, C, D_NA), v_ctx.reshape(B, C, D_NA), _natten_bias(na_rpb[0], S // GRID_W)).reshape(T, D_NA)

    x1, h2, pq = _postmix(y_hy, y_na, gate, x2, g1, sh2, sc2, norm2_g[0][None, :],
                          w_hy_out[0].astype(BF16), w_na_out[0].astype(BF16), w_out[0].astype(BF16),
                          peer_w_q[0].astype(BF16), S)

    keys_bf = peer_keys[0].astype(BF16).reshape(2 * PEER_HEADS, PEER_N_KEYS, PEER_D_HALF)
    sel_a, sel_b, sel_g = _route(pq, keys_bf)
    w = _wbuild(sel_a, sel_b, sel_g)
    out = _peer_dense(h2, w, peer_u[0].astype(BF16), peer_v[0].astype(BF16), x1, g2, S)
    return out.reshape(B, S, D)
```

```python
import functools
import math

import numpy as np
import jax
import jax.numpy as jnp
from jax import lax
from jax.experimental import pallas as pl
from jax.experimental.pallas import tpu as pltpu

F32 = jnp.float32
BF16 = jnp.bfloat16

D_MODEL = 1024
GRID_W = 64
EPS = 1e-6
N_MOD = 6

D_HY = 512
HY_ORDER = 2
HY_BANDS = 16
HY_DECAY_TARGET = 1e-2
HY_FAST_PCT = 0.3
HY_SLOW_PCT = 1.5

NA_HEADS = 8
HEAD_DIM = 64
D_NA = NA_HEADS * HEAD_DIM
WIN_ROWS = 8
WIN_COLS = 16
ROPE_THETA = 10000.0
NEG_INF = -1e30

PEER_HEADS = 8
PEER_N_KEYS = 128
PEER_N_EXPERTS = PEER_N_KEYS * PEER_N_KEYS
PEER_TOPK = 16
PEER_D_KEY = 256
PEER_D_HALF = PEER_D_KEY // 2

COL_HY = 0
COL_Q = COL_HY + 3 * D_HY
COL_K = COL_Q + D_NA
COL_V = COL_K + D_NA
COL_G_HY = COL_V + D_NA
N_PROJ = COL_G_HY + 2 * D_MODEL

VMEM_LIMIT = 56 * 1024 * 1024
TOK_TILE = 256
PEER_TOK_TILE = 512
PEER_EXP_TILE = 2048


def _cparams(sem):
    return pltpu.CompilerParams(dimension_semantics=sem, vmem_limit_bytes=VMEM_LIMIT)


def _dot(a, b):
    return jnp.dot(a, b, preferred_element_type=F32)


def _dot_nt(a, b):
    return lax.dot_general(a, b, (((1,), (1,)), ((), ())), preferred_element_type=F32)


def _modulated_norm(x, gain, shift, scale):
    ms = jnp.mean(x * x, axis=-1, keepdims=True)
    return (x * lax.rsqrt(ms + EPS) * gain) * (1.0 + scale) + shift


def _head_rms(z, seg, gain):
    zz = z * z
    hi = zz.astype(BF16)
    lo = (zz - hi.astype(F32)).astype(BF16)
    ms = _dot(hi, seg) + _dot(lo, seg)
    return z * lax.rsqrt(ms + EPS) * gain


def _rope(z, cos, sin_signed):
    n = z.shape[-1]
    lane = lax.broadcasted_iota(jnp.int32, z.shape, 1)
    first = (lane // (HEAD_DIM // 4)) % 2 == 0
    partner = jnp.where(first, pltpu.roll(z, n - HEAD_DIM // 4, 1), pltpu.roll(z, HEAD_DIM // 4, 1))
    return z * cos + partner * sin_signed


def _inproj_kernel(x_ref, shift_ref, scale_ref, g_ref, w_ref, b_ref, cos_ref, sin_ref, qg_ref, kg_ref,
                   seg_ref, zh_ref, q_ref, k_ref, v_ref, gate_ref):
    h = _modulated_norm(x_ref[...], g_ref[...], shift_ref[...], scale_ref[...]).astype(BF16)

    def proj(lo, hi):
        return _dot(h, w_ref[:, lo:hi]) + b_ref[:, lo:hi]

    zh_ref[...] = proj(COL_HY, COL_Q)
    cos, sin = cos_ref[...], sin_ref[...]
    seg = seg_ref[...]
    q = _rope(_head_rms(proj(COL_Q, COL_K), seg, qg_ref[...]), cos, sin)
    q_ref[...] = q.astype(BF16)
    k = _rope(_head_rms(proj(COL_K, COL_V), seg, kg_ref[...]), cos, sin)
    k_ref[...] = k.astype(BF16)
    v_ref[...] = proj(COL_V, COL_G_HY).astype(BF16)
    gate_ref[...] = jax.nn.sigmoid(proj(COL_G_HY, N_PROJ)).astype(BF16)


def _inproj(x2, shift, scale, gain, w_bf, b_in, cos, sin, qg, kg, seg, seq):
    T = x2.shape[0]
    tiles_per_batch = seq // TOK_TILE
    row = lambda i: (i, 0)
    per_batch = lambda i: (i // tiles_per_batch, 0, 0)
    const = lambda i: (0, 0)
    pos = lambda i: (i % tiles_per_batch, 0)
    return pl.pallas_call(
        _inproj_kernel,
        grid=(T // TOK_TILE,),
        in_specs=[
            pl.BlockSpec((TOK_TILE, D_MODEL), row),
            pl.BlockSpec((None, 1, D_MODEL), per_batch),
            pl.BlockSpec((None, 1, D_MODEL), per_batch),
            pl.BlockSpec((1, D_MODEL), const),
            pl.BlockSpec((D_MODEL, N_PROJ), const),
            pl.BlockSpec((1, N_PROJ), const),
            pl.BlockSpec((TOK_TILE, D_NA), pos),
            pl.BlockSpec((TOK_TILE, D_NA), pos),
            pl.BlockSpec((1, D_NA), const),
            pl.BlockSpec((1, D_NA), const),
            pl.BlockSpec((D_NA, D_NA), const),
        ],
        out_specs=[
            pl.BlockSpec((TOK_TILE, 3 * D_HY), row),
            pl.BlockSpec((TOK_TILE, D_NA), row),
            pl.BlockSpec((TOK_TILE, D_NA), row),
            pl.BlockSpec((TOK_TILE, D_NA), row),
            pl.BlockSpec((TOK_TILE, 2 * D_MODEL), row),
        ],
        out_shape=[
            jax.ShapeDtypeStruct((T, 3 * D_HY), F32),
            jax.ShapeDtypeStruct((T, D_NA), BF16),
            jax.ShapeDtypeStruct((T, D_NA), BF16),
            jax.ShapeDtypeStruct((T, D_NA), BF16),
            jax.ShapeDtypeStruct((T, 2 * D_MODEL), BF16),
        ],
        compiler_params=_cparams(("parallel",)),
        name="inproj",
    )(x2, shift, scale, gain, w_bf, b_in, cos, sin, qg, kg, seg)


def _ctxkv_kernel(x_ref, shift_ref, scale_ref, g_ref, wk_ref, bk_ref, wv_ref, bv_ref, kg_ref, seg_ref,
                  k_ref, v_ref):
    h = _modulated_norm(x_ref[...], g_ref[...], shift_ref[...], scale_ref[...]).astype(BF16)
    k = _head_rms(_dot(h, wk_ref[...]) + bk_ref[...], seg_ref[...], kg_ref[...])
    k_ref[...] = k.astype(BF16)
    v_ref[...] = (_dot(h, wv_ref[...]) + bv_ref[...]).astype(BF16)


def _ctxkv(ctx2, shift, scale, gain, wk, bk, wv, bv, kg, seg):
    T = ctx2.shape[0]
    row = lambda i: (i, 0)
    const = lambda i: (0, 0)
    return pl.pallas_call(
        _ctxkv_kernel,
        grid=(T // TOK_TILE,),
        in_specs=[
            pl.BlockSpec((TOK_TILE, D_MODEL), row),
            pl.BlockSpec((1, D_MODEL), const),
            pl.BlockSpec((1, D_MODEL), const),
            pl.BlockSpec((1, D_MODEL), const),
            pl.BlockSpec((D_MODEL, D_NA), const),
            pl.BlockSpec((1, D_NA), const),
            pl.BlockSpec((D_MODEL, D_NA), const),
            pl.BlockSpec((1, D_NA), const),
            pl.BlockSpec((1, D_NA), const),
            pl.BlockSpec((D_NA, D_NA), const),
        ],
        out_specs=[pl.BlockSpec((TOK_TILE, D_NA), row), pl.BlockSpec((TOK_TILE, D_NA), row)],
        out_shape=[jax.ShapeDtypeStruct((T, D_NA), BF16), jax.ShapeDtypeStruct((T, D_NA), BF16)],
        compiler_params=_cparams(("parallel",)),
        name="ctxkv",
    )(ctx2, shift, scale, gain, wk, bk, wv, bv, kg, seg)


NA_BLOCK_ROWS = 4
NA_KEY_ROWS = WIN_ROWS + NA_BLOCK_ROWS
NA_CLASSES = 3


def _natten_kernel(q_ref, k_ref, v_ref, kc_ref, vc_ref, bias_ref, o_ref, *, rows):
    r0 = pl.program_id(1) * NA_BLOCK_ROWS
    key_start = jnp.clip(r0 - WIN_ROWS // 2, 0, rows - NA_KEY_ROWS)
    start = pl.multiple_of(key_start * GRID_W, GRID_W * NA_BLOCK_ROWS)
    n_q = NA_BLOCK_ROWS * GRID_W
    n_loc = NA_KEY_ROWS * GRID_W
    scale = jnp.asarray(HEAD_DIM ** -0.5, BF16)
    pair = 2 * HEAD_DIM
    lane = lax.broadcasted_iota(jnp.int32, (n_q, pair), 1)
    for p in range(NA_HEADS // 2):
        cols = slice(p * pair, (p + 1) * pair)
        qp = q_ref[:, cols] * scale
        kp = k_ref[pl.ds(start, n_loc), cols]
        vp = v_ref[pl.ds(start, n_loc), cols]
        kcp = kc_ref[:, cols]
        vcp = vc_ref[:, cols]
        outs = []
        for hh in range(2):
            qm = jnp.where((lane // HEAD_DIM) == hh, qp, jnp.zeros_like(qp))
            s_loc = _dot_nt(qm, kp) + bias_ref[2 * p + hh]
            s_ctx = _dot_nt(qm, kcp)
            m = jnp.maximum(jnp.max(s_loc, axis=-1, keepdims=True), jnp.max(s_ctx, axis=-1, keepdims=True))
            e_loc = jnp.exp(s_loc - m)
            e_ctx = jnp.exp(s_ctx - m)
            denom = jnp.sum(e_loc, axis=-1, keepdims=True) + jnp.sum(e_ctx, axis=-1, keepdims=True)
            o = _dot(e_loc.astype(BF16), vp) + _dot(e_ctx.astype(BF16), vcp)
            outs.append(o / denom)
        o_ref[:, cols] = jnp.where((lane // HEAD_DIM) == 0, outs[0], outs[1]).astype(o_ref.dtype)


def _natten(q, k, v, kc, vc, bias):
    B, S, _ = q.shape
    rows = S // GRID_W
    n_blk = rows // NA_BLOCK_ROWS
    C = kc.shape[1]
    n_q = NA_BLOCK_ROWS * GRID_W
    single = pl.Buffered(1)
    block_class = lambda b, i: (jnp.where(i == 0, 0, jnp.where(i == n_blk - 1, 2, 1)), 0, 0, 0)
    return pl.pallas_call(
        functools.partial(_natten_kernel, rows=rows),
        grid=(B, n_blk),
        in_specs=[
            pl.BlockSpec((None, n_q, D_NA), lambda b, i: (b, i, 0)),
            pl.BlockSpec((None, S, D_NA), lambda b, i: (b, 0, 0), pipeline_mode=single),
            pl.BlockSpec((None, S, D_NA), lambda b, i: (b, 0, 0), pipeline_mode=single),
            pl.BlockSpec((None, C, D_NA), lambda b, i: (b, 0, 0)),
            pl.BlockSpec((None, C, D_NA), lambda b, i: (b, 0, 0)),
            pl.BlockSpec((None,) + bias.shape[1:], block_class),
        ],
        out_specs=pl.BlockSpec((None, n_q, D_NA), lambda b, i: (b, i, 0)),
        out_shape=jax.ShapeDtypeStruct((B, S, D_NA), BF16),
        compiler_params=_cparams(("parallel", "arbitrary")),
        name="natten",
    )(q, k, v, kc, vc, bias)


def _natten_bias(rpb, rows):
    col = np.arange(GRID_W)
    dc = np.clip(col[None, :] - col[:, None], -(WIN_COLS - 1), WIN_COLS - 1) + (WIN_COLS - 1)
    col_start = np.clip(col - WIN_COLS // 2, 0, GRID_W - WIN_COLS)
    col_mask = (col[None, :] >= col_start[:, None]) & (col[None, :] < col_start[:, None] + WIN_COLS)
    r0 = np.array([0, 2 * NA_BLOCK_ROWS, rows - NA_BLOCK_ROWS])
    r = r0[:, None] + np.arange(NA_BLOCK_ROWS)[None, :]
    key_row = np.clip(r0 - WIN_ROWS // 2, 0, rows - NA_KEY_ROWS)[:, None, None] + np.arange(NA_KEY_ROWS)
    win_start = np.clip(r - WIN_ROWS // 2, 0, rows - WIN_ROWS)[:, :, None]
    row_mask = (key_row >= win_start) & (key_row < win_start + WIN_ROWS)
    dr = np.clip(key_row - r[:, :, None] + (WIN_ROWS - 1), 0, 2 * WIN_ROWS - 2)
    pick_c = jnp.asarray(dc[None] == np.arange(2 * WIN_COLS - 1)[:, None, None], dtype=F32)
    pick_r = jnp.asarray(dr[..., None] == np.arange(2 * WIN_ROWS - 1), dtype=F32)
    b = jnp.einsum('crkd,hde,eqx->chrqkx', pick_r, rpb.astype(F32), pick_c, precision=HI)
    mask = row_mask[:, None, :, None, :, None] & col_mask[None, None, None, :, None, :]
    b = jnp.where(mask, b, NEG_INF)
    return b.reshape(NA_CLASSES, NA_HEADS, NA_BLOCK_ROWS * GRID_W, NA_KEY_ROWS * GRID_W)


def _rope_tables(seq):
    pos = jnp.arange(seq, dtype=jnp.int32)
    rows = (pos // GRID_W).astype(F32)
    cols = (pos % GRID_W).astype(F32)
    nf = HEAD_DIM // 4
    inv = ROPE_THETA ** (-jnp.arange(nf, dtype=F32) / nf)
    ar = rows[:, None] * inv[None, :]
    ac = cols[:, None] * inv[None, :]
    cos = jnp.concatenate([jnp.cos(ar), jnp.cos(ar), jnp.cos(ac), jnp.cos(ac)], axis=-1)
    sin = jnp.concatenate([-jnp.sin(ar), jnp.sin(ar), -jnp.sin(ac), jnp.sin(ac)], axis=-1)
    return jnp.tile(cos, (1, NA_HEADS)), jnp.tile(sin, (1, NA_HEADS))


def _postmix_kernel(yh_ref, yn_ref, gate_ref, x_ref, g1_ref, sh2_ref, sc2_ref, n2_ref, whyo_ref, wnao_ref,
                    wout_ref, wq_ref, x1_ref, h2_ref, pq_ref):
    a = _dot(yh_ref[...].astype(BF16), whyo_ref[...])
    b = _dot(yn_ref[...], wnao_ref[...])
    gate = gate_ref[...].astype(F32)
    merged = gate[:, :D_MODEL] * a + gate[:, D_MODEL:] * b
    x1 = x_ref[...] + g1_ref[...] * _dot(merged.astype(BF16), wout_ref[...])
    x1_ref[...] = x1
    h2 = _modulated_norm(x1, n2_ref[...], sh2_ref[...], sc2_ref[...]).astype(BF16)
    h2_ref[...] = h2
    for hp in range(2 * PEER_HEADS):
        pq_ref[hp] = _dot(h2, wq_ref[:, hp * PEER_D_HALF:(hp + 1) * PEER_D_HALF]).astype(BF16)


def _postmix(yh, yn, gate, x2, g1, sh2, sc2, n2g, whyo, wnao, wout, wq, seq):
    T = x2.shape[0]
    tiles_per_batch = seq // TOK_TILE
    row = lambda i: (i, 0)
    per_batch = lambda i: (i // tiles_per_batch, 0, 0)
    const = lambda i: (0, 0)
    nq = wq.shape[1]
    return pl.pallas_call(
        _postmix_kernel,
        grid=(T // TOK_TILE,),
        in_specs=[
            pl.BlockSpec((TOK_TILE, D_HY), row),
            pl.BlockSpec((TOK_TILE, D_NA), row),
            pl.BlockSpec((TOK_TILE, 2 * D_MODEL), row),
            pl.BlockSpec((TOK_TILE, D_MODEL), row),
            pl.BlockSpec((None, 1, D_MODEL), per_batch),
            pl.BlockSpec((None, 1, D_MODEL), per_batch),
            pl.BlockSpec((None, 1, D_MODEL), per_batch),
            pl.BlockSpec((1, D_MODEL), const),
            pl.BlockSpec((D_HY, D_MODEL), const),
            pl.BlockSpec((D_NA, D_MODEL), const),
            pl.BlockSpec((D_MODEL, D_MODEL), const),
            pl.BlockSpec((D_MODEL, nq), const),
        ],
        out_specs=[
            pl.BlockSpec((TOK_TILE, D_MODEL), row),
            pl.BlockSpec((TOK_TILE, D_MODEL), row),
            pl.BlockSpec((nq // PEER_D_HALF, TOK_TILE, PEER_D_HALF), lambda i: (0, i, 0)),
        ],
        out_shape=[
            jax.ShapeDtypeStruct((T, D_MODEL), F32),
            jax.ShapeDtypeStruct((T, D_MODEL), BF16),
            jax.ShapeDtypeStruct((nq // PEER_D_HALF, T, PEER_D_HALF), BF16),
        ],
        compiler_params=_cparams(("parallel",)),
        name="postmix",
    )(yh, yn, gate, x2, g1, sh2, sc2, n2g, whyo, wnao, wout, wq)


SUBLANES = 8
ROUTE_TILE = 256


def _argmax_tree(vals, idxs):
    while len(vals) > 1:
        nv, ni = [], []
        for j in range(0, len(vals) - 1, 2):
            right = vals[j + 1] > vals[j]
            nv.append(jnp.where(right, vals[j + 1], vals[j]))
            ni.append(jnp.where(right, idxs[j + 1], idxs[j]))
        if len(vals) % 2:
            nv.append(vals[-1])
            ni.append(idxs[-1])
        vals, idxs = nv, ni
    return vals[0], idxs[0]


def _route_kernel(pq_ref, keys_ref, a_ref, b_ref, g_ref, tv_ref, ti_ref):
    rt = pq_ref.shape[1]
    n_grp = PEER_N_KEYS // SUBLANES
    sub = lax.broadcasted_iota(jnp.int32, (SUBLANES, rt), 0)
    key_idx = [sub + SUBLANES * g for g in range(n_grp)]

    def first_level(hp, carry):
        s = _dot_nt(keys_ref[hp], pq_ref[hp]).reshape(n_grp, SUBLANES, rt)
        slabs = [s[g] for g in range(n_grp)]
        h, p = hp // 2, hp % 2
        for j in range(PEER_TOPK):
            v, i = _argmax_tree(slabs, key_idx)
            for shift in (4, 2, 1):
                v2, i2 = pltpu.roll(v, shift, 0), pltpu.roll(i, shift, 0)
                take = (v2 > v) | ((v2 == v) & (i2 < i))
                v, i = jnp.where(take, v2, v), jnp.where(take, i2, i)
            tv_ref[p, j, pl.ds(h, 1), :] = v[0:1]
            ti_ref[p, j, pl.ds(h, 1), :] = i[0:1]
            slabs = [jnp.where(key_idx[g] == i, -jnp.inf, slabs[g]) for g in range(n_grp)]
        return carry

    lax.fori_loop(0, 2 * PEER_HEADS, first_level, 0, unroll=2)

    pairs = [(a, b) for a in range(PEER_TOPK) for b in range(PEER_TOPK) if (a + 1) * (b + 1) <= PEER_TOPK]
    cand = [tv_ref[0, a] + tv_ref[1, b] for a, b in pairs]
    pos = [jnp.full((PEER_HEADS, rt), a * PEER_TOPK + b, jnp.int32) for a, b in pairs]
    best, sel_a, sel_b = [], [], []
    for j in range(PEER_TOPK):
        v, w = _argmax_tree(cand, pos)
        best.append(v)
        ia = jnp.zeros((PEER_HEADS, rt), jnp.int32)
        ib = jnp.zeros((PEER_HEADS, rt), jnp.int32)
        for n, (a, b) in enumerate(pairs):
            hit = w == (a * PEER_TOPK + b)
            cand[n] = jnp.where(hit, -jnp.inf, cand[n])
            ia = jnp.where(hit, ti_ref[0, a], ia)
            ib = jnp.where(hit, ti_ref[1, b], ib)
        sel_a.append(ia)
        sel_b.append(ib)
    ex = [jnp.exp(v - best[0]) for v in best]
    denom = functools.reduce(lambda x, y: x + y, ex)
    gates = [e / denom for e in ex]
    a_ref[...] = jnp.concatenate(sel_a, axis=0).T
    b_ref[...] = jnp.concatenate(sel_b, axis=0).T
    g_ref[...] = jnp.concatenate(gates, axis=0).T


def _route(pq, keys_bf):
    n_hp, T, _ = pq.shape
    n_sel = PEER_HEADS * PEER_TOPK
    row = lambda i: (i, 0)
    return pl.pallas_call(
        _route_kernel,
        grid=(T // ROUTE_TILE,),
        in_specs=[
            pl.BlockSpec((n_hp, ROUTE_TILE, PEER_D_HALF), lambda i: (0, i, 0)),
            pl.BlockSpec((n_hp, PEER_N_KEYS, PEER_D_HALF), lambda i: (0, 0, 0)),
        ],
        out_specs=[pl.BlockSpec((ROUTE_TILE, n_sel), row)] * 3,
        out_shape=[
            jax.ShapeDtypeStruct((T, n_sel), jnp.int32),
            jax.ShapeDtypeStruct((T, n_sel), jnp.int32),
            jax.ShapeDtypeStruct((T, n_sel), F32),
        ],
        scratch_shapes=[
            pltpu.VMEM((2, PEER_TOPK, PEER_HEADS, ROUTE_TILE), F32),
            pltpu.VMEM((2, PEER_TOPK, PEER_HEADS, ROUTE_TILE), jnp.int32),
        ],
        compiler_params=_cparams(("parallel",)),
        name="route",
    )(pq, keys_bf)


WBUILD_TILE = 128


WBUILD_PITCH = 136
WBUILD_UNROLL = 32


def _wbuild_kernel(a_ref, b_ref, g_ref, w_ref, s_ref):
    n = PEER_N_KEYS
    tb = a_ref.shape[0]
    key = lax.broadcasted_iota(jnp.int32, (n, a_ref.shape[1]), 0)

    def body(t, carry):
        a = a_ref[pl.ds(t, 1), :]
        b = b_ref[pl.ds(t, 1), :]
        g = g_ref[pl.ds(t, 1), :]
        ga = jnp.where(a == key, g, 0.0).astype(BF16)
        ob = jnp.where(b == key, 1.0, 0.0).astype(BF16)
        s_ref[pl.ds(pl.multiple_of(t * WBUILD_PITCH, SUBLANES), n), :] = _dot_nt(ga, ob)
        return carry

    lax.fori_loop(0, tb, body, 0, unroll=WBUILD_UNROLL)

    def emit(i1, carry):
        w_ref[i1] = s_ref[pl.ds(i1, tb, stride=WBUILD_PITCH), :].astype(BF16)
        return carry

    lax.fori_loop(0, n, emit, 0, unroll=WBUILD_UNROLL)


def _wbuild(a, b, g):
    T, n_sel = a.shape
    row = lambda i: (i, 0)
    return pl.pallas_call(
        _wbuild_kernel,
        grid=(T // WBUILD_TILE,),
        in_specs=[pl.BlockSpec((WBUILD_TILE, n_sel), row)] * 3,
        out_specs=pl.BlockSpec((None, PEER_N_KEYS, WBUILD_TILE, PEER_N_KEYS), lambda i: (i, 0, 0, 0)),
        out_shape=jax.ShapeDtypeStruct((T // WBUILD_TILE, PEER_N_KEYS, WBUILD_TILE, PEER_N_KEYS), BF16),
        scratch_shapes=[pltpu.VMEM((WBUILD_TILE * WBUILD_PITCH, PEER_N_KEYS), F32)],
        compiler_params=_cparams(("parallel",)),
        name="wbuild",
    )(a, b, g)


def _peer_dense_kernel(h2_ref, w_ref, u_ref, v_ref, x1_ref, g2_ref, o_ref, acc_ref):
    e = pl.program_id(1)

    @pl.when(e == 0)
    def _():
        acc_ref[...] = jnp.zeros_like(acc_ref)

    a = _dot_nt(h2_ref[...], u_ref[...])
    act = 0.5 * a * (1.0 + lax.erf(a * (2.0 ** -0.5)))
    w = jnp.concatenate(
        [jnp.concatenate([w_ref[tb, i] for i in range(w_ref.shape[1])], axis=1) for tb in range(w_ref.shape[0])],
        axis=0)
    acc_ref[...] += _dot((w.astype(F32) * act).astype(BF16), v_ref[...])

    @pl.when(e == pl.num_programs(1) - 1)
    def _():
        o_ref[...] = x1_ref[...] + g2_ref[...] * acc_ref[...]


def _peer_dense(h2, w, u_bf, v_bf, x1, g2, seq):
    T = h2.shape[0]
    tiles_per_batch = seq // PEER_TOK_TILE
    n_exp = u_bf.shape[0]
    return pl.pallas_call(
        _peer_dense_kernel,
        grid=(T // PEER_TOK_TILE, n_exp // PEER_EXP_TILE),
        in_specs=[
            pl.BlockSpec((PEER_TOK_TILE, D_MODEL), lambda i, e: (i, 0)),
            pl.BlockSpec((PEER_TOK_TILE // WBUILD_TILE, PEER_EXP_TILE // PEER_N_KEYS, WBUILD_TILE, PEER_N_KEYS),
                         lambda i, e: (i, e, 0, 0)),
            pl.BlockSpec((PEER_EXP_TILE, D_MODEL), lambda i, e: (e, 0)),
            pl.BlockSpec((PEER_EXP_TILE, D_MODEL), lambda i, e: (e, 0)),
            pl.BlockSpec((PEER_TOK_TILE, D_MODEL), lambda i, e: (i, 0)),
            pl.BlockSpec((None, 1, D_MODEL), lambda i, e: (i // tiles_per_batch, 0, 0)),
        ],
        out_specs=pl.BlockSpec((PEER_TOK_TILE, D_MODEL), lambda i, e: (i, 0)),
        out_shape=jax.ShapeDtypeStruct((T, D_MODEL), F32),
        scratch_shapes=[pltpu.VMEM((PEER_TOK_TILE, D_MODEL), F32)],
        compiler_params=_cparams(("parallel", "arbitrary")),
        name="peer_dense",
    )(h2, w, u_bf, v_bf, x1, g2)


HI = lax.Precision.HIGHEST


HY_LANES = 128


def _shortconv_kernel(z_ref, w_ref, b_ref, o_ref):
    z = z_ref[...]
    n = z.shape[0]
    row = lax.broadcasted_iota(jnp.int32, z.shape, 0)
    prev = jnp.where(row == 0, 0.0, pltpu.roll(z, 1, 0))
    nxt = jnp.where(row == n - 1, 0.0, pltpu.roll(z, n - 1, 0))
    o_ref[...] = w_ref[0:1] * prev + w_ref[1:2] * z + w_ref[2:3] * nxt + b_ref[...]


def _shortconv(zh, w, b):
    B, L, C = zh.shape
    return pl.pallas_call(
        _shortconv_kernel,
        grid=(B, C // HY_LANES),
        in_specs=[
            pl.BlockSpec((None, L, HY_LANES), lambda i, j: (i, 0, j)),
            pl.BlockSpec((3, HY_LANES), lambda i, j: (0, j)),
            pl.BlockSpec((1, HY_LANES), lambda i, j: (0, j)),
        ],
        out_specs=pl.BlockSpec((None, L, HY_LANES), lambda i, j: (i, 0, j)),
        out_shape=jax.ShapeDtypeStruct((B, L, C), F32),
        compiler_params=_cparams(("parallel", "parallel")),
        name="shortconv",
    )(zh, w, b)


FFT_R = 128
FFT_N = FFT_R * FFT_R
FFT_PITCH = 136
FFT_K1_CHUNK = 16
FFT_UNROLL = 8


def _fft_tables():
    r = jnp.arange(FFT_R, dtype=jnp.int32)
    n2, k1, n1 = r[:, None, None], r[None, :, None], r[None, None, :]
    ang = (2.0 * math.pi / FFT_N) * ((k1 * (FFT_R * n1 + n2)) % FFT_N).astype(F32)
    c, s = jnp.cos(ang), jnp.sin(ang)
    g_full = jnp.concatenate([c, -s], axis=1)
    h_half = jnp.concatenate([jnp.swapaxes(c, 1, 2), -jnp.swapaxes(s, 1, 2)], axis=2)[:, :FFT_R // 2] / FFT_N
    ang2 = (2.0 * math.pi / FFT_R) * ((r[:, None] * r[None, :]) % FFT_R).astype(F32)
    c2, s2 = jnp.cos(ang2), jnp.sin(ang2)
    f_fwd = jnp.concatenate([jnp.concatenate([c2, s2], 1), jnp.concatenate([-s2, c2], 1)], 0)
    f_inv = jnp.concatenate([jnp.concatenate([c2, -s2], 1), jnp.concatenate([s2, c2], 1)], 0)
    return g_full.astype(BF16), h_half.astype(BF16), f_fwd.astype(BF16), f_inv.astype(BF16)


def _fft_stage1(load_slab, g_ref, bre_ref, bim_ref):
    def body(n2, carry):
        out = _dot(g_ref[n2], load_slab(n2).astype(BF16))
        bre_ref[pl.ds(n2, FFT_R, stride=FFT_PITCH), :] = out[:FFT_R]
        bim_ref[pl.ds(n2, FFT_R, stride=FFT_PITCH), :] = out[FFT_R:]
        return carry

    lax.fori_loop(0, FFT_R, body, 0, unroll=FFT_UNROLL)


def _k1_rows(k1):
    return pl.ds(pl.multiple_of(k1 * FFT_PITCH, SUBLANES), FFT_R)


def _load_k1_pair(bre_ref, bim_ref, k1):
    return jnp.concatenate(
        [jnp.concatenate([bre_ref[_k1_rows(k1 + d), :], bim_ref[_k1_rows(k1 + d), :]], axis=0) for d in range(2)],
        axis=1)


FILT_ROWS = 512
FILT_FEAT = 128


def _filt_kernel(feat_ref, delta_ref, w1_ref, b1_ref, w2_ref, b2_ref, w3_ref, b3_ref, w4_ref, freq_ref,
                 h_ref, sum_ref):
    @pl.when(pl.program_id(0) == 0)
    def _():
        sum_ref[...] = jnp.zeros_like(sum_ref)

    feat = feat_ref[...]
    freq = freq_ref[...]
    dot = lambda a, b: jnp.dot(a, b, precision=HI, preferred_element_type=F32)
    h = jnp.sin(freq * (dot(feat, w1_ref[...]) + b1_ref[...]))
    h = jnp.sin(freq * (dot(h, w2_ref[...]) + b2_ref[...]))
    h = jnp.sin(freq * (dot(h, w3_ref[...]) + b3_ref[...]))
    h = dot(h, w4_ref[...])
    n_feat = 1 + 2 * HY_BANDS
    t = feat[:, 0:1]
    forward = feat[:, n_feat:n_feat + 1] > 0.5
    keep = feat[:, n_feat + 1:n_feat + 2]
    decay = jnp.exp(-t * delta_ref[...])
    outs = []
    for o in range(HY_ORDER):
        base = o * 2 * D_HY
        outs.append(jnp.where(forward, h[:, base:base + D_HY], h[:, base + D_HY:base + 2 * D_HY]) * decay)
    out = jnp.concatenate(outs, axis=1)
    sum_ref[...] += jnp.sum(jnp.abs(out), axis=0, keepdims=True)
    h_ref[...] = out * keep


def _filt_features(L):
    r = np.arange(FFT_N)
    n = FFT_R * (r % FFT_R) + r // FFT_R
    j = np.where(n < L, n, np.where(n > L, 2 * L - n, 0))
    jj = jnp.asarray(j, dtype=F32)[:, None]
    t = jj / (L - 1.0)
    w = (2.0 * math.pi / L) * jj
    bands = jnp.linspace(1e-4, HY_BANDS - 1.0, HY_BANDS, dtype=F32)[None, :]
    flags = jnp.asarray(np.stack([n < L, n != L], axis=1), dtype=F32)
    feats = jnp.concatenate([t, jnp.cos(bands * w), -jnp.sin(bands * w), flags], axis=-1)
    return jnp.pad(feats, ((0, 0), (0, FILT_FEAT - feats.shape[1])))


def _filters(p, L):
    feats = _filt_features(L)
    deltas = jnp.abs(jnp.linspace(math.log(HY_DECAY_TARGET) / HY_SLOW_PCT,
                                  math.log(HY_DECAY_TARGET) / HY_FAST_PCT, D_HY, dtype=F32))[None, :]
    w1 = jnp.pad(p['hy_f1_w'], ((0, FILT_FEAT - p['hy_f1_w'].shape[0]), (0, 0)))
    n_out = HY_ORDER * D_HY
    const = lambda i: (0, 0)
    full = lambda a: pl.BlockSpec(a.shape, const)
    args = [deltas, w1, p['hy_f1_b'][None], p['hy_f2_w'], p['hy_f2_b'][None], p['hy_f3_w'], p['hy_f3_b'][None],
            p['hy_f4_w'], p['hy_sin_freq'][None]]
    return pl.pallas_call(
        _filt_kernel,
        grid=(FFT_N // FILT_ROWS,),
        in_specs=[pl.BlockSpec((FILT_ROWS, FILT_FEAT), lambda i: (i, 0))] + [full(a) for a in args],
        out_specs=[pl.BlockSpec((FILT_ROWS, n_out), lambda i: (i, 0)), pl.BlockSpec((1, n_out), const)],
        out_shape=[jax.ShapeDtypeStruct((FFT_N, n_out), F32), jax.ShapeDtypeStruct((1, n_out), F32)],
        compiler_params=_cparams(("arbitrary",)),
        name="hyena_filter",
    )(feats, *args)


def _fspec_kernel(h_ref, sum_ref, g_ref, f_ref, o_ref, bre_ref, bim_ref):
    c = pl.program_id(1)

    @pl.when(c == 0)
    def _():
        _fft_stage1(lambda n2: h_ref[pl.ds(pl.multiple_of(n2 * FFT_R, FFT_R), FFT_R), :], g_ref, bre_ref, bim_ref)

    inv = 1.0 / sum_ref[...]

    lanes = h_ref.shape[1]

    def slab_pair(j, carry):
        x = _dot(f_ref[...], _load_k1_pair(bre_ref, bim_ref, c * FFT_K1_CHUNK + 2 * j).astype(BF16))
        for d in range(2):
            rows = pl.ds(pl.multiple_of((2 * j + d) * FFT_R, FFT_R), FFT_R)
            o_ref[0, rows, :] = x[:FFT_R, d * lanes:(d + 1) * lanes] * inv
            o_ref[1, rows, :] = x[FFT_R:, d * lanes:(d + 1) * lanes] * inv
        return carry

    lax.fori_loop(0, FFT_K1_CHUNK // 2, slab_pair, 0, unroll=FFT_UNROLL // 2)


def _filter_spectrum(h, hsum, g_full, f_fwd):
    n_ch = h.shape[1]
    single = pl.Buffered(1)
    return pl.pallas_call(
        _fspec_kernel,
        grid=(n_ch // HY_LANES, FFT_R // FFT_K1_CHUNK),
        in_specs=[
            pl.BlockSpec((FFT_N, HY_LANES), lambda j, c: (0, j), pipeline_mode=single),
            pl.BlockSpec((1, HY_LANES), lambda j, c: (0, j)),
            pl.BlockSpec(g_full.shape, lambda j, c: (0, 0, 0), pipeline_mode=single),
            pl.BlockSpec(f_fwd.shape, lambda j, c: (0, 0)),
        ],
        out_specs=pl.BlockSpec((2, FFT_K1_CHUNK * FFT_R, HY_LANES), lambda j, c: (0, c, j)),
        out_shape=jax.ShapeDtypeStruct((2, FFT_N, n_ch), F32),
        scratch_shapes=[pltpu.VMEM((FFT_R * FFT_PITCH, HY_LANES), F32)] * 2,
        compiler_params=_cparams(("parallel", "arbitrary")),
        name="hyena_filter_fft",
    )(h, hsum, g_full, f_fwd)


def _hyconv_kernel(u_ref, gate_ref, kf_ref, skip_ref, g_ref, h_ref, f_ref, fi_ref, o_ref, bre_ref, bim_ref):
    c = pl.program_id(2)
    half = FFT_R // 2

    def rows_of(n2):
        return pl.ds(pl.multiple_of(n2 * half, half), half)

    @pl.when(c == 0)
    def _():
        _fft_stage1(lambda n2: u_ref[rows_of(n2), :], g_ref, bre_ref, bim_ref)

    lanes = u_ref.shape[1]

    def slab_pair(j, carry):
        k1 = c * FFT_K1_CHUNK + 2 * j
        x = _dot(f_ref[...], _load_k1_pair(bre_ref, bim_ref, k1).astype(BF16))
        xr, xi = x[:FFT_R], x[FFT_R:]
        kr, ki = [jnp.concatenate([kf_ref[part, pl.ds(pl.multiple_of((2 * j + d) * FFT_R, FFT_R), FFT_R), :]
                                   for d in range(2)], axis=1) for part in range(2)]
        y = jnp.concatenate([xr * kr - xi * ki, xr * ki + xi * kr], axis=0)
        z = _dot(fi_ref[...], y.astype(BF16))
        for d in range(2):
            bre_ref[_k1_rows(k1 + d), :] = z[:FFT_R, d * lanes:(d + 1) * lanes]
            bim_ref[_k1_rows(k1 + d), :] = z[FFT_R:, d * lanes:(d + 1) * lanes]
        return carry

    lax.fori_loop(0, FFT_K1_CHUNK // 2, slab_pair, 0, unroll=FFT_UNROLL // 2)

    @pl.when(c == pl.num_programs(2) - 1)
    def _():
        def body(n2, carry):
            z = jnp.concatenate([bre_ref[pl.ds(n2, FFT_R, stride=FFT_PITCH), :],
                                 bim_ref[pl.ds(n2, FFT_R, stride=FFT_PITCH), :]], axis=0)
            conv = _dot(h_ref[n2], z.astype(BF16))
            rows = rows_of(n2)
            o_ref[rows, :] = gate_ref[rows, :] * (conv + skip_ref[...] * u_ref[rows, :])
            return carry

        lax.fori_loop(0, FFT_R, body, 0, unroll=FFT_UNROLL)


def _hyconv(u, u_col, gate, gate_col, kf, kf_col, skip, tables):
    g_half, h_half, f_fwd, f_inv = tables
    B, L, _ = u.shape
    single = pl.Buffered(1)
    return pl.pallas_call(
        _hyconv_kernel,
        grid=(B, D_HY // HY_LANES, FFT_R // FFT_K1_CHUNK),
        in_specs=[
            pl.BlockSpec((None, L, HY_LANES), lambda b, j, c: (b, 0, u_col + j), pipeline_mode=single),
            pl.BlockSpec((None, L, HY_LANES), lambda b, j, c: (b, 0, gate_col + j), pipeline_mode=single),
            pl.BlockSpec((2, FFT_K1_CHUNK * FFT_R, HY_LANES), lambda b, j, c: (0, c, kf_col + j)),
            pl.BlockSpec((1, HY_LANES), lambda b, j, c: (0, j)),
            pl.BlockSpec(g_half.shape, lambda b, j, c: (0, 0, 0), pipeline_mode=single),
            pl.BlockSpec(h_half.shape, lambda b, j, c: (0, 0, 0), pipeline_mode=single),
            pl.BlockSpec(f_fwd.shape, lambda b, j, c: (0, 0)),
            pl.BlockSpec(f_inv.shape, lambda b, j, c: (0, 0)),
        ],
        out_specs=pl.BlockSpec((None, L, HY_LANES), lambda b, j, c: (b, 0, j)),
        out_shape=jax.ShapeDtypeStruct((B, L, D_HY), F32),
        scratch_shapes=[pltpu.VMEM((FFT_R * FFT_PITCH, HY_LANES), F32)] * 2,
        compiler_params=_cparams(("parallel", "parallel", "arbitrary")),
        name="hyena_conv",
    )(u, gate, kf, skip, g_half, h_half, f_fwd, f_inv)


def _slab_major(a, n1):
    B, L, C = a.shape
    return a.reshape(B, n1, L // n1, C).transpose(0, 2, 1, 3).reshape(B, L, C)


def _hyena(zh, p):
    B, L, _ = zh.shape
    assert 2 * L == FFT_N
    g_full, h_half, f_fwd, f_inv = _fft_tables()
    tables = (g_full[:, :, :FFT_R // 2], h_half, f_fwd, f_inv)
    filt, filt_sum = _filters(p, L)
    both_dirs = filt_sum[:, :]
    kf = _filter_spectrum(filt, both_dirs, g_full, f_fwd)
    zc = _slab_major(_shortconv(zh, p['hy_conv_w'], p['hy_conv_b'][None]), FFT_R // 2)
    lanes = D_HY // HY_LANES
    y = _hyconv(zc, 0, zc, lanes, kf, 0, p['hy_skip'][0:1], tables)
    y = _hyconv(y, 0, zc, 2 * lanes, kf, lanes, p['hy_skip'][1:2], tables)
    return _slab_major(y, L // (FFT_R // 2))


def kernel(x, c, ctx, c_ctx, norm1_g, norm2_g, w_ada, b_ada, w_in, b_in, hy_conv_w, hy_conv_b, hy_f1_w, hy_f1_b, hy_f2_w, hy_f2_b, hy_f3_w, hy_f3_b, hy_f4_w, hy_sin_freq, hy_skip, q_norm_g, k_norm_g, na_rpb, w_hy_out, w_na_out, w_out, peer_w_q, peer_keys, peer_u, peer_v):
    assert w_in.shape[0] == 1, "single-layer block"
    B, S, D = x.shape
    C = ctx.shape[1]
    T = B * S

    m_lat = (jax.nn.silu(c) @ w_ada[0] + b_ada[0]).reshape(B, N_MOD, 1, D)
    m_ctx = (jax.nn.silu(c_ctx) @ w_ada[0] + b_ada[0]).reshape(N_MOD, 1, D)
    sh1, sc1, g1, sh2, sc2, g2 = [m_lat[:, i] for i in range(N_MOD)]

    w_in_bf = w_in[0].astype(BF16)
    b_in2 = b_in[0][None, :]
    gain1 = norm1_g[0][None, :]
    qg = jnp.tile(q_norm_g[0], NA_HEADS)[None, :]
    kg = jnp.tile(k_norm_g[0], NA_HEADS)[None, :]
    seg = jnp.asarray(np.kron(np.eye(NA_HEADS), np.full((HEAD_DIM, HEAD_DIM), 1.0 / HEAD_DIM)), dtype=BF16)
    cos, sin = _rope_tables(S)

    x2 = x.reshape(T, D)
    zh, q, k, v, gate = _inproj(x2, sh1, sc1, gain1, w_in_bf, b_in2, cos, sin, qg, kg, seg, S)

    k_ctx, v_ctx = _ctxkv(ctx.reshape(B * C, D), m_ctx[0], m_ctx[1], gain1,
                          w_in_bf[:, COL_K:COL_V], b_in2[:, COL_K:COL_V],
                          w_in_bf[:, COL_V:COL_G_HY], b_in2[:, COL_V:COL_G_HY], kg, seg)

    p = {'hy_conv_w': hy_conv_w[0], 'hy_conv_b': hy_conv_b[0], 'hy_f1_w': hy_f1_w[0], 'hy_f1_b': hy_f1_b[0],
         'hy_f2_w': hy_f2_w[0], 'hy_f2_b': hy_f2_b[0], 'hy_f3_w': hy_f3_w[0], 'hy_f3_b': hy_f3_b[0],
         'hy_f4_w': hy_f4_w[0], 'hy_sin_freq': hy_sin_freq[0], 'hy_skip': hy_skip[0]}
    y_hy = _hyena(zh.reshape(B, S, 3 * D_HY), p).reshape(T, D_HY)

    y_na = _natten(q.reshape(B, S, D_NA), k.reshape(B, S, D_NA), v.reshape(B, S, D_NA),
                   k_ctx.reshape(B, C, D_NA), v_ctx.reshape(B, C, D_NA), _natten_bias(na_rpb[0], S // GRID_W)).reshape(T, D_NA)

    x1, h2, pq = _postmix(y_hy, y_na, gate, x2, g1, sh2, sc2, norm2_g[0][None, :],
                          w_hy_out[0].astype(BF16), w_na_out[0].astype(BF16), w_out[0].astype(BF16),
                          peer_w_q[0].astype(BF16), S)

    keys_bf = peer_keys[0].astype(BF16).reshape(2 * PEER_HEADS, PEER_N_KEYS, PEER_D_HALF)
    sel_a, sel_b, sel_g = _route(pq, keys_bf)
    w = _wbuild(sel_a, sel_b, sel_g)
    out = _peer_dense(h2, w, peer_u[0].astype(BF16), peer_v[0].astype(BF16), x1, g2, S)
    return out.reshape(B, S, D)
```

```python
import functools
import math

import numpy as np
import jax
import jax.numpy as jnp
from jax import lax
from jax.experimental import pallas as pl
from jax.experimental.pallas import tpu as pltpu

F32 = jnp.float32
BF16 = jnp.bfloat16

D_MODEL = 1024
GRID_W = 64
EPS = 1e-6
N_MOD = 6

D_HY = 512
HY_ORDER = 2
HY_BANDS = 16
HY_DECAY_TARGET = 1e-2
HY_FAST_PCT = 0.3
HY_SLOW_PCT = 1.5

NA_HEADS = 8
HEAD_DIM = 64
D_NA = NA_HEADS * HEAD_DIM
WIN_ROWS = 8
WIN_COLS = 16
ROPE_THETA = 10000.0
NEG_INF = -1e30

PEER_HEADS = 8
PEER_N_KEYS = 128
PEER_N_EXPERTS = PEER_N_KEYS * PEER_N_KEYS
PEER_TOPK = 16
PEER_D_KEY = 256
PEER_D_HALF = PEER_D_KEY // 2

COL_HY = 0
COL_Q = COL_HY + 3 * D_HY
COL_K = COL_Q + D_NA
COL_V = COL_K + D_NA
COL_G_HY = COL_V + D_NA
N_PROJ = COL_G_HY + 2 * D_MODEL

VMEM_LIMIT = 56 * 1024 * 1024
TOK_TILE = 256
PEER_TOK_TILE = 512
PEER_EXP_TILE = 2048


def _cparams(sem):
    return pltpu.CompilerParams(dimension_semantics=sem, vmem_limit_bytes=VMEM_LIMIT)


def _dot(a, b):
    return jnp.dot(a, b, preferred_element_type=F32)


def _dot_nt(a, b):
    return lax.dot_general(a, b, (((1,), (1,)), ((), ())), preferred_element_type=F32)


def _modulated_norm(x, gain, shift, scale):
    ms = jnp.mean(x * x, axis=-1, keepdims=True)
    return (x * lax.rsqrt(ms + EPS) * gain) * (1.0 + scale) + shift


def _head_rms(z, seg, gain):
    zz = z * z
    hi = zz.astype(BF16)
    lo = (zz - hi.astype(F32)).astype(BF16)
    ms = _dot(hi, seg) + _dot(lo, seg)
    return z * lax.rsqrt(ms + EPS) * gain


def _rope(z, cos, sin_signed):
    n = z.shape[-1]
    lane = lax.broadcasted_iota(jnp.int32, z.shape, 1)
    first = (lane // (HEAD_DIM // 4)) % 2 == 0
    partner = jnp.where(first, pltpu.roll(z, n - HEAD_DIM // 4, 1), pltpu.roll(z, HEAD_DIM // 4, 1))
    return z * cos + partner * sin_signed


def _inproj_kernel(x_ref, shift_ref, scale_ref, g_ref, w_ref, b_ref, cos_ref, sin_ref, qg_ref, kg_ref,
                   seg_ref, zh_ref, q_ref, k_ref, v_ref, gate_ref):
    h = _modulated_norm(x_ref[...], g_ref[...], shift_ref[...], scale_ref[...]).astype(BF16)

    def proj(lo, hi):
        return _dot(h, w_ref[:, lo:hi]) + b_ref[:, lo:hi]

    zh_ref[...] = proj(COL_HY, COL_Q)
    cos, sin = cos_ref[...], sin_ref[...]
    seg = seg_ref[...]
    q = _rope(_head_rms(proj(COL_Q, COL_K), seg, qg_ref[...]), cos, sin)
    q_ref[...] = q.astype(BF16)
    k = _rope(_head_rms(proj(COL_K, COL_V), seg, kg_ref[...]), cos, sin)
    k_ref[...] = k.astype(BF16)
    v_ref[...] = proj(COL_V, COL_G_HY).astype(BF16)
    gate_ref[...] = jax.nn.sigmoid(proj(COL_G_HY, N_PROJ)).astype(BF16)


def _inproj(x2, shift, scale, gain, w_bf, b_in, cos, sin, qg, kg, seg, seq):
    T = x2.shape[0]
    tiles_per_batch = seq // TOK_TILE
    row = lambda i: (i, 0)
    per_batch = lambda i: (i // tiles_per_batch, 0, 0)
    const = lambda i: (0, 0)
    pos = lambda i: (i % tiles_per_batch, 0)
    return pl.pallas_call(
        _inproj_kernel,
        grid=(T // TOK_TILE,),
        in_specs=[
            pl.BlockSpec((TOK_TILE, D_MODEL), row),
            pl.BlockSpec((None, 1, D_MODEL), per_batch),
            pl.BlockSpec((None, 1, D_MODEL), per_batch),
            pl.BlockSpec((1, D_MODEL), const),
            pl.BlockSpec((D_MODEL, N_PROJ), const),
            pl.BlockSpec((1, N_PROJ), const),
            pl.BlockSpec((TOK_TILE, D_NA), pos),
            pl.BlockSpec((TOK_TILE, D_NA), pos),
            pl.BlockSpec((1, D_NA), const),
            pl.BlockSpec((1, D_NA), const),
            pl.BlockSpec((D_NA, D_NA), const),
        ],
        out_specs=[
            pl.BlockSpec((TOK_TILE, 3 * D_HY), row),
            pl.BlockSpec((TOK_TILE, D_NA), row),
            pl.BlockSpec((TOK_TILE, D_NA), row),
            pl.BlockSpec((TOK_TILE, D_NA), row),
            pl.BlockSpec((TOK_TILE, 2 * D_MODEL), row),
        ],
        out_shape=[
            jax.ShapeDtypeStruct((T, 3 * D_HY), F32),
            jax.ShapeDtypeStruct((T, D_NA), BF16),
            jax.ShapeDtypeStruct((T, D_NA), BF16),
            jax.ShapeDtypeStruct((T, D_NA), BF16),
            jax.ShapeDtypeStruct((T, 2 * D_MODEL), BF16),
        ],
        compiler_params=_cparams(("parallel",)),
        name="inproj",
    )(x2, shift, scale, gain, w_bf, b_in, cos, sin, qg, kg, seg)


def _ctxkv_kernel(x_ref, shift_ref, scale_ref, g_ref, wk_ref, bk_ref, wv_ref, bv_ref, kg_ref, seg_ref,
                  k_ref, v_ref):
    h = _modulated_norm(x_ref[...], g_ref[...], shift_ref[...], scale_ref[...]).astype(BF16)
    k = _head_rms(_dot(h, wk_ref[...]) + bk_ref[...], seg_ref[...], kg_ref[...])
    k_ref[...] = k.astype(BF16)
    v_ref[...] = (_dot(h, wv_ref[...]) + bv_ref[...]).astype(BF16)


def _ctxkv(ctx2, shift, scale, gain, wk, bk, wv, bv, kg, seg):
    T = ctx2.shape[0]
    row = lambda i: (i, 0)
    const = lambda i: (0, 0)
    return pl.pallas_call(
        _ctxkv_kernel,
        grid=(T // TOK_TILE,),
        in_specs=[
            pl.BlockSpec((TOK_TILE, D_MODEL), row),
            pl.BlockSpec((1, D_MODEL), const),
            pl.BlockSpec((1, D_MODEL), const),
            pl.BlockSpec((1, D_MODEL), const),
            pl.BlockSpec((D_MODEL, D_NA), const),
            pl.BlockSpec((1, D_NA), const),
            pl.BlockSpec((D_MODEL, D_NA), const),
            pl.BlockSpec((1, D_NA), const),
            pl.BlockSpec((1, D_NA), const),
            pl.BlockSpec((D_NA, D_NA), const),
        ],
        out_specs=[pl.BlockSpec((TOK_TILE, D_NA), row), pl.BlockSpec((TOK_TILE, D_NA), row)],
        out_shape=[jax.ShapeDtypeStruct((T, D_NA), BF16), jax.ShapeDtypeStruct((T, D_NA), BF16)],
        compiler_params=_cparams(("parallel",)),
        name="ctxkv",
    )(ctx2, shift, scale, gain, wk, bk, wv, bv, kg, seg)


NA_BLOCK_ROWS = 4
NA_KEY_ROWS = WIN_ROWS + NA_BLOCK_ROWS
NA_CLASSES = 3


def _natten_kernel(q_ref, k_ref, v_ref, kc_ref, vc_ref, bias_ref, o_ref, *, rows):
    r0 = pl.program_id(1) * NA_BLOCK_ROWS
    key_start = jnp.clip(r0 - WIN_ROWS // 2, 0, rows - NA_KEY_ROWS)
    start = pl.multiple_of(key_start * GRID_W, GRID_W * NA_BLOCK_ROWS)
    n_q = NA_BLOCK_ROWS * GRID_W
    n_loc = NA_KEY_ROWS * GRID_W
    scale = jnp.asarray(HEAD_DIM ** -0.5, BF16)
    pair = 2 * HEAD_DIM
    lane = lax.broadcasted_iota(jnp.int32, (n_q, pair), 1)
    for p in range(NA_HEADS // 2):
        cols = slice(p * pair, (p + 1) * pair)
        qp = q_ref[:, cols] * scale
        kp = k_ref[pl.ds(start, n_loc), cols]
        vp = v_ref[pl.ds(start, n_loc), cols]
        kcp = kc_ref[:, cols]
        vcp = vc_ref[:, cols]
        outs = []
        for hh in range(2):
            qm = jnp.where((lane // HEAD_DIM) == hh, qp, jnp.zeros_like(qp))
            s_loc = _dot_nt(qm, kp) + bias_ref[2 * p + hh]
            s_ctx = _dot_nt(qm, kcp)
            m = jnp.maximum(jnp.max(s_loc, axis=-1, keepdims=True), jnp.max(s_ctx, axis=-1, keepdims=True))
            e_loc = jnp.exp(s_loc - m)
            e_ctx = jnp.exp(s_ctx - m)
            denom = jnp.sum(e_loc, axis=-1, keepdims=True) + jnp.sum(e_ctx, axis=-1, keepdims=True)
            o = _dot(e_loc.astype(BF16), vp) + _dot(e_ctx.astype(BF16), vcp)
            outs.append(o / denom)
        o_ref[:, cols] = jnp.where((lane // HEAD_DIM) == 0, outs[0], outs[1]).astype(o_ref.dtype)


def _natten(q, k, v, kc, vc, bias):
    B, S, _ = q.shape
    rows = S // GRID_W
    n_blk = rows // NA_BLOCK_ROWS
    C = kc.shape[1]
    n_q = NA_BLOCK_ROWS * GRID_W
    single = pl.Buffered(1)
    block_class = lambda b, i: (jnp.where(i == 0, 0, jnp.where(i == n_blk - 1, 2, 1)), 0, 0, 0)
    return pl.pallas_call(
        functools.partial(_natten_kernel, rows=rows),
        grid=(B, n_blk),
        in_specs=[
            pl.BlockSpec((None, n_q, D_NA), lambda b, i: (b, i, 0)),
            pl.BlockSpec((None, S, D_NA), lambda b, i: (b, 0, 0), pipeline_mode=single),
            pl.BlockSpec((None, S, D_NA), lambda b, i: (b, 0, 0), pipeline_mode=single),
            pl.BlockSpec((None, C, D_NA), lambda b, i: (b, 0, 0)),
            pl.BlockSpec((None, C, D_NA), lambda b, i: (b, 0, 0)),
            pl.BlockSpec((None,) + bias.shape[1:], block_class),
        ],
        out_specs=pl.BlockSpec((None, n_q, D_NA), lambda b, i: (b, i, 0)),
        out_shape=jax.ShapeDtypeStruct((B, S, D_NA), BF16),
        compiler_params=_cparams(("parallel", "arbitrary")),
        name="natten",
    )(q, k, v, kc, vc, bias)


def _natten_bias(rpb, rows):
    col = np.arange(GRID_W)
    dc = np.clip(col[None, :] - col[:, None], -(WIN_COLS - 1), WIN_COLS - 1) + (WIN_COLS - 1)
    col_start = np.clip(col - WIN_COLS // 2, 0, GRID_W - WIN_COLS)
    col_mask = (col[None, :] >= col_start[:, None]) & (col[None, :] < col_start[:, None] + WIN_COLS)
    r0 = np.array([0, 2 * NA_BLOCK_ROWS, rows - NA_BLOCK_ROWS])
    r = r0[:, None] + np.arange(NA_BLOCK_ROWS)[None, :]
    key_row = np.clip(r0 - WIN_ROWS // 2, 0, rows - NA_KEY_ROWS)[:, None, None] + np.arange(NA_KEY_ROWS)
    win_start = np.clip(r - WIN_ROWS // 2, 0, rows - WIN_ROWS)[:, :, None]
    row_mask = (key_row >= win_start) & (key_row < win_start + WIN_ROWS)
    dr = np.clip(key_row - r[:, :, None] + (WIN_ROWS - 1), 0, 2 * WIN_ROWS - 2)
    pick_c = jnp.asarray(dc[None] == np.arange(2 * WIN_COLS - 1)[:, None, None], dtype=F32)
    pick_r = jnp.asarray(dr[..., None] == np.arange(2 * WIN_ROWS - 1), dtype=F32)
    b = jnp.einsum('crkd,hde,eqx->chrqkx', pick_r, rpb.astype(F32), pick_c, precision=HI)
    mask = row_mask[:, None, :, None, :, None] & col_mask[None, None, None, :, None, :]
    b = jnp.where(mask, b, NEG_INF)
    return b.reshape(NA_CLASSES, NA_HEADS, NA_BLOCK_ROWS * GRID_W, NA_KEY_ROWS * GRID_W)


def _rope_tables(seq):
    pos = jnp.arange(seq, dtype=jnp.int32)
    rows = (pos // GRID_W).astype(F32)
    cols = (pos % GRID_W).astype(F32)
    nf = HEAD_DIM // 4
    inv = ROPE_THETA ** (-jnp.arange(nf, dtype=F32) / nf)
    ar = rows[:, None] * inv[None, :]
    ac = cols[:, None] * inv[None, :]
    cos = jnp.concatenate([jnp.cos(ar), jnp.cos(ar), jnp.cos(ac), jnp.cos(ac)], axis=-1)
    sin = jnp.concatenate([-jnp.sin(ar), jnp.sin(ar), -jnp.sin(ac), jnp.sin(ac)], axis=-1)
    return jnp.tile(cos, (1, NA_HEADS)), jnp.tile(sin, (1, NA_HEADS))


def _postmix_kernel(yh_ref, yn_ref, gate_ref, x_ref, g1_ref, sh2_ref, sc2_ref, n2_ref, whyo_ref, wnao_ref,
                    wout_ref, wq_ref, x1_ref, h2_ref, pq_ref):
    a = _dot(yh_ref[...].astype(BF16), whyo_ref[...])
    b = _dot(yn_ref[...], wnao_ref[...])
    gate = gate_ref[...].astype(F32)
    merged = gate[:, :D_MODEL] * a + gate[:, D_MODEL:] * b
    x1 = x_ref[...] + g1_ref[...] * _dot(merged.astype(BF16), wout_ref[...])
    x1_ref[...] = x1
    h2 = _modulated_norm(x1, n2_ref[...], sh2_ref[...], sc2_ref[...]).astype(BF16)
    h2_ref[...] = h2
    for hp in range(2 * PEER_HEADS):
        pq_ref[hp] = _dot(h2, wq_ref[:, hp * PEER_D_HALF:(hp + 1) * PEER_D_HALF]).astype(BF16)


def _postmix(yh, yn, gate, x2, g1, sh2, sc2, n2g, whyo, wnao, wout, wq, seq):
    T = x2.shape[0]
    tiles_per_batch = seq // TOK_TILE
    row = lambda i: (i, 0)
    per_batch = lambda i: (i // tiles_per_batch, 0, 0)
    const = lambda i: (0, 0)
    nq = wq.shape[1]
    return pl.pallas_call(
        _postmix_kernel,
        grid=(T // TOK_TILE,),
        in_specs=[
            pl.BlockSpec((TOK_TILE, D_HY), row),
            pl.BlockSpec((TOK_TILE, D_NA), row),
            pl.BlockSpec((TOK_TILE, 2 * D_MODEL), row),
            pl.BlockSpec((TOK_TILE, D_MODEL), row),
            pl.BlockSpec((None, 1, D_MODEL), per_batch),
            pl.BlockSpec((None, 1, D_MODEL), per_batch),
            pl.BlockSpec((None, 1, D_MODEL), per_batch),
            pl.BlockSpec((1, D_MODEL), const),
            pl.BlockSpec((D_HY, D_MODEL), const),
            pl.BlockSpec((D_NA, D_MODEL), const),
            pl.BlockSpec((D_MODEL, D_MODEL), const),
            pl.BlockSpec((D_MODEL, nq), const),
        ],
        out_specs=[
            pl.BlockSpec((TOK_TILE, D_MODEL), row),
            pl.BlockSpec((TOK_TILE, D_MODEL), row),
            pl.BlockSpec((nq // PEER_D_HALF, TOK_TILE, PEER_D_HALF), lambda i: (0, i, 0)),
        ],
        out_shape=[
            jax.ShapeDtypeStruct((T, D_MODEL), F32),
            jax.ShapeDtypeStruct((T, D_MODEL), BF16),
            jax.ShapeDtypeStruct((nq // PEER_D_HALF, T, PEER_D_HALF), BF16),
        ],
        compiler_params=_cparams(("parallel",)),
        name="postmix",
    )(yh, yn, gate, x2, g1, sh2, sc2, n2g, whyo, wnao, wout, wq)


SUBLANES = 8
ROUTE_TILE = 256


def _argmax_tree(vals, idxs):
    while len(vals) > 1:
        nv, ni = [], []
        for j in range(0, len(vals) - 1, 2):
            right = vals[j + 1] > vals[j]
            nv.append(jnp.where(right, vals[j + 1], vals[j]))
            ni.append(jnp.where(right, idxs[j + 1], idxs[j]))
        if len(vals) % 2:
            nv.append(vals[-1])
            ni.append(idxs[-1])
        vals, idxs = nv, ni
    return vals[0], idxs[0]


W_BLOCK = 128
W_PITCH = 136
ROUTE_SLICES = 8


def _route_w_kernel(pq_ref, keys_ref, wa_ref, wb_ref, tv_ref, ti_ref, sa_ref, sb_ref, sg_ref, s_ref):
    step = pl.program_id(0)
    cur = step % 2
    prev = 1 - cur
    rt = pq_ref.shape[1]
    n = PEER_N_KEYS
    n_grp = n // SUBLANES
    sub = lax.broadcasted_iota(jnp.int32, (SUBLANES, rt), 0)
    key_idx = [sub + SUBLANES * g for g in range(n_grp)]
    key_row = lax.broadcasted_iota(jnp.int32, (n, sa_ref.shape[2]), 0)

    @pl.when(step == 0)
    def _():
        sa_ref[...] = jnp.zeros_like(sa_ref)
        sb_ref[...] = jnp.zeros_like(sb_ref)
        sg_ref[...] = jnp.zeros_like(sg_ref)
        s_ref[1] = jnp.zeros(s_ref.shape[1:], F32)

    def first_level(hp):
        s = _dot_nt(keys_ref[hp], pq_ref[hp]).reshape(n_grp, SUBLANES, rt)
        slabs = [s[g] for g in range(n_grp)]
        h, p = hp // 2, hp % 2
        for j in range(PEER_TOPK):
            v, i = _argmax_tree(slabs, key_idx)
            for shift in (4, 2, 1):
                v2, i2 = pltpu.roll(v, shift, 0), pltpu.roll(i, shift, 0)
                take = (v2 > v) | ((v2 == v) & (i2 < i))
                v, i = jnp.where(take, v2, v), jnp.where(take, i2, i)
            tv_ref[p, j, pl.ds(h, 1), :] = v[0:1]
            ti_ref[p, j, pl.ds(h, 1), :] = i[0:1]
            slabs = [jnp.where(key_idx[g] == i, -jnp.inf, slabs[g]) for g in range(n_grp)]

    def token_weights(blk, t):
        row = pl.ds(blk * W_BLOCK + t, 1)
        ga = jnp.where(sa_ref[prev, row, :] == key_row, sg_ref[prev, row, :], 0.0).astype(BF16)
        ob = jnp.where(sb_ref[prev, row, :] == key_row, 1.0, 0.0).astype(BF16)
        s_ref[blk, pl.ds(pl.multiple_of(t * W_PITCH, SUBLANES), n), :] = _dot_nt(ga, ob)

    def emit(w_ref, blk, i1):
        w_ref[i1] = s_ref[blk, pl.ds(i1, W_BLOCK, stride=W_PITCH), :].astype(BF16)

    per_slice = W_BLOCK // (ROUTE_SLICES // 2)

    def make_slice(blk, w_ref, emit_blk):
        def body(k, carry):
            for d in range(2 * PEER_HEADS // ROUTE_SLICES):
                first_level((blk * (ROUTE_SLICES // 2) + k) * (2 * PEER_HEADS // ROUTE_SLICES) + d)
            for j in range(per_slice):
                token_weights(blk, k * per_slice + j)
            for j in range(per_slice):
                emit(w_ref, emit_blk, k * per_slice + j)
            return carry
        return body

    lax.fori_loop(0, ROUTE_SLICES // 2, make_slice(0, wb_ref, 1), 0)
    lax.fori_loop(0, ROUTE_SLICES // 2, make_slice(1, wa_ref, 0), 0)

    pairs = [(a, b) for a in range(PEER_TOPK) for b in range(PEER_TOPK) if (a + 1) * (b + 1) <= PEER_TOPK]
    cand = [tv_ref[0, a] + tv_ref[1, b] for a, b in pairs]
    pos = [jnp.full((PEER_HEADS, rt), a * PEER_TOPK + b, jnp.int32) for a, b in pairs]
    best, sel_a, sel_b = [], [], []
    for j in range(PEER_TOPK):
        v, w = _argmax_tree(cand, pos)
        best.append(v)
        ia = jnp.zeros((PEER_HEADS, rt), jnp.int32)
        ib = jnp.zeros((PEER_HEADS, rt), jnp.int32)
        for n, (a, b) in enumerate(pairs):
            hit = w == (a * PEER_TOPK + b)
            cand[n] = jnp.where(hit, -jnp.inf, cand[n])
            ia = jnp.where(hit, ti_ref[0, a], ia)
            ib = jnp.where(hit, ti_ref[1, b], ib)
        sel_a.append(ia)
        sel_b.append(ib)
    ex = [jnp.exp(v - best[0]) for v in best]
    denom = functools.reduce(lambda x, y: x + y, ex)
    gates = [e / denom for e in ex]
    sa_ref[cur] = jnp.concatenate(sel_a, axis=0).T
    sb_ref[cur] = jnp.concatenate(sel_b, axis=0).T
    sg_ref[cur] = jnp.concatenate(gates, axis=0).T


def _route_w(pq, keys_bf):
    n_hp, T, _ = pq.shape
    assert ROUTE_TILE == 2 * W_BLOCK
    n_tiles = T // ROUTE_TILE
    n_sel = PEER_HEADS * PEER_TOPK
    w_shape = jax.ShapeDtypeStruct((n_tiles, PEER_N_KEYS, W_BLOCK, PEER_N_KEYS), BF16)
    w_block = (None, PEER_N_KEYS, W_BLOCK, PEER_N_KEYS)
    clamp = lambda i: jnp.clip(i, 0, n_tiles - 1)
    return pl.pallas_call(
        _route_w_kernel,
        grid=(n_tiles + 2,),
        in_specs=[
            pl.BlockSpec((n_hp, ROUTE_TILE, PEER_D_HALF), lambda i: (0, clamp(i), 0)),
            pl.BlockSpec((n_hp, PEER_N_KEYS, PEER_D_HALF), lambda i: (0, 0, 0)),
        ],
        out_specs=[pl.BlockSpec(w_block, lambda i: (clamp(i - 1), 0, 0, 0)),
                   pl.BlockSpec(w_block, lambda i: (clamp(i - 2), 0, 0, 0))],
        out_shape=[w_shape, w_shape],
        scratch_shapes=[
            pltpu.VMEM((2, PEER_TOPK, PEER_HEADS, ROUTE_TILE), F32),
            pltpu.VMEM((2, PEER_TOPK, PEER_HEADS, ROUTE_TILE), jnp.int32),
            pltpu.VMEM((2, ROUTE_TILE, n_sel), jnp.int32),
            pltpu.VMEM((2, ROUTE_TILE, n_sel), jnp.int32),
            pltpu.VMEM((2, ROUTE_TILE, n_sel), F32),
            pltpu.VMEM((2, W_BLOCK * W_PITCH, PEER_N_KEYS), F32),
        ],
        compiler_params=_cparams(("arbitrary",)),
        name="route_w",
    )(pq, keys_bf)


def _peer_dense_kernel(h2_ref, wa_ref, wb_ref, u_ref, v_ref, x1_ref, g2_ref, o_ref, acc_ref):
    e = pl.program_id(1)

    @pl.when(e == 0)
    def _():
        acc_ref[...] = jnp.zeros_like(acc_ref)

    a = _dot_nt(h2_ref[...], u_ref[...])
    act = 0.5 * a * (1.0 + lax.erf(a * (2.0 ** -0.5)))
    w = jnp.concatenate(
        [jnp.concatenate([ref[tile, i] for i in range(ref.shape[1])], axis=1)
         for tile in range(wa_ref.shape[0]) for ref in (wa_ref, wb_ref)], axis=0)
    acc_ref[...] += _dot((w.astype(F32) * act).astype(BF16), v_ref[...])

    @pl.when(e == pl.num_programs(1) - 1)
    def _():
        o_ref[...] = x1_ref[...] + g2_ref[...] * acc_ref[...]


def _peer_dense(h2, wa, wb, u_bf, v_bf, x1, g2, seq):
    T = h2.shape[0]
    tiles_per_batch = seq // PEER_TOK_TILE
    n_exp = u_bf.shape[0]
    w_spec = pl.BlockSpec((PEER_TOK_TILE // ROUTE_TILE, PEER_EXP_TILE // PEER_N_KEYS, W_BLOCK, PEER_N_KEYS),
                          lambda i, e: (i, e, 0, 0))
    return pl.pallas_call(
        _peer_dense_kernel,
        grid=(T // PEER_TOK_TILE, n_exp // PEER_EXP_TILE),
        in_specs=[
            pl.BlockSpec((PEER_TOK_TILE, D_MODEL), lambda i, e: (i, 0)),
            w_spec,
            w_spec,
            pl.BlockSpec((PEER_EXP_TILE, D_MODEL), lambda i, e: (e, 0)),
            pl.BlockSpec((PEER_EXP_TILE, D_MODEL), lambda i, e: (e, 0)),
            pl.BlockSpec((PEER_TOK_TILE, D_MODEL), lambda i, e: (i, 0)),
            pl.BlockSpec((None, 1, D_MODEL), lambda i, e: (i // tiles_per_batch, 0, 0)),
        ],
        out_specs=pl.BlockSpec((PEER_TOK_TILE, D_MODEL), lambda i, e: (i, 0)),
        out_shape=jax.ShapeDtypeStruct((T, D_MODEL), F32),
        scratch_shapes=[pltpu.VMEM((PEER_TOK_TILE, D_MODEL), F32)],
        compiler_params=_cparams(("parallel", "arbitrary")),
        name="peer_dense",
    )(h2, wa, wb, u_bf, v_bf, x1, g2)


HI = lax.Precision.HIGHEST


HY_LANES = 128


def _shortconv_kernel(z_ref, w_ref, b_ref, o_ref):
    z = z_ref[...]
    n = z.shape[0]
    row = lax.broadcasted_iota(jnp.int32, z.shape, 0)
    prev = jnp.where(row == 0, 0.0, pltpu.roll(z, 1, 0))
    nxt = jnp.where(row == n - 1, 0.0, pltpu.roll(z, n - 1, 0))
    o_ref[...] = w_ref[0:1] * prev + w_ref[1:2] * z + w_ref[2:3] * nxt + b_ref[...]


def _shortconv(zh, w, b):
    B, L, C = zh.shape
    return pl.pallas_call(
        _shortconv_kernel,
        grid=(B, C // HY_LANES),
        in_specs=[
            pl.BlockSpec((None, L, HY_LANES), lambda i, j: (i, 0, j)),
            pl.BlockSpec((3, HY_LANES), lambda i, j: (0, j)),
            pl.BlockSpec((1, HY_LANES), lambda i, j: (0, j)),
        ],
        out_specs=pl.BlockSpec((None, L, HY_LANES), lambda i, j: (i, 0, j)),
        out_shape=jax.ShapeDtypeStruct((B, L, C), F32),
        compiler_params=_cparams(("parallel", "parallel")),
        name="shortconv",
    )(zh, w, b)


FFT_R = 128
FFT_N = FFT_R * FFT_R
FFT_PITCH = 136
FFT_K1_CHUNK = 16
FFT_UNROLL = 8


def _fft_tables():
    r = jnp.arange(FFT_R, dtype=jnp.int32)
    n2, k1, n1 = r[:, None, None], r[None, :, None], r[None, None, :]
    ang = (2.0 * math.pi / FFT_N) * ((k1 * (FFT_R * n1 + n2)) % FFT_N).astype(F32)
    c, s = jnp.cos(ang), jnp.sin(ang)
    g_full = jnp.concatenate([c, -s], axis=1)
    h_half = jnp.concatenate([jnp.swapaxes(c, 1, 2), -jnp.swapaxes(s, 1, 2)], axis=2)[:, :FFT_R // 2] / FFT_N
    ang2 = (2.0 * math.pi / FFT_R) * ((r[:, None] * r[None, :]) % FFT_R).astype(F32)
    c2, s2 = jnp.cos(ang2), jnp.sin(ang2)
    f_fwd = jnp.concatenate([jnp.concatenate([c2, s2], 1), jnp.concatenate([-s2, c2], 1)], 0)
    f_inv = jnp.concatenate([jnp.concatenate([c2, -s2], 1), jnp.concatenate([s2, c2], 1)], 0)
    return g_full.astype(BF16), h_half.astype(BF16), f_fwd.astype(BF16), f_inv.astype(BF16)


def _fft_stage1(load_slab, g_ref, bre_ref, bim_ref):
    def body(n2, carry):
        out = _dot(g_ref[n2], load_slab(n2).astype(BF16))
        bre_ref[pl.ds(n2, FFT_R, stride=FFT_PITCH), :] = out[:FFT_R]
        bim_ref[pl.ds(n2, FFT_R, stride=FFT_PITCH), :] = out[FFT_R:]
        return carry

    lax.fori_loop(0, FFT_R, body, 0, unroll=FFT_UNROLL)


def _k1_rows(k1):
    return pl.ds(pl.multiple_of(k1 * FFT_PITCH, SUBLANES), FFT_R)


def _load_k1_pair(bre_ref, bim_ref, k1):
    return jnp.concatenate(
        [jnp.concatenate([bre_ref[_k1_rows(k1 + d), :], bim_ref[_k1_rows(k1 + d), :]], axis=0) for d in range(2)],
        axis=1)


FILT_ROWS = 512
FILT_FEAT = 128


def _filt_kernel(feat_ref, delta_ref, w1_ref, b1_ref, w2_ref, b2_ref, w3_ref, b3_ref, w4_ref, freq_ref,
                 h_ref, sum_ref):
    @pl.when(pl.program_id(0) == 0)
    def _():
        sum_ref[...] = jnp.zeros_like(sum_ref)

    feat = feat_ref[...]
    freq = freq_ref[...]
    dot = lambda a, b: jnp.dot(a, b, precision=HI, preferred_element_type=F32)
    h = jnp.sin(freq * (dot(feat, w1_ref[...]) + b1_ref[...]))
    h = jnp.sin(freq * (dot(h, w2_ref[...]) + b2_ref[...]))
    h = jnp.sin(freq * (dot(h, w3_ref[...]) + b3_ref[...]))
    h = dot(h, w4_ref[...])
    n_feat = 1 + 2 * HY_BANDS
    t = feat[:, 0:1]
    forward = feat[:, n_feat:n_feat + 1] > 0.5
    keep = feat[:, n_feat + 1:n_feat + 2]
    decay = jnp.exp(-t * delta_ref[...])
    outs = []
    for o in range(HY_ORDER):
        base = o * 2 * D_HY
        outs.append(jnp.where(forward, h[:, base:base + D_HY], h[:, base + D_HY:base + 2 * D_HY]) * decay)
    out = jnp.concatenate(outs, axis=1)
    sum_ref[...] += jnp.sum(jnp.abs(out), axis=0, keepdims=True)
    h_ref[...] = out * keep


def _filt_features(L):
    r = np.arange(FFT_N)
    n = FFT_R * (r % FFT_R) + r // FFT_R
    j = np.where(n < L, n, np.where(n > L, 2 * L - n, 0))
    jj = jnp.asarray(j, dtype=F32)[:, None]
    t = jj / (L - 1.0)
    w = (2.0 * math.pi / L) * jj
    bands = jnp.linspace(1e-4, HY_BANDS - 1.0, HY_BANDS, dtype=F32)[None, :]
    flags = jnp.asarray(np.stack([n < L, n != L], axis=1), dtype=F32)
    feats = jnp.concatenate([t, jnp.cos(bands * w), -jnp.sin(bands * w), flags], axis=-1)
    return jnp.pad(feats, ((0, 0), (0, FILT_FEAT - feats.shape[1])))


def _filters(p, L):
    feats = _filt_features(L)
    deltas = jnp.abs(jnp.linspace(math.log(HY_DECAY_TARGET) / HY_SLOW_PCT,
                                  math.log(HY_DECAY_TARGET) / HY_FAST_PCT, D_HY, dtype=F32))[None, :]
    w1 = jnp.pad(p['hy_f1_w'], ((0, FILT_FEAT - p['hy_f1_w'].shape[0]), (0, 0)))
    n_out = HY_ORDER * D_HY
    const = lambda i: (0, 0)
    full = lambda a: pl.BlockSpec(a.shape, const)
    args = [deltas, w1, p['hy_f1_b'][None], p['hy_f2_w'], p['hy_f2_b'][None], p['hy_f3_w'], p['hy_f3_b'][None],
            p['hy_f4_w'], p['hy_sin_freq'][None]]
    return pl.pallas_call(
        _filt_kernel,
        grid=(FFT_N // FILT_ROWS,),
        in_specs=[pl.BlockSpec((FILT_ROWS, FILT_FEAT), lambda i: (i, 0))] + [full(a) for a in args],
        out_specs=[pl.BlockSpec((FILT_ROWS, n_out), lambda i: (i, 0)), pl.BlockSpec((1, n_out), const)],
        out_shape=[jax.ShapeDtypeStruct((FFT_N, n_out), F32), jax.ShapeDtypeStruct((1, n_out), F32)],
        compiler_params=_cparams(("arbitrary",)),
        name="hyena_filter",
    )(feats, *args)


def _fspec_kernel(h_ref, sum_ref, g_ref, f_ref, o_ref, bre_ref, bim_ref):
    c = pl.program_id(1)

    @pl.when(c == 0)
    def _():
        _fft_stage1(lambda n2: h_ref[pl.ds(pl.multiple_of(n2 * FFT_R, FFT_R), FFT_R), :], g_ref, bre_ref, bim_ref)

    inv = 1.0 / sum_ref[...]

    lanes = h_ref.shape[1]

    def slab_pair(j, carry):
        x = _dot(f_ref[...], _load_k1_pair(bre_ref, bim_ref, c * FFT_K1_CHUNK + 2 * j).astype(BF16))
        for d in range(2):
            rows = pl.ds(pl.multiple_of((2 * j + d) * FFT_R, FFT_R), FFT_R)
            o_ref[0, rows, :] = x[:FFT_R, d * lanes:(d + 1) * lanes] * inv
            o_ref[1, rows, :] = x[FFT_R:, d * lanes:(d + 1) * lanes] * inv
        return carry

    lax.fori_loop(0, FFT_K1_CHUNK // 2, slab_pair, 0, unroll=FFT_UNROLL // 2)


def _filter_spectrum(h, hsum, g_full, f_fwd):
    n_ch = h.shape[1]
    single = pl.Buffered(1)
    return pl.pallas_call(
        _fspec_kernel,
        grid=(n_ch // HY_LANES, FFT_R // FFT_K1_CHUNK),
        in_specs=[
            pl.BlockSpec((FFT_N, HY_LANES), lambda j, c: (0, j), pipeline_mode=single),
            pl.BlockSpec((1, HY_LANES), lambda j, c: (0, j)),
            pl.BlockSpec(g_full.shape, lambda j, c: (0, 0, 0), pipeline_mode=single),
            pl.BlockSpec(f_fwd.shape, lambda j, c: (0, 0)),
        ],
        out_specs=pl.BlockSpec((2, FFT_K1_CHUNK * FFT_R, HY_LANES), lambda j, c: (0, c, j)),
        out_shape=jax.ShapeDtypeStruct((2, FFT_N, n_ch), F32),
        scratch_shapes=[pltpu.VMEM((FFT_R * FFT_PITCH, HY_LANES), F32)] * 2,
        compiler_params=_cparams(("parallel", "arbitrary")),
        name="hyena_filter_fft",
    )(h, hsum, g_full, f_fwd)


def _hyconv_kernel(u_ref, gate_ref, kf_ref, skip_ref, g_ref, h_ref, f_ref, fi_ref, o_ref, bre_ref, bim_ref):
    c = pl.program_id(2)
    half = FFT_R // 2

    def rows_of(n2):
        return pl.ds(pl.multiple_of(n2 * half, half), half)

    @pl.when(c == 0)
    def _():
        _fft_stage1(lambda n2: u_ref[rows_of(n2), :], g_ref, bre_ref, bim_ref)

    lanes = u_ref.shape[1]

    def slab_pair(j, carry):
        k1 = c * FFT_K1_CHUNK + 2 * j
        x = _dot(f_ref[...], _load_k1_pair(bre_ref, bim_ref, k1).astype(BF16))
        xr, xi = x[:FFT_R], x[FFT_R:]
        kr, ki = [jnp.concatenate([kf_ref[part, pl.ds(pl.multiple_of((2 * j + d) * FFT_R, FFT_R), FFT_R), :]
                                   for d in range(2)], axis=1) for part in range(2)]
        y = jnp.concatenate([xr * kr - xi * ki, xr * ki + xi * kr], axis=0)
        z = _dot(fi_ref[...], y.astype(BF16))
        for d in range(2):
            bre_ref[_k1_rows(k1 + d), :] = z[:FFT_R, d * lanes:(d + 1) * lanes]
            bim_ref[_k1_rows(k1 + d), :] = z[FFT_R:, d * lanes:(d + 1) * lanes]
        return carry

    lax.fori_loop(0, FFT_K1_CHUNK // 2, slab_pair, 0, unroll=FFT_UNROLL // 2)

    @pl.when(c == pl.num_programs(2) - 1)
    def _():
        def body(n2, carry):
            z = jnp.concatenate([bre_ref[pl.ds(n2, FFT_R, stride=FFT_PITCH), :],
                                 bim_ref[pl.ds(n2, FFT_R, stride=FFT_PITCH), :]], axis=0)
            conv = _dot(h_ref[n2], z.astype(BF16))
            rows = rows_of(n2)
            o_ref[rows, :] = gate_ref[rows, :] * (conv + skip_ref[...] * u_ref[rows, :])
            return carry

        lax.fori_loop(0, FFT_R, body, 0, unroll=FFT_UNROLL)


def _hyconv(u, u_col, gate, gate_col, kf, kf_col, skip, tables):
    g_half, h_half, f_fwd, f_inv = tables
    B, L, _ = u.shape
    single = pl.Buffered(1)
    return pl.pallas_call(
        _hyconv_kernel,
        grid=(B, D_HY // HY_LANES, FFT_R // FFT_K1_CHUNK),
        in_specs=[
            pl.BlockSpec((None, L, HY_LANES), lambda b, j, c: (b, 0, u_col + j), pipeline_mode=single),
            pl.BlockSpec((None, L, HY_LANES), lambda b, j, c: (b, 0, gate_col + j), pipeline_mode=single),
            pl.BlockSpec((2, FFT_K1_CHUNK * FFT_R, HY_LANES), lambda b, j, c: (0, c, kf_col + j)),
            pl.BlockSpec((1, HY_LANES), lambda b, j, c: (0, j)),
            pl.BlockSpec(g_half.shape, lambda b, j, c: (0, 0, 0), pipeline_mode=single),
            pl.BlockSpec(h_half.shape, lambda b, j, c: (0, 0, 0), pipeline_mode=single),
            pl.BlockSpec(f_fwd.shape, lambda b, j, c: (0, 0)),
            pl.BlockSpec(f_inv.shape, lambda b, j, c: (0, 0)),
        ],
        out_specs=pl.BlockSpec((None, L, HY_LANES), lambda b, j, c: (b, 0, j)),
        out_shape=jax.ShapeDtypeStruct((B, L, D_HY), F32),
        scratch_shapes=[pltpu.VMEM((FFT_R * FFT_PITCH, HY_LANES), F32)] * 2,
        compiler_params=_cparams(("parallel", "parallel", "arbitrary")),
        name="hyena_conv",
    )(u, gate, kf, skip, g_half, h_half, f_fwd, f_inv)


def _slab_major(a, n1):
    B, L, C = a.shape
    return a.reshape(B, n1, L // n1, C).transpose(0, 2, 1, 3).reshape(B, L, C)


def _hyena(zh, p):
    B, L, _ = zh.shape
    assert 2 * L == FFT_N
    g_full, h_half, f_fwd, f_inv = _fft_tables()
    tables = (g_full[:, :, :FFT_R // 2], h_half, f_fwd, f_inv)
    filt, filt_sum = _filters(p, L)
    both_dirs = filt_sum[:, :]
    kf = _filter_spectrum(filt, both_dirs, g_full, f_fwd)
    zc = _slab_major(_shortconv(zh, p['hy_conv_w'], p['hy_conv_b'][None]), FFT_R // 2)
    lanes = D_HY // HY_LANES
    y = _hyconv(zc, 0, zc, lanes, kf, 0, p['hy_skip'][0:1], tables)
    y = _hyconv(y, 0, zc, 2 * lanes, kf, lanes, p['hy_skip'][1:2], tables)
    return _slab_major(y, L // (FFT_R // 2))


def kernel(x, c, ctx, c_ctx, norm1_g, norm2_g, w_ada, b_ada, w_in, b_in, hy_conv_w, hy_conv_b, hy_f1_w, hy_f1_b, hy_f2_w, hy_f2_b, hy_f3_w, hy_f3_b, hy_f4_w, hy_sin_freq, hy_skip, q_norm_g, k_norm_g, na_rpb, w_hy_out, w_na_out, w_out, peer_w_q, peer_keys, peer_u, peer_v):
    assert w_in.shape[0] == 1, "single-layer block"
    B, S, D = x.shape
    C = ctx.shape[1]
    T = B * S

    m_lat = (jax.nn.silu(c) @ w_ada[0] + b_ada[0]).reshape(B, N_MOD, 1, D)
    m_ctx = (jax.nn.silu(c_ctx) @ w_ada[0] + b_ada[0]).reshape(N_MOD, 1, D)
    sh1, sc1, g1, sh2, sc2, g2 = [m_lat[:, i] for i in range(N_MOD)]

    w_in_bf = w_in[0].astype(BF16)
    b_in2 = b_in[0][None, :]
    gain1 = norm1_g[0][None, :]
    qg = jnp.tile(q_norm_g[0], NA_HEADS)[None, :]
    kg = jnp.tile(k_norm_g[0], NA_HEADS)[None, :]
    seg = jnp.asarray(np.kron(np.eye(NA_HEADS), np.full((HEAD_DIM, HEAD_DIM), 1.0 / HEAD_DIM)), dtype=BF16)
    cos, sin = _rope_tables(S)

    x2 = x.reshape(T, D)
    zh, q, k, v, gate = _inproj(x2, sh1, sc1, gain1, w_in_bf, b_in2, cos, sin, qg, kg, seg, S)

    k_ctx, v_ctx = _ctxkv(ctx.reshape(B * C, D), m_ctx[0], m_ctx[1], gain1,
                          w_in_bf[:, COL_K:COL_V], b_in2[:, COL_K:COL_V],
                          w_in_bf[:, COL_V:COL_G_HY], b_in2[:, COL_V:COL_G_HY], kg, seg)

    p = {'hy_conv_w': hy_conv_w[0], 'hy_conv_b': hy_conv_b[0], 'hy_f1_w': hy_f1_w[0], 'hy_f1_b': hy_f1_b[0],
         'hy_f2_w': hy_f2_w[0], 'hy_f2_b': hy_f2_b[0], 'hy_f3_w': hy_f3_w[0], 'hy_f3_b': hy_f3_b[0],
         'hy_f4_w': hy_f4_w[0], 'hy_sin_freq': hy_sin_freq[0], 'hy_skip': hy_skip[0]}
    y_hy = _hyena(zh.reshape(B, S, 3 * D_HY), p).reshape(T, D_HY)

    y_na = _natten(q.reshape(B, S, D_NA), k.reshape(B, S, D_NA), v.reshape(B, S, D_NA),
                   k_ctx.reshape(B, C, D_NA), v_ctx.reshape(B, C, D_NA), _natten_bias(na_rpb[0], S // GRID_W)).reshape(T, D_NA)

    x1, h2, pq = _postmix(y_hy, y_na, gate, x2, g1, sh2, sc2, norm2_g[0][None, :],
                          w_hy_out[0].astype(BF16), w_na_out[0].astype(BF16), w_out[0].astype(BF16),
                          peer_w_q[0].astype(BF16), S)

    keys_bf = peer_keys[0].astype(BF16).reshape(2 * PEER_HEADS, PEER_N_KEYS, PEER_D_HALF)
    wa, wb = _route_w(pq, keys_bf)
    out = _peer_dense(h2, wa, wb, peer_u[0].astype(BF16), peer_v[0].astype(BF16), x1, g2, S)
    return out.reshape(B, S, D)
```

```python
import functools
import math

import numpy as np
import jax
import jax.numpy as jnp
from jax import lax
from jax.experimental import pallas as pl
from jax.experimental.pallas import tpu as pltpu

F32 = jnp.float32
BF16 = jnp.bfloat16

D_MODEL = 1024
GRID_W = 64
EPS = 1e-6
N_MOD = 6

D_HY = 512
HY_ORDER = 2
HY_BANDS = 16
HY_DECAY_TARGET = 1e-2
HY_FAST_PCT = 0.3
HY_SLOW_PCT = 1.5

NA_HEADS = 8
HEAD_DIM = 64
D_NA = NA_HEADS * HEAD_DIM
WIN_ROWS = 8
WIN_COLS = 16
ROPE_THETA = 10000.0
NEG_INF = -1e30

PEER_HEADS = 8
PEER_N_KEYS = 128
PEER_N_EXPERTS = PEER_N_KEYS * PEER_N_KEYS
PEER_TOPK = 16
PEER_D_KEY = 256
PEER_D_HALF = PEER_D_KEY // 2

COL_HY = 0
COL_Q = COL_HY + 3 * D_HY
COL_K = COL_Q + D_NA
COL_V = COL_K + D_NA
COL_G_HY = COL_V + D_NA
N_PROJ = COL_G_HY + 2 * D_MODEL

VMEM_LIMIT = 56 * 1024 * 1024
TOK_TILE = 256
PEER_TOK_TILE = 512
PEER_EXP_TILE = 2048


def _cparams(sem):
    return pltpu.CompilerParams(dimension_semantics=sem, vmem_limit_bytes=VMEM_LIMIT)


def _dot(a, b):
    return jnp.dot(a, b, preferred_element_type=F32)


def _dot_nt(a, b):
    return lax.dot_general(a, b, (((1,), (1,)), ((), ())), preferred_element_type=F32)


def _modulated_norm(x, gain, shift, scale):
    ms = jnp.mean(x * x, axis=-1, keepdims=True)
    return (x * lax.rsqrt(ms + EPS) * gain) * (1.0 + scale) + shift


def _head_rms(z, seg, gain):
    zz = z * z
    hi = zz.astype(BF16)
    lo = (zz - hi.astype(F32)).astype(BF16)
    ms = _dot(hi, seg) + _dot(lo, seg)
    return z * lax.rsqrt(ms + EPS) * gain


def _rope(z, cos, sin_signed):
    n = z.shape[-1]
    lane = lax.broadcasted_iota(jnp.int32, z.shape, 1)
    first = (lane // (HEAD_DIM // 4)) % 2 == 0
    partner = jnp.where(first, pltpu.roll(z, n - HEAD_DIM // 4, 1), pltpu.roll(z, HEAD_DIM // 4, 1))
    return z * cos + partner * sin_signed


def _inproj_kernel(x_ref, shift_ref, scale_ref, g_ref, w_ref, b_ref, cos_ref, sin_ref, qg_ref, kg_ref,
                   seg_ref, zh_ref, q_ref, k_ref, v_ref, gate_ref):
    h = _modulated_norm(x_ref[...], g_ref[...], shift_ref[...], scale_ref[...]).astype(BF16)

    def proj(lo, hi):
        return _dot(h, w_ref[:, lo:hi]) + b_ref[:, lo:hi]

    zh_ref[...] = proj(COL_HY, COL_Q)
    cos, sin = cos_ref[...], sin_ref[...]
    seg = seg_ref[...]
    q = _rope(_head_rms(proj(COL_Q, COL_K), seg, qg_ref[...]), cos, sin)
    q_ref[...] = q.astype(BF16)
    k = _rope(_head_rms(proj(COL_K, COL_V), seg, kg_ref[...]), cos, sin)
    k_ref[...] = k.astype(BF16)
    v_ref[...] = proj(COL_V, COL_G_HY).astype(BF16)
    gate_ref[...] = jax.nn.sigmoid(proj(COL_G_HY, N_PROJ)).astype(BF16)


def _inproj(x2, shift, scale, gain, w_bf, b_in, cos, sin, qg, kg, seg, seq):
    T = x2.shape[0]
    tiles_per_batch = seq // TOK_TILE
    row = lambda i: (i, 0)
    per_batch = lambda i: (i // tiles_per_batch, 0, 0)
    const = lambda i: (0, 0)
    pos = lambda i: (i % tiles_per_batch, 0)
    return pl.pallas_call(
        _inproj_kernel,
        grid=(T // TOK_TILE,),
        in_specs=[
            pl.BlockSpec((TOK_TILE, D_MODEL), row),
            pl.BlockSpec((None, 1, D_MODEL), per_batch),
            pl.BlockSpec((None, 1, D_MODEL), per_batch),
            pl.BlockSpec((1, D_MODEL), const),
            pl.BlockSpec((D_MODEL, N_PROJ), const),
            pl.BlockSpec((1, N_PROJ), const),
            pl.BlockSpec((TOK_TILE, D_NA), pos),
            pl.BlockSpec((TOK_TILE, D_NA), pos),
            pl.BlockSpec((1, D_NA), const),
            pl.BlockSpec((1, D_NA), const),
            pl.BlockSpec((D_NA, D_NA), const),
        ],
        out_specs=[
            pl.BlockSpec((TOK_TILE, 3 * D_HY), row),
            pl.BlockSpec((TOK_TILE, D_NA), row),
            pl.BlockSpec((TOK_TILE, D_NA), row),
            pl.BlockSpec((TOK_TILE, D_NA), row),
            pl.BlockSpec((TOK_TILE, 2 * D_MODEL), row),
        ],
        out_shape=[
            jax.ShapeDtypeStruct((T, 3 * D_HY), F32),
            jax.ShapeDtypeStruct((T, D_NA), BF16),
            jax.ShapeDtypeStruct((T, D_NA), BF16),
            jax.ShapeDtypeStruct((T, D_NA), BF16),
            jax.ShapeDtypeStruct((T, 2 * D_MODEL), BF16),
        ],
        compiler_params=_cparams(("parallel",)),
        name="inproj",
    )(x2, shift, scale, gain, w_bf, b_in, cos, sin, qg, kg, seg)


def _ctxkv_kernel(x_ref, shift_ref, scale_ref, g_ref, wk_ref, bk_ref, wv_ref, bv_ref, kg_ref, seg_ref,
                  k_ref, v_ref):
    h = _modulated_norm(x_ref[...], g_ref[...], shift_ref[...], scale_ref[...]).astype(BF16)
    k = _head_rms(_dot(h, wk_ref[...]) + bk_ref[...], seg_ref[...], kg_ref[...])
    k_ref[...] = k.astype(BF16)
    v_ref[...] = (_dot(h, wv_ref[...]) + bv_ref[...]).astype(BF16)


def _ctxkv(ctx2, shift, scale, gain, wk, bk, wv, bv, kg, seg):
    T = ctx2.shape[0]
    row = lambda i: (i, 0)
    const = lambda i: (0, 0)
    return pl.pallas_call(
        _ctxkv_kernel,
        grid=(T // TOK_TILE,),
        in_specs=[
            pl.BlockSpec((TOK_TILE, D_MODEL), row),
            pl.BlockSpec((1, D_MODEL), const),
            pl.BlockSpec((1, D_MODEL), const),
            pl.BlockSpec((1, D_MODEL), const),
            pl.BlockSpec((D_MODEL, D_NA), const),
            pl.BlockSpec((1, D_NA), const),
            pl.BlockSpec((D_MODEL, D_NA), const),
            pl.BlockSpec((1, D_NA), const),
            pl.BlockSpec((1, D_NA), const),
            pl.BlockSpec((D_NA, D_NA), const),
        ],
        out_specs=[pl.BlockSpec((TOK_TILE, D_NA), row), pl.BlockSpec((TOK_TILE, D_NA), row)],
        out_shape=[jax.ShapeDtypeStruct((T, D_NA), BF16), jax.ShapeDtypeStruct((T, D_NA), BF16)],
        compiler_params=_cparams(("parallel",)),
        name="ctxkv",
    )(ctx2, shift, scale, gain, wk, bk, wv, bv, kg, seg)


NA_BLOCK_ROWS = 4
NA_KEY_ROWS = WIN_ROWS + NA_BLOCK_ROWS
NA_CLASSES = 3


def _natten_kernel(q_ref, k_ref, v_ref, kc_ref, vc_ref, bias_ref, o_ref, *, rows):
    r0 = pl.program_id(1) * NA_BLOCK_ROWS
    key_start = jnp.clip(r0 - WIN_ROWS // 2, 0, rows - NA_KEY_ROWS)
    start = pl.multiple_of(key_start * GRID_W, GRID_W * NA_BLOCK_ROWS)
    n_q = NA_BLOCK_ROWS * GRID_W
    n_loc = NA_KEY_ROWS * GRID_W
    scale = jnp.asarray(HEAD_DIM ** -0.5, BF16)
    pair = 2 * HEAD_DIM
    lane = lax.broadcasted_iota(jnp.int32, (n_q, pair), 1)
    for p in range(NA_HEADS // 2):
        cols = slice(p * pair, (p + 1) * pair)
        qp = q_ref[:, cols] * scale
        kp = k_ref[pl.ds(start, n_loc), cols]
        vp = v_ref[pl.ds(start, n_loc), cols]
        kcp = kc_ref[:, cols]
        vcp = vc_ref[:, cols]
        outs = []
        for hh in range(2):
            qm = jnp.where((lane // HEAD_DIM) == hh, qp, jnp.zeros_like(qp))
            s_loc = _dot_nt(qm, kp) + bias_ref[2 * p + hh]
            s_ctx = _dot_nt(qm, kcp)
            m = jnp.maximum(jnp.max(s_loc, axis=-1, keepdims=True), jnp.max(s_ctx, axis=-1, keepdims=True))
            e_loc = jnp.exp(s_loc - m)
            e_ctx = jnp.exp(s_ctx - m)
            denom = jnp.sum(e_loc, axis=-1, keepdims=True) + jnp.sum(e_ctx, axis=-1, keepdims=True)
            o = _dot(e_loc.astype(BF16), vp) + _dot(e_ctx.astype(BF16), vcp)
            outs.append(o / denom)
        o_ref[:, cols] = jnp.where((lane // HEAD_DIM) == 0, outs[0], outs[1]).astype(o_ref.dtype)


def _natten(q, k, v, kc, vc, bias):
    B, S, _ = q.shape
    rows = S // GRID_W
    n_blk = rows // NA_BLOCK_ROWS
    C = kc.shape[1]
    n_q = NA_BLOCK_ROWS * GRID_W
    single = pl.Buffered(1)
    block_class = lambda b, i: (jnp.where(i == 0, 0, jnp.where(i == n_blk - 1, 2, 1)), 0, 0, 0)
    return pl.pallas_call(
        functools.partial(_natten_kernel, rows=rows),
        grid=(B, n_blk),
        in_specs=[
            pl.BlockSpec((None, n_q, D_NA), lambda b, i: (b, i, 0)),
            pl.BlockSpec((None, S, D_NA), lambda b, i: (b, 0, 0), pipeline_mode=single),
            pl.BlockSpec((None, S, D_NA), lambda b, i: (b, 0, 0), pipeline_mode=single),
            pl.BlockSpec((None, C, D_NA), lambda b, i: (b, 0, 0)),
            pl.BlockSpec((None, C, D_NA), lambda b, i: (b, 0, 0)),
            pl.BlockSpec((None,) + bias.shape[1:], block_class),
        ],
        out_specs=pl.BlockSpec((None, n_q, D_NA), lambda b, i: (b, i, 0)),
        out_shape=jax.ShapeDtypeStruct((B, S, D_NA), BF16),
        compiler_params=_cparams(("parallel", "arbitrary")),
        name="natten",
    )(q, k, v, kc, vc, bias)


def _natten_bias(rpb, rows):
    col = np.arange(GRID_W)
    dc = np.clip(col[None, :] - col[:, None], -(WIN_COLS - 1), WIN_COLS - 1) + (WIN_COLS - 1)
    col_start = np.clip(col - WIN_COLS // 2, 0, GRID_W - WIN_COLS)
    col_mask = (col[None, :] >= col_start[:, None]) & (col[None, :] < col_start[:, None] + WIN_COLS)
    r0 = np.array([0, 2 * NA_BLOCK_ROWS, rows - NA_BLOCK_ROWS])
    r = r0[:, None] + np.arange(NA_BLOCK_ROWS)[None, :]
    key_row = np.clip(r0 - WIN_ROWS // 2, 0, rows - NA_KEY_ROWS)[:, None, None] + np.arange(NA_KEY_ROWS)
    win_start = np.clip(r - WIN_ROWS // 2, 0, rows - WIN_ROWS)[:, :, None]
    row_mask = (key_row >= win_start) & (key_row < win_start + WIN_ROWS)
    dr = np.clip(key_row - r[:, :, None] + (WIN_ROWS - 1), 0, 2 * WIN_ROWS - 2)
    pick_c = jnp.asarray(dc[None] == np.arange(2 * WIN_COLS - 1)[:, None, None], dtype=F32)
    pick_r = jnp.asarray(dr[..., None] == np.arange(2 * WIN_ROWS - 1), dtype=F32)
    b = jnp.einsum('crkd,hde,eqx->chrqkx', pick_r, rpb.astype(F32), pick_c, precision=HI)
    mask = row_mask[:, None, :, None, :, None] & col_mask[None, None, None, :, None, :]
    b = jnp.where(mask, b, NEG_INF)
    return b.reshape(NA_CLASSES, NA_HEADS, NA_BLOCK_ROWS * GRID_W, NA_KEY_ROWS * GRID_W)


def _rope_tables(seq):
    pos = jnp.arange(seq, dtype=jnp.int32)
    rows = (pos // GRID_W).astype(F32)
    cols = (pos % GRID_W).astype(F32)
    nf = HEAD_DIM // 4
    inv = ROPE_THETA ** (-jnp.arange(nf, dtype=F32) / nf)
    ar = rows[:, None] * inv[None, :]
    ac = cols[:, None] * inv[None, :]
    cos = jnp.concatenate([jnp.cos(ar), jnp.cos(ar), jnp.cos(ac), jnp.cos(ac)], axis=-1)
    sin = jnp.concatenate([-jnp.sin(ar), jnp.sin(ar), -jnp.sin(ac), jnp.sin(ac)], axis=-1)
    return jnp.tile(cos, (1, NA_HEADS)), jnp.tile(sin, (1, NA_HEADS))


def _postmix_kernel(yh_ref, yn_ref, gate_ref, x_ref, g1_ref, sh2_ref, sc2_ref, n2_ref, whyo_ref, wnao_ref,
                    wout_ref, wq_ref, x1_ref, h2_ref, pq_ref):
    a = _dot(yh_ref[...].astype(BF16), whyo_ref[...])
    b = _dot(yn_ref[...], wnao_ref[...])
    gate = gate_ref[...].astype(F32)
    merged = gate[:, :D_MODEL] * a + gate[:, D_MODEL:] * b
    x1 = x_ref[...] + g1_ref[...] * _dot(merged.astype(BF16), wout_ref[...])
    x1_ref[...] = x1
    h2 = _modulated_norm(x1, n2_ref[...], sh2_ref[...], sc2_ref[...]).astype(BF16)
    h2_ref[...] = h2
    for hp in range(2 * PEER_HEADS):
        pq_ref[hp] = _dot(h2, wq_ref[:, hp * PEER_D_HALF:(hp + 1) * PEER_D_HALF]).astype(BF16)


def _postmix(yh, yn, gate, x2, g1, sh2, sc2, n2g, whyo, wnao, wout, wq, seq):
    T = x2.shape[0]
    tiles_per_batch = seq // TOK_TILE
    row = lambda i: (i, 0)
    per_batch = lambda i: (i // tiles_per_batch, 0, 0)
    const = lambda i: (0, 0)
    nq = wq.shape[1]
    return pl.pallas_call(
        _postmix_kernel,
        grid=(T // TOK_TILE,),
        in_specs=[
            pl.BlockSpec((TOK_TILE, D_HY), row),
            pl.BlockSpec((TOK_TILE, D_NA), row),
            pl.BlockSpec((TOK_TILE, 2 * D_MODEL), row),
            pl.BlockSpec((TOK_TILE, D_MODEL), row),
            pl.BlockSpec((None, 1, D_MODEL), per_batch),
            pl.BlockSpec((None, 1, D_MODEL), per_batch),
            pl.BlockSpec((None, 1, D_MODEL), per_batch),
            pl.BlockSpec((1, D_MODEL), const),
            pl.BlockSpec((D_HY, D_MODEL), const),
            pl.BlockSpec((D_NA, D_MODEL), const),
            pl.BlockSpec((D_MODEL, D_MODEL), const),
            pl.BlockSpec((D_MODEL, nq), const),
        ],
        out_specs=[
            pl.BlockSpec((TOK_TILE, D_MODEL), row),
            pl.BlockSpec((TOK_TILE, D_MODEL), row),
            pl.BlockSpec((nq // PEER_D_HALF, TOK_TILE, PEER_D_HALF), lambda i: (0, i, 0)),
        ],
        out_shape=[
            jax.ShapeDtypeStruct((T, D_MODEL), F32),
            jax.ShapeDtypeStruct((T, D_MODEL), BF16),
            jax.ShapeDtypeStruct((nq // PEER_D_HALF, T, PEER_D_HALF), BF16),
        ],
        compiler_params=_cparams(("parallel",)),
        name="postmix",
    )(yh, yn, gate, x2, g1, sh2, sc2, n2g, whyo, wnao, wout, wq)


SUBLANES = 8
ROUTE_TILE = 256


def _argmax_tree(vals, idxs):
    while len(vals) > 1:
        nv, ni = [], []
        for j in range(0, len(vals) - 1, 2):
            right = vals[j + 1] > vals[j]
            nv.append(jnp.where(right, vals[j + 1], vals[j]))
            ni.append(jnp.where(right, idxs[j + 1], idxs[j]))
        if len(vals) % 2:
            nv.append(vals[-1])
            ni.append(idxs[-1])
        vals, idxs = nv, ni
    return vals[0], idxs[0]


W_BLOCK = 128
W_PITCH = 136
ROUTE_SLICES = 8


def _route_w_kernel(pq_ref, keys_ref, wa_ref, wb_ref, tv_ref, ti_ref, sa_ref, sb_ref, sg_ref, s_ref):
    step = pl.program_id(0)
    cur = step % 2
    prev = 1 - cur
    rt = pq_ref.shape[1]
    n = PEER_N_KEYS
    n_grp = n // SUBLANES
    sub = lax.broadcasted_iota(jnp.int32, (SUBLANES, rt), 0)
    key_idx = [sub + SUBLANES * g for g in range(n_grp)]
    key_row = lax.broadcasted_iota(jnp.int32, (n, sa_ref.shape[2]), 0)

    @pl.when(step == 0)
    def _():
        sa_ref[...] = jnp.zeros_like(sa_ref)
        sb_ref[...] = jnp.zeros_like(sb_ref)
        sg_ref[...] = jnp.zeros_like(sg_ref)
        s_ref[1] = jnp.zeros(s_ref.shape[1:], F32)

    def first_level(hp):
        s = _dot_nt(keys_ref[hp], pq_ref[hp]).reshape(n_grp, SUBLANES, rt)
        slabs = [s[g] for g in range(n_grp)]
        h, p = hp // 2, hp % 2
        for j in range(PEER_TOPK):
            v, i = _argmax_tree(slabs, key_idx)
            for shift in (4, 2, 1):
                v2, i2 = pltpu.roll(v, shift, 0), pltpu.roll(i, shift, 0)
                take = (v2 > v) | ((v2 == v) & (i2 < i))
                v, i = jnp.where(take, v2, v), jnp.where(take, i2, i)
            tv_ref[p, j, pl.ds(h, 1), :] = v[0:1]
            ti_ref[p, j, pl.ds(h, 1), :] = i[0:1]
            slabs = [jnp.where(key_idx[g] == i, -jnp.inf, slabs[g]) for g in range(n_grp)]

    def token_weights(blk, t):
        row = pl.ds(blk * W_BLOCK + t, 1)
        ga = jnp.where(sa_ref[prev, row, :] == key_row, sg_ref[prev, row, :], 0.0).astype(BF16)
        ob = jnp.where(sb_ref[prev, row, :] == key_row, 1.0, 0.0).astype(BF16)
        s_ref[blk, pl.ds(pl.multiple_of(t * W_PITCH, SUBLANES), n), :] = _dot_nt(ga, ob)

    def emit(w_ref, blk, i1):
        w_ref[i1] = s_ref[blk, pl.ds(i1, W_BLOCK, stride=W_PITCH), :].astype(BF16)

    per_slice = W_BLOCK // (ROUTE_SLICES // 2)

    def make_slice(blk, w_ref, emit_blk):
        def body(k, carry):
            for d in range(2 * PEER_HEADS // ROUTE_SLICES):
                first_level((blk * (ROUTE_SLICES // 2) + k) * (2 * PEER_HEADS // ROUTE_SLICES) + d)
            for j in range(per_slice):
                token_weights(blk, k * per_slice + j)
            for j in range(per_slice):
                emit(w_ref, emit_blk, k * per_slice + j)
            return carry
        return body

    lax.fori_loop(0, ROUTE_SLICES // 2, make_slice(0, wb_ref, 1), 0)
    lax.fori_loop(0, ROUTE_SLICES // 2, make_slice(1, wa_ref, 0), 0)

    pairs = [(a, b) for a in range(PEER_TOPK) for b in range(PEER_TOPK) if (a + 1) * (b + 1) <= PEER_TOPK]
    cand = [tv_ref[0, a] + tv_ref[1, b] for a, b in pairs]
    pos = [jnp.full((PEER_HEADS, rt), a * PEER_TOPK + b, jnp.int32) for a, b in pairs]
    best, sel_a, sel_b = [], [], []
    for j in range(PEER_TOPK):
        v, w = _argmax_tree(cand, pos)
        best.append(v)
        ia = jnp.zeros((PEER_HEADS, rt), jnp.int32)
        ib = jnp.zeros((PEER_HEADS, rt), jnp.int32)
        for n, (a, b) in enumerate(pairs):
            hit = w == (a * PEER_TOPK + b)
            cand[n] = jnp.where(hit, -jnp.inf, cand[n])
            ia = jnp.where(hit, ti_ref[0, a], ia)
            ib = jnp.where(hit, ti_ref[1, b], ib)
        sel_a.append(ia)
        sel_b.append(ib)
    ex = [jnp.exp(v - best[0]) for v in best]
    denom = functools.reduce(lambda x, y: x + y, ex)
    gates = [e / denom for e in ex]
    sa_ref[cur] = jnp.concatenate(sel_a, axis=0).T
    sb_ref[cur] = jnp.concatenate(sel_b, axis=0).T
    sg_ref[cur] = jnp.concatenate(gates, axis=0).T


def _route_w(pq, keys_bf):
    n_hp, T, _ = pq.shape
    assert ROUTE_TILE == 2 * W_BLOCK
    n_tiles = T // ROUTE_TILE
    n_sel = PEER_HEADS * PEER_TOPK
    w_shape = jax.ShapeDtypeStruct((n_tiles, PEER_N_KEYS, W_BLOCK, PEER_N_KEYS), BF16)
    w_block = (None, PEER_N_KEYS, W_BLOCK, PEER_N_KEYS)
    clamp = lambda i: jnp.clip(i, 0, n_tiles - 1)
    return pl.pallas_call(
        _route_w_kernel,
        grid=(n_tiles + 2,),
        in_specs=[
            pl.BlockSpec((n_hp, ROUTE_TILE, PEER_D_HALF), lambda i: (0, clamp(i), 0)),
            pl.BlockSpec((n_hp, PEER_N_KEYS, PEER_D_HALF), lambda i: (0, 0, 0)),
        ],
        out_specs=[pl.BlockSpec(w_block, lambda i: (clamp(i - 1), 0, 0, 0)),
                   pl.BlockSpec(w_block, lambda i: (clamp(i - 2), 0, 0, 0))],
        out_shape=[w_shape, w_shape],
        scratch_shapes=[
            pltpu.VMEM((2, PEER_TOPK, PEER_HEADS, ROUTE_TILE), F32),
            pltpu.VMEM((2, PEER_TOPK, PEER_HEADS, ROUTE_TILE), jnp.int32),
            pltpu.VMEM((2, ROUTE_TILE, n_sel), jnp.int32),
            pltpu.VMEM((2, ROUTE_TILE, n_sel), jnp.int32),
            pltpu.VMEM((2, ROUTE_TILE, n_sel), F32),
            pltpu.VMEM((2, W_BLOCK * W_PITCH, PEER_N_KEYS), F32),
        ],
        compiler_params=_cparams(("arbitrary",)),
        name="route_w",
    )(pq, keys_bf)


LANES = 128
SORT_TILE = SUBLANES * LANES


def _batcher_sort_network(n):
    ces = []

    def merge(lo, m, r):
        step = r * 2
        if step < m:
            merge(lo, m, step)
            merge(lo + r, m, step)
            ces.extend((i, i + r) for i in range(lo + r, lo + m - r, step))
        else:
            ces.append((lo, lo + r))

    def sort(lo, m):
        if m > 1:
            sort(lo, m // 2)
            sort(lo + m // 2, m // 2)
            merge(lo, m, 1)

    sort(0, n)
    return ces


def _bitonic_merge_network(n):
    ces, s = [], n // 2
    while s >= 1:
        ces.extend((i, i + s) for i in range(n) if (i & s) == 0)
        s //= 2
    return ces


SORT_NET = _batcher_sort_network(PEER_TOPK)
MERGE_NET = _bitonic_merge_network(PEER_TOPK)


def _compare_exchange(v, x, net):
    for i, j in net:
        c = v[j] > v[i]
        v[i], v[j] = jnp.where(c, v[j], v[i]), jnp.where(c, v[i], v[j])
        x[i], x[j] = jnp.where(c, x[j], x[i]), jnp.where(c, x[i], x[j])


def _merge_top(a, b, dropped):
    (va, xa), (vb, xb) = a, b
    k = len(va)
    v, x = [], []
    for r in range(k):
        c = vb[k - 1 - r] > va[r]
        v.append(jnp.where(c, vb[k - 1 - r], va[r]))
        x.append(jnp.where(c, xb[k - 1 - r], xa[r]))
        dropped = jnp.maximum(dropped, jnp.where(c, va[r], vb[k - 1 - r]))
    _compare_exchange(v, x, MERGE_NET)
    return (v, x), dropped


def _merge_all(lists, dropped):
    while len(lists) > 1:
        nxt = []
        for j in range(0, len(lists) - 1, 2):
            m, dropped = _merge_top(lists[j], lists[j + 1], dropped)
            nxt.append(m)
        if len(lists) % 2:
            nxt.append(lists[-1])
        lists = nxt
    return lists[0], dropped


def _tie_risk(v, dropped):
    risk = v[-1] <= dropped
    for r in range(len(v) - 1):
        risk = risk | (v[r] <= v[r + 1])
    return risk.astype(jnp.int32)


def _level2_chains():
    pairs = [(a, b) for a in range(PEER_TOPK) for b in range(PEER_TOPK) if (a + 1) * (b + 1) <= PEER_TOPK]
    n_rows = 4
    chains = [[(a, b) for (a2, b) in pairs if a2 == a] for a in range(n_rows)]
    rest = [p for p in pairs if p[0] >= n_rows]
    for b in sorted({b for _, b in rest}):
        chains.append([(a, b2) for (a, b2) in rest if b2 == b])
    assert sorted(sum(chains, [])) == pairs
    return pairs, chains


def _route_kernel(pq_ref, keys_ref, a_ref, b_ref, g_ref, sx_ref, tv_ref, ti_ref, bs_ref, es_ref,
                  oa_ref, ob_ref, og_ref):
    n = PEER_N_KEYS
    k = PEER_TOPK
    shape = (SUBLANES, LANES)
    neg = jnp.full(shape, -jnp.inf, F32)
    pairs, chains = _level2_chains()

    def scores(hp):
        s = _dot_nt(keys_ref[hp], pq_ref[hp])
        for rg in range(n // SUBLANES):
            for lt in range(SUBLANES):
                sx_ref[pl.ds(rg * SUBLANES * SUBLANES + lt, SUBLANES, stride=SUBLANES), :] = (
                    s[rg * SUBLANES:(rg + 1) * SUBLANES, lt * LANES:(lt + 1) * LANES])
        return [sx_ref[pl.ds(key * SUBLANES, SUBLANES), :] for key in range(n)]

    def store_top(hp, v, x):
        for r in range(k):
            tv_ref[hp, r] = v[r]
            ti_ref[hp, r] = x[r]

    def level1_sorted(hp, risk):
        vals = scores(hp)
        lists = []
        for q in range(n // k):
            v = vals[q * k:(q + 1) * k]
            x = [jnp.full(shape, q * k + r, jnp.int32) for r in range(k)]
            _compare_exchange(v, x, SORT_NET)
            lists.append((v, x))
        (v, x), dropped = _merge_all(lists, neg)
        store_top(hp, v, x)
        return risk | _tie_risk(v, dropped)

    def level1_exact(hp, carry):
        vals = scores(hp)
        idx = [jnp.full(shape, key, jnp.int32) for key in range(n)]
        v, x = [], []
        for r in range(k):
            best, where_ = _argmax_tree(vals, idx)
            v.append(best)
            x.append(where_)
            vals = [jnp.where(where_ == key, -jnp.inf, vals[key]) for key in range(n)]
        store_top(hp, v, x)
        return carry

    def candidates(h):
        s1 = [tv_ref[2 * h, r] for r in range(k)]
        s2 = [tv_ref[2 * h + 1, r] for r in range(k)]
        e1 = [ti_ref[2 * h, r] * n for r in range(k)]
        e2 = [ti_ref[2 * h + 1, r] for r in range(k)]
        return (lambda a, b: s1[a] + s2[b]), (lambda a, b: e1[a] + e2[b])

    def store_best(h, v, x):
        for r in range(k):
            bs_ref[h, r] = v[r]
            es_ref[h, r] = x[r]

    def level2_sorted(h, risk):
        val, expert = candidates(h)
        lists = []
        for chain in chains:
            pad = k - len(chain)
            lists.append(([val(a, b) for a, b in chain] + [neg] * pad,
                          [expert(a, b) for a, b in chain] + [jnp.zeros(shape, jnp.int32)] * pad))
        (v, x), dropped = _merge_all(lists, neg)
        store_best(h, v, x)
        return risk | _tie_risk(v, dropped)

    def level2_exact(h, carry):
        val, expert = candidates(h)
        cand = [val(a, b) for a, b in pairs]
        exps = [expert(a, b) for a, b in pairs]
        pos = [jnp.full(shape, a * k + b, jnp.int32) for a, b in pairs]
        v, x = [], []
        for r in range(k):
            best, where_ = _argmax_tree(cand, pos)
            chosen = jnp.zeros(shape, jnp.int32)
            for m, (a, b) in enumerate(pairs):
                hit = where_ == (a * k + b)
                cand[m] = jnp.where(hit, -jnp.inf, cand[m])
                chosen = jnp.where(hit, exps[m], chosen)
            v.append(best)
            x.append(chosen)
        store_best(h, v, x)
        return carry

    risk = lax.fori_loop(0, 2 * PEER_HEADS, level1_sorted, jnp.zeros(shape, jnp.int32))
    risk = lax.fori_loop(0, PEER_HEADS, level2_sorted, risk)

    @pl.when(jnp.max(risk) > 0)
    def _():
        lax.fori_loop(0, 2 * PEER_HEADS, level1_exact, 0)
        lax.fori_loop(0, PEER_HEADS, level2_exact, 0)

    for h in range(PEER_HEADS):
        best = [bs_ref[h, r] for r in range(k)]
        ex = [jnp.exp(v - best[0]) for v in best]
        denom = functools.reduce(lambda p, q: p + q, ex)
        for r in range(k):
            rows = pl.ds((h * k + r) * SUBLANES, SUBLANES)
            e = es_ref[h, r]
            oa_ref[rows, :] = jnp.right_shift(e, n.bit_length() - 1)
            ob_ref[rows, :] = jnp.bitwise_and(e, n - 1)
            og_ref[rows, :] = ex[r] / denom
    for s in range(SUBLANES):
        rows = pl.ds(s * LANES, LANES)
        plane = pl.ds(s, PEER_HEADS * k, stride=SUBLANES)
        a_ref[rows, :] = oa_ref[plane, :].T
        b_ref[rows, :] = ob_ref[plane, :].T
        g_ref[rows, :] = og_ref[plane, :].T


def _route(pq, keys_bf):
    n_hp, T, _ = pq.shape
    n_sel = PEER_HEADS * PEER_TOPK
    row = lambda i: (i, 0)
    reg = (SUBLANES, LANES)
    return pl.pallas_call(
        _route_kernel,
        grid=(T // SORT_TILE,),
        in_specs=[
            pl.BlockSpec((n_hp, SORT_TILE, PEER_D_HALF), lambda i: (0, i, 0)),
            pl.BlockSpec((n_hp, PEER_N_KEYS, PEER_D_HALF), lambda i: (0, 0, 0)),
        ],
        out_specs=[pl.BlockSpec((SORT_TILE, n_sel), row)] * 3,
        out_shape=[
            jax.ShapeDtypeStruct((T, n_sel), jnp.int32),
            jax.ShapeDtypeStruct((T, n_sel), jnp.int32),
            jax.ShapeDtypeStruct((T, n_sel), F32),
        ],
        scratch_shapes=[
            pltpu.VMEM((PEER_N_KEYS * SUBLANES, LANES), F32),
            pltpu.VMEM((n_hp, PEER_TOPK) + reg, F32),
            pltpu.VMEM((n_hp, PEER_TOPK) + reg, jnp.int32),
            pltpu.VMEM((PEER_HEADS, PEER_TOPK) + reg, F32),
            pltpu.VMEM((PEER_HEADS, PEER_TOPK) + reg, jnp.int32),
            pltpu.VMEM((n_sel * SUBLANES, LANES), jnp.int32),
            pltpu.VMEM((n_sel * SUBLANES, LANES), jnp.int32),
            pltpu.VMEM((n_sel * SUBLANES, LANES), F32),
        ],
        compiler_params=_cparams(("parallel",)),
        name="route",
    )(pq, keys_bf)


WBUILD_UNROLL = 32


def _wbuild_kernel(a_ref, b_ref, g_ref, w_ref, s_ref):
    n = PEER_N_KEYS
    tb = a_ref.shape[0]
    key = lax.broadcasted_iota(jnp.int32, (n, a_ref.shape[1]), 0)

    def body(t, carry):
        a = a_ref[pl.ds(t, 1), :]
        b = b_ref[pl.ds(t, 1), :]
        g = g_ref[pl.ds(t, 1), :]
        ga = jnp.where(a == key, g, 0.0).astype(BF16)
        ob = jnp.where(b == key, 1.0, 0.0).astype(BF16)
        s_ref[pl.ds(pl.multiple_of(t * W_PITCH, SUBLANES), n), :] = _dot_nt(ga, ob)
        return carry

    lax.fori_loop(0, tb, body, 0, unroll=WBUILD_UNROLL)

    def emit(i1, carry):
        w_ref[i1] = s_ref[pl.ds(i1, tb, stride=W_PITCH), :].astype(BF16)
        return carry

    lax.fori_loop(0, n, emit, 0, unroll=WBUILD_UNROLL)


def _wbuild(a, b, g):
    T, n_sel = a.shape
    row = lambda i: (i, 0)
    return pl.pallas_call(
        _wbuild_kernel,
        grid=(T // W_BLOCK,),
        in_specs=[pl.BlockSpec((W_BLOCK, n_sel), row)] * 3,
        out_specs=pl.BlockSpec((None, PEER_N_KEYS, W_BLOCK, PEER_N_KEYS), lambda i: (i, 0, 0, 0)),
        out_shape=jax.ShapeDtypeStruct((T // W_BLOCK, PEER_N_KEYS, W_BLOCK, PEER_N_KEYS), BF16),
        scratch_shapes=[pltpu.VMEM((W_BLOCK * W_PITCH, PEER_N_KEYS), F32)],
        compiler_params=_cparams(("parallel",)),
        name="wbuild",
    )(a, b, g)


def _peer_dense_kernel(h2_ref, w_ref, u_ref, v_ref, x1_ref, g2_ref, o_ref, acc_ref):
    e = pl.program_id(1)

    @pl.when(e == 0)
    def _():
        acc_ref[...] = jnp.zeros_like(acc_ref)

    a = _dot_nt(h2_ref[...], u_ref[...])
    act = 0.5 * a * (1.0 + lax.erf(a * (2.0 ** -0.5)))
    w = jnp.concatenate(
        [jnp.concatenate([w_ref[blk, i] for i in range(w_ref.shape[1])], axis=1) for blk in range(w_ref.shape[0])],
        axis=0)
    acc_ref[...] += _dot((w.astype(F32) * act).astype(BF16), v_ref[...])

    @pl.when(e == pl.num_programs(1) - 1)
    def _():
        o_ref[...] = x1_ref[...] + g2_ref[...] * acc_ref[...]


def _peer_dense(h2, w, u_bf, v_bf, x1, g2, seq):
    T = h2.shape[0]
    tiles_per_batch = seq // PEER_TOK_TILE
    n_exp = u_bf.shape[0]
    return pl.pallas_call(
        _peer_dense_kernel,
        grid=(T // PEER_TOK_TILE, n_exp // PEER_EXP_TILE),
        in_specs=[
            pl.BlockSpec((PEER_TOK_TILE, D_MODEL), lambda i, e: (i, 0)),
            pl.BlockSpec((PEER_TOK_TILE // W_BLOCK, PEER_EXP_TILE // PEER_N_KEYS, W_BLOCK, PEER_N_KEYS),
                         lambda i, e: (i, e, 0, 0)),
            pl.BlockSpec((PEER_EXP_TILE, D_MODEL), lambda i, e: (e, 0)),
            pl.BlockSpec((PEER_EXP_TILE, D_MODEL), lambda i, e: (e, 0)),
            pl.BlockSpec((PEER_TOK_TILE, D_MODEL), lambda i, e: (i, 0)),
            pl.BlockSpec((None, 1, D_MODEL), lambda i, e: (i // tiles_per_batch, 0, 0)),
        ],
        out_specs=pl.BlockSpec((PEER_TOK_TILE, D_MODEL), lambda i, e: (i, 0)),
        out_shape=jax.ShapeDtypeStruct((T, D_MODEL), F32),
        scratch_shapes=[pltpu.VMEM((PEER_TOK_TILE, D_MODEL), F32)],
        compiler_params=_cparams(("parallel", "arbitrary")),
        name="peer_dense",
    )(h2, w, u_bf, v_bf, x1, g2)


HI = lax.Precision.HIGHEST


HY_LANES = 128


def _shortconv_kernel(z_ref, w_ref, b_ref, o_ref):
    z = z_ref[...]
    n = z.shape[0]
    row = lax.broadcasted_iota(jnp.int32, z.shape, 0)
    prev = jnp.where(row == 0, 0.0, pltpu.roll(z, 1, 0))
    nxt = jnp.where(row == n - 1, 0.0, pltpu.roll(z, n - 1, 0))
    o_ref[...] = w_ref[0:1] * prev + w_ref[1:2] * z + w_ref[2:3] * nxt + b_ref[...]


def _shortconv(zh, w, b):
    B, L, C = zh.shape
    return pl.pallas_call(
        _shortconv_kernel,
        grid=(B, C // HY_LANES),
        in_specs=[
            pl.BlockSpec((None, L, HY_LANES), lambda i, j: (i, 0, j)),
            pl.BlockSpec((3, HY_LANES), lambda i, j: (0, j)),
            pl.BlockSpec((1, HY_LANES), lambda i, j: (0, j)),
        ],
        out_specs=pl.BlockSpec((None, L, HY_LANES), lambda i, j: (i, 0, j)),
        out_shape=jax.ShapeDtypeStruct((B, L, C), F32),
        compiler_params=_cparams(("parallel", "parallel")),
        name="shortconv",
    )(zh, w, b)


FFT_R = 128
FFT_N = FFT_R * FFT_R
FFT_PITCH = 136
FFT_K1_CHUNK = 16
FFT_UNROLL = 8


def _fft_tables():
    r = jnp.arange(FFT_R, dtype=jnp.int32)
    n2, k1, n1 = r[:, None, None], r[None, :, None], r[None, None, :]
    ang = (2.0 * math.pi / FFT_N) * ((k1 * (FFT_R * n1 + n2)) % FFT_N).astype(F32)
    c, s = jnp.cos(ang), jnp.sin(ang)
    g_full = jnp.concatenate([c, -s], axis=1)
    h_half = jnp.concatenate([jnp.swapaxes(c, 1, 2), -jnp.swapaxes(s, 1, 2)], axis=2)[:, :FFT_R // 2] / FFT_N
    ang2 = (2.0 * math.pi / FFT_R) * ((r[:, None] * r[None, :]) % FFT_R).astype(F32)
    c2, s2 = jnp.cos(ang2), jnp.sin(ang2)
    f_fwd = jnp.concatenate([jnp.concatenate([c2, s2], 1), jnp.concatenate([-s2, c2], 1)], 0)
    f_inv = jnp.concatenate([jnp.concatenate([c2, -s2], 1), jnp.concatenate([s2, c2], 1)], 0)
    return g_full.astype(BF16), h_half.astype(BF16), f_fwd.astype(BF16), f_inv.astype(BF16)


def _fft_stage1(load_slab, g_ref, bre_ref, bim_ref):
    def body(n2, carry):
        out = _dot(g_ref[n2], load_slab(n2).astype(BF16))
        bre_ref[pl.ds(n2, FFT_R, stride=FFT_PITCH), :] = out[:FFT_R]
        bim_ref[pl.ds(n2, FFT_R, stride=FFT_PITCH), :] = out[FFT_R:]
        return carry

    lax.fori_loop(0, FFT_R, body, 0, unroll=FFT_UNROLL)


def _k1_rows(k1):
    return pl.ds(pl.multiple_of(k1 * FFT_PITCH, SUBLANES), FFT_R)


def _load_k1_pair(bre_ref, bim_ref, k1):
    return jnp.concatenate(
        [jnp.concatenate([bre_ref[_k1_rows(k1 + d), :], bim_ref[_k1_rows(k1 + d), :]], axis=0) for d in range(2)],
        axis=1)


FILT_ROWS = 512
FILT_FEAT = 128


def _filt_kernel(feat_ref, delta_ref, w1_ref, b1_ref, w2_ref, b2_ref, w3_ref, b3_ref, w4_ref, freq_ref,
                 h_ref, sum_ref):
    @pl.when(pl.program_id(0) == 0)
    def _():
        sum_ref[...] = jnp.zeros_like(sum_ref)

    feat = feat_ref[...]
    freq = freq_ref[...]
    dot = lambda a, b: jnp.dot(a, b, precision=HI, preferred_element_type=F32)
    h = jnp.sin(freq * (dot(feat, w1_ref[...]) + b1_ref[...]))
    h = jnp.sin(freq * (dot(h, w2_ref[...]) + b2_ref[...]))
    h = jnp.sin(freq * (dot(h, w3_ref[...]) + b3_ref[...]))
    h = dot(h, w4_ref[...])
    n_feat = 1 + 2 * HY_BANDS
    t = feat[:, 0:1]
    forward = feat[:, n_feat:n_feat + 1] > 0.5
    keep = feat[:, n_feat + 1:n_feat + 2]
    decay = jnp.exp(-t * delta_ref[...])
    outs = []
    for o in range(HY_ORDER):
        base = o * 2 * D_HY
        outs.append(jnp.where(forward, h[:, base:base + D_HY], h[:, base + D_HY:base + 2 * D_HY]) * decay)
    out = jnp.concatenate(outs, axis=1)
    sum_ref[...] += jnp.sum(jnp.abs(out), axis=0, keepdims=True)
    h_ref[...] = out * keep


def _filt_features(L):
    r = np.arange(FFT_N)
    n = FFT_R * (r % FFT_R) + r // FFT_R
    j = np.where(n < L, n, np.where(n > L, 2 * L - n, 0))
    jj = jnp.asarray(j, dtype=F32)[:, None]
    t = jj / (L - 1.0)
    w = (2.0 * math.pi / L) * jj
    bands = jnp.linspace(1e-4, HY_BANDS - 1.0, HY_BANDS, dtype=F32)[None, :]
    flags = jnp.asarray(np.stack([n < L, n != L], axis=1), dtype=F32)
    feats = jnp.concatenate([t, jnp.cos(bands * w), -jnp.sin(bands * w), flags], axis=-1)
    return jnp.pad(feats, ((0, 0), (0, FILT_FEAT - feats.shape[1])))


def _filters(p, L):
    feats = _filt_features(L)
    deltas = jnp.abs(jnp.linspace(math.log(HY_DECAY_TARGET) / HY_SLOW_PCT,
                                  math.log(HY_DECAY_TARGET) / HY_FAST_PCT, D_HY, dtype=F32))[None, :]
    w1 = jnp.pad(p['hy_f1_w'], ((0, FILT_FEAT - p['hy_f1_w'].shape[0]), (0, 0)))
    n_out = HY_ORDER * D_HY
    const = lambda i: (0, 0)
    full = lambda a: pl.BlockSpec(a.shape, const)
    args = [deltas, w1, p['hy_f1_b'][None], p['hy_f2_w'], p['hy_f2_b'][None], p['hy_f3_w'], p['hy_f3_b'][None],
            p['hy_f4_w'], p['hy_sin_freq'][None]]
    return pl.pallas_call(
        _filt_kernel,
        grid=(FFT_N // FILT_ROWS,),
        in_specs=[pl.BlockSpec((FILT_ROWS, FILT_FEAT), lambda i: (i, 0))] + [full(a) for a in args],
        out_specs=[pl.BlockSpec((FILT_ROWS, n_out), lambda i: (i, 0)), pl.BlockSpec((1, n_out), const)],
        out_shape=[jax.ShapeDtypeStruct((FFT_N, n_out), F32), jax.ShapeDtypeStruct((1, n_out), F32)],
        compiler_params=_cparams(("arbitrary",)),
        name="hyena_filter",
    )(feats, *args)


def _fspec_kernel(h_ref, sum_ref, g_ref, f_ref, o_ref, bre_ref, bim_ref):
    c = pl.program_id(1)

    @pl.when(c == 0)
    def _():
        _fft_stage1(lambda n2: h_ref[pl.ds(pl.multiple_of(n2 * FFT_R, FFT_R), FFT_R), :], g_ref, bre_ref, bim_ref)

    inv = 1.0 / sum_ref[...]

    lanes = h_ref.shape[1]

    def slab_pair(j, carry):
        x = _dot(f_ref[...], _load_k1_pair(bre_ref, bim_ref, c * FFT_K1_CHUNK + 2 * j).astype(BF16))
        for d in range(2):
            rows = pl.ds(pl.multiple_of((2 * j + d) * FFT_R, FFT_R), FFT_R)
            o_ref[0, rows, :] = x[:FFT_R, d * lanes:(d + 1) * lanes] * inv
            o_ref[1, rows, :] = x[FFT_R:, d * lanes:(d + 1) * lanes] * inv
        return carry

    lax.fori_loop(0, FFT_K1_CHUNK // 2, slab_pair, 0, unroll=FFT_UNROLL // 2)


def _filter_spectrum(h, hsum, g_full, f_fwd):
    n_ch = h.shape[1]
    single = pl.Buffered(1)
    return pl.pallas_call(
        _fspec_kernel,
        grid=(n_ch // HY_LANES, FFT_R // FFT_K1_CHUNK),
        in_specs=[
            pl.BlockSpec((FFT_N, HY_LANES), lambda j, c: (0, j), pipeline_mode=single),
            pl.BlockSpec((1, HY_LANES), lambda j, c: (0, j)),
            pl.BlockSpec(g_full.shape, lambda j, c: (0, 0, 0), pipeline_mode=single),
            pl.BlockSpec(f_fwd.shape, lambda j, c: (0, 0)),
        ],
        out_specs=pl.BlockSpec((2, FFT_K1_CHUNK * FFT_R, HY_LANES), lambda j, c: (0, c, j)),
        out_shape=jax.ShapeDtypeStruct((2, FFT_N, n_ch), F32),
        scratch_shapes=[pltpu.VMEM((FFT_R * FFT_PITCH, HY_LANES), F32)] * 2,
        compiler_params=_cparams(("parallel", "arbitrary")),
        name="hyena_filter_fft",
    )(h, hsum, g_full, f_fwd)


def _hyconv_kernel(u_ref, gate_ref, kf_ref, skip_ref, g_ref, h_ref, f_ref, fi_ref, o_ref, bre_ref, bim_ref):
    c = pl.program_id(2)
    half = FFT_R // 2

    def rows_of(n2):
        return pl.ds(pl.multiple_of(n2 * half, half), half)

    @pl.when(c == 0)
    def _():
        _fft_stage1(lambda n2: u_ref[rows_of(n2), :], g_ref, bre_ref, bim_ref)

    lanes = u_ref.shape[1]

    def slab_pair(j, carry):
        k1 = c * FFT_K1_CHUNK + 2 * j
        x = _dot(f_ref[...], _load_k1_pair(bre_ref, bim_ref, k1).astype(BF16))
        xr, xi = x[:FFT_R], x[FFT_R:]
        kr, ki = [jnp.concatenate([kf_ref[part, pl.ds(pl.multiple_of((2 * j + d) * FFT_R, FFT_R), FFT_R), :]
                                   for d in range(2)], axis=1) for part in range(2)]
        y = jnp.concatenate([xr * kr - xi * ki, xr * ki + xi * kr], axis=0)
        z = _dot(fi_ref[...], y.astype(BF16))
        for d in range(2):
            bre_ref[_k1_rows(k1 + d), :] = z[:FFT_R, d * lanes:(d + 1) * lanes]
            bim_ref[_k1_rows(k1 + d), :] = z[FFT_R:, d * lanes:(d + 1) * lanes]
        return carry

    lax.fori_loop(0, FFT_K1_CHUNK // 2, slab_pair, 0, unroll=FFT_UNROLL // 2)

    @pl.when(c == pl.num_programs(2) - 1)
    def _():
        def body(n2, carry):
            z = jnp.concatenate([bre_ref[pl.ds(n2, FFT_R, stride=FFT_PITCH), :],
                                 bim_ref[pl.ds(n2, FFT_R, stride=FFT_PITCH), :]], axis=0)
            conv = _dot(h_ref[n2], z.astype(BF16))
            rows = rows_of(n2)
            o_ref[rows, :] = gate_ref[rows, :] * (conv + skip_ref[...] * u_ref[rows, :])
            return carry

        lax.fori_loop(0, FFT_R, body, 0, unroll=FFT_UNROLL)


def _hyconv(u, u_col, gate, gate_col, kf, kf_col, skip, tables):
    g_half, h_half, f_fwd, f_inv = tables
    B, L, _ = u.shape
    single = pl.Buffered(1)
    return pl.pallas_call(
        _hyconv_kernel,
        grid=(B, D_HY // HY_LANES, FFT_R // FFT_K1_CHUNK),
        in_specs=[
            pl.BlockSpec((None, L, HY_LANES), lambda b, j, c: (b, 0, u_col + j), pipeline_mode=single),
            pl.BlockSpec((None, L, HY_LANES), lambda b, j, c: (b, 0, gate_col + j), pipeline_mode=single),
            pl.BlockSpec((2, FFT_K1_CHUNK * FFT_R, HY_LANES), lambda b, j, c: (0, c, kf_col + j)),
            pl.BlockSpec((1, HY_LANES), lambda b, j, c: (0, j)),
            pl.BlockSpec(g_half.shape, lambda b, j, c: (0, 0, 0), pipeline_mode=single),
            pl.BlockSpec(h_half.shape, lambda b, j, c: (0, 0, 0), pipeline_mode=single),
            pl.BlockSpec(f_fwd.shape, lambda b, j, c: (0, 0)),
            pl.BlockSpec(f_inv.shape, lambda b, j, c: (0, 0)),
        ],
        out_specs=pl.BlockSpec((None, L, HY_LANES), lambda b, j, c: (b, 0, j)),
        out_shape=jax.ShapeDtypeStruct((B, L, D_HY), F32),
        scratch_shapes=[pltpu.VMEM((FFT_R * FFT_PITCH, HY_LANES), F32)] * 2,
        compiler_params=_cparams(("parallel", "parallel", "arbitrary")),
        name="hyena_conv",
    )(u, gate, kf, skip, g_half, h_half, f_fwd, f_inv)


def _slab_major(a, n1):
    B, L, C = a.shape
    return a.reshape(B, n1, L // n1, C).transpose(0, 2, 1, 3).reshape(B, L, C)


def _hyena(zh, p):
    B, L, _ = zh.shape
    assert 2 * L == FFT_N
    g_full, h_half, f_fwd, f_inv = _fft_tables()
    tables = (g_full[:, :, :FFT_R // 2], h_half, f_fwd, f_inv)
    filt, filt_sum = _filters(p, L)
    both_dirs = filt_sum[:, :]
    kf = _filter_spectrum(filt, both_dirs, g_full, f_fwd)
    zc = _slab_major(_shortconv(zh, p['hy_conv_w'], p['hy_conv_b'][None]), FFT_R // 2)
    lanes = D_HY // HY_LANES
    y = _hyconv(zc, 0, zc, lanes, kf, 0, p['hy_skip'][0:1], tables)
    y = _hyconv(y, 0, zc, 2 * lanes, kf, lanes, p['hy_skip'][1:2], tables)
    return _slab_major(y, L // (FFT_R // 2))


def kernel(x, c, ctx, c_ctx, norm1_g, norm2_g, w_ada, b_ada, w_in, b_in, hy_conv_w, hy_conv_b, hy_f1_w, hy_f1_b, hy_f2_w, hy_f2_b, hy_f3_w, hy_f3_b, hy_f4_w, hy_sin_freq, hy_skip, q_norm_g, k_norm_g, na_rpb, w_hy_out, w_na_out, w_out, peer_w_q, peer_keys, peer_u, peer_v):
    assert w_in.shape[0] == 1, "single-layer block"
    B, S, D = x.shape
    C = ctx.shape[1]
    T = B * S

    m_lat = (jax.nn.silu(c) @ w_ada[0] + b_ada[0]).reshape(B, N_MOD, 1, D)
    m_ctx = (jax.nn.silu(c_ctx) @ w_ada[0] + b_ada[0]).reshape(N_MOD, 1, D)
    sh1, sc1, g1, sh2, sc2, g2 = [m_lat[:, i] for i in range(N_MOD)]

    w_in_bf = w_in[0].astype(BF16)
    b_in2 = b_in[0][None, :]
    gain1 = norm1_g[0][None, :]
    qg = jnp.tile(q_norm_g[0], NA_HEADS)[None, :]
    kg = jnp.tile(k_norm_g[0], NA_HEADS)[None, :]
    seg = jnp.asarray(np.kron(np.eye(NA_HEADS), np.full((HEAD_DIM, HEAD_DIM), 1.0 / HEAD_DIM)), dtype=BF16)
    cos, sin = _rope_tables(S)

    x2 = x.reshape(T, D)
    zh, q, k, v, gate = _inproj(x2, sh1, sc1, gain1, w_in_bf, b_in2, cos, sin, qg, kg, seg, S)

    k_ctx, v_ctx = _ctxkv(ctx.reshape(B * C, D), m_ctx[0], m_ctx[1], gain1,
                          w_in_bf[:, COL_K:COL_V], b_in2[:, COL_K:COL_V],
                          w_in_bf[:, COL_V:COL_G_HY], b_in2[:, COL_V:COL_G_HY], kg, seg)

    p = {'hy_conv_w': hy_conv_w[0], 'hy_conv_b': hy_conv_b[0], 'hy_f1_w': hy_f1_w[0], 'hy_f1_b': hy_f1_b[0],
         'hy_f2_w': hy_f2_w[0], 'hy_f2_b': hy_f2_b[0], 'hy_f3_w': hy_f3_w[0], 'hy_f3_b': hy_f3_b[0],
         'hy_f4_w': hy_f4_w[0], 'hy_sin_freq': hy_sin_freq[0], 'hy_skip': hy_skip[0]}
    y_hy = _hyena(zh.reshape(B, S, 3 * D_HY), p).reshape(T, D_HY)

    y_na = _natten(q.reshape(B, S, D_NA), k.reshape(B, S, D_NA), v.reshape(B, S, D_NA),
                   k_ctx.reshape(B, C, D_NA), v_ctx.reshape(B, C, D_NA), _natten_bias(na_rpb[0], S // GRID_W)).reshape(T, D_NA)

    x1, h2, pq = _postmix(y_hy, y_na, gate, x2, g1, sh2, sc2, norm2_g[0][None, :],
                          w_hy_out[0].astype(BF16), w_na_out[0].astype(BF16), w_out[0].astype(BF16),
                          peer_w_q[0].astype(BF16), S)

    keys_bf = peer_keys[0].astype(BF16).reshape(2 * PEER_HEADS, PEER_N_KEYS, PEER_D_HALF)
    sel_a, sel_b, sel_g = _route(pq, keys_bf)
    w = _wbuild(sel_a, sel_b, sel_g)
    out = _peer_dense(h2, w, peer_u[0].astype(BF16), peer_v[0].astype(BF16), x1, g2, S)
    return out.reshape(B, S, D)
```

```python
import functools
import math

import numpy as np
import jax
import jax.numpy as jnp
from jax import lax
from jax.experimental import pallas as pl
from jax.experimental.pallas import tpu as pltpu

F32 = jnp.float32
BF16 = jnp.bfloat16

D_MODEL = 1024
GRID_W = 64
EPS = 1e-6
N_MOD = 6

D_HY = 512
HY_ORDER = 2
HY_BANDS = 16
HY_DECAY_TARGET = 1e-2
HY_FAST_PCT = 0.3
HY_SLOW_PCT = 1.5

NA_HEADS = 8
HEAD_DIM = 64
D_NA = NA_HEADS * HEAD_DIM
WIN_ROWS = 8
WIN_COLS = 16
ROPE_THETA = 10000.0
NEG_INF = -1e30

PEER_HEADS = 8
PEER_N_KEYS = 128
PEER_N_EXPERTS = PEER_N_KEYS * PEER_N_KEYS
PEER_TOPK = 16
PEER_D_KEY = 256
PEER_D_HALF = PEER_D_KEY // 2

COL_HY = 0
COL_Q = COL_HY + 3 * D_HY
COL_K = COL_Q + D_NA
COL_V = COL_K + D_NA
COL_G_HY = COL_V + D_NA
N_PROJ = COL_G_HY + 2 * D_MODEL

VMEM_LIMIT = 56 * 1024 * 1024
TOK_TILE = 256
PEER_TOK_TILE = 512
PEER_EXP_TILE = 2048


def _cparams(sem):
    return pltpu.CompilerParams(dimension_semantics=sem, vmem_limit_bytes=VMEM_LIMIT)


def _dot(a, b):
    return jnp.dot(a, b, preferred_element_type=F32)


def _dot_nt(a, b):
    return lax.dot_general(a, b, (((1,), (1,)), ((), ())), preferred_element_type=F32)


def _modulated_norm(x, gain, shift, scale):
    ms = jnp.mean(x * x, axis=-1, keepdims=True)
    return (x * lax.rsqrt(ms + EPS) * gain) * (1.0 + scale) + shift


def _head_rms(z, seg, gain):
    zz = z * z
    hi = zz.astype(BF16)
    lo = (zz - hi.astype(F32)).astype(BF16)
    ms = _dot(hi, seg) + _dot(lo, seg)
    return z * lax.rsqrt(ms + EPS) * gain


def _rope(z, cos, sin_signed):
    n = z.shape[-1]
    lane = lax.broadcasted_iota(jnp.int32, z.shape, 1)
    first = (lane // (HEAD_DIM // 4)) % 2 == 0
    partner = jnp.where(first, pltpu.roll(z, n - HEAD_DIM // 4, 1), pltpu.roll(z, HEAD_DIM // 4, 1))
    return z * cos + partner * sin_signed


def _inproj_kernel(x_ref, shift_ref, scale_ref, g_ref, w_ref, b_ref, cos_ref, sin_ref, qg_ref, kg_ref,
                   seg_ref, zh_ref, q_ref, k_ref, v_ref, gate_ref):
    h = _modulated_norm(x_ref[...], g_ref[...], shift_ref[...], scale_ref[...]).astype(BF16)

    def proj(lo, hi):
        return _dot(h, w_ref[:, lo:hi]) + b_ref[:, lo:hi]

    zh_ref[...] = proj(COL_HY, COL_Q)
    cos, sin = cos_ref[...], sin_ref[...]
    seg = seg_ref[...]
    q = _rope(_head_rms(proj(COL_Q, COL_K), seg, qg_ref[...]), cos, sin)
    q_ref[...] = q.astype(BF16)
    k = _rope(_head_rms(proj(COL_K, COL_V), seg, kg_ref[...]), cos, sin)
    k_ref[...] = k.astype(BF16)
    v_ref[...] = proj(COL_V, COL_G_HY).astype(BF16)
    gate_ref[...] = jax.nn.sigmoid(proj(COL_G_HY, N_PROJ)).astype(BF16)


def _inproj(x2, shift, scale, gain, w_bf, b_in, cos, sin, qg, kg, seg, seq):
    T = x2.shape[0]
    tiles_per_batch = seq // TOK_TILE
    row = lambda i: (i, 0)
    per_batch = lambda i: (i // tiles_per_batch, 0, 0)
    const = lambda i: (0, 0)
    pos = lambda i: (i % tiles_per_batch, 0)
    return pl.pallas_call(
        _inproj_kernel,
        grid=(T // TOK_TILE,),
        in_specs=[
            pl.BlockSpec((TOK_TILE, D_MODEL), row),
            pl.BlockSpec((None, 1, D_MODEL), per_batch),
            pl.BlockSpec((None, 1, D_MODEL), per_batch),
            pl.BlockSpec((1, D_MODEL), const),
            pl.BlockSpec((D_MODEL, N_PROJ), const),
            pl.BlockSpec((1, N_PROJ), const),
            pl.BlockSpec((TOK_TILE, D_NA), pos),
            pl.BlockSpec((TOK_TILE, D_NA), pos),
            pl.BlockSpec((1, D_NA), const),
            pl.BlockSpec((1, D_NA), const),
            pl.BlockSpec((D_NA, D_NA), const),
        ],
        out_specs=[
            pl.BlockSpec((TOK_TILE, 3 * D_HY), row),
            pl.BlockSpec((TOK_TILE, D_NA), row),
            pl.BlockSpec((TOK_TILE, D_NA), row),
            pl.BlockSpec((TOK_TILE, D_NA), row),
            pl.BlockSpec((TOK_TILE, 2 * D_MODEL), row),
        ],
        out_shape=[
            jax.ShapeDtypeStruct((T, 3 * D_HY), F32),
            jax.ShapeDtypeStruct((T, D_NA), BF16),
            jax.ShapeDtypeStruct((T, D_NA), BF16),
            jax.ShapeDtypeStruct((T, D_NA), BF16),
            jax.ShapeDtypeStruct((T, 2 * D_MODEL), BF16),
        ],
        compiler_params=_cparams(("parallel",)),
        name="inproj",
    )(x2, shift, scale, gain, w_bf, b_in, cos, sin, qg, kg, seg)


def _ctxkv_kernel(x_ref, shift_ref, scale_ref, g_ref, wk_ref, bk_ref, wv_ref, bv_ref, kg_ref, seg_ref,
                  k_ref, v_ref):
    h = _modulated_norm(x_ref[...], g_ref[...], shift_ref[...], scale_ref[...]).astype(BF16)
    k = _head_rms(_dot(h, wk_ref[...]) + bk_ref[...], seg_ref[...], kg_ref[...])
    k_ref[...] = k.astype(BF16)
    v_ref[...] = (_dot(h, wv_ref[...]) + bv_ref[...]).astype(BF16)


def _ctxkv(ctx2, shift, scale, gain, wk, bk, wv, bv, kg, seg):
    T = ctx2.shape[0]
    row = lambda i: (i, 0)
    const = lambda i: (0, 0)
    return pl.pallas_call(
        _ctxkv_kernel,
        grid=(T // TOK_TILE,),
        in_specs=[
            pl.BlockSpec((TOK_TILE, D_MODEL), row),
            pl.BlockSpec((1, D_MODEL), const),
            pl.BlockSpec((1, D_MODEL), const),
            pl.BlockSpec((1, D_MODEL), const),
            pl.BlockSpec((D_MODEL, D_NA), const),
            pl.BlockSpec((1, D_NA), const),
            pl.BlockSpec((D_MODEL, D_NA), const),
            pl.BlockSpec((1, D_NA), const),
            pl.BlockSpec((1, D_NA), const),
            pl.BlockSpec((D_NA, D_NA), const),
        ],
        out_specs=[pl.BlockSpec((TOK_TILE, D_NA), row), pl.BlockSpec((TOK_TILE, D_NA), row)],
        out_shape=[jax.ShapeDtypeStruct((T, D_NA), BF16), jax.ShapeDtypeStruct((T, D_NA), BF16)],
        compiler_params=_cparams(("parallel",)),
        name="ctxkv",
    )(ctx2, shift, scale, gain, wk, bk, wv, bv, kg, seg)


NA_BLOCK_ROWS = 4
NA_KEY_ROWS = WIN_ROWS + NA_BLOCK_ROWS
NA_CLASSES = 3


def _natten_kernel(q_ref, k_ref, v_ref, kc_ref, vc_ref, bias_ref, o_ref, *, rows):
    r0 = pl.program_id(1) * NA_BLOCK_ROWS
    key_start = jnp.clip(r0 - WIN_ROWS // 2, 0, rows - NA_KEY_ROWS)
    start = pl.multiple_of(key_start * GRID_W, GRID_W * NA_BLOCK_ROWS)
    n_q = NA_BLOCK_ROWS * GRID_W
    n_loc = NA_KEY_ROWS * GRID_W
    scale = jnp.asarray(HEAD_DIM ** -0.5, BF16)
    pair = 2 * HEAD_DIM
    lane = lax.broadcasted_iota(jnp.int32, (n_q, pair), 1)
    for p in range(NA_HEADS // 2):
        cols = slice(p * pair, (p + 1) * pair)
        qp = q_ref[:, cols] * scale
        kp = k_ref[pl.ds(start, n_loc), cols]
        vp = v_ref[pl.ds(start, n_loc), cols]
        kcp = kc_ref[:, cols]
        vcp = vc_ref[:, cols]
        outs = []
        for hh in range(2):
            qm = jnp.where((lane // HEAD_DIM) == hh, qp, jnp.zeros_like(qp))
            s_loc = _dot_nt(qm, kp) + bias_ref[2 * p + hh]
            s_ctx = _dot_nt(qm, kcp)
            m = jnp.maximum(jnp.max(s_loc, axis=-1, keepdims=True), jnp.max(s_ctx, axis=-1, keepdims=True))
            e_loc = jnp.exp(s_loc - m)
            e_ctx = jnp.exp(s_ctx - m)
            denom = jnp.sum(e_loc, axis=-1, keepdims=True) + jnp.sum(e_ctx, axis=-1, keepdims=True)
            o = _dot(e_loc.astype(BF16), vp) + _dot(e_ctx.astype(BF16), vcp)
            outs.append(o / denom)
        o_ref[:, cols] = jnp.where((lane // HEAD_DIM) == 0, outs[0], outs[1]).astype(o_ref.dtype)


def _natten(q, k, v, kc, vc, bias):
    B, S, _ = q.shape
    rows = S // GRID_W
    n_blk = rows // NA_BLOCK_ROWS
    C = kc.shape[1]
    n_q = NA_BLOCK_ROWS * GRID_W
    single = pl.Buffered(1)
    block_class = lambda b, i: (jnp.where(i == 0, 0, jnp.where(i == n_blk - 1, 2, 1)), 0, 0, 0)
    return pl.pallas_call(
        functools.partial(_natten_kernel, rows=rows),
        grid=(B, n_blk),
        in_specs=[
            pl.BlockSpec((None, n_q, D_NA), lambda b, i: (b, i, 0)),
            pl.BlockSpec((None, S, D_NA), lambda b, i: (b, 0, 0), pipeline_mode=single),
            pl.BlockSpec((None, S, D_NA), lambda b, i: (b, 0, 0), pipeline_mode=single),
            pl.BlockSpec((None, C, D_NA), lambda b, i: (b, 0, 0)),
            pl.BlockSpec((None, C, D_NA), lambda b, i: (b, 0, 0)),
            pl.BlockSpec((None,) + bias.shape[1:], block_class),
        ],
        out_specs=pl.BlockSpec((None, n_q, D_NA), lambda b, i: (b, i, 0)),
        out_shape=jax.ShapeDtypeStruct((B, S, D_NA), BF16),
        compiler_params=_cparams(("parallel", "arbitrary")),
        name="natten",
    )(q, k, v, kc, vc, bias)


def _natten_bias(rpb, rows):
    col = np.arange(GRID_W)
    dc = np.clip(col[None, :] - col[:, None], -(WIN_COLS - 1), WIN_COLS - 1) + (WIN_COLS - 1)
    col_start = np.clip(col - WIN_COLS // 2, 0, GRID_W - WIN_COLS)
    col_mask = (col[None, :] >= col_start[:, None]) & (col[None, :] < col_start[:, None] + WIN_COLS)
    r0 = np.array([0, 2 * NA_BLOCK_ROWS, rows - NA_BLOCK_ROWS])
    r = r0[:, None] + np.arange(NA_BLOCK_ROWS)[None, :]
    key_row = np.clip(r0 - WIN_ROWS // 2, 0, rows - NA_KEY_ROWS)[:, None, None] + np.arange(NA_KEY_ROWS)
    win_start = np.clip(r - WIN_ROWS // 2, 0, rows - WIN_ROWS)[:, :, None]
    row_mask = (key_row >= win_start) & (key_row < win_start + WIN_ROWS)
    dr = np.clip(key_row - r[:, :, None] + (WIN_ROWS - 1), 0, 2 * WIN_ROWS - 2)
    pick_c = jnp.asarray(dc[None] == np.arange(2 * WIN_COLS - 1)[:, None, None], dtype=F32)
    pick_r = jnp.asarray(dr[..., None] == np.arange(2 * WIN_ROWS - 1), dtype=F32)
    b = jnp.einsum('crkd,hde,eqx->chrqkx', pick_r, rpb.astype(F32), pick_c, precision=HI)
    mask = row_mask[:, None, :, None, :, None] & col_mask[None, None, None, :, None, :]
    b = jnp.where(mask, b, NEG_INF)
    return b.reshape(NA_CLASSES, NA_HEADS, NA_BLOCK_ROWS * GRID_W, NA_KEY_ROWS * GRID_W)


def _rope_tables(seq):
    pos = jnp.arange(seq, dtype=jnp.int32)
    rows = (pos // GRID_W).astype(F32)
    cols = (pos % GRID_W).astype(F32)
    nf = HEAD_DIM // 4
    inv = ROPE_THETA ** (-jnp.arange(nf, dtype=F32) / nf)
    ar = rows[:, None] * inv[None, :]
    ac = cols[:, None] * inv[None, :]
    cos = jnp.concatenate([jnp.cos(ar), jnp.cos(ar), jnp.cos(ac), jnp.cos(ac)], axis=-1)
    sin = jnp.concatenate([-jnp.sin(ar), jnp.sin(ar), -jnp.sin(ac), jnp.sin(ac)], axis=-1)
    return jnp.tile(cos, (1, NA_HEADS)), jnp.tile(sin, (1, NA_HEADS))


def _postmix_kernel(yh_ref, yn_ref, gate_ref, x_ref, g1_ref, sh2_ref, sc2_ref, n2_ref, whyo_ref, wnao_ref,
                    wout_ref, wq_ref, x1_ref, h2_ref, pq_ref):
    a = _dot(yh_ref[...].astype(BF16), whyo_ref[...])
    b = _dot(yn_ref[...], wnao_ref[...])
    gate = gate_ref[...].astype(F32)
    merged = gate[:, :D_MODEL] * a + gate[:, D_MODEL:] * b
    x1 = x_ref[...] + g1_ref[...] * _dot(merged.astype(BF16), wout_ref[...])
    x1_ref[...] = x1
    h2 = _modulated_norm(x1, n2_ref[...], sh2_ref[...], sc2_ref[...]).astype(BF16)
    h2_ref[...] = h2
    for hp in range(2 * PEER_HEADS):
        pq_ref[hp] = _dot(h2, wq_ref[:, hp * PEER_D_HALF:(hp + 1) * PEER_D_HALF]).astype(BF16)


def _postmix(yh, yn, gate, x2, g1, sh2, sc2, n2g, whyo, wnao, wout, wq, seq):
    T = x2.shape[0]
    tiles_per_batch = seq // TOK_TILE
    row = lambda i: (i, 0)
    per_batch = lambda i: (i // tiles_per_batch, 0, 0)
    const = lambda i: (0, 0)
    nq = wq.shape[1]
    return pl.pallas_call(
        _postmix_kernel,
        grid=(T // TOK_TILE,),
        in_specs=[
            pl.BlockSpec((TOK_TILE, D_HY), row),
            pl.BlockSpec((TOK_TILE, D_NA), row),
            pl.BlockSpec((TOK_TILE, 2 * D_MODEL), row),
            pl.BlockSpec((TOK_TILE, D_MODEL), row),
            pl.BlockSpec((None, 1, D_MODEL), per_batch),
            pl.BlockSpec((None, 1, D_MODEL), per_batch),
            pl.BlockSpec((None, 1, D_MODEL), per_batch),
            pl.BlockSpec((1, D_MODEL), const),
            pl.BlockSpec((D_HY, D_MODEL), const),
            pl.BlockSpec((D_NA, D_MODEL), const),
            pl.BlockSpec((D_MODEL, D_MODEL), const),
            pl.BlockSpec((D_MODEL, nq), const),
        ],
        out_specs=[
            pl.BlockSpec((TOK_TILE, D_MODEL), row),
            pl.BlockSpec((TOK_TILE, D_MODEL), row),
            pl.BlockSpec((nq // PEER_D_HALF, TOK_TILE, PEER_D_HALF), lambda i: (0, i, 0)),
        ],
        out_shape=[
            jax.ShapeDtypeStruct((T, D_MODEL), F32),
            jax.ShapeDtypeStruct((T, D_MODEL), BF16),
            jax.ShapeDtypeStruct((nq // PEER_D_HALF, T, PEER_D_HALF), BF16),
        ],
        compiler_params=_cparams(("parallel",)),
        name="postmix",
    )(yh, yn, gate, x2, g1, sh2, sc2, n2g, whyo, wnao, wout, wq)


SUBLANES = 8
ROUTE_TILE = 256


def _argmax_tree(vals, idxs):
    while len(vals) > 1:
        nv, ni = [], []
        for j in range(0, len(vals) - 1, 2):
            right = vals[j + 1] > vals[j]
            nv.append(jnp.where(right, vals[j + 1], vals[j]))
            ni.append(jnp.where(right, idxs[j + 1], idxs[j]))
        if len(vals) % 2:
            nv.append(vals[-1])
            ni.append(idxs[-1])
        vals, idxs = nv, ni
    return vals[0], idxs[0]


W_BLOCK = 128
W_PITCH = 136
ROUTE_SLICES = 8


def _route_w_kernel(pq_ref, keys_ref, wa_ref, wb_ref, tv_ref, ti_ref, sa_ref, sb_ref, sg_ref, s_ref):
    step = pl.program_id(0)
    cur = step % 2
    prev = 1 - cur
    rt = pq_ref.shape[1]
    n = PEER_N_KEYS
    n_grp = n // SUBLANES
    sub = lax.broadcasted_iota(jnp.int32, (SUBLANES, rt), 0)
    key_idx = [sub + SUBLANES * g for g in range(n_grp)]
    key_row = lax.broadcasted_iota(jnp.int32, (n, sa_ref.shape[2]), 0)

    @pl.when(step == 0)
    def _():
        sa_ref[...] = jnp.zeros_like(sa_ref)
        sb_ref[...] = jnp.zeros_like(sb_ref)
        sg_ref[...] = jnp.zeros_like(sg_ref)
        s_ref[1] = jnp.zeros(s_ref.shape[1:], F32)

    def first_level(hp):
        s = _dot_nt(keys_ref[hp], pq_ref[hp]).reshape(n_grp, SUBLANES, rt)
        slabs = [s[g] for g in range(n_grp)]
        h, p = hp // 2, hp % 2
        for j in range(PEER_TOPK):
            v, i = _argmax_tree(slabs, key_idx)
            for shift in (4, 2, 1):
                v2, i2 = pltpu.roll(v, shift, 0), pltpu.roll(i, shift, 0)
                take = (v2 > v) | ((v2 == v) & (i2 < i))
                v, i = jnp.where(take, v2, v), jnp.where(take, i2, i)
            tv_ref[p, j, pl.ds(h, 1), :] = v[0:1]
            ti_ref[p, j, pl.ds(h, 1), :] = i[0:1]
            slabs = [jnp.where(key_idx[g] == i, -jnp.inf, slabs[g]) for g in range(n_grp)]

    def token_weights(blk, t):
        row = pl.ds(blk * W_BLOCK + t, 1)
        ga = jnp.where(sa_ref[prev, row, :] == key_row, sg_ref[prev, row, :], 0.0).astype(BF16)
        ob = jnp.where(sb_ref[prev, row, :] == key_row, 1.0, 0.0).astype(BF16)
        s_ref[blk, pl.ds(pl.multiple_of(t * W_PITCH, SUBLANES), n), :] = _dot_nt(ga, ob)

    def emit(w_ref, blk, i1):
        w_ref[i1] = s_ref[blk, pl.ds(i1, W_BLOCK, stride=W_PITCH), :].astype(BF16)

    per_slice = W_BLOCK // (ROUTE_SLICES // 2)

    def make_slice(blk, w_ref, emit_blk):
        def body(k, carry):
            for d in range(2 * PEER_HEADS // ROUTE_SLICES):
                first_level((blk * (ROUTE_SLICES // 2) + k) * (2 * PEER_HEADS // ROUTE_SLICES) + d)
            for j in range(per_slice):
                token_weights(blk, k * per_slice + j)
            for j in range(per_slice):
                emit(w_ref, emit_blk, k * per_slice + j)
            return carry
        return body

    lax.fori_loop(0, ROUTE_SLICES // 2, make_slice(0, wb_ref, 1), 0)
    lax.fori_loop(0, ROUTE_SLICES // 2, make_slice(1, wa_ref, 0), 0)

    pairs = [(a, b) for a in range(PEER_TOPK) for b in range(PEER_TOPK) if (a + 1) * (b + 1) <= PEER_TOPK]
    cand = [tv_ref[0, a] + tv_ref[1, b] for a, b in pairs]
    pos = [jnp.full((PEER_HEADS, rt), a * PEER_TOPK + b, jnp.int32) for a, b in pairs]
    best, sel_a, sel_b = [], [], []
    for j in range(PEER_TOPK):
        v, w = _argmax_tree(cand, pos)
        best.append(v)
        ia = jnp.zeros((PEER_HEADS, rt), jnp.int32)
        ib = jnp.zeros((PEER_HEADS, rt), jnp.int32)
        for n, (a, b) in enumerate(pairs):
            hit = w == (a * PEER_TOPK + b)
            cand[n] = jnp.where(hit, -jnp.inf, cand[n])
            ia = jnp.where(hit, ti_ref[0, a], ia)
            ib = jnp.where(hit, ti_ref[1, b], ib)
        sel_a.append(ia)
        sel_b.append(ib)
    ex = [jnp.exp(v - best[0]) for v in best]
    denom = functools.reduce(lambda x, y: x + y, ex)
    gates = [e / denom for e in ex]
    sa_ref[cur] = jnp.concatenate(sel_a, axis=0).T
    sb_ref[cur] = jnp.concatenate(sel_b, axis=0).T
    sg_ref[cur] = jnp.concatenate(gates, axis=0).T


def _route_w(pq, keys_bf):
    n_hp, T, _ = pq.shape
    assert ROUTE_TILE == 2 * W_BLOCK
    n_tiles = T // ROUTE_TILE
    n_sel = PEER_HEADS * PEER_TOPK
    w_shape = jax.ShapeDtypeStruct((n_tiles, PEER_N_KEYS, W_BLOCK, PEER_N_KEYS), BF16)
    w_block = (None, PEER_N_KEYS, W_BLOCK, PEER_N_KEYS)
    clamp = lambda i: jnp.clip(i, 0, n_tiles - 1)
    return pl.pallas_call(
        _route_w_kernel,
        grid=(n_tiles + 2,),
        in_specs=[
            pl.BlockSpec((n_hp, ROUTE_TILE, PEER_D_HALF), lambda i: (0, clamp(i), 0)),
            pl.BlockSpec((n_hp, PEER_N_KEYS, PEER_D_HALF), lambda i: (0, 0, 0)),
        ],
        out_specs=[pl.BlockSpec(w_block, lambda i: (clamp(i - 1), 0, 0, 0)),
                   pl.BlockSpec(w_block, lambda i: (clamp(i - 2), 0, 0, 0))],
        out_shape=[w_shape, w_shape],
        scratch_shapes=[
            pltpu.VMEM((2, PEER_TOPK, PEER_HEADS, ROUTE_TILE), F32),
            pltpu.VMEM((2, PEER_TOPK, PEER_HEADS, ROUTE_TILE), jnp.int32),
            pltpu.VMEM((2, ROUTE_TILE, n_sel), jnp.int32),
            pltpu.VMEM((2, ROUTE_TILE, n_sel), jnp.int32),
            pltpu.VMEM((2, ROUTE_TILE, n_sel), F32),
            pltpu.VMEM((2, W_BLOCK * W_PITCH, PEER_N_KEYS), F32),
        ],
        compiler_params=_cparams(("arbitrary",)),
        name="route_w",
    )(pq, keys_bf)


LANES = 128
SORT_TILE = SUBLANES * LANES


def _batcher_sort_network(n):
    ces = []

    def merge(lo, m, r):
        step = r * 2
        if step < m:
            merge(lo, m, step)
            merge(lo + r, m, step)
            ces.extend((i, i + r) for i in range(lo + r, lo + m - r, step))
        else:
            ces.append((lo, lo + r))

    def sort(lo, m):
        if m > 1:
            sort(lo, m // 2)
            sort(lo + m // 2, m // 2)
            merge(lo, m, 1)

    sort(0, n)
    return ces


def _bitonic_merge_network(n):
    ces, s = [], n // 2
    while s >= 1:
        ces.extend((i, i + s) for i in range(n) if (i & s) == 0)
        s //= 2
    return ces


SORT_NET = _batcher_sort_network(PEER_TOPK)
MERGE_NET = _bitonic_merge_network(PEER_TOPK)


def _compare_exchange(v, x, net):
    for i, j in net:
        c = v[j] > v[i]
        v[i], v[j] = jnp.where(c, v[j], v[i]), jnp.where(c, v[i], v[j])
        x[i], x[j] = jnp.where(c, x[j], x[i]), jnp.where(c, x[i], x[j])


def _merge_top(a, b, dropped):
    (va, xa), (vb, xb) = a, b
    k = len(va)
    v, x = [], []
    for r in range(k):
        c = vb[k - 1 - r] > va[r]
        v.append(jnp.where(c, vb[k - 1 - r], va[r]))
        x.append(jnp.where(c, xb[k - 1 - r], xa[r]))
        dropped = jnp.maximum(dropped, jnp.where(c, va[r], vb[k - 1 - r]))
    _compare_exchange(v, x, MERGE_NET)
    return (v, x), dropped


def _merge_all(lists, dropped):
    while len(lists) > 1:
        nxt = []
        for j in range(0, len(lists) - 1, 2):
            m, dropped = _merge_top(lists[j], lists[j + 1], dropped)
            nxt.append(m)
        if len(lists) % 2:
            nxt.append(lists[-1])
        lists = nxt
    return lists[0], dropped


def _settle_ties(v, x, dropped):
    x = list(x)
    for r in range(len(v) - 1):
        swap = (v[r] == v[r + 1]) & (x[r] > x[r + 1])
        x[r], x[r + 1] = jnp.where(swap, x[r + 1], x[r]), jnp.where(swap, x[r], x[r + 1])
    risk = v[-1] <= dropped
    for r in range(len(v) - 1):
        risk = risk | ((v[r] == v[r + 1]) & (x[r] > x[r + 1]))
    return x, risk.astype(jnp.int32)


def _level2_chains():
    pairs = [(a, b) for a in range(PEER_TOPK) for b in range(PEER_TOPK) if (a + 1) * (b + 1) <= PEER_TOPK]
    n_rows = 4
    chains = [[(a, b) for (a2, b) in pairs if a2 == a] for a in range(n_rows)]
    rest = [p for p in pairs if p[0] >= n_rows]
    for b in sorted({b for _, b in rest}):
        chains.append([(a, b2) for (a, b2) in rest if b2 == b])
    assert sorted(sum(chains, [])) == pairs
    return pairs, chains


def _route_kernel(pq_ref, keys_ref, a_ref, b_ref, g_ref, sx_ref, tv_ref, ti_ref, bs_ref, es_ref,
                  oa_ref, ob_ref, og_ref):
    n = PEER_N_KEYS
    k = PEER_TOPK
    shape = (SUBLANES, LANES)
    neg = jnp.full(shape, -jnp.inf, F32)
    pairs, chains = _level2_chains()

    def scores(hp):
        s = _dot_nt(keys_ref[hp], pq_ref[hp])
        for rg in range(n // SUBLANES):
            for lt in range(SUBLANES):
                sx_ref[pl.ds(rg * SUBLANES * SUBLANES + lt, SUBLANES, stride=SUBLANES), :] = (
                    s[rg * SUBLANES:(rg + 1) * SUBLANES, lt * LANES:(lt + 1) * LANES])
        return [sx_ref[pl.ds(key * SUBLANES, SUBLANES), :] for key in range(n)]

    def store_top(hp, v, x):
        for r in range(k):
            tv_ref[hp, r] = v[r]
            ti_ref[hp, r] = x[r]

    def level1(hp, carry):
        vals = scores(hp)
        lists = []
        for q in range(n // k):
            v = vals[q * k:(q + 1) * k]
            x = [jnp.full(shape, q * k + r, jnp.int32) for r in range(k)]
            _compare_exchange(v, x, SORT_NET)
            lists.append((v, x))
        (v, x), dropped = _merge_all(lists, neg)
        x, risk = _settle_ties(v, x, dropped)
        store_top(hp, v, x)

        @pl.when(jnp.max(risk) > 0)
        def _():
            level1_exact(hp)

        return carry

    def level1_exact(hp):
        vals = scores(hp)
        idx = [jnp.full(shape, key, jnp.int32) for key in range(n)]
        v, x = [], []
        for r in range(k):
            best, where_ = _argmax_tree(vals, idx)
            v.append(best)
            x.append(where_)
            vals = [jnp.where(where_ == key, -jnp.inf, vals[key]) for key in range(n)]
        store_top(hp, v, x)

    def candidates(h):
        s1 = [tv_ref[2 * h, r] for r in range(k)]
        s2 = [tv_ref[2 * h + 1, r] for r in range(k)]
        e1 = [ti_ref[2 * h, r] * n for r in range(k)]
        e2 = [ti_ref[2 * h + 1, r] for r in range(k)]
        return (lambda a, b: s1[a] + s2[b]), (lambda a, b: e1[a] + e2[b] + (a * k + b) * PEER_N_EXPERTS)

    def store_best(h, v, x):
        for r in range(k):
            bs_ref[h, r] = v[r]
            es_ref[h, r] = x[r]

    def level2(h, carry):
        val, expert = candidates(h)
        lists = []
        for chain in chains:
            pad = k - len(chain)
            lists.append(([val(a, b) for a, b in chain] + [neg] * pad,
                          [expert(a, b) for a, b in chain] + [jnp.zeros(shape, jnp.int32)] * pad))
        (v, x), dropped = _merge_all(lists, neg)
        x, risk = _settle_ties(v, x, dropped)
        store_best(h, v, x)

        @pl.when(jnp.max(risk) > 0)
        def _():
            level2_exact(h)

        return carry

    def level2_exact(h):
        val, expert = candidates(h)
        cand = [val(a, b) for a, b in pairs]
        exps = [expert(a, b) for a, b in pairs]
        pos = [jnp.full(shape, a * k + b, jnp.int32) for a, b in pairs]
        v, x = [], []
        for r in range(k):
            best, where_ = _argmax_tree(cand, pos)
            chosen = jnp.zeros(shape, jnp.int32)
            for m, (a, b) in enumerate(pairs):
                hit = where_ == (a * k + b)
                cand[m] = jnp.where(hit, -jnp.inf, cand[m])
                chosen = jnp.where(hit, exps[m], chosen)
            v.append(best)
            x.append(chosen)
        store_best(h, v, x)

    lax.fori_loop(0, 2 * PEER_HEADS, level1, 0)
    lax.fori_loop(0, PEER_HEADS, level2, 0)

    for h in range(PEER_HEADS):
        best = [bs_ref[h, r] for r in range(k)]
        ex = [jnp.exp(v - best[0]) for v in best]
        denom = functools.reduce(lambda p, q: p + q, ex)
        for r in range(k):
            rows = pl.ds((h * k + r) * SUBLANES, SUBLANES)
            e = jnp.bitwise_and(es_ref[h, r], PEER_N_EXPERTS - 1)
            oa_ref[rows, :] = jnp.right_shift(e, n.bit_length() - 1)
            ob_ref[rows, :] = jnp.bitwise_and(e, n - 1)
            og_ref[rows, :] = ex[r] / denom
    for s in range(SUBLANES):
        rows = pl.ds(s * LANES, LANES)
        plane = pl.ds(s, PEER_HEADS * k, stride=SUBLANES)
        a_ref[rows, :] = oa_ref[plane, :].T
        b_ref[rows, :] = ob_ref[plane, :].T
        g_ref[rows, :] = og_ref[plane, :].T


def _route(pq, keys_bf):
    n_hp, T, _ = pq.shape
    n_sel = PEER_HEADS * PEER_TOPK
    row = lambda i: (i, 0)
    reg = (SUBLANES, LANES)
    return pl.pallas_call(
        _route_kernel,
        grid=(T // SORT_TILE,),
        in_specs=[
            pl.BlockSpec((n_hp, SORT_TILE, PEER_D_HALF), lambda i: (0, i, 0)),
            pl.BlockSpec((n_hp, PEER_N_KEYS, PEER_D_HALF), lambda i: (0, 0, 0)),
        ],
        out_specs=[pl.BlockSpec((SORT_TILE, n_sel), row)] * 3,
        out_shape=[
            jax.ShapeDtypeStruct((T, n_sel), jnp.int32),
            jax.ShapeDtypeStruct((T, n_sel), jnp.int32),
            jax.ShapeDtypeStruct((T, n_sel), F32),
        ],
        scratch_shapes=[
            pltpu.VMEM((PEER_N_KEYS * SUBLANES, LANES), F32),
            pltpu.VMEM((n_hp, PEER_TOPK) + reg, F32),
            pltpu.VMEM((n_hp, PEER_TOPK) + reg, jnp.int32),
            pltpu.VMEM((PEER_HEADS, PEER_TOPK) + reg, F32),
            pltpu.VMEM((PEER_HEADS, PEER_TOPK) + reg, jnp.int32),
            pltpu.VMEM((n_sel * SUBLANES, LANES), jnp.int32),
            pltpu.VMEM((n_sel * SUBLANES, LANES), jnp.int32),
            pltpu.VMEM((n_sel * SUBLANES, LANES), F32),
        ],
        compiler_params=_cparams(("parallel",)),
        name="route",
    )(pq, keys_bf)


WBUILD_UNROLL = 32


def _wbuild_kernel(a_ref, b_ref, g_ref, w_ref, s_ref):
    n = PEER_N_KEYS
    tb = a_ref.shape[0]
    key = lax.broadcasted_iota(jnp.int32, (n, a_ref.shape[1]), 0)

    def body(t, carry):
        a = a_ref[pl.ds(t, 1), :]
        b = b_ref[pl.ds(t, 1), :]
        g = g_ref[pl.ds(t, 1), :]
        ga = jnp.where(a == key, g, 0.0).astype(BF16)
        ob = jnp.where(b == key, 1.0, 0.0).astype(BF16)
        s_ref[pl.ds(pl.multiple_of(t * W_PITCH, SUBLANES), n), :] = _dot_nt(ga, ob)
        return carry

    lax.fori_loop(0, tb, body, 0, unroll=WBUILD_UNROLL)

    def emit(i1, carry):
        w_ref[i1] = s_ref[pl.ds(i1, tb, stride=W_PITCH), :].astype(BF16)
        return carry

    lax.fori_loop(0, n, emit, 0, unroll=WBUILD_UNROLL)


def _wbuild(a, b, g):
    T, n_sel = a.shape
    row = lambda i: (i, 0)
    return pl.pallas_call(
        _wbuild_kernel,
        grid=(T // W_BLOCK,),
        in_specs=[pl.BlockSpec((W_BLOCK, n_sel), row)] * 3,
        out_specs=pl.BlockSpec((None, PEER_N_KEYS, W_BLOCK, PEER_N_KEYS), lambda i: (i, 0, 0, 0)),
        out_shape=jax.ShapeDtypeStruct((T // W_BLOCK, PEER_N_KEYS, W_BLOCK, PEER_N_KEYS), BF16),
        scratch_shapes=[pltpu.VMEM((W_BLOCK * W_PITCH, PEER_N_KEYS), F32)],
        compiler_params=_cparams(("parallel",)),
        name="wbuild",
    )(a, b, g)


def _peer_dense_kernel(h2_ref, w_ref, u_ref, v_ref, x1_ref, g2_ref, o_ref, acc_ref):
    e = pl.program_id(1)

    @pl.when(e == 0)
    def _():
        acc_ref[...] = jnp.zeros_like(acc_ref)

    a = _dot_nt(h2_ref[...], u_ref[...])
    act = 0.5 * a * (1.0 + lax.erf(a * (2.0 ** -0.5)))
    w = jnp.concatenate(
        [jnp.concatenate([w_ref[blk, i] for i in range(w_ref.shape[1])], axis=1) for blk in range(w_ref.shape[0])],
        axis=0)
    acc_ref[...] += _dot((w.astype(F32) * act).astype(BF16), v_ref[...])

    @pl.when(e == pl.num_programs(1) - 1)
    def _():
        o_ref[...] = x1_ref[...] + g2_ref[...] * acc_ref[...]


def _peer_dense(h2, w, u_bf, v_bf, x1, g2, seq):
    T = h2.shape[0]
    tiles_per_batch = seq // PEER_TOK_TILE
    n_exp = u_bf.shape[0]
    return pl.pallas_call(
        _peer_dense_kernel,
        grid=(T // PEER_TOK_TILE, n_exp // PEER_EXP_TILE),
        in_specs=[
            pl.BlockSpec((PEER_TOK_TILE, D_MODEL), lambda i, e: (i, 0)),
            pl.BlockSpec((PEER_TOK_TILE // W_BLOCK, PEER_EXP_TILE // PEER_N_KEYS, W_BLOCK, PEER_N_KEYS),
                         lambda i, e: (i, e, 0, 0)),
            pl.BlockSpec((PEER_EXP_TILE, D_MODEL), lambda i, e: (e, 0)),
            pl.BlockSpec((PEER_EXP_TILE, D_MODEL), lambda i, e: (e, 0)),
            pl.BlockSpec((PEER_TOK_TILE, D_MODEL), lambda i, e: (i, 0)),
            pl.BlockSpec((None, 1, D_MODEL), lambda i, e: (i // tiles_per_batch, 0, 0)),
        ],
        out_specs=pl.BlockSpec((PEER_TOK_TILE, D_MODEL), lambda i, e: (i, 0)),
        out_shape=jax.ShapeDtypeStruct((T, D_MODEL), F32),
        scratch_shapes=[pltpu.VMEM((PEER_TOK_TILE, D_MODEL), F32)],
        compiler_params=_cparams(("parallel", "arbitrary")),
        name="peer_dense",
    )(h2, w, u_bf, v_bf, x1, g2)


HI = lax.Precision.HIGHEST


HY_LANES = 128


def _shortconv_kernel(z_ref, w_ref, b_ref, o_ref):
    z = z_ref[...]
    n = z.shape[0]
    row = lax.broadcasted_iota(jnp.int32, z.shape, 0)
    prev = jnp.where(row == 0, 0.0, pltpu.roll(z, 1, 0))
    nxt = jnp.where(row == n - 1, 0.0, pltpu.roll(z, n - 1, 0))
    o_ref[...] = w_ref[0:1] * prev + w_ref[1:2] * z + w_ref[2:3] * nxt + b_ref[...]


def _shortconv(zh, w, b):
    B, L, C = zh.shape
    return pl.pallas_call(
        _shortconv_kernel,
        grid=(B, C // HY_LANES),
        in_specs=[
            pl.BlockSpec((None, L, HY_LANES), lambda i, j: (i, 0, j)),
            pl.BlockSpec((3, HY_LANES), lambda i, j: (0, j)),
            pl.BlockSpec((1, HY_LANES), lambda i, j: (0, j)),
        ],
        out_specs=pl.BlockSpec((None, L, HY_LANES), lambda i, j: (i, 0, j)),
        out_shape=jax.ShapeDtypeStruct((B, L, C), F32),
        compiler_params=_cparams(("parallel", "parallel")),
        name="shortconv",
    )(zh, w, b)


FFT_R = 128
FFT_N = FFT_R * FFT_R
FFT_PITCH = 136
FFT_K1_CHUNK = 16
FFT_UNROLL = 8


def _fft_tables():
    r = jnp.arange(FFT_R, dtype=jnp.int32)
    n2, k1, n1 = r[:, None, None], r[None, :, None], r[None, None, :]
    ang = (2.0 * math.pi / FFT_N) * ((k1 * (FFT_R * n1 + n2)) % FFT_N).astype(F32)
    c, s = jnp.cos(ang), jnp.sin(ang)
    g_full = jnp.concatenate([c, -s], axis=1)
    h_half = jnp.concatenate([jnp.swapaxes(c, 1, 2), -jnp.swapaxes(s, 1, 2)], axis=2)[:, :FFT_R // 2] / FFT_N
    ang2 = (2.0 * math.pi / FFT_R) * ((r[:, None] * r[None, :]) % FFT_R).astype(F32)
    c2, s2 = jnp.cos(ang2), jnp.sin(ang2)
    f_fwd = jnp.concatenate([jnp.concatenate([c2, s2], 1), jnp.concatenate([-s2, c2], 1)], 0)
    f_inv = jnp.concatenate([jnp.concatenate([c2, -s2], 1), jnp.concatenate([s2, c2], 1)], 0)
    return g_full.astype(BF16), h_half.astype(BF16), f_fwd.astype(BF16), f_inv.astype(BF16)


def _fft_stage1(load_slab, g_ref, bre_ref, bim_ref):
    def body(n2, carry):
        out = _dot(g_ref[n2], load_slab(n2).astype(BF16))
        bre_ref[pl.ds(n2, FFT_R, stride=FFT_PITCH), :] = out[:FFT_R]
        bim_ref[pl.ds(n2, FFT_R, stride=FFT_PITCH), :] = out[FFT_R:]
        return carry

    lax.fori_loop(0, FFT_R, body, 0, unroll=FFT_UNROLL)


def _k1_rows(k1):
    return pl.ds(pl.multiple_of(k1 * FFT_PITCH, SUBLANES), FFT_R)


def _load_k1_pair(bre_ref, bim_ref, k1):
    return jnp.concatenate(
        [jnp.concatenate([bre_ref[_k1_rows(k1 + d), :], bim_ref[_k1_rows(k1 + d), :]], axis=0) for d in range(2)],
        axis=1)


FILT_ROWS = 512
FILT_FEAT = 128


def _filt_kernel(feat_ref, delta_ref, w1_ref, b1_ref, w2_ref, b2_ref, w3_ref, b3_ref, w4_ref, freq_ref,
                 h_ref, sum_ref):
    @pl.when(pl.program_id(0) == 0)
    def _():
        sum_ref[...] = jnp.zeros_like(sum_ref)

    feat = feat_ref[...]
    freq = freq_ref[...]
    dot = lambda a, b: jnp.dot(a, b, precision=HI, preferred_element_type=F32)
    h = jnp.sin(freq * (dot(feat, w1_ref[...]) + b1_ref[...]))
    h = jnp.sin(freq * (dot(h, w2_ref[...]) + b2_ref[...]))
    h = jnp.sin(freq * (dot(h, w3_ref[...]) + b3_ref[...]))
    h = dot(h, w4_ref[...])
    n_feat = 1 + 2 * HY_BANDS
    t = feat[:, 0:1]
    forward = feat[:, n_feat:n_feat + 1] > 0.5
    keep = feat[:, n_feat + 1:n_feat + 2]
    decay = jnp.exp(-t * delta_ref[...])
    outs = []
    for o in range(HY_ORDER):
        base = o * 2 * D_HY
        outs.append(jnp.where(forward, h[:, base:base + D_HY], h[:, base + D_HY:base + 2 * D_HY]) * decay)
    out = jnp.concatenate(outs, axis=1)
    sum_ref[...] += jnp.sum(jnp.abs(out), axis=0, keepdims=True)
    h_ref[...] = out * keep


def _filt_features(L):
    r = np.arange(FFT_N)
    n = FFT_R * (r % FFT_R) + r // FFT_R
    j = np.where(n < L, n, np.where(n > L, 2 * L - n, 0))
    jj = jnp.asarray(j, dtype=F32)[:, None]
    t = jj / (L - 1.0)
    w = (2.0 * math.pi / L) * jj
    bands = jnp.linspace(1e-4, HY_BANDS - 1.0, HY_BANDS, dtype=F32)[None, :]
    flags = jnp.asarray(np.stack([n < L, n != L], axis=1), dtype=F32)
    feats = jnp.concatenate([t, jnp.cos(bands * w), -jnp.sin(bands * w), flags], axis=-1)
    return jnp.pad(feats, ((0, 0), (0, FILT_FEAT - feats.shape[1])))


def _filters(p, L):
    feats = _filt_features(L)
    deltas = jnp.abs(jnp.linspace(math.log(HY_DECAY_TARGET) / HY_SLOW_PCT,
                                  math.log(HY_DECAY_TARGET) / HY_FAST_PCT, D_HY, dtype=F32))[None, :]
    w1 = jnp.pad(p['hy_f1_w'], ((0, FILT_FEAT - p['hy_f1_w'].shape[0]), (0, 0)))
    n_out = HY_ORDER * D_HY
    const = lambda i: (0, 0)
    full = lambda a: pl.BlockSpec(a.shape, const)
    args = [deltas, w1, p['hy_f1_b'][None], p['hy_f2_w'], p['hy_f2_b'][None], p['hy_f3_w'], p['hy_f3_b'][None],
            p['hy_f4_w'], p['hy_sin_freq'][None]]
    return pl.pallas_call(
        _filt_kernel,
        grid=(FFT_N // FILT_ROWS,),
        in_specs=[pl.BlockSpec((FILT_ROWS, FILT_FEAT), lambda i: (i, 0))] + [full(a) for a in args],
        out_specs=[pl.BlockSpec((FILT_ROWS, n_out), lambda i: (i, 0)), pl.BlockSpec((1, n_out), const)],
        out_shape=[jax.ShapeDtypeStruct((FFT_N, n_out), F32), jax.ShapeDtypeStruct((1, n_out), F32)],
        compiler_params=_cparams(("arbitrary",)),
        name="hyena_filter",
    )(feats, *args)


def _fspec_kernel(h_ref, sum_ref, g_ref, f_ref, o_ref, bre_ref, bim_ref):
    c = pl.program_id(1)

    @pl.when(c == 0)
    def _():
        _fft_stage1(lambda n2: h_ref[pl.ds(pl.multiple_of(n2 * FFT_R, FFT_R), FFT_R), :], g_ref, bre_ref, bim_ref)

    inv = 1.0 / sum_ref[...]

    lanes = h_ref.shape[1]

    def slab_pair(j, carry):
        x = _dot(f_ref[...], _load_k1_pair(bre_ref, bim_ref, c * FFT_K1_CHUNK + 2 * j).astype(BF16))
        for d in range(2):
            rows = pl.ds(pl.multiple_of((2 * j + d) * FFT_R, FFT_R), FFT_R)
            o_ref[0, rows, :] = x[:FFT_R, d * lanes:(d + 1) * lanes] * inv
            o_ref[1, rows, :] = x[FFT_R:, d * lanes:(d + 1) * lanes] * inv
        return carry

    lax.fori_loop(0, FFT_K1_CHUNK // 2, slab_pair, 0, unroll=FFT_UNROLL // 2)


def _filter_spectrum(h, hsum, g_full, f_fwd):
    n_ch = h.shape[1]
    single = pl.Buffered(1)
    return pl.pallas_call(
        _fspec_kernel,
        grid=(n_ch // HY_LANES, FFT_R // FFT_K1_CHUNK),
        in_specs=[
            pl.BlockSpec((FFT_N, HY_LANES), lambda j, c: (0, j), pipeline_mode=single),
            pl.BlockSpec((1, HY_LANES), lambda j, c: (0, j)),
            pl.BlockSpec(g_full.shape, lambda j, c: (0, 0, 0), pipeline_mode=single),
            pl.BlockSpec(f_fwd.shape, lambda j, c: (0, 0)),
        ],
        out_specs=pl.BlockSpec((2, FFT_K1_CHUNK * FFT_R, HY_LANES), lambda j, c: (0, c, j)),
        out_shape=jax.ShapeDtypeStruct((2, FFT_N, n_ch), F32),
        scratch_shapes=[pltpu.VMEM((FFT_R * FFT_PITCH, HY_LANES), F32)] * 2,
        compiler_params=_cparams(("parallel", "arbitrary")),
        name="hyena_filter_fft",
    )(h, hsum, g_full, f_fwd)


def _hyconv_kernel(u_ref, gate_ref, kf_ref, skip_ref, g_ref, h_ref, f_ref, fi_ref, o_ref, bre_ref, bim_ref):
    c = pl.program_id(2)
    half = FFT_R // 2

    def rows_of(n2):
        return pl.ds(pl.multiple_of(n2 * half, half), half)

    @pl.when(c == 0)
    def _():
        _fft_stage1(lambda n2: u_ref[rows_of(n2), :], g_ref, bre_ref, bim_ref)

    lanes = u_ref.shape[1]

    def slab_pair(j, carry):
        k1 = c * FFT_K1_CHUNK + 2 * j
        x = _dot(f_ref[...], _load_k1_pair(bre_ref, bim_ref, k1).astype(BF16))
        xr, xi = x[:FFT_R], x[FFT_R:]
        kr, ki = [jnp.concatenate([kf_ref[part, pl.ds(pl.multiple_of((2 * j + d) * FFT_R, FFT_R), FFT_R), :]
                                   for d in range(2)], axis=1) for part in range(2)]
        y = jnp.concatenate([xr * kr - xi * ki, xr * ki + xi * kr], axis=0)
        z = _dot(fi_ref[...], y.astype(BF16))
        for d in range(2):
            bre_ref[_k1_rows(k1 + d), :] = z[:FFT_R, d * lanes:(d + 1) * lanes]
            bim_ref[_k1_rows(k1 + d), :] = z[FFT_R:, d * lanes:(d + 1) * lanes]
        return carry

    lax.fori_loop(0, FFT_K1_CHUNK // 2, slab_pair, 0, unroll=FFT_UNROLL // 2)

    @pl.when(c == pl.num_programs(2) - 1)
    def _():
        def body(n2, carry):
            z = jnp.concatenate([bre_ref[pl.ds(n2, FFT_R, stride=FFT_PITCH), :],
                                 bim_ref[pl.ds(n2, FFT_R, stride=FFT_PITCH), :]], axis=0)
            conv = _dot(h_ref[n2], z.astype(BF16))
            rows = rows_of(n2)
            o_ref[rows, :] = gate_ref[rows, :] * (conv + skip_ref[...] * u_ref[rows, :])
            return carry

        lax.fori_loop(0, FFT_R, body, 0, unroll=FFT_UNROLL)


def _hyconv(u, u_col, gate, gate_col, kf, kf_col, skip, tables):
    g_half, h_half, f_fwd, f_inv = tables
    B, L, _ = u.shape
    single = pl.Buffered(1)
    return pl.pallas_call(
        _hyconv_kernel,
        grid=(B, D_HY // HY_LANES, FFT_R // FFT_K1_CHUNK),
        in_specs=[
            pl.BlockSpec((None, L, HY_LANES), lambda b, j, c: (b, 0, u_col + j), pipeline_mode=single),
            pl.BlockSpec((None, L, HY_LANES), lambda b, j, c: (b, 0, gate_col + j), pipeline_mode=single),
            pl.BlockSpec((2, FFT_K1_CHUNK * FFT_R, HY_LANES), lambda b, j, c: (0, c, kf_col + j)),
            pl.BlockSpec((1, HY_LANES), lambda b, j, c: (0, j)),
            pl.BlockSpec(g_half.shape, lambda b, j, c: (0, 0, 0), pipeline_mode=single),
            pl.BlockSpec(h_half.shape, lambda b, j, c: (0, 0, 0), pipeline_mode=single),
            pl.BlockSpec(f_fwd.shape, lambda b, j, c: (0, 0)),
            pl.BlockSpec(f_inv.shape, lambda b, j, c: (0, 0)),
        ],
        out_specs=pl.BlockSpec((None, L, HY_LANES), lambda b, j, c: (b, 0, j)),
        out_shape=jax.ShapeDtypeStruct((B, L, D_HY), F32),
        scratch_shapes=[pltpu.VMEM((FFT_R * FFT_PITCH, HY_LANES), F32)] * 2,
        compiler_params=_cparams(("parallel", "parallel", "arbitrary")),
        name="hyena_conv",
    )(u, gate, kf, skip, g_half, h_half, f_fwd, f_inv)


def _slab_major(a, n1):
    B, L, C = a.shape
    return a.reshape(B, n1, L // n1, C).transpose(0, 2, 1, 3).reshape(B, L, C)


def _hyena(zh, p):
    B, L, _ = zh.shape
    assert 2 * L == FFT_N
    g_full, h_half, f_fwd, f_inv = _fft_tables()
    tables = (g_full[:, :, :FFT_R // 2], h_half, f_fwd, f_inv)
    filt, filt_sum = _filters(p, L)
    both_dirs = filt_sum[:, :]
    kf = _filter_spectrum(filt, both_dirs, g_full, f_fwd)
    zc = _slab_major(_shortconv(zh, p['hy_conv_w'], p['hy_conv_b'][None]), FFT_R // 2)
    lanes = D_HY // HY_LANES
    y = _hyconv(zc, 0, zc, lanes, kf, 0, p['hy_skip'][0:1], tables)
    y = _hyconv(y, 0, zc, 2 * lanes, kf, lanes, p['hy_skip'][1:2], tables)
    return _slab_major(y, L // (FFT_R // 2))


def kernel(x, c, ctx, c_ctx, norm1_g, norm2_g, w_ada, b_ada, w_in, b_in, hy_conv_w, hy_conv_b, hy_f1_w, hy_f1_b, hy_f2_w, hy_f2_b, hy_f3_w, hy_f3_b, hy_f4_w, hy_sin_freq, hy_skip, q_norm_g, k_norm_g, na_rpb, w_hy_out, w_na_out, w_out, peer_w_q, peer_keys, peer_u, peer_v):
    assert w_in.shape[0] == 1, "single-layer block"
    B, S, D = x.shape
    C = ctx.shape[1]
    T = B * S

    m_lat = (jax.nn.silu(c) @ w_ada[0] + b_ada[0]).reshape(B, N_MOD, 1, D)
    m_ctx = (jax.nn.silu(c_ctx) @ w_ada[0] + b_ada[0]).reshape(N_MOD, 1, D)
    sh1, sc1, g1, sh2, sc2, g2 = [m_lat[:, i] for i in range(N_MOD)]

    w_in_bf = w_in[0].astype(BF16)
    b_in2 = b_in[0][None, :]
    gain1 = norm1_g[0][None, :]
    qg = jnp.tile(q_norm_g[0], NA_HEADS)[None, :]
    kg = jnp.tile(k_norm_g[0], NA_HEADS)[None, :]
    seg = jnp.asarray(np.kron(np.eye(NA_HEADS), np.full((HEAD_DIM, HEAD_DIM), 1.0 / HEAD_DIM)), dtype=BF16)
    cos, sin = _rope_tables(S)

    x2 = x.reshape(T, D)
    zh, q, k, v, gate = _inproj(x2, sh1, sc1, gain1, w_in_bf, b_in2, cos, sin, qg, kg, seg, S)

    k_ctx, v_ctx = _ctxkv(ctx.reshape(B * C, D), m_ctx[0], m_ctx[1], gain1,
                          w_in_bf[:, COL_K:COL_V], b_in2[:, COL_K:COL_V],
                          w_in_bf[:, COL_V:COL_G_HY], b_in2[:, COL_V:COL_G_HY], kg, seg)

    p = {'hy_conv_w': hy_conv_w[0], 'hy_conv_b': hy_conv_b[0], 'hy_f1_w': hy_f1_w[0], 'hy_f1_b': hy_f1_b[0],
         'hy_f2_w': hy_f2_w[0], 'hy_f2_b': hy_f2_b[0], 'hy_f3_w': hy_f3_w[0], 'hy_f3_b': hy_f3_b[0],
         'hy_f4_w': hy_f4_w[0], 'hy_sin_freq': hy_sin_freq[0], 'hy_skip': hy_skip[0]}
    y_hy = _hyena(zh.reshape(B, S, 3 * D_HY), p).reshape(T, D_HY)

    y_na = _natten(q.reshape(B, S, D_NA), k.reshape(B, S, D_NA), v.reshape(B, S, D_NA),
                   k_ctx.reshape(B, C, D_NA), v_ctx.reshape(B, C, D_NA), _natten_bias(na_rpb[0], S // GRID_W)).reshape(T, D_NA)

    x1, h2, pq = _postmix(y_hy, y_na, gate, x2, g1, sh2, sc2, norm2_g[0][None, :],
                          w_hy_out[0].astype(BF16), w_na_out[0].astype(BF16), w_out[0].astype(BF16),
                          peer_w_q[0].astype(BF16), S)

    keys_bf = peer_keys[0].astype(BF16).reshape(2 * PEER_HEADS, PEER_N_KEYS, PEER_D_HALF)
    sel_a, sel_b, sel_g = _route(pq, keys_bf)
    w = _wbuild(sel_a, sel_b, sel_g)
    out = _peer_dense(h2, w, peer_u[0].astype(BF16), peer_v[0].astype(BF16), x1, g2, S)
    return out.reshape(B, S, D)
```

```python
import functools
import math

import numpy as np
import jax
import jax.numpy as jnp
from jax import lax
from jax.experimental import pallas as pl
from jax.experimental.pallas import tpu as pltpu

F32 = jnp.float32
BF16 = jnp.bfloat16

D_MODEL = 1024
GRID_W = 64
EPS = 1e-6
N_MOD = 6

D_HY = 512
HY_ORDER = 2
HY_BANDS = 16
HY_DECAY_TARGET = 1e-2
HY_FAST_PCT = 0.3
HY_SLOW_PCT = 1.5

NA_HEADS = 8
HEAD_DIM = 64
D_NA = NA_HEADS * HEAD_DIM
WIN_ROWS = 8
WIN_COLS = 16
ROPE_THETA = 10000.0
NEG_INF = -1e30

PEER_HEADS = 8
PEER_N_KEYS = 128
PEER_N_EXPERTS = PEER_N_KEYS * PEER_N_KEYS
PEER_TOPK = 16
PEER_D_KEY = 256
PEER_D_HALF = PEER_D_KEY // 2

COL_HY = 0
COL_Q = COL_HY + 3 * D_HY
COL_K = COL_Q + D_NA
COL_V = COL_K + D_NA
COL_G_HY = COL_V + D_NA
N_PROJ = COL_G_HY + 2 * D_MODEL

VMEM_LIMIT = 56 * 1024 * 1024
TOK_TILE = 256
PEER_TOK_TILE = 512
PEER_EXP_TILE = 2048


def _cparams(sem):
    return pltpu.CompilerParams(dimension_semantics=sem, vmem_limit_bytes=VMEM_LIMIT)


def _dot(a, b):
    return jnp.dot(a, b, preferred_element_type=F32)


def _dot_nt(a, b):
    return lax.dot_general(a, b, (((1,), (1,)), ((), ())), preferred_element_type=F32)


def _modulated_norm(x, gain, shift, scale):
    ms = jnp.mean(x * x, axis=-1, keepdims=True)
    return (x * lax.rsqrt(ms + EPS) * gain) * (1.0 + scale) + shift


def _head_rms(z, seg, gain):
    ms = _dot((z * z).astype(BF16), seg)
    return z * lax.rsqrt(ms + EPS) * gain


def _rope(z, cos, sin_signed):
    n = z.shape[-1]
    lane = lax.broadcasted_iota(jnp.int32, z.shape, 1)
    first = (lane // (HEAD_DIM // 4)) % 2 == 0
    partner = jnp.where(first, pltpu.roll(z, n - HEAD_DIM // 4, 1), pltpu.roll(z, HEAD_DIM // 4, 1))
    return z * cos + partner * sin_signed


def _inproj_kernel(x_ref, shift_ref, scale_ref, g_ref, w_ref, b_ref, cos_ref, sin_ref, qg_ref, kg_ref,
                   seg_ref, zh_ref, q_ref, k_ref, v_ref, gate_ref):
    h = _modulated_norm(x_ref[...], g_ref[...], shift_ref[...], scale_ref[...]).astype(BF16)

    def proj(lo, hi):
        return _dot(h, w_ref[:, lo:hi]) + b_ref[:, lo:hi]

    zh_ref[...] = proj(COL_HY, COL_Q)
    cos, sin = cos_ref[...], sin_ref[...]
    seg = seg_ref[...]
    q = _rope(_head_rms(proj(COL_Q, COL_K), seg, qg_ref[...]), cos, sin)
    q_ref[...] = q.astype(BF16)
    k = _rope(_head_rms(proj(COL_K, COL_V), seg, kg_ref[...]), cos, sin)
    k_ref[...] = k.astype(BF16)
    v_ref[...] = proj(COL_V, COL_G_HY).astype(BF16)
    gate_ref[...] = jax.nn.sigmoid(proj(COL_G_HY, N_PROJ)).astype(BF16)


def _inproj(x2, shift, scale, gain, w_bf, b_in, cos, sin, qg, kg, seg, seq):
    T = x2.shape[0]
    tiles_per_batch = seq // TOK_TILE
    row = lambda i: (i, 0)
    per_batch = lambda i: (i // tiles_per_batch, 0, 0)
    const = lambda i: (0, 0)
    pos = lambda i: (i % tiles_per_batch, 0)
    return pl.pallas_call(
        _inproj_kernel,
        grid=(T // TOK_TILE,),
        in_specs=[
            pl.BlockSpec((TOK_TILE, D_MODEL), row),
            pl.BlockSpec((None, 1, D_MODEL), per_batch),
            pl.BlockSpec((None, 1, D_MODEL), per_batch),
            pl.BlockSpec((1, D_MODEL), const),
            pl.BlockSpec((D_MODEL, N_PROJ), const),
            pl.BlockSpec((1, N_PROJ), const),
            pl.BlockSpec((TOK_TILE, D_NA), pos),
            pl.BlockSpec((TOK_TILE, D_NA), pos),
            pl.BlockSpec((1, D_NA), const),
            pl.BlockSpec((1, D_NA), const),
            pl.BlockSpec((D_NA, D_NA), const),
        ],
        out_specs=[
            pl.BlockSpec((TOK_TILE, 3 * D_HY), row),
            pl.BlockSpec((TOK_TILE, D_NA), row),
            pl.BlockSpec((TOK_TILE, D_NA), row),
            pl.BlockSpec((TOK_TILE, D_NA), row),
            pl.BlockSpec((TOK_TILE, 2 * D_MODEL), row),
        ],
        out_shape=[
            jax.ShapeDtypeStruct((T, 3 * D_HY), F32),
            jax.ShapeDtypeStruct((T, D_NA), BF16),
            jax.ShapeDtypeStruct((T, D_NA), BF16),
            jax.ShapeDtypeStruct((T, D_NA), BF16),
            jax.ShapeDtypeStruct((T, 2 * D_MODEL), BF16),
        ],
        compiler_params=_cparams(("parallel",)),
        name="inproj",
    )(x2, shift, scale, gain, w_bf, b_in, cos, sin, qg, kg, seg)


def _ctxkv_kernel(x_ref, shift_ref, scale_ref, g_ref, wk_ref, bk_ref, wv_ref, bv_ref, kg_ref, seg_ref,
                  k_ref, v_ref):
    h = _modulated_norm(x_ref[...], g_ref[...], shift_ref[...], scale_ref[...]).astype(BF16)
    k = _head_rms(_dot(h, wk_ref[...]) + bk_ref[...], seg_ref[...], kg_ref[...])
    k_ref[...] = k.astype(BF16)
    v_ref[...] = (_dot(h, wv_ref[...]) + bv_ref[...]).astype(BF16)


def _ctxkv(ctx2, shift, scale, gain, wk, bk, wv, bv, kg, seg):
    T = ctx2.shape[0]
    row = lambda i: (i, 0)
    const = lambda i: (0, 0)
    return pl.pallas_call(
        _ctxkv_kernel,
        grid=(T // TOK_TILE,),
        in_specs=[
            pl.BlockSpec((TOK_TILE, D_MODEL), row),
            pl.BlockSpec((1, D_MODEL), const),
            pl.BlockSpec((1, D_MODEL), const),
            pl.BlockSpec((1, D_MODEL), const),
            pl.BlockSpec((D_MODEL, D_NA), const),
            pl.BlockSpec((1, D_NA), const),
            pl.BlockSpec((D_MODEL, D_NA), const),
            pl.BlockSpec((1, D_NA), const),
            pl.BlockSpec((1, D_NA), const),
            pl.BlockSpec((D_NA, D_NA), const),
        ],
        out_specs=[pl.BlockSpec((TOK_TILE, D_NA), row), pl.BlockSpec((TOK_TILE, D_NA), row)],
        out_shape=[jax.ShapeDtypeStruct((T, D_NA), BF16), jax.ShapeDtypeStruct((T, D_NA), BF16)],
        compiler_params=_cparams(("parallel",)),
        name="ctxkv",
    )(ctx2, shift, scale, gain, wk, bk, wv, bv, kg, seg)


NA_BLOCK_ROWS = 4
NA_KEY_ROWS = WIN_ROWS + NA_BLOCK_ROWS
NA_CLASSES = 3


def _natten_kernel(q_ref, k_ref, v_ref, kc_ref, vc_ref, bias_ref, o_ref, *, rows):
    r0 = pl.program_id(1) * NA_BLOCK_ROWS
    key_start = jnp.clip(r0 - WIN_ROWS // 2, 0, rows - NA_KEY_ROWS)
    start = pl.multiple_of(key_start * GRID_W, GRID_W * NA_BLOCK_ROWS)
    n_q = NA_BLOCK_ROWS * GRID_W
    n_loc = NA_KEY_ROWS * GRID_W
    scale = jnp.asarray(HEAD_DIM ** -0.5, BF16)
    pair = 2 * HEAD_DIM
    lane = lax.broadcasted_iota(jnp.int32, (n_q, pair), 1)
    for p in range(NA_HEADS // 2):
        cols = slice(p * pair, (p + 1) * pair)
        qp = q_ref[:, cols] * scale
        kp = k_ref[pl.ds(start, n_loc), cols]
        vp = v_ref[pl.ds(start, n_loc), cols]
        kcp = kc_ref[:, cols]
        vcp = vc_ref[:, cols]
        outs = []
        for hh in range(2):
            qm = jnp.where((lane // HEAD_DIM) == hh, qp, jnp.zeros_like(qp))
            s_loc = _dot_nt(qm, kp) + bias_ref[2 * p + hh]
            s_ctx = _dot_nt(qm, kcp)
            m = jnp.maximum(jnp.max(s_loc, axis=-1, keepdims=True), jnp.max(s_ctx, axis=-1, keepdims=True))
            e_loc = jnp.exp(s_loc - m)
            e_ctx = jnp.exp(s_ctx - m)
            denom = jnp.sum(e_loc, axis=-1, keepdims=True) + jnp.sum(e_ctx, axis=-1, keepdims=True)
            o = _dot(e_loc.astype(BF16), vp) + _dot(e_ctx.astype(BF16), vcp)
            outs.append(o / denom)
        o_ref[:, cols] = jnp.where((lane // HEAD_DIM) == 0, outs[0], outs[1]).astype(o_ref.dtype)


def _natten(q, k, v, kc, vc, bias):
    B, S, _ = q.shape
    rows = S // GRID_W
    n_blk = rows // NA_BLOCK_ROWS
    C = kc.shape[1]
    n_q = NA_BLOCK_ROWS * GRID_W
    single = pl.Buffered(1)
    block_class = lambda b, i: (jnp.where(i == 0, 0, jnp.where(i == n_blk - 1, 2, 1)), 0, 0, 0)
    return pl.pallas_call(
        functools.partial(_natten_kernel, rows=rows),
        grid=(B, n_blk),
        in_specs=[
            pl.BlockSpec((None, n_q, D_NA), lambda b, i: (b, i, 0)),
            pl.BlockSpec((None, S, D_NA), lambda b, i: (b, 0, 0), pipeline_mode=single),
            pl.BlockSpec((None, S, D_NA), lambda b, i: (b, 0, 0), pipeline_mode=single),
            pl.BlockSpec((None, C, D_NA), lambda b, i: (b, 0, 0)),
            pl.BlockSpec((None, C, D_NA), lambda b, i: (b, 0, 0)),
            pl.BlockSpec((None,) + bias.shape[1:], block_class),
        ],
        out_specs=pl.BlockSpec((None, n_q, D_NA), lambda b, i: (b, i, 0)),
        out_shape=jax.ShapeDtypeStruct((B, S, D_NA), BF16),
        compiler_params=_cparams(("parallel", "arbitrary")),
        name="natten",
    )(q, k, v, kc, vc, bias)


def _natten_bias(rpb, rows):
    col = np.arange(GRID_W)
    dc = np.clip(col[None, :] - col[:, None], -(WIN_COLS - 1), WIN_COLS - 1) + (WIN_COLS - 1)
    col_start = np.clip(col - WIN_COLS // 2, 0, GRID_W - WIN_COLS)
    col_mask = (col[None, :] >= col_start[:, None]) & (col[None, :] < col_start[:, None] + WIN_COLS)
    r0 = np.array([0, 2 * NA_BLOCK_ROWS, rows - NA_BLOCK_ROWS])
    r = r0[:, None] + np.arange(NA_BLOCK_ROWS)[None, :]
    key_row = np.clip(r0 - WIN_ROWS // 2, 0, rows - NA_KEY_ROWS)[:, None, None] + np.arange(NA_KEY_ROWS)
    win_start = np.clip(r - WIN_ROWS // 2, 0, rows - WIN_ROWS)[:, :, None]
    row_mask = (key_row >= win_start) & (key_row < win_start + WIN_ROWS)
    dr = np.clip(key_row - r[:, :, None] + (WIN_ROWS - 1), 0, 2 * WIN_ROWS - 2)
    pick_c = jnp.asarray(dc[None] == np.arange(2 * WIN_COLS - 1)[:, None, None], dtype=F32)
    pick_r = jnp.asarray(dr[..., None] == np.arange(2 * WIN_ROWS - 1), dtype=F32)
    b = jnp.einsum('crkd,hde,eqx->chrqkx', pick_r, rpb.astype(F32), pick_c, precision=HI)
    mask = row_mask[:, None, :, None, :, None] & col_mask[None, None, None, :, None, :]
    b = jnp.where(mask, b, NEG_INF)
    return b.reshape(NA_CLASSES, NA_HEADS, NA_BLOCK_ROWS * GRID_W, NA_KEY_ROWS * GRID_W)


def _rope_tables(seq):
    pos = jnp.arange(seq, dtype=jnp.int32)
    rows = (pos // GRID_W).astype(F32)
    cols = (pos % GRID_W).astype(F32)
    nf = HEAD_DIM // 4
    inv = ROPE_THETA ** (-jnp.arange(nf, dtype=F32) / nf)
    ar = rows[:, None] * inv[None, :]
    ac = cols[:, None] * inv[None, :]
    cos = jnp.concatenate([jnp.cos(ar), jnp.cos(ar), jnp.cos(ac), jnp.cos(ac)], axis=-1)
    sin = jnp.concatenate([-jnp.sin(ar), jnp.sin(ar), -jnp.sin(ac), jnp.sin(ac)], axis=-1)
    return jnp.tile(cos, (1, NA_HEADS)), jnp.tile(sin, (1, NA_HEADS))


def _postmix_kernel(yh_ref, yn_ref, gate_ref, x_ref, g1_ref, sh2_ref, sc2_ref, n2_ref, whyo_ref, wnao_ref,
                    wout_ref, wq_ref, x1_ref, h2_ref, pq_ref):
    a = _dot(yh_ref[...].astype(BF16), whyo_ref[...])
    b = _dot(yn_ref[...], wnao_ref[...])
    gate = gate_ref[...].astype(F32)
    merged = gate[:, :D_MODEL] * a + gate[:, D_MODEL:] * b
    x1 = x_ref[...] + g1_ref[...] * _dot(merged.astype(BF16), wout_ref[...])
    x1_ref[...] = x1
    h2 = _modulated_norm(x1, n2_ref[...], sh2_ref[...], sc2_ref[...]).astype(BF16)
    h2_ref[...] = h2
    for hp in range(2 * PEER_HEADS):
        pq_ref[hp] = _dot(h2, wq_ref[:, hp * PEER_D_HALF:(hp + 1) * PEER_D_HALF]).astype(BF16)


def _postmix(yh, yn, gate, x2, g1, sh2, sc2, n2g, whyo, wnao, wout, wq, seq):
    T = x2.shape[0]
    tiles_per_batch = seq // TOK_TILE
    row = lambda i: (i, 0)
    per_batch = lambda i: (i // tiles_per_batch, 0, 0)
    const = lambda i: (0, 0)
    nq = wq.shape[1]
    return pl.pallas_call(
        _postmix_kernel,
        grid=(T // TOK_TILE,),
        in_specs=[
            pl.BlockSpec((TOK_TILE, D_HY), row),
            pl.BlockSpec((TOK_TILE, D_NA), row),
            pl.BlockSpec((TOK_TILE, 2 * D_MODEL), row),
            pl.BlockSpec((TOK_TILE, D_MODEL), row),
            pl.BlockSpec((None, 1, D_MODEL), per_batch),
            pl.BlockSpec((None, 1, D_MODEL), per_batch),
            pl.BlockSpec((None, 1, D_MODEL), per_batch),
            pl.BlockSpec((1, D_MODEL), const),
            pl.BlockSpec((D_HY, D_MODEL), const),
            pl.BlockSpec((D_NA, D_MODEL), const),
            pl.BlockSpec((D_MODEL, D_MODEL), const),
            pl.BlockSpec((D_MODEL, nq), const),
        ],
        out_specs=[
            pl.BlockSpec((TOK_TILE, D_MODEL), row),
            pl.BlockSpec((TOK_TILE, D_MODEL), row),
            pl.BlockSpec((nq // PEER_D_HALF, TOK_TILE, PEER_D_HALF), lambda i: (0, i, 0)),
        ],
        out_shape=[
            jax.ShapeDtypeStruct((T, D_MODEL), F32),
            jax.ShapeDtypeStruct((T, D_MODEL), BF16),
            jax.ShapeDtypeStruct((nq // PEER_D_HALF, T, PEER_D_HALF), BF16),
        ],
        compiler_params=_cparams(("parallel",)),
        name="postmix",
    )(yh, yn, gate, x2, g1, sh2, sc2, n2g, whyo, wnao, wout, wq)


SUBLANES = 8
LANES = 128


def _argmax_tree(vals, idxs):
    while len(vals) > 1:
        nv, ni = [], []
        for j in range(0, len(vals) - 1, 2):
            right = vals[j + 1] > vals[j]
            nv.append(jnp.where(right, vals[j + 1], vals[j]))
            ni.append(jnp.where(right, idxs[j + 1], idxs[j]))
        if len(vals) % 2:
            nv.append(vals[-1])
            ni.append(idxs[-1])
        vals, idxs = nv, ni
    return vals[0], idxs[0]


W_BLOCK = 128
W_PITCH = 136


SORT_TILE = SUBLANES * LANES


def _batcher_sort_network(n):
    ces = []

    def merge(lo, m, r):
        step = r * 2
        if step < m:
            merge(lo, m, step)
            merge(lo + r, m, step)
            ces.extend((i, i + r) for i in range(lo + r, lo + m - r, step))
        else:
            ces.append((lo, lo + r))

    def sort(lo, m):
        if m > 1:
            sort(lo, m // 2)
            sort(lo + m // 2, m // 2)
            merge(lo, m, 1)

    sort(0, n)
    return ces


def _bitonic_merge_network(n):
    ces, s = [], n // 2
    while s >= 1:
        ces.extend((i, i + s) for i in range(n) if (i & s) == 0)
        s //= 2
    return ces


SORT_NET = _batcher_sort_network(PEER_TOPK)
MERGE_NET = _bitonic_merge_network(PEER_TOPK)


def _compare_exchange(v, x, net):
    for i, j in net:
        c = v[j] > v[i]
        v[i], v[j] = jnp.where(c, v[j], v[i]), jnp.where(c, v[i], v[j])
        x[i], x[j] = jnp.where(c, x[j], x[i]), jnp.where(c, x[i], x[j])


def _merge_top(a, b, dropped):
    (va, xa), (vb, xb) = a, b
    k = len(va)
    v, x = [], []
    for r in range(k):
        c = vb[k - 1 - r] > va[r]
        v.append(jnp.where(c, vb[k - 1 - r], va[r]))
        x.append(jnp.where(c, xb[k - 1 - r], xa[r]))
        dropped = jnp.maximum(dropped, jnp.where(c, va[r], vb[k - 1 - r]))
    _compare_exchange(v, x, MERGE_NET)
    return (v, x), dropped


def _merge_all(lists, dropped):
    while len(lists) > 1:
        nxt = []
        for j in range(0, len(lists) - 1, 2):
            m, dropped = _merge_top(lists[j], lists[j + 1], dropped)
            nxt.append(m)
        if len(lists) % 2:
            nxt.append(lists[-1])
        lists = nxt
    return lists[0], dropped


def _settle_ties(v, x, dropped):
    x = list(x)
    for r in range(len(v) - 1):
        swap = (v[r] == v[r + 1]) & (x[r] > x[r + 1])
        x[r], x[r + 1] = jnp.where(swap, x[r + 1], x[r]), jnp.where(swap, x[r], x[r + 1])
    risk = v[-1] <= dropped
    for r in range(len(v) - 1):
        risk = risk | ((v[r] == v[r + 1]) & (x[r] > x[r + 1]))
    return x, risk.astype(jnp.int32)


def _level2_chains():
    pairs = [(a, b) for a in range(PEER_TOPK) for b in range(PEER_TOPK) if (a + 1) * (b + 1) <= PEER_TOPK]
    n_rows = 4
    chains = [[(a, b) for (a2, b) in pairs if a2 == a] for a in range(n_rows)]
    rest = [p for p in pairs if p[0] >= n_rows]
    for b in sorted({b for _, b in rest}):
        chains.append([(a, b2) for (a, b2) in rest if b2 == b])
    assert sorted(sum(chains, [])) == pairs
    return pairs, chains


def _route_kernel(pq_ref, keys_ref, a_ref, b_ref, g_ref, sx_ref, tv_ref, ti_ref, bs_ref, es_ref,
                  oa_ref, ob_ref, og_ref):
    n = PEER_N_KEYS
    k = PEER_TOPK
    shape = (SUBLANES, LANES)
    neg = jnp.full(shape, -jnp.inf, F32)
    pairs, chains = _level2_chains()

    def scores(hp):
        s = _dot_nt(keys_ref[hp], pq_ref[hp])
        for rg in range(n // SUBLANES):
            for lt in range(SUBLANES):
                sx_ref[pl.ds(rg * SUBLANES * SUBLANES + lt, SUBLANES, stride=SUBLANES), :] = (
                    s[rg * SUBLANES:(rg + 1) * SUBLANES, lt * LANES:(lt + 1) * LANES])
        return [sx_ref[pl.ds(key * SUBLANES, SUBLANES), :] for key in range(n)]

    def store_top(hp, v, x):
        for r in range(k):
            tv_ref[hp, r] = v[r]
            ti_ref[hp, r] = x[r]

    def level1(hp, carry):
        vals = scores(hp)
        lists = []
        for q in range(n // k):
            v = vals[q * k:(q + 1) * k]
            x = [jnp.full(shape, q * k + r, jnp.int32) for r in range(k)]
            _compare_exchange(v, x, SORT_NET)
            lists.append((v, x))
        (v, x), dropped = _merge_all(lists, neg)
        x, risk = _settle_ties(v, x, dropped)
        store_top(hp, v, x)

        @pl.when(jnp.max(risk) > 0)
        def _():
            level1_exact(hp)

        return carry

    def level1_exact(hp):
        vals = scores(hp)
        idx = [jnp.full(shape, key, jnp.int32) for key in range(n)]
        v, x = [], []
        for r in range(k):
            best, where_ = _argmax_tree(vals, idx)
            v.append(best)
            x.append(where_)
            vals = [jnp.where(where_ == key, -jnp.inf, vals[key]) for key in range(n)]
        store_top(hp, v, x)

    def candidates(h):
        s1 = [tv_ref[2 * h, r] for r in range(k)]
        s2 = [tv_ref[2 * h + 1, r] for r in range(k)]
        e1 = [ti_ref[2 * h, r] * n for r in range(k)]
        e2 = [ti_ref[2 * h + 1, r] for r in range(k)]
        return (lambda a, b: s1[a] + s2[b]), (lambda a, b: e1[a] + e2[b] + (a * k + b) * PEER_N_EXPERTS)

    def store_best(h, v, x):
        for r in range(k):
            bs_ref[h, r] = v[r]
            es_ref[h, r] = x[r]

    def level2(h, carry):
        val, expert = candidates(h)
        lists = []
        for chain in chains:
            pad = k - len(chain)
            lists.append(([val(a, b) for a, b in chain] + [neg] * pad,
                          [expert(a, b) for a, b in chain] + [jnp.zeros(shape, jnp.int32)] * pad))
        (v, x), dropped = _merge_all(lists, neg)
        x, risk = _settle_ties(v, x, dropped)
        store_best(h, v, x)

        @pl.when(jnp.max(risk) > 0)
        def _():
            level2_exact(h)

        return carry

    def level2_exact(h):
        val, expert = candidates(h)
        cand = [val(a, b) for a, b in pairs]
        exps = [expert(a, b) for a, b in pairs]
        pos = [jnp.full(shape, a * k + b, jnp.int32) for a, b in pairs]
        v, x = [], []
        for r in range(k):
            best, where_ = _argmax_tree(cand, pos)
            chosen = jnp.zeros(shape, jnp.int32)
            for m, (a, b) in enumerate(pairs):
                hit = where_ == (a * k + b)
                cand[m] = jnp.where(hit, -jnp.inf, cand[m])
                chosen = jnp.where(hit, exps[m], chosen)
            v.append(best)
            x.append(chosen)
        store_best(h, v, x)

    lax.fori_loop(0, 2 * PEER_HEADS, level1, 0)
    lax.fori_loop(0, PEER_HEADS, level2, 0)

    for h in range(PEER_HEADS):
        best = [bs_ref[h, r] for r in range(k)]
        ex = [jnp.exp(v - best[0]) for v in best]
        denom = functools.reduce(lambda p, q: p + q, ex)
        for r in range(k):
            rows = pl.ds((h * k + r) * SUBLANES, SUBLANES)
            e = jnp.bitwise_and(es_ref[h, r], PEER_N_EXPERTS - 1)
            oa_ref[rows, :] = jnp.right_shift(e, n.bit_length() - 1)
            ob_ref[rows, :] = jnp.bitwise_and(e, n - 1)
            og_ref[rows, :] = ex[r] / denom
    for s in range(SUBLANES):
        rows = pl.ds(s * LANES, LANES)
        plane = pl.ds(s, PEER_HEADS * k, stride=SUBLANES)
        a_ref[rows, :] = oa_ref[plane, :].T
        b_ref[rows, :] = ob_ref[plane, :].T
        g_ref[rows, :] = og_ref[plane, :].T


def _route(pq, keys_bf):
    n_hp, T, _ = pq.shape
    n_sel = PEER_HEADS * PEER_TOPK
    row = lambda i: (i, 0)
    reg = (SUBLANES, LANES)
    return pl.pallas_call(
        _route_kernel,
        grid=(T // SORT_TILE,),
        in_specs=[
            pl.BlockSpec((n_hp, SORT_TILE, PEER_D_HALF), lambda i: (0, i, 0)),
            pl.BlockSpec((n_hp, PEER_N_KEYS, PEER_D_HALF), lambda i: (0, 0, 0)),
        ],
        out_specs=[pl.BlockSpec((SORT_TILE, n_sel), row)] * 3,
        out_shape=[
            jax.ShapeDtypeStruct((T, n_sel), jnp.int32),
            jax.ShapeDtypeStruct((T, n_sel), jnp.int32),
            jax.ShapeDtypeStruct((T, n_sel), F32),
        ],
        scratch_shapes=[
            pltpu.VMEM((PEER_N_KEYS * SUBLANES, LANES), F32),
            pltpu.VMEM((n_hp, PEER_TOPK) + reg, F32),
            pltpu.VMEM((n_hp, PEER_TOPK) + reg, jnp.int32),
            pltpu.VMEM((PEER_HEADS, PEER_TOPK) + reg, F32),
            pltpu.VMEM((PEER_HEADS, PEER_TOPK) + reg, jnp.int32),
            pltpu.VMEM((n_sel * SUBLANES, LANES), jnp.int32),
            pltpu.VMEM((n_sel * SUBLANES, LANES), jnp.int32),
            pltpu.VMEM((n_sel * SUBLANES, LANES), F32),
        ],
        compiler_params=_cparams(("parallel",)),
        name="route",
    )(pq, keys_bf)


WBUILD_UNROLL = 32


def _wbuild_kernel(a_ref, b_ref, g_ref, w_ref, s_ref):
    n = PEER_N_KEYS
    tb = a_ref.shape[0]
    key = lax.broadcasted_iota(jnp.int32, (n, a_ref.shape[1]), 0)

    def body(t, carry):
        a = a_ref[pl.ds(t, 1), :]
        b = b_ref[pl.ds(t, 1), :]
        g = g_ref[pl.ds(t, 1), :]
        ga = jnp.where(a == key, g, 0.0).astype(BF16)
        ob = jnp.where(b == key, 1.0, 0.0).astype(BF16)
        s_ref[pl.ds(pl.multiple_of(t * W_PITCH, SUBLANES), n), :] = _dot_nt(ga, ob)
        return carry

    lax.fori_loop(0, tb, body, 0, unroll=WBUILD_UNROLL)

    def emit(i1, carry):
        w_ref[i1] = s_ref[pl.ds(i1, tb, stride=W_PITCH), :].astype(BF16)
        return carry

    lax.fori_loop(0, n, emit, 0, unroll=WBUILD_UNROLL)


def _wbuild(a, b, g):
    T, n_sel = a.shape
    row = lambda i: (i, 0)
    return pl.pallas_call(
        _wbuild_kernel,
        grid=(T // W_BLOCK,),
        in_specs=[pl.BlockSpec((W_BLOCK, n_sel), row)] * 3,
        out_specs=pl.BlockSpec((None, PEER_N_KEYS, W_BLOCK, PEER_N_KEYS), lambda i: (i, 0, 0, 0)),
        out_shape=jax.ShapeDtypeStruct((T // W_BLOCK, PEER_N_KEYS, W_BLOCK, PEER_N_KEYS), BF16),
        scratch_shapes=[pltpu.VMEM((W_BLOCK * W_PITCH, PEER_N_KEYS), F32)],
        compiler_params=_cparams(("parallel",)),
        name="wbuild",
    )(a, b, g)


def _peer_dense_kernel(h2_ref, w_ref, u_ref, v_ref, x1_ref, g2_ref, o_ref, acc_ref):
    e = pl.program_id(1)

    @pl.when(e == 0)
    def _():
        acc_ref[...] = jnp.zeros_like(acc_ref)

    a = _dot_nt(h2_ref[...], u_ref[...])
    act = 0.5 * a * (1.0 + lax.erf(a * (2.0 ** -0.5)))
    w = jnp.concatenate(
        [jnp.concatenate([w_ref[blk, i] for i in range(w_ref.shape[1])], axis=1) for blk in range(w_ref.shape[0])],
        axis=0)
    acc_ref[...] += _dot((w.astype(F32) * act).astype(BF16), v_ref[...])

    @pl.when(e == pl.num_programs(1) - 1)
    def _():
        o_ref[...] = x1_ref[...] + g2_ref[...] * acc_ref[...]


def _peer_dense(h2, w, u_bf, v_bf, x1, g2, seq):
    T = h2.shape[0]
    tiles_per_batch = seq // PEER_TOK_TILE
    n_exp = u_bf.shape[0]
    return pl.pallas_call(
        _peer_dense_kernel,
        grid=(T // PEER_TOK_TILE, n_exp // PEER_EXP_TILE),
        in_specs=[
            pl.BlockSpec((PEER_TOK_TILE, D_MODEL), lambda i, e: (i, 0)),
            pl.BlockSpec((PEER_TOK_TILE // W_BLOCK, PEER_EXP_TILE // PEER_N_KEYS, W_BLOCK, PEER_N_KEYS),
                         lambda i, e: (i, e, 0, 0)),
            pl.BlockSpec((PEER_EXP_TILE, D_MODEL), lambda i, e: (e, 0)),
            pl.BlockSpec((PEER_EXP_TILE, D_MODEL), lambda i, e: (e, 0)),
            pl.BlockSpec((PEER_TOK_TILE, D_MODEL), lambda i, e: (i, 0)),
            pl.BlockSpec((None, 1, D_MODEL), lambda i, e: (i // tiles_per_batch, 0, 0)),
        ],
        out_specs=pl.BlockSpec((PEER_TOK_TILE, D_MODEL), lambda i, e: (i, 0)),
        out_shape=jax.ShapeDtypeStruct((T, D_MODEL), F32),
        scratch_shapes=[pltpu.VMEM((PEER_TOK_TILE, D_MODEL), F32)],
        compiler_params=_cparams(("parallel", "arbitrary")),
        name="peer_dense",
    )(h2, w, u_bf, v_bf, x1, g2)


HI = lax.Precision.HIGHEST


HY_LANES = 128


def _shortconv_kernel(z_ref, w_ref, b_ref, o_ref):
    z = z_ref[...]
    n = z.shape[0]
    row = lax.broadcasted_iota(jnp.int32, z.shape, 0)
    prev = jnp.where(row == 0, 0.0, pltpu.roll(z, 1, 0))
    nxt = jnp.where(row == n - 1, 0.0, pltpu.roll(z, n - 1, 0))
    o_ref[...] = w_ref[0:1] * prev + w_ref[1:2] * z + w_ref[2:3] * nxt + b_ref[...]


def _shortconv(zh, w, b):
    B, L, C = zh.shape
    return pl.pallas_call(
        _shortconv_kernel,
        grid=(B, C // HY_LANES),
        in_specs=[
            pl.BlockSpec((None, L, HY_LANES), lambda i, j: (i, 0, j)),
            pl.BlockSpec((3, HY_LANES), lambda i, j: (0, j)),
            pl.BlockSpec((1, HY_LANES), lambda i, j: (0, j)),
        ],
        out_specs=pl.BlockSpec((None, L, HY_LANES), lambda i, j: (i, 0, j)),
        out_shape=jax.ShapeDtypeStruct((B, L, C), F32),
        compiler_params=_cparams(("parallel", "parallel")),
        name="shortconv",
    )(zh, w, b)


FFT_R = 128
FFT_N = FFT_R * FFT_R
FFT_PITCH = 136
FFT_K1_CHUNK = 16
FFT_UNROLL = 8


def _fft_tables():
    r = jnp.arange(FFT_R, dtype=jnp.int32)
    n2, k1, n1 = r[:, None, None], r[None, :, None], r[None, None, :]
    ang = (2.0 * math.pi / FFT_N) * ((k1 * (FFT_R * n1 + n2)) % FFT_N).astype(F32)
    c, s = jnp.cos(ang), jnp.sin(ang)
    g_full = jnp.concatenate([c, -s], axis=1)
    h_half = jnp.concatenate([jnp.swapaxes(c, 1, 2), -jnp.swapaxes(s, 1, 2)], axis=2)[:, :FFT_R // 2] / FFT_N
    ang2 = (2.0 * math.pi / FFT_R) * ((r[:, None] * r[None, :]) % FFT_R).astype(F32)
    c2, s2 = jnp.cos(ang2), jnp.sin(ang2)
    f_fwd = jnp.concatenate([jnp.concatenate([c2, s2], 1), jnp.concatenate([-s2, c2], 1)], 0)
    f_inv = jnp.concatenate([jnp.concatenate([c2, -s2], 1), jnp.concatenate([s2, c2], 1)], 0)
    return g_full.astype(BF16), h_half.astype(BF16), f_fwd.astype(BF16), f_inv.astype(BF16)


def _fft_stage1(load_slab, g_ref, bre_ref, bim_ref):
    def body(n2, carry):
        out = _dot(g_ref[n2], load_slab(n2).astype(BF16))
        bre_ref[pl.ds(n2, FFT_R, stride=FFT_PITCH), :] = out[:FFT_R]
        bim_ref[pl.ds(n2, FFT_R, stride=FFT_PITCH), :] = out[FFT_R:]
        return carry

    lax.fori_loop(0, FFT_R, body, 0, unroll=FFT_UNROLL)


def _k1_rows(k1):
    return pl.ds(pl.multiple_of(k1 * FFT_PITCH, SUBLANES), FFT_R)


def _load_k1_pair(bre_ref, bim_ref, k1):
    return jnp.concatenate(
        [jnp.concatenate([bre_ref[_k1_rows(k1 + d), :], bim_ref[_k1_rows(k1 + d), :]], axis=0) for d in range(2)],
        axis=1)


FILT_ROWS = 512
FILT_FEAT = 128


def _filt_kernel(feat_ref, delta_ref, w1_ref, b1_ref, w2_ref, b2_ref, w3_ref, b3_ref, w4_ref, freq_ref,
                 h_ref, sum_ref):
    @pl.when(pl.program_id(0) == 0)
    def _():
        sum_ref[...] = jnp.zeros_like(sum_ref)

    feat = feat_ref[...]
    freq = freq_ref[...]
    dot = lambda a, b: jnp.dot(a, b, precision=HI, preferred_element_type=F32)
    h = jnp.sin(freq * (dot(feat, w1_ref[...]) + b1_ref[...]))
    h = jnp.sin(freq * (dot(h, w2_ref[...]) + b2_ref[...]))
    h = jnp.sin(freq * (dot(h, w3_ref[...]) + b3_ref[...]))
    hi = h.astype(BF16)
    lo = (h - hi.astype(F32)).astype(BF16)
    h = _dot(hi, w4_ref[...]) + _dot(lo, w4_ref[...])
    n_feat = 1 + 2 * HY_BANDS
    t = feat[:, 0:1]
    forward = feat[:, n_feat:n_feat + 1] > 0.5
    keep = feat[:, n_feat + 1:n_feat + 2]
    decay = jnp.exp(-t * delta_ref[...])
    outs = []
    for o in range(HY_ORDER):
        base = o * 2 * D_HY
        outs.append(jnp.where(forward, h[:, base:base + D_HY], h[:, base + D_HY:base + 2 * D_HY]) * decay)
    out = jnp.concatenate(outs, axis=1)
    sum_ref[...] += jnp.sum(jnp.abs(out), axis=0, keepdims=True)
    h_ref[...] = out * keep


def _filt_features(L):
    r = np.arange(FFT_N)
    n = FFT_R * (r % FFT_R) + r // FFT_R
    j = np.where(n < L, n, np.where(n > L, 2 * L - n, 0))
    jj = jnp.asarray(j, dtype=F32)[:, None]
    t = jj / (L - 1.0)
    w = (2.0 * math.pi / L) * jj
    bands = jnp.linspace(1e-4, HY_BANDS - 1.0, HY_BANDS, dtype=F32)[None, :]
    flags = jnp.asarray(np.stack([n < L, n != L], axis=1), dtype=F32)
    feats = jnp.concatenate([t, jnp.cos(bands * w), -jnp.sin(bands * w), flags], axis=-1)
    return jnp.pad(feats, ((0, 0), (0, FILT_FEAT - feats.shape[1])))


def _filters(p, L):
    feats = _filt_features(L)
    deltas = jnp.abs(jnp.linspace(math.log(HY_DECAY_TARGET) / HY_SLOW_PCT,
                                  math.log(HY_DECAY_TARGET) / HY_FAST_PCT, D_HY, dtype=F32))[None, :]
    w1 = jnp.pad(p['hy_f1_w'], ((0, FILT_FEAT - p['hy_f1_w'].shape[0]), (0, 0)))
    n_out = HY_ORDER * D_HY
    const = lambda i: (0, 0)
    full = lambda a: pl.BlockSpec(a.shape, const)
    args = [deltas, w1, p['hy_f1_b'][None], p['hy_f2_w'], p['hy_f2_b'][None], p['hy_f3_w'], p['hy_f3_b'][None],
            p['hy_f4_w'].astype(BF16), p['hy_sin_freq'][None]]
    return pl.pallas_call(
        _filt_kernel,
        grid=(FFT_N // FILT_ROWS,),
        in_specs=[pl.BlockSpec((FILT_ROWS, FILT_FEAT), lambda i: (i, 0))] + [full(a) for a in args],
        out_specs=[pl.BlockSpec((FILT_ROWS, n_out), lambda i: (i, 0)), pl.BlockSpec((1, n_out), const)],
        out_shape=[jax.ShapeDtypeStruct((FFT_N, n_out), F32), jax.ShapeDtypeStruct((1, n_out), F32)],
        compiler_params=_cparams(("arbitrary",)),
        name="hyena_filter",
    )(feats, *args)


def _fspec_kernel(h_ref, sum_ref, g_ref, f_ref, o_ref, bre_ref, bim_ref):
    c = pl.program_id(1)

    @pl.when(c == 0)
    def _():
        _fft_stage1(lambda n2: h_ref[pl.ds(pl.multiple_of(n2 * FFT_R, FFT_R), FFT_R), :], g_ref, bre_ref, bim_ref)

    inv = 1.0 / sum_ref[...]

    lanes = h_ref.shape[1]

    def slab_pair(j, carry):
        x = _dot(f_ref[...], _load_k1_pair(bre_ref, bim_ref, c * FFT_K1_CHUNK + 2 * j).astype(BF16))
        for d in range(2):
            rows = pl.ds(pl.multiple_of((2 * j + d) * FFT_R, FFT_R), FFT_R)
            o_ref[0, rows, :] = x[:FFT_R, d * lanes:(d + 1) * lanes] * inv
            o_ref[1, rows, :] = x[FFT_R:, d * lanes:(d + 1) * lanes] * inv
        return carry

    lax.fori_loop(0, FFT_K1_CHUNK // 2, slab_pair, 0, unroll=FFT_UNROLL // 2)


def _filter_spectrum(h, hsum, g_full, f_fwd):
    n_ch = h.shape[1]
    single = pl.Buffered(1)
    return pl.pallas_call(
        _fspec_kernel,
        grid=(n_ch // HY_LANES, FFT_R // FFT_K1_CHUNK),
        in_specs=[
            pl.BlockSpec((FFT_N, HY_LANES), lambda j, c: (0, j), pipeline_mode=single),
            pl.BlockSpec((1, HY_LANES), lambda j, c: (0, j)),
            pl.BlockSpec(g_full.shape, lambda j, c: (0, 0, 0), pipeline_mode=single),
            pl.BlockSpec(f_fwd.shape, lambda j, c: (0, 0)),
        ],
        out_specs=pl.BlockSpec((2, FFT_K1_CHUNK * FFT_R, HY_LANES), lambda j, c: (0, c, j)),
        out_shape=jax.ShapeDtypeStruct((2, FFT_N, n_ch), F32),
        scratch_shapes=[pltpu.VMEM((FFT_R * FFT_PITCH, HY_LANES), F32)] * 2,
        compiler_params=_cparams(("parallel", "arbitrary")),
        name="hyena_filter_fft",
    )(h, hsum, g_full, f_fwd)


def _hyconv_kernel(u_ref, gate_ref, kf_ref, skip_ref, g_ref, h_ref, f_ref, fi_ref, o_ref, bre_ref, bim_ref):
    c = pl.program_id(2)
    half = FFT_R // 2

    def rows_of(n2):
        return pl.ds(pl.multiple_of(n2 * half, half), half)

    @pl.when(c == 0)
    def _():
        _fft_stage1(lambda n2: u_ref[rows_of(n2), :], g_ref, bre_ref, bim_ref)

    lanes = u_ref.shape[1]

    def slab_pair(j, carry):
        k1 = c * FFT_K1_CHUNK + 2 * j
        x = _dot(f_ref[...], _load_k1_pair(bre_ref, bim_ref, k1).astype(BF16))
        xr, xi = x[:FFT_R], x[FFT_R:]
        kr, ki = [jnp.concatenate([kf_ref[part, pl.ds(pl.multiple_of((2 * j + d) * FFT_R, FFT_R), FFT_R), :]
                                   for d in range(2)], axis=1) for part in range(2)]
        y = jnp.concatenate([xr * kr - xi * ki, xr * ki + xi * kr], axis=0)
        z = _dot(fi_ref[...], y.astype(BF16))
        for d in range(2):
            bre_ref[_k1_rows(k1 + d), :] = z[:FFT_R, d * lanes:(d + 1) * lanes]
            bim_ref[_k1_rows(k1 + d), :] = z[FFT_R:, d * lanes:(d + 1) * lanes]
        return carry

    lax.fori_loop(0, FFT_K1_CHUNK // 2, slab_pair, 0, unroll=FFT_UNROLL // 2)

    @pl.when(c == pl.num_programs(2) - 1)
    def _():
        def body(n2, carry):
            z = jnp.concatenate([bre_ref[pl.ds(n2, FFT_R, stride=FFT_PITCH), :],
                                 bim_ref[pl.ds(n2, FFT_R, stride=FFT_PITCH), :]], axis=0)
            conv = _dot(h_ref[n2], z.astype(BF16))
            rows = rows_of(n2)
            o_ref[rows, :] = gate_ref[rows, :] * (conv + skip_ref[...] * u_ref[rows, :])
            return carry

        lax.fori_loop(0, FFT_R, body, 0, unroll=FFT_UNROLL)


def _hyconv(u, u_col, gate, gate_col, kf, kf_col, skip, tables):
    g_half, h_half, f_fwd, f_inv = tables
    B, L, _ = u.shape
    single = pl.Buffered(1)
    return pl.pallas_call(
        _hyconv_kernel,
        grid=(B, D_HY // HY_LANES, FFT_R // FFT_K1_CHUNK),
        in_specs=[
            pl.BlockSpec((None, L, HY_LANES), lambda b, j, c: (b, 0, u_col + j), pipeline_mode=single),
            pl.BlockSpec((None, L, HY_LANES), lambda b, j, c: (b, 0, gate_col + j), pipeline_mode=single),
            pl.BlockSpec((2, FFT_K1_CHUNK * FFT_R, HY_LANES), lambda b, j, c: (0, c, kf_col + j)),
            pl.BlockSpec((1, HY_LANES), lambda b, j, c: (0, j)),
            pl.BlockSpec(g_half.shape, lambda b, j, c: (0, 0, 0), pipeline_mode=single),
            pl.BlockSpec(h_half.shape, lambda b, j, c: (0, 0, 0), pipeline_mode=single),
            pl.BlockSpec(f_fwd.shape, lambda b, j, c: (0, 0)),
            pl.BlockSpec(f_inv.shape, lambda b, j, c: (0, 0)),
        ],
        out_specs=pl.BlockSpec((None, L, HY_LANES), lambda b, j, c: (b, 0, j)),
        out_shape=jax.ShapeDtypeStruct((B, L, D_HY), F32),
        scratch_shapes=[pltpu.VMEM((FFT_R * FFT_PITCH, HY_LANES), F32)] * 2,
        compiler_params=_cparams(("parallel", "parallel", "arbitrary")),
        name="hyena_conv",
    )(u, gate, kf, skip, g_half, h_half, f_fwd, f_inv)


def _slab_major(a, n1):
    B, L, C = a.shape
    return a.reshape(B, n1, L // n1, C).transpose(0, 2, 1, 3).reshape(B, L, C)


def _hyena(zh, p):
    B, L, _ = zh.shape
    assert 2 * L == FFT_N
    g_full, h_half, f_fwd, f_inv = _fft_tables()
    tables = (g_full[:, :, :FFT_R // 2], h_half, f_fwd, f_inv)
    filt, filt_sum = _filters(p, L)
    both_dirs = filt_sum[:, :]
    kf = _filter_spectrum(filt, both_dirs, g_full, f_fwd)
    zc = _slab_major(_shortconv(zh, p['hy_conv_w'], p['hy_conv_b'][None]), FFT_R // 2)
    lanes = D_HY // HY_LANES
    y = _hyconv(zc, 0, zc, lanes, kf, 0, p['hy_skip'][0:1], tables)
    y = _hyconv(y, 0, zc, 2 * lanes, kf, lanes, p['hy_skip'][1:2], tables)
    return _slab_major(y, L // (FFT_R // 2))


def kernel(x, c, ctx, c_ctx, norm1_g, norm2_g, w_ada, b_ada, w_in, b_in, hy_conv_w, hy_conv_b, hy_f1_w, hy_f1_b, hy_f2_w, hy_f2_b, hy_f3_w, hy_f3_b, hy_f4_w, hy_sin_freq, hy_skip, q_norm_g, k_norm_g, na_rpb, w_hy_out, w_na_out, w_out, peer_w_q, peer_keys, peer_u, peer_v):
    assert w_in.shape[0] == 1, "single-layer block"
    B, S, D = x.shape
    C = ctx.shape[1]
    T = B * S

    m_lat = (jax.nn.silu(c) @ w_ada[0] + b_ada[0]).reshape(B, N_MOD, 1, D)
    m_ctx = (jax.nn.silu(c_ctx) @ w_ada[0] + b_ada[0]).reshape(N_MOD, 1, D)
    sh1, sc1, g1, sh2, sc2, g2 = [m_lat[:, i] for i in range(N_MOD)]

    w_in_bf = w_in[0].astype(BF16)
    b_in2 = b_in[0][None, :]
    gain1 = norm1_g[0][None, :]
    qg = jnp.tile(q_norm_g[0], NA_HEADS)[None, :]
    kg = jnp.tile(k_norm_g[0], NA_HEADS)[None, :]
    seg = jnp.asarray(np.kron(np.eye(NA_HEADS), np.full((HEAD_DIM, HEAD_DIM), 1.0 / HEAD_DIM)), dtype=BF16)
    cos, sin = _rope_tables(S)

    x2 = x.reshape(T, D)
    zh, q, k, v, gate = _inproj(x2, sh1, sc1, gain1, w_in_bf, b_in2, cos, sin, qg, kg, seg, S)

    k_ctx, v_ctx = _ctxkv(ctx.reshape(B * C, D), m_ctx[0], m_ctx[1], gain1,
                          w_in_bf[:, COL_K:COL_V], b_in2[:, COL_K:COL_V],
                          w_in_bf[:, COL_V:COL_G_HY], b_in2[:, COL_V:COL_G_HY], kg, seg)

    p = {'hy_conv_w': hy_conv_w[0], 'hy_conv_b': hy_conv_b[0], 'hy_f1_w': hy_f1_w[0], 'hy_f1_b': hy_f1_b[0],
         'hy_f2_w': hy_f2_w[0], 'hy_f2_b': hy_f2_b[0], 'hy_f3_w': hy_f3_w[0], 'hy_f3_b': hy_f3_b[0],
         'hy_f4_w': hy_f4_w[0], 'hy_sin_freq': hy_sin_freq[0], 'hy_skip': hy_skip[0]}
    y_hy = _hyena(zh.reshape(B, S, 3 * D_HY), p).reshape(T, D_HY)

    y_na = _natten(q.reshape(B, S, D_NA), k.reshape(B, S, D_NA), v.reshape(B, S, D_NA),
                   k_ctx.reshape(B, C, D_NA), v_ctx.reshape(B, C, D_NA), _natten_bias(na_rpb[0], S // GRID_W)).reshape(T, D_NA)

    x1, h2, pq = _postmix(y_hy, y_na, gate, x2, g1, sh2, sc2, norm2_g[0][None, :],
                          w_hy_out[0].astype(BF16), w_na_out[0].astype(BF16), w_out[0].astype(BF16),
                          peer_w_q[0].astype(BF16), S)

    keys_bf = peer_keys[0].astype(BF16).reshape(2 * PEER_HEADS, PEER_N_KEYS, PEER_D_HALF)
    sel_a, sel_b, sel_g = _route(pq, keys_bf)
    w = _wbuild(sel_a, sel_b, sel_g)
    out = _peer_dense(h2, w, peer_u[0].astype(BF16), peer_v[0].astype(BF16), x1, g2, S)
    return out.reshape(B, S, D)
```

```python
import functools
import math

import numpy as np
import jax
import jax.numpy as jnp
from jax import lax
from jax.experimental import pallas as pl
from jax.experimental.pallas import tpu as pltpu

F32 = jnp.float32
BF16 = jnp.bfloat16

D_MODEL = 1024
GRID_W = 64
EPS = 1e-6
N_MOD = 6

D_HY = 512
HY_ORDER = 2
HY_BANDS = 16
HY_DECAY_TARGET = 1e-2
HY_FAST_PCT = 0.3
HY_SLOW_PCT = 1.5

NA_HEADS = 8
HEAD_DIM = 64
D_NA = NA_HEADS * HEAD_DIM
WIN_ROWS = 8
WIN_COLS = 16
ROPE_THETA = 10000.0
NEG_INF = -1e30

PEER_HEADS = 8
PEER_N_KEYS = 128
PEER_N_EXPERTS = PEER_N_KEYS * PEER_N_KEYS
PEER_TOPK = 16
PEER_D_KEY = 256
PEER_D_HALF = PEER_D_KEY // 2

COL_HY = 0
COL_Q = COL_HY + 3 * D_HY
COL_K = COL_Q + D_NA
COL_V = COL_K + D_NA
COL_G_HY = COL_V + D_NA
N_PROJ = COL_G_HY + 2 * D_MODEL

VMEM_LIMIT = 56 * 1024 * 1024
TOK_TILE = 256
PEER_TOK_TILE = 512
PEER_EXP_TILE = 2048


def _cparams(sem):
    return pltpu.CompilerParams(dimension_semantics=sem, vmem_limit_bytes=VMEM_LIMIT)


def _dot(a, b):
    return jnp.dot(a, b, preferred_element_type=F32)


def _dot_nt(a, b):
    return lax.dot_general(a, b, (((1,), (1,)), ((), ())), preferred_element_type=F32)


def _modulated_norm(x, gain, shift, scale):
    ms = jnp.mean(x * x, axis=-1, keepdims=True)
    return (x * lax.rsqrt(ms + EPS) * gain) * (1.0 + scale) + shift


def _head_rms(z, seg, gain):
    ms = _dot((z * z).astype(BF16), seg)
    return z * lax.rsqrt(ms + EPS) * gain


def _rope(z, cos, sin_signed):
    n = z.shape[-1]
    lane = lax.broadcasted_iota(jnp.int32, z.shape, 1)
    first = (lane // (HEAD_DIM // 4)) % 2 == 0
    partner = jnp.where(first, pltpu.roll(z, n - HEAD_DIM // 4, 1), pltpu.roll(z, HEAD_DIM // 4, 1))
    return z * cos + partner * sin_signed


def _inproj_kernel(x_ref, shift_ref, scale_ref, g_ref, w_ref, b_ref, cos_ref, sin_ref, qg_ref, kg_ref,
                   seg_ref, zh_ref, q_ref, k_ref, v_ref, gate_ref):
    h = _modulated_norm(x_ref[...], g_ref[...], shift_ref[...], scale_ref[...]).astype(BF16)

    def proj(lo, hi):
        return _dot(h, w_ref[:, lo:hi]) + b_ref[:, lo:hi]

    zh_ref[...] = proj(COL_HY, COL_Q)
    cos, sin = cos_ref[...], sin_ref[...]
    seg = seg_ref[...]
    q = _rope(_head_rms(proj(COL_Q, COL_K), seg, qg_ref[...]), cos, sin)
    q_ref[...] = q.astype(BF16)
    k = _rope(_head_rms(proj(COL_K, COL_V), seg, kg_ref[...]), cos, sin)
    k_ref[...] = k.astype(BF16)
    v_ref[...] = proj(COL_V, COL_G_HY).astype(BF16)
    gate_ref[...] = jax.nn.sigmoid(proj(COL_G_HY, N_PROJ)).astype(BF16)


def _inproj(x2, shift, scale, gain, w_bf, b_in, cos, sin, qg, kg, seg, seq):
    T = x2.shape[0]
    tiles_per_batch = seq // TOK_TILE
    row = lambda i: (i, 0)
    per_batch = lambda i: (i // tiles_per_batch, 0, 0)
    const = lambda i: (0, 0)
    pos = lambda i: (i % tiles_per_batch, 0)
    return pl.pallas_call(
        _inproj_kernel,
        grid=(T // TOK_TILE,),
        in_specs=[
            pl.BlockSpec((TOK_TILE, D_MODEL), row),
            pl.BlockSpec((None, 1, D_MODEL), per_batch),
            pl.BlockSpec((None, 1, D_MODEL), per_batch),
            pl.BlockSpec((1, D_MODEL), const),
            pl.BlockSpec((D_MODEL, N_PROJ), const),
            pl.BlockSpec((1, N_PROJ), const),
            pl.BlockSpec((TOK_TILE, D_NA), pos),
            pl.BlockSpec((TOK_TILE, D_NA), pos),
            pl.BlockSpec((1, D_NA), const),
            pl.BlockSpec((1, D_NA), const),
            pl.BlockSpec((D_NA, D_NA), const),
        ],
        out_specs=[
            pl.BlockSpec((TOK_TILE, 3 * D_HY), row),
            pl.BlockSpec((TOK_TILE, D_NA), row),
            pl.BlockSpec((TOK_TILE, D_NA), row),
            pl.BlockSpec((TOK_TILE, D_NA), row),
            pl.BlockSpec((TOK_TILE, 2 * D_MODEL), row),
        ],
        out_shape=[
            jax.ShapeDtypeStruct((T, 3 * D_HY), F32),
            jax.ShapeDtypeStruct((T, D_NA), BF16),
            jax.ShapeDtypeStruct((T, D_NA), BF16),
            jax.ShapeDtypeStruct((T, D_NA), BF16),
            jax.ShapeDtypeStruct((T, 2 * D_MODEL), BF16),
        ],
        compiler_params=_cparams(("parallel",)),
        name="inproj",
    )(x2, shift, scale, gain, w_bf, b_in, cos, sin, qg, kg, seg)


def _ctxkv_kernel(x_ref, shift_ref, scale_ref, g_ref, wk_ref, bk_ref, wv_ref, bv_ref, kg_ref, seg_ref,
                  k_ref, v_ref):
    h = _modulated_norm(x_ref[...], g_ref[...], shift_ref[...], scale_ref[...]).astype(BF16)
    k = _head_rms(_dot(h, wk_ref[...]) + bk_ref[...], seg_ref[...], kg_ref[...])
    k_ref[...] = k.astype(BF16)
    v_ref[...] = (_dot(h, wv_ref[...]) + bv_ref[...]).astype(BF16)


def _ctxkv(ctx2, shift, scale, gain, wk, bk, wv, bv, kg, seg):
    T = ctx2.shape[0]
    row = lambda i: (i, 0)
    const = lambda i: (0, 0)
    return pl.pallas_call(
        _ctxkv_kernel,
        grid=(T // TOK_TILE,),
        in_specs=[
            pl.BlockSpec((TOK_TILE, D_MODEL), row),
            pl.BlockSpec((1, D_MODEL), const),
            pl.BlockSpec((1, D_MODEL), const),
            pl.BlockSpec((1, D_MODEL), const),
            pl.BlockSpec((D_MODEL, D_NA), const),
            pl.BlockSpec((1, D_NA), const),
            pl.BlockSpec((D_MODEL, D_NA), const),
            pl.BlockSpec((1, D_NA), const),
            pl.BlockSpec((1, D_NA), const),
            pl.BlockSpec((D_NA, D_NA), const),
        ],
        out_specs=[pl.BlockSpec((TOK_TILE, D_NA), row), pl.BlockSpec((TOK_TILE, D_NA), row)],
        out_shape=[jax.ShapeDtypeStruct((T, D_NA), BF16), jax.ShapeDtypeStruct((T, D_NA), BF16)],
        compiler_params=_cparams(("parallel",)),
        name="ctxkv",
    )(ctx2, shift, scale, gain, wk, bk, wv, bv, kg, seg)


NA_BLOCK_ROWS = 4
NA_KEY_ROWS = WIN_ROWS + NA_BLOCK_ROWS
NA_CLASSES = 3


def _natten_kernel(q_ref, k_ref, v_ref, kc_ref, vc_ref, bias_ref, o_ref, *, rows):
    r0 = pl.program_id(1) * NA_BLOCK_ROWS
    key_start = jnp.clip(r0 - WIN_ROWS // 2, 0, rows - NA_KEY_ROWS)
    start = pl.multiple_of(key_start * GRID_W, GRID_W * NA_BLOCK_ROWS)
    n_q = NA_BLOCK_ROWS * GRID_W
    n_loc = NA_KEY_ROWS * GRID_W
    scale = jnp.asarray(HEAD_DIM ** -0.5, BF16)
    pair = 2 * HEAD_DIM
    lane = lax.broadcasted_iota(jnp.int32, (n_q, pair), 1)
    for p in range(NA_HEADS // 2):
        cols = slice(p * pair, (p + 1) * pair)
        qp = q_ref[:, cols] * scale
        kp = k_ref[pl.ds(start, n_loc), cols]
        vp = v_ref[pl.ds(start, n_loc), cols]
        kcp = kc_ref[:, cols]
        vcp = vc_ref[:, cols]
        outs = []
        for hh in range(2):
            qm = jnp.where((lane // HEAD_DIM) == hh, qp, jnp.zeros_like(qp))
            s_loc = _dot_nt(qm, kp) + bias_ref[2 * p + hh]
            s_ctx = _dot_nt(qm, kcp)
            m = jnp.maximum(jnp.max(s_loc, axis=-1, keepdims=True), jnp.max(s_ctx, axis=-1, keepdims=True))
            e_loc = jnp.exp(s_loc - m)
            e_ctx = jnp.exp(s_ctx - m)
            denom = jnp.sum(e_loc, axis=-1, keepdims=True) + jnp.sum(e_ctx, axis=-1, keepdims=True)
            o = _dot(e_loc.astype(BF16), vp) + _dot(e_ctx.astype(BF16), vcp)
            outs.append(o / denom)
        o_ref[:, cols] = jnp.where((lane // HEAD_DIM) == 0, outs[0], outs[1]).astype(o_ref.dtype)


def _natten(q, k, v, kc, vc, bias):
    B, S, _ = q.shape
    rows = S // GRID_W
    n_blk = rows // NA_BLOCK_ROWS
    C = kc.shape[1]
    n_q = NA_BLOCK_ROWS * GRID_W
    single = pl.Buffered(1)
    block_class = lambda b, i: (jnp.where(i == 0, 0, jnp.where(i == n_blk - 1, 2, 1)), 0, 0, 0)
    return pl.pallas_call(
        functools.partial(_natten_kernel, rows=rows),
        grid=(B, n_blk),
        in_specs=[
            pl.BlockSpec((None, n_q, D_NA), lambda b, i: (b, i, 0)),
            pl.BlockSpec((None, S, D_NA), lambda b, i: (b, 0, 0), pipeline_mode=single),
            pl.BlockSpec((None, S, D_NA), lambda b, i: (b, 0, 0), pipeline_mode=single),
            pl.BlockSpec((None, C, D_NA), lambda b, i: (b, 0, 0)),
            pl.BlockSpec((None, C, D_NA), lambda b, i: (b, 0, 0)),
            pl.BlockSpec((None,) + bias.shape[1:], block_class),
        ],
        out_specs=pl.BlockSpec((None, n_q, D_NA), lambda b, i: (b, i, 0)),
        out_shape=jax.ShapeDtypeStruct((B, S, D_NA), BF16),
        compiler_params=_cparams(("parallel", "arbitrary")),
        name="natten",
    )(q, k, v, kc, vc, bias)


def _natten_bias(rpb, rows):
    col = np.arange(GRID_W)
    dc = np.clip(col[None, :] - col[:, None], -(WIN_COLS - 1), WIN_COLS - 1) + (WIN_COLS - 1)
    col_start = np.clip(col - WIN_COLS // 2, 0, GRID_W - WIN_COLS)
    col_mask = (col[None, :] >= col_start[:, None]) & (col[None, :] < col_start[:, None] + WIN_COLS)
    r0 = np.array([0, 2 * NA_BLOCK_ROWS, rows - NA_BLOCK_ROWS])
    r = r0[:, None] + np.arange(NA_BLOCK_ROWS)[None, :]
    key_row = np.clip(r0 - WIN_ROWS // 2, 0, rows - NA_KEY_ROWS)[:, None, None] + np.arange(NA_KEY_ROWS)
    win_start = np.clip(r - WIN_ROWS // 2, 0, rows - WIN_ROWS)[:, :, None]
    row_mask = (key_row >= win_start) & (key_row < win_start + WIN_ROWS)
    dr = np.clip(key_row - r[:, :, None] + (WIN_ROWS - 1), 0, 2 * WIN_ROWS - 2)
    pick_c = jnp.asarray(dc[None] == np.arange(2 * WIN_COLS - 1)[:, None, None], dtype=F32)
    pick_r = jnp.asarray(dr[..., None] == np.arange(2 * WIN_ROWS - 1), dtype=F32)
    b = jnp.einsum('crkd,hde,eqx->chrqkx', pick_r, rpb.astype(F32), pick_c, precision=HI)
    mask = row_mask[:, None, :, None, :, None] & col_mask[None, None, None, :, None, :]
    b = jnp.where(mask, b, NEG_INF)
    return b.reshape(NA_CLASSES, NA_HEADS, NA_BLOCK_ROWS * GRID_W, NA_KEY_ROWS * GRID_W)


def _rope_tables(seq):
    pos = jnp.arange(seq, dtype=jnp.int32)
    rows = (pos // GRID_W).astype(F32)
    cols = (pos % GRID_W).astype(F32)
    nf = HEAD_DIM // 4
    inv = ROPE_THETA ** (-jnp.arange(nf, dtype=F32) / nf)
    ar = rows[:, None] * inv[None, :]
    ac = cols[:, None] * inv[None, :]
    cos = jnp.concatenate([jnp.cos(ar), jnp.cos(ar), jnp.cos(ac), jnp.cos(ac)], axis=-1)
    sin = jnp.concatenate([-jnp.sin(ar), jnp.sin(ar), -jnp.sin(ac), jnp.sin(ac)], axis=-1)
    return jnp.tile(cos, (1, NA_HEADS)), jnp.tile(sin, (1, NA_HEADS))


def _postmix_kernel(yh_ref, yn_ref, gate_ref, x_ref, g1_ref, sh2_ref, sc2_ref, n2_ref, whyo_ref, wnao_ref,
                    wout_ref, wq_ref, x1_ref, h2_ref, pq_ref):
    a = _dot(yh_ref[...].astype(BF16), whyo_ref[...])
    b = _dot(yn_ref[...], wnao_ref[...])
    gate = gate_ref[...].astype(F32)
    merged = gate[:, :D_MODEL] * a + gate[:, D_MODEL:] * b
    x1 = x_ref[...] + g1_ref[...] * _dot(merged.astype(BF16), wout_ref[...])
    x1_ref[...] = x1
    h2 = _modulated_norm(x1, n2_ref[...], sh2_ref[...], sc2_ref[...]).astype(BF16)
    h2_ref[...] = h2
    for hp in range(2 * PEER_HEADS):
        pq_ref[hp] = _dot(h2, wq_ref[:, hp * PEER_D_HALF:(hp + 1) * PEER_D_HALF]).astype(BF16)


def _postmix(yh, yn, gate, x2, g1, sh2, sc2, n2g, whyo, wnao, wout, wq, seq):
    T = x2.shape[0]
    tiles_per_batch = seq // TOK_TILE
    row = lambda i: (i, 0)
    per_batch = lambda i: (i // tiles_per_batch, 0, 0)
    const = lambda i: (0, 0)
    nq = wq.shape[1]
    return pl.pallas_call(
        _postmix_kernel,
        grid=(T // TOK_TILE,),
        in_specs=[
            pl.BlockSpec((TOK_TILE, D_HY), row),
            pl.BlockSpec((TOK_TILE, D_NA), row),
            pl.BlockSpec((TOK_TILE, 2 * D_MODEL), row),
            pl.BlockSpec((TOK_TILE, D_MODEL), row),
            pl.BlockSpec((None, 1, D_MODEL), per_batch),
            pl.BlockSpec((None, 1, D_MODEL), per_batch),
            pl.BlockSpec((None, 1, D_MODEL), per_batch),
            pl.BlockSpec((1, D_MODEL), const),
            pl.BlockSpec((D_HY, D_MODEL), const),
            pl.BlockSpec((D_NA, D_MODEL), const),
            pl.BlockSpec((D_MODEL, D_MODEL), const),
            pl.BlockSpec((D_MODEL, nq), const),
        ],
        out_specs=[
            pl.BlockSpec((TOK_TILE, D_MODEL), row),
            pl.BlockSpec((TOK_TILE, D_MODEL), row),
            pl.BlockSpec((nq // PEER_D_HALF, TOK_TILE, PEER_D_HALF), lambda i: (0, i, 0)),
        ],
        out_shape=[
            jax.ShapeDtypeStruct((T, D_MODEL), F32),
            jax.ShapeDtypeStruct((T, D_MODEL), BF16),
            jax.ShapeDtypeStruct((nq // PEER_D_HALF, T, PEER_D_HALF), BF16),
        ],
        compiler_params=_cparams(("parallel",)),
        name="postmix",
    )(yh, yn, gate, x2, g1, sh2, sc2, n2g, whyo, wnao, wout, wq)


SUBLANES = 8
LANES = 128


def _argmax_tree(vals, idxs):
    while len(vals) > 1:
        nv, ni = [], []
        for j in range(0, len(vals) - 1, 2):
            right = vals[j + 1] > vals[j]
            nv.append(jnp.where(right, vals[j + 1], vals[j]))
            ni.append(jnp.where(right, idxs[j + 1], idxs[j]))
        if len(vals) % 2:
            nv.append(vals[-1])
            ni.append(idxs[-1])
        vals, idxs = nv, ni
    return vals[0], idxs[0]


W_BLOCK = 128
W_PITCH = 136


SORT_TILE = SUBLANES * LANES


def _batcher_sort_network(n):
    ces = []

    def merge(lo, m, r):
        step = r * 2
        if step < m:
            merge(lo, m, step)
            merge(lo + r, m, step)
            ces.extend((i, i + r) for i in range(lo + r, lo + m - r, step))
        else:
            ces.append((lo, lo + r))

    def sort(lo, m):
        if m > 1:
            sort(lo, m // 2)
            sort(lo + m // 2, m // 2)
            merge(lo, m, 1)

    sort(0, n)
    return ces


def _bitonic_merge_network(n):
    ces, s = [], n // 2
    while s >= 1:
        ces.extend((i, i + s) for i in range(n) if (i & s) == 0)
        s //= 2
    return ces


SORT_NET = _batcher_sort_network(PEER_TOPK)
MERGE_NET = _bitonic_merge_network(PEER_TOPK)


def _compare_exchange(v, x, net):
    for i, j in net:
        c = v[j] > v[i]
        v[i], v[j] = jnp.where(c, v[j], v[i]), jnp.where(c, v[i], v[j])
        x[i], x[j] = jnp.where(c, x[j], x[i]), jnp.where(c, x[i], x[j])


def _merge_top(a, b, dropped):
    (va, xa), (vb, xb) = a, b
    k = len(va)
    v, x = [], []
    for r in range(k):
        c = vb[k - 1 - r] > va[r]
        v.append(jnp.where(c, vb[k - 1 - r], va[r]))
        x.append(jnp.where(c, xb[k - 1 - r], xa[r]))
        dropped = jnp.maximum(dropped, jnp.where(c, va[r], vb[k - 1 - r]))
    _compare_exchange(v, x, MERGE_NET)
    return (v, x), dropped


def _merge_all(lists, dropped):
    while len(lists) > 1:
        nxt = []
        for j in range(0, len(lists) - 1, 2):
            m, dropped = _merge_top(lists[j], lists[j + 1], dropped)
            nxt.append(m)
        if len(lists) % 2:
            nxt.append(lists[-1])
        lists = nxt
    return lists[0], dropped


def _settle_ties(v, x, dropped):
    x = list(x)
    for r in range(len(v) - 1):
        swap = (v[r] == v[r + 1]) & (x[r] > x[r + 1])
        x[r], x[r + 1] = jnp.where(swap, x[r + 1], x[r]), jnp.where(swap, x[r], x[r + 1])
    risk = v[-1] <= dropped
    for r in range(len(v) - 1):
        risk = risk | ((v[r] == v[r + 1]) & (x[r] > x[r + 1]))
    return x, risk.astype(jnp.int32)


def _level2_chains():
    pairs = [(a, b) for a in range(PEER_TOPK) for b in range(PEER_TOPK) if (a + 1) * (b + 1) <= PEER_TOPK]
    n_rows = 4
    chains = [[(a, b) for (a2, b) in pairs if a2 == a] for a in range(n_rows)]
    rest = [p for p in pairs if p[0] >= n_rows]
    for b in sorted({b for _, b in rest}):
        chains.append([(a, b2) for (a, b2) in rest if b2 == b])
    assert sorted(sum(chains, [])) == pairs
    return pairs, chains


def _route_kernel(pq_ref, keys_ref, a_ref, b_ref, g_ref, sx_ref, tv_ref, ti_ref, bs_ref, es_ref,
                  oa_ref, ob_ref, og_ref):
    n = PEER_N_KEYS
    k = PEER_TOPK
    shape = (SUBLANES, LANES)
    neg = jnp.full(shape, -jnp.inf, F32)
    pairs, chains = _level2_chains()

    def scores(hp):
        s = _dot_nt(keys_ref[hp], pq_ref[hp])
        for rg in range(n // SUBLANES):
            for lt in range(SUBLANES):
                sx_ref[pl.ds(rg * SUBLANES * SUBLANES + lt, SUBLANES, stride=SUBLANES), :] = (
                    s[rg * SUBLANES:(rg + 1) * SUBLANES, lt * LANES:(lt + 1) * LANES])
        return [sx_ref[pl.ds(key * SUBLANES, SUBLANES), :] for key in range(n)]

    def store_top(hp, v, x):
        for r in range(k):
            tv_ref[hp, r] = v[r]
            ti_ref[hp, r] = x[r]

    def level1(hp, carry):
        vals = scores(hp)
        lists = []
        for q in range(n // k):
            v = vals[q * k:(q + 1) * k]
            x = [jnp.full(shape, q * k + r, jnp.int32) for r in range(k)]
            _compare_exchange(v, x, SORT_NET)
            lists.append((v, x))
        (v, x), dropped = _merge_all(lists, neg)
        x, risk = _settle_ties(v, x, dropped)
        store_top(hp, v, x)

        @pl.when(jnp.max(risk) > 0)
        def _():
            level1_exact(hp)

        return carry

    def level1_exact(hp):
        vals = scores(hp)
        idx = [jnp.full(shape, key, jnp.int32) for key in range(n)]
        v, x = [], []
        for r in range(k):
            best, where_ = _argmax_tree(vals, idx)
            v.append(best)
            x.append(where_)
            vals = [jnp.where(where_ == key, -jnp.inf, vals[key]) for key in range(n)]
        store_top(hp, v, x)

    def candidates(h):
        s1 = [tv_ref[2 * h, r] for r in range(k)]
        s2 = [tv_ref[2 * h + 1, r] for r in range(k)]
        e1 = [ti_ref[2 * h, r] * n for r in range(k)]
        e2 = [ti_ref[2 * h + 1, r] for r in range(k)]
        return (lambda a, b: s1[a] + s2[b]), (lambda a, b: e1[a] + e2[b] + (a * k + b) * PEER_N_EXPERTS)

    def store_best(h, v, x):
        for r in range(k):
            bs_ref[h, r] = v[r]
            es_ref[h, r] = x[r]

    def level2(h, carry):
        val, expert = candidates(h)
        lists = []
        for chain in chains:
            pad = k - len(chain)
            lists.append(([val(a, b) for a, b in chain] + [neg] * pad,
                          [expert(a, b) for a, b in chain] + [jnp.zeros(shape, jnp.int32)] * pad))
        (v, x), dropped = _merge_all(lists, neg)
        x, risk = _settle_ties(v, x, dropped)
        store_best(h, v, x)

        @pl.when(jnp.max(risk) > 0)
        def _():
            level2_exact(h)

        return carry

    def level2_exact(h):
        val, expert = candidates(h)
        cand = [val(a, b) for a, b in pairs]
        exps = [expert(a, b) for a, b in pairs]
        pos = [jnp.full(shape, a * k + b, jnp.int32) for a, b in pairs]
        v, x = [], []
        for r in range(k):
            best, where_ = _argmax_tree(cand, pos)
            chosen = jnp.zeros(shape, jnp.int32)
            for m, (a, b) in enumerate(pairs):
                hit = where_ == (a * k + b)
                cand[m] = jnp.where(hit, -jnp.inf, cand[m])
                chosen = jnp.where(hit, exps[m], chosen)
            v.append(best)
            x.append(chosen)
        store_best(h, v, x)

    lax.fori_loop(0, 2 * PEER_HEADS, level1, 0)
    lax.fori_loop(0, PEER_HEADS, level2, 0)

    for h in range(PEER_HEADS):
        best = [bs_ref[h, r] for r in range(k)]
        ex = [jnp.exp(v - best[0]) for v in best]
        denom = functools.reduce(lambda p, q: p + q, ex)
        for r in range(k):
            rows = pl.ds((h * k + r) * SUBLANES, SUBLANES)
            e = jnp.bitwise_and(es_ref[h, r], PEER_N_EXPERTS - 1)
            oa_ref[rows, :] = jnp.right_shift(e, n.bit_length() - 1)
            ob_ref[rows, :] = jnp.bitwise_and(e, n - 1)
            og_ref[rows, :] = ex[r] / denom
    for s in range(SUBLANES):
        rows = pl.ds(s * LANES, LANES)
        plane = pl.ds(s, PEER_HEADS * k, stride=SUBLANES)
        a_ref[rows, :] = oa_ref[plane, :].T
        b_ref[rows, :] = ob_ref[plane, :].T
        g_ref[rows, :] = og_ref[plane, :].T


def _route(pq, keys_bf):
    n_hp, T, _ = pq.shape
    n_sel = PEER_HEADS * PEER_TOPK
    row = lambda i: (i, 0)
    reg = (SUBLANES, LANES)
    return pl.pallas_call(
        _route_kernel,
        grid=(T // SORT_TILE,),
        in_specs=[
            pl.BlockSpec((n_hp, SORT_TILE, PEER_D_HALF), lambda i: (0, i, 0)),
            pl.BlockSpec((n_hp, PEER_N_KEYS, PEER_D_HALF), lambda i: (0, 0, 0)),
        ],
        out_specs=[pl.BlockSpec((SORT_TILE, n_sel), row)] * 3,
        out_shape=[
            jax.ShapeDtypeStruct((T, n_sel), jnp.int32),
            jax.ShapeDtypeStruct((T, n_sel), jnp.int32),
            jax.ShapeDtypeStruct((T, n_sel), F32),
        ],
        scratch_shapes=[
            pltpu.VMEM((PEER_N_KEYS * SUBLANES, LANES), F32),
            pltpu.VMEM((n_hp, PEER_TOPK) + reg, F32),
            pltpu.VMEM((n_hp, PEER_TOPK) + reg, jnp.int32),
            pltpu.VMEM((PEER_HEADS, PEER_TOPK) + reg, F32),
            pltpu.VMEM((PEER_HEADS, PEER_TOPK) + reg, jnp.int32),
            pltpu.VMEM((n_sel * SUBLANES, LANES), jnp.int32),
            pltpu.VMEM((n_sel * SUBLANES, LANES), jnp.int32),
            pltpu.VMEM((n_sel * SUBLANES, LANES), F32),
        ],
        compiler_params=_cparams(("parallel",)),
        name="route",
    )(pq, keys_bf)


WBUILD_UNROLL = 32


def _wbuild_kernel(a_ref, b_ref, g_ref, w_ref, s_ref):
    n = PEER_N_KEYS
    tb = a_ref.shape[0]
    key = lax.broadcasted_iota(jnp.int32, (n, a_ref.shape[1]), 0)

    def body(t, carry):
        a = a_ref[pl.ds(t, 1), :]
        b = b_ref[pl.ds(t, 1), :]
        g = g_ref[pl.ds(t, 1), :]
        ga = jnp.where(a == key, g, 0.0).astype(BF16)
        ob = jnp.where(b == key, 1.0, 0.0).astype(BF16)
        s_ref[pl.ds(pl.multiple_of(t * W_PITCH, SUBLANES), n), :] = _dot_nt(ga, ob)
        return carry

    lax.fori_loop(0, tb, body, 0, unroll=WBUILD_UNROLL)

    def emit(i1, carry):
        w_ref[i1] = s_ref[pl.ds(i1, tb, stride=W_PITCH), :].astype(BF16)
        return carry

    lax.fori_loop(0, n, emit, 0, unroll=WBUILD_UNROLL)


def _wbuild(a, b, g):
    T, n_sel = a.shape
    row = lambda i: (i, 0)
    return pl.pallas_call(
        _wbuild_kernel,
        grid=(T // W_BLOCK,),
        in_specs=[pl.BlockSpec((W_BLOCK, n_sel), row)] * 3,
        out_specs=pl.BlockSpec((None, PEER_N_KEYS, W_BLOCK, PEER_N_KEYS), lambda i: (i, 0, 0, 0)),
        out_shape=jax.ShapeDtypeStruct((T // W_BLOCK, PEER_N_KEYS, W_BLOCK, PEER_N_KEYS), BF16),
        scratch_shapes=[pltpu.VMEM((W_BLOCK * W_PITCH, PEER_N_KEYS), F32)],
        compiler_params=_cparams(("parallel",)),
        name="wbuild",
    )(a, b, g)


def _peer_dense_kernel(h2_ref, w_ref, u_ref, v_ref, x1_ref, g2_ref, o_ref, acc_ref):
    e = pl.program_id(1)

    @pl.when(e == 0)
    def _():
        acc_ref[...] = jnp.zeros_like(acc_ref)

    a = _dot_nt(h2_ref[...], u_ref[...])
    act = 0.5 * a * (1.0 + lax.erf(a * (2.0 ** -0.5)))
    w = jnp.concatenate(
        [jnp.concatenate([w_ref[blk, i] for i in range(w_ref.shape[1])], axis=1) for blk in range(w_ref.shape[0])],
        axis=0)
    acc_ref[...] += _dot((w.astype(F32) * act).astype(BF16), v_ref[...])

    @pl.when(e == pl.num_programs(1) - 1)
    def _():
        o_ref[...] = x1_ref[...] + g2_ref[...] * acc_ref[...]


def _peer_dense(h2, w, u_bf, v_bf, x1, g2, seq):
    T = h2.shape[0]
    tiles_per_batch = seq // PEER_TOK_TILE
    n_exp = u_bf.shape[0]
    return pl.pallas_call(
        _peer_dense_kernel,
        grid=(T // PEER_TOK_TILE, n_exp // PEER_EXP_TILE),
        in_specs=[
            pl.BlockSpec((PEER_TOK_TILE, D_MODEL), lambda i, e: (i, 0)),
            pl.BlockSpec((PEER_TOK_TILE // W_BLOCK, PEER_EXP_TILE // PEER_N_KEYS, W_BLOCK, PEER_N_KEYS),
                         lambda i, e: (i, e, 0, 0)),
            pl.BlockSpec((PEER_EXP_TILE, D_MODEL), lambda i, e: (e, 0)),
            pl.BlockSpec((PEER_EXP_TILE, D_MODEL), lambda i, e: (e, 0)),
            pl.BlockSpec((PEER_TOK_TILE, D_MODEL), lambda i, e: (i, 0)),
            pl.BlockSpec((None, 1, D_MODEL), lambda i, e: (i // tiles_per_batch, 0, 0)),
        ],
        out_specs=pl.BlockSpec((PEER_TOK_TILE, D_MODEL), lambda i, e: (i, 0)),
        out_shape=jax.ShapeDtypeStruct((T, D_MODEL), F32),
        scratch_shapes=[pltpu.VMEM((PEER_TOK_TILE, D_MODEL), F32)],
        compiler_params=_cparams(("parallel", "arbitrary")),
        name="peer_dense",
    )(h2, w, u_bf, v_bf, x1, g2)


HI = lax.Precision.HIGHEST


HY_LANES = 128


def _shortconv_kernel(z_ref, w_ref, b_ref, o_ref):
    z = z_ref[...]
    n = z.shape[0]
    row = lax.broadcasted_iota(jnp.int32, z.shape, 0)
    prev = jnp.where(row == 0, 0.0, pltpu.roll(z, 1, 0))
    nxt = jnp.where(row == n - 1, 0.0, pltpu.roll(z, n - 1, 0))
    o_ref[...] = w_ref[0:1] * prev + w_ref[1:2] * z + w_ref[2:3] * nxt + b_ref[...]


def _shortconv(zh, w, b):
    B, L, C = zh.shape
    return pl.pallas_call(
        _shortconv_kernel,
        grid=(B, C // HY_LANES),
        in_specs=[
            pl.BlockSpec((None, L, HY_LANES), lambda i, j: (i, 0, j)),
            pl.BlockSpec((3, HY_LANES), lambda i, j: (0, j)),
            pl.BlockSpec((1, HY_LANES), lambda i, j: (0, j)),
        ],
        out_specs=pl.BlockSpec((None, L, HY_LANES), lambda i, j: (i, 0, j)),
        out_shape=jax.ShapeDtypeStruct((B, L, C), F32),
        compiler_params=_cparams(("parallel", "parallel")),
        name="shortconv",
    )(zh, w, b)


FFT_R = 128
FFT_N = FFT_R * FFT_R
FFT_PITCH = 136
FFT_UNROLL = 8
FFT_K1_USED = FFT_R // 2 + 1
FFT_K1_SLABS = 66
FFT_K1_ROWS = 72
FFT_K1_CHUNK = 22
assert FFT_K1_SLABS % FFT_K1_CHUNK == 0 and FFT_K1_CHUNK % 2 == 0 and FFT_K1_USED <= FFT_K1_SLABS <= FFT_K1_ROWS


def _fft_tables():
    r = jnp.arange(FFT_R, dtype=jnp.int32)
    kk = jnp.arange(FFT_K1_ROWS, dtype=jnp.int32)
    n2, k1, n1 = r[:, None, None], kk[None, :, None], r[None, None, :]
    ang = (2.0 * math.pi / FFT_N) * ((k1 * (FFT_R * n1 + n2)) % FFT_N).astype(F32)
    used = (k1 < FFT_K1_USED).astype(F32)
    c, s = jnp.cos(ang) * used, jnp.sin(ang) * used
    g_full = jnp.concatenate([c, -s], axis=1)
    twice = jnp.where((k1 == 0) | (k1 == FFT_R // 2), 1.0, 2.0) / FFT_N
    h_half = jnp.concatenate([jnp.swapaxes(c * twice, 1, 2), -jnp.swapaxes(s * twice, 1, 2)],
                             axis=2)[:, :FFT_R // 2]
    ang2 = (2.0 * math.pi / FFT_R) * ((r[:, None] * r[None, :]) % FFT_R).astype(F32)
    c2, s2 = jnp.cos(ang2), jnp.sin(ang2)
    f_fwd = jnp.concatenate([jnp.concatenate([c2, s2], 1), jnp.concatenate([-s2, c2], 1)], 0)
    f_inv = jnp.concatenate([jnp.concatenate([c2, -s2], 1), jnp.concatenate([s2, c2], 1)], 0)
    return g_full.astype(BF16), h_half.astype(BF16), f_fwd.astype(BF16), f_inv.astype(BF16)


def _fft_stage1(load_slab, g_ref, bre_ref, bim_ref):
    def body(n2, carry):
        out = _dot(g_ref[n2], load_slab(n2).astype(BF16))
        bre_ref[pl.ds(n2, FFT_K1_ROWS, stride=FFT_PITCH), :] = out[:FFT_K1_ROWS]
        bim_ref[pl.ds(n2, FFT_K1_ROWS, stride=FFT_PITCH), :] = out[FFT_K1_ROWS:]
        return carry

    lax.fori_loop(0, FFT_R, body, 0, unroll=FFT_UNROLL)


def _k1_rows(k1):
    return pl.ds(pl.multiple_of(k1 * FFT_PITCH, SUBLANES), FFT_R)


def _load_k1_pair(bre_ref, bim_ref, k1):
    return jnp.concatenate(
        [jnp.concatenate([bre_ref[_k1_rows(k1 + d), :], bim_ref[_k1_rows(k1 + d), :]], axis=0) for d in range(2)],
        axis=1)


FILT_ROWS = 512
FILT_FEAT = 128


def _filt_kernel(feat_ref, delta_ref, w1_ref, b1_ref, w2_ref, b2_ref, w3_ref, b3_ref, w4_ref, freq_ref,
                 h_ref, sum_ref):
    @pl.when(pl.program_id(0) == 0)
    def _():
        sum_ref[...] = jnp.zeros_like(sum_ref)

    feat = feat_ref[...]
    freq = freq_ref[...]
    dot = lambda a, b: jnp.dot(a, b, precision=HI, preferred_element_type=F32)
    h = jnp.sin(freq * (dot(feat, w1_ref[...]) + b1_ref[...]))
    h = jnp.sin(freq * (dot(h, w2_ref[...]) + b2_ref[...]))
    h = jnp.sin(freq * (dot(h, w3_ref[...]) + b3_ref[...]))
    hi = h.astype(BF16)
    lo = (h - hi.astype(F32)).astype(BF16)
    h = _dot(hi, w4_ref[...]) + _dot(lo, w4_ref[...])
    n_feat = 1 + 2 * HY_BANDS
    t = feat[:, 0:1]
    forward = feat[:, n_feat:n_feat + 1] > 0.5
    keep = feat[:, n_feat + 1:n_feat + 2]
    decay = jnp.exp(-t * delta_ref[...])
    outs = []
    for o in range(HY_ORDER):
        base = o * 2 * D_HY
        outs.append(jnp.where(forward, h[:, base:base + D_HY], h[:, base + D_HY:base + 2 * D_HY]) * decay)
    out = jnp.concatenate(outs, axis=1)
    sum_ref[...] += jnp.sum(jnp.abs(out), axis=0, keepdims=True)
    h_ref[...] = out * keep


def _filt_features(L):
    r = np.arange(FFT_N)
    n = FFT_R * (r % FFT_R) + r // FFT_R
    j = np.where(n < L, n, np.where(n > L, 2 * L - n, 0))
    jj = jnp.asarray(j, dtype=F32)[:, None]
    t = jj / (L - 1.0)
    w = (2.0 * math.pi / L) * jj
    bands = jnp.linspace(1e-4, HY_BANDS - 1.0, HY_BANDS, dtype=F32)[None, :]
    flags = jnp.asarray(np.stack([n < L, n != L], axis=1), dtype=F32)
    feats = jnp.concatenate([t, jnp.cos(bands * w), -jnp.sin(bands * w), flags], axis=-1)
    return jnp.pad(feats, ((0, 0), (0, FILT_FEAT - feats.shape[1])))


def _filters(p, L):
    feats = _filt_features(L)
    deltas = jnp.abs(jnp.linspace(math.log(HY_DECAY_TARGET) / HY_SLOW_PCT,
                                  math.log(HY_DECAY_TARGET) / HY_FAST_PCT, D_HY, dtype=F32))[None, :]
    w1 = jnp.pad(p['hy_f1_w'], ((0, FILT_FEAT - p['hy_f1_w'].shape[0]), (0, 0)))
    n_out = HY_ORDER * D_HY
    const = lambda i: (0, 0)
    full = lambda a: pl.BlockSpec(a.shape, const)
    args = [deltas, w1, p['hy_f1_b'][None], p['hy_f2_w'], p['hy_f2_b'][None], p['hy_f3_w'], p['hy_f3_b'][None],
            p['hy_f4_w'].astype(BF16), p['hy_sin_freq'][None]]
    return pl.pallas_call(
        _filt_kernel,
        grid=(FFT_N // FILT_ROWS,),
        in_specs=[pl.BlockSpec((FILT_ROWS, FILT_FEAT), lambda i: (i, 0))] + [full(a) for a in args],
        out_specs=[pl.BlockSpec((FILT_ROWS, n_out), lambda i: (i, 0)), pl.BlockSpec((1, n_out), const)],
        out_shape=[jax.ShapeDtypeStruct((FFT_N, n_out), F32), jax.ShapeDtypeStruct((1, n_out), F32)],
        compiler_params=_cparams(("arbitrary",)),
        name="hyena_filter",
    )(feats, *args)


def _fspec_kernel(h_ref, sum_ref, g_ref, f_ref, o_ref, bre_ref, bim_ref):
    c = pl.program_id(1)

    @pl.when(c == 0)
    def _():
        _fft_stage1(lambda n2: h_ref[pl.ds(pl.multiple_of(n2 * FFT_R, FFT_R), FFT_R), :], g_ref, bre_ref, bim_ref)

    inv = 1.0 / sum_ref[...]

    lanes = h_ref.shape[1]

    def slab_pair(j, carry):
        x = _dot(f_ref[...], _load_k1_pair(bre_ref, bim_ref, c * FFT_K1_CHUNK + 2 * j).astype(BF16))
        for d in range(2):
            rows = pl.ds(pl.multiple_of((2 * j + d) * FFT_R, FFT_R), FFT_R)
            o_ref[0, rows, :] = x[:FFT_R, d * lanes:(d + 1) * lanes] * inv
            o_ref[1, rows, :] = x[FFT_R:, d * lanes:(d + 1) * lanes] * inv
        return carry

    lax.fori_loop(0, FFT_K1_CHUNK // 2, slab_pair, 0, unroll=True)


def _filter_spectrum(h, hsum, g_full, f_fwd):
    n_ch = h.shape[1]
    single = pl.Buffered(1)
    return pl.pallas_call(
        _fspec_kernel,
        grid=(n_ch // HY_LANES, FFT_K1_SLABS // FFT_K1_CHUNK),
        in_specs=[
            pl.BlockSpec((FFT_N, HY_LANES), lambda j, c: (0, j), pipeline_mode=single),
            pl.BlockSpec((1, HY_LANES), lambda j, c: (0, j)),
            pl.BlockSpec(g_full.shape, lambda j, c: (0, 0, 0), pipeline_mode=single),
            pl.BlockSpec(f_fwd.shape, lambda j, c: (0, 0)),
        ],
        out_specs=pl.BlockSpec((2, FFT_K1_CHUNK * FFT_R, HY_LANES), lambda j, c: (0, c, j)),
        out_shape=jax.ShapeDtypeStruct((2, FFT_K1_SLABS * FFT_R, n_ch), F32),
        scratch_shapes=[pltpu.VMEM((FFT_K1_ROWS * FFT_PITCH, HY_LANES), F32)] * 2,
        compiler_params=_cparams(("parallel", "arbitrary")),
        name="hyena_filter_fft",
    )(h, hsum, g_full, f_fwd)


def _hyconv_kernel(u_ref, gate_ref, kf_ref, skip_ref, g_ref, h_ref, f_ref, fi_ref, o_ref, bre_ref, bim_ref):
    c = pl.program_id(2)
    half = FFT_R // 2

    def rows_of(n2):
        return pl.ds(pl.multiple_of(n2 * half, half), half)

    @pl.when(c == 0)
    def _():
        _fft_stage1(lambda n2: u_ref[rows_of(n2), :], g_ref, bre_ref, bim_ref)

    lanes = u_ref.shape[1]

    def slab_pair(j, carry):
        k1 = c * FFT_K1_CHUNK + 2 * j
        x = _dot(f_ref[...], _load_k1_pair(bre_ref, bim_ref, k1).astype(BF16))
        xr, xi = x[:FFT_R], x[FFT_R:]
        kr, ki = [jnp.concatenate([kf_ref[part, pl.ds(pl.multiple_of((2 * j + d) * FFT_R, FFT_R), FFT_R), :]
                                   for d in range(2)], axis=1) for part in range(2)]
        y = jnp.concatenate([xr * kr - xi * ki, xr * ki + xi * kr], axis=0)
        z = _dot(fi_ref[...], y.astype(BF16))
        for d in range(2):
            bre_ref[_k1_rows(k1 + d), :] = z[:FFT_R, d * lanes:(d + 1) * lanes]
            bim_ref[_k1_rows(k1 + d), :] = z[FFT_R:, d * lanes:(d + 1) * lanes]
        return carry

    lax.fori_loop(0, FFT_K1_CHUNK // 2, slab_pair, 0, unroll=True)

    @pl.when(c == pl.num_programs(2) - 1)
    def _():
        def body(n2, carry):
            z = jnp.concatenate([bre_ref[pl.ds(n2, FFT_K1_ROWS, stride=FFT_PITCH), :],
                                 bim_ref[pl.ds(n2, FFT_K1_ROWS, stride=FFT_PITCH), :]], axis=0)
            conv = _dot(h_ref[n2], z.astype(BF16))
            rows = rows_of(n2)
            o_ref[rows, :] = gate_ref[rows, :] * (conv + skip_ref[...] * u_ref[rows, :])
            return carry

        lax.fori_loop(0, FFT_R, body, 0, unroll=FFT_UNROLL)


def _hyconv(u, u_col, gate, gate_col, kf, kf_col, skip, tables):
    g_half, h_half, f_fwd, f_inv = tables
    B, L, _ = u.shape
    single = pl.Buffered(1)
    return pl.pallas_call(
        _hyconv_kernel,
        grid=(B, D_HY // HY_LANES, FFT_K1_SLABS // FFT_K1_CHUNK),
        in_specs=[
            pl.BlockSpec((None, L, HY_LANES), lambda b, j, c: (b, 0, u_col + j), pipeline_mode=single),
            pl.BlockSpec((None, L, HY_LANES), lambda b, j, c: (b, 0, gate_col + j), pipeline_mode=single),
            pl.BlockSpec((2, FFT_K1_CHUNK * FFT_R, HY_LANES), lambda b, j, c: (0, c, kf_col + j)),
            pl.BlockSpec((1, HY_LANES), lambda b, j, c: (0, j)),
            pl.BlockSpec(g_half.shape, lambda b, j, c: (0, 0, 0), pipeline_mode=single),
            pl.BlockSpec(h_half.shape, lambda b, j, c: (0, 0, 0), pipeline_mode=single),
            pl.BlockSpec(f_fwd.shape, lambda b, j, c: (0, 0)),
            pl.BlockSpec(f_inv.shape, lambda b, j, c: (0, 0)),
        ],
        out_specs=pl.BlockSpec((None, L, HY_LANES), lambda b, j, c: (b, 0, j)),
        out_shape=jax.ShapeDtypeStruct((B, L, D_HY), F32),
        scratch_shapes=[pltpu.VMEM((FFT_K1_ROWS * FFT_PITCH, HY_LANES), F32)] * 2,
        compiler_params=_cparams(("parallel", "parallel", "arbitrary")),
        name="hyena_conv",
    )(u, gate, kf, skip, g_half, h_half, f_fwd, f_inv)


def _slab_major(a, n1):
    B, L, C = a.shape
    return a.reshape(B, n1, L // n1, C).transpose(0, 2, 1, 3).reshape(B, L, C)


def _hyena(zh, p):
    B, L, _ = zh.shape
    assert 2 * L == FFT_N
    g_full, h_half, f_fwd, f_inv = _fft_tables()
    tables = (g_full[:, :, :FFT_R // 2], h_half, f_fwd, f_inv)
    filt, filt_sum = _filters(p, L)
    both_dirs = filt_sum[:, :]
    kf = _filter_spectrum(filt, both_dirs, g_full, f_fwd)
    zc = _slab_major(_shortconv(zh, p['hy_conv_w'], p['hy_conv_b'][None]), FFT_R // 2)
    lanes = D_HY // HY_LANES
    y = _hyconv(zc, 0, zc, lanes, kf, 0, p['hy_skip'][0:1], tables)
    y = _hyconv(y, 0, zc, 2 * lanes, kf, lanes, p['hy_skip'][1:2], tables)
    return _slab_major(y, L // (FFT_R // 2))


def kernel(x, c, ctx, c_ctx, norm1_g, norm2_g, w_ada, b_ada, w_in, b_in, hy_conv_w, hy_conv_b, hy_f1_w, hy_f1_b, hy_f2_w, hy_f2_b, hy_f3_w, hy_f3_b, hy_f4_w, hy_sin_freq, hy_skip, q_norm_g, k_norm_g, na_rpb, w_hy_out, w_na_out, w_out, peer_w_q, peer_keys, peer_u, peer_v):
    assert w_in.shape[0] == 1, "single-layer block"
    B, S, D = x.shape
    C = ctx.shape[1]
    T = B * S

    m_lat = (jax.nn.silu(c) @ w_ada[0] + b_ada[0]).reshape(B, N_MOD, 1, D)
    m_ctx = (jax.nn.silu(c_ctx) @ w_ada[0] + b_ada[0]).reshape(N_MOD, 1, D)
    sh1, sc1, g1, sh2, sc2, g2 = [m_lat[:, i] for i in range(N_MOD)]

    w_in_bf = w_in[0].astype(BF16)
    b_in2 = b_in[0][None, :]
    gain1 = norm1_g[0][None, :]
    qg = jnp.tile(q_norm_g[0], NA_HEADS)[None, :]
    kg = jnp.tile(k_norm_g[0], NA_HEADS)[None, :]
    seg = jnp.asarray(np.kron(np.eye(NA_HEADS), np.full((HEAD_DIM, HEAD_DIM), 1.0 / HEAD_DIM)), dtype=BF16)
    cos, sin = _rope_tables(S)

    x2 = x.reshape(T, D)
    zh, q, k, v, gate = _inproj(x2, sh1, sc1, gain1, w_in_bf, b_in2, cos, sin, qg, kg, seg, S)

    k_ctx, v_ctx = _ctxkv(ctx.reshape(B * C, D), m_ctx[0], m_ctx[1], gain1,
                          w_in_bf[:, COL_K:COL_V], b_in2[:, COL_K:COL_V],
                          w_in_bf[:, COL_V:COL_G_HY], b_in2[:, COL_V:COL_G_HY], kg, seg)

    p = {'hy_conv_w': hy_conv_w[0], 'hy_conv_b': hy_conv_b[0], 'hy_f1_w': hy_f1_w[0], 'hy_f1_b': hy_f1_b[0],
         'hy_f2_w': hy_f2_w[0], 'hy_f2_b': hy_f2_b[0], 'hy_f3_w': hy_f3_w[0], 'hy_f3_b': hy_f3_b[0],
         'hy_f4_w': hy_f4_w[0], 'hy_sin_freq': hy_sin_freq[0], 'hy_skip': hy_skip[0]}
    y_hy = _hyena(zh.reshape(B, S, 3 * D_HY), p).reshape(T, D_HY)

    y_na = _natten(q.reshape(B, S, D_NA), k.reshape(B, S, D_NA), v.reshape(B, S, D_NA),
                   k_ctx.reshape(B, C, D_NA), v_ctx.reshape(B, C, D_NA), _natten_bias(na_rpb[0], S // GRID_W)).reshape(T, D_NA)

    x1, h2, pq = _postmix(y_hy, y_na, gate, x2, g1, sh2, sc2, norm2_g[0][None, :],
                          w_hy_out[0].astype(BF16), w_na_out[0].astype(BF16), w_out[0].astype(BF16),
                          peer_w_q[0].astype(BF16), S)

    keys_bf = peer_keys[0].astype(BF16).reshape(2 * PEER_HEADS, PEER_N_KEYS, PEER_D_HALF)
    sel_a, sel_b, sel_g = _route(pq, keys_bf)
    w = _wbuild(sel_a, sel_b, sel_g)
    out = _peer_dense(h2, w, peer_u[0].astype(BF16), peer_v[0].astype(BF16), x1, g2, S)
    return out.reshape(B, S, D)
```

```python
import functools
import math

import numpy as np
import jax
import jax.numpy as jnp
from jax import lax
from jax.experimental import pallas as pl
from jax.experimental.pallas import tpu as pltpu

F32 = jnp.float32
BF16 = jnp.bfloat16

D_MODEL = 1024
GRID_W = 64
EPS = 1e-6
N_MOD = 6

D_HY = 512
HY_ORDER = 2
HY_BANDS = 16
HY_DECAY_TARGET = 1e-2
HY_FAST_PCT = 0.3
HY_SLOW_PCT = 1.5

NA_HEADS = 8
HEAD_DIM = 64
D_NA = NA_HEADS * HEAD_DIM
WIN_ROWS = 8
WIN_COLS = 16
ROPE_THETA = 10000.0
NEG_INF = -1e30

PEER_HEADS = 8
PEER_N_KEYS = 128
PEER_N_EXPERTS = PEER_N_KEYS * PEER_N_KEYS
PEER_TOPK = 16
PEER_D_KEY = 256
PEER_D_HALF = PEER_D_KEY // 2

COL_HY = 0
COL_Q = COL_HY + 3 * D_HY
COL_K = COL_Q + D_NA
COL_V = COL_K + D_NA
COL_G_HY = COL_V + D_NA
N_PROJ = COL_G_HY + 2 * D_MODEL

VMEM_LIMIT = 56 * 1024 * 1024
TOK_TILE = 256
PEER_TOK_TILE = 512
PEER_EXP_TILE = 2048


def _cparams(sem):
    return pltpu.CompilerParams(dimension_semantics=sem, vmem_limit_bytes=VMEM_LIMIT)


def _dot(a, b):
    return jnp.dot(a, b, preferred_element_type=F32)


def _dot_nt(a, b):
    return lax.dot_general(a, b, (((1,), (1,)), ((), ())), preferred_element_type=F32)


def _modulated_norm(x, gain, shift, scale):
    ms = jnp.mean(x * x, axis=-1, keepdims=True)
    return (x * lax.rsqrt(ms + EPS) * gain) * (1.0 + scale) + shift


def _head_rms(z, seg, gain):
    ms = _dot((z * z).astype(BF16), seg)
    return z * lax.rsqrt(ms + EPS) * gain


def _rope(z, cos, sin_signed):
    n = z.shape[-1]
    lane = lax.broadcasted_iota(jnp.int32, z.shape, 1)
    first = (lane // (HEAD_DIM // 4)) % 2 == 0
    partner = jnp.where(first, pltpu.roll(z, n - HEAD_DIM // 4, 1), pltpu.roll(z, HEAD_DIM // 4, 1))
    return z * cos + partner * sin_signed


def _inproj_kernel(x_ref, shift_ref, scale_ref, g_ref, w_ref, b_ref, cos_ref, sin_ref, qg_ref, kg_ref,
                   seg_ref, zh_ref, q_ref, k_ref, v_ref, gate_ref):
    h = _modulated_norm(x_ref[...], g_ref[...], shift_ref[...], scale_ref[...]).astype(BF16)

    def proj(lo, hi):
        return _dot(h, w_ref[:, lo:hi]) + b_ref[:, lo:hi]

    zh_ref[...] = proj(COL_HY, COL_Q)
    reps = D_NA // cos_ref.shape[1]
    cos = jnp.concatenate([cos_ref[...]] * reps, axis=1)
    sin = jnp.concatenate([sin_ref[...]] * reps, axis=1)
    seg = seg_ref[...]
    q = _rope(_head_rms(proj(COL_Q, COL_K), seg, qg_ref[...]), cos, sin)
    q_ref[...] = q.astype(BF16)
    k = _rope(_head_rms(proj(COL_K, COL_V), seg, kg_ref[...]), cos, sin)
    k_ref[...] = k.astype(BF16)
    v_ref[...] = proj(COL_V, COL_G_HY).astype(BF16)
    gate_ref[...] = jax.nn.sigmoid(proj(COL_G_HY, N_PROJ)).astype(BF16)


def _inproj(x2, shift, scale, gain, w_bf, b_in, cos, sin, qg, kg, seg, seq):
    T = x2.shape[0]
    tiles_per_batch = seq // TOK_TILE
    row = lambda i: (i, 0)
    per_batch = lambda i: (i // tiles_per_batch, 0, 0)
    const = lambda i: (0, 0)
    pos = lambda i: (i % tiles_per_batch, 0)
    return pl.pallas_call(
        _inproj_kernel,
        grid=(T // TOK_TILE,),
        in_specs=[
            pl.BlockSpec((TOK_TILE, D_MODEL), row),
            pl.BlockSpec((None, 1, D_MODEL), per_batch),
            pl.BlockSpec((None, 1, D_MODEL), per_batch),
            pl.BlockSpec((1, D_MODEL), const),
            pl.BlockSpec((D_MODEL, N_PROJ), const),
            pl.BlockSpec((1, N_PROJ), const),
            pl.BlockSpec((TOK_TILE, cos.shape[1]), pos),
            pl.BlockSpec((TOK_TILE, sin.shape[1]), pos),
            pl.BlockSpec((1, D_NA), const),
            pl.BlockSpec((1, D_NA), const),
            pl.BlockSpec((D_NA, D_NA), const),
        ],
        out_specs=[
            pl.BlockSpec((TOK_TILE, 3 * D_HY), row),
            pl.BlockSpec((TOK_TILE, D_NA), row),
            pl.BlockSpec((TOK_TILE, D_NA), row),
            pl.BlockSpec((TOK_TILE, D_NA), row),
            pl.BlockSpec((TOK_TILE, 2 * D_MODEL), row),
        ],
        out_shape=[
            jax.ShapeDtypeStruct((T, 3 * D_HY), F32),
            jax.ShapeDtypeStruct((T, D_NA), BF16),
            jax.ShapeDtypeStruct((T, D_NA), BF16),
            jax.ShapeDtypeStruct((T, D_NA), BF16),
            jax.ShapeDtypeStruct((T, 2 * D_MODEL), BF16),
        ],
        compiler_params=_cparams(("parallel",)),
        name="inproj",
    )(x2, shift, scale, gain, w_bf, b_in, cos, sin, qg, kg, seg)


def _ctxkv_kernel(x_ref, shift_ref, scale_ref, g_ref, wk_ref, bk_ref, wv_ref, bv_ref, kg_ref, seg_ref,
                  k_ref, v_ref):
    h = _modulated_norm(x_ref[...], g_ref[...], shift_ref[...], scale_ref[...]).astype(BF16)
    k = _head_rms(_dot(h, wk_ref[...]) + bk_ref[...], seg_ref[...], kg_ref[...])
    k_ref[...] = k.astype(BF16)
    v_ref[...] = (_dot(h, wv_ref[...]) + bv_ref[...]).astype(BF16)


def _ctxkv(ctx2, shift, scale, gain, wk, bk, wv, bv, kg, seg):
    T = ctx2.shape[0]
    row = lambda i: (i, 0)
    const = lambda i: (0, 0)
    return pl.pallas_call(
        _ctxkv_kernel,
        grid=(T // TOK_TILE,),
        in_specs=[
            pl.BlockSpec((TOK_TILE, D_MODEL), row),
            pl.BlockSpec((1, D_MODEL), const),
            pl.BlockSpec((1, D_MODEL), const),
            pl.BlockSpec((1, D_MODEL), const),
            pl.BlockSpec((D_MODEL, D_NA), const),
            pl.BlockSpec((1, D_NA), const),
            pl.BlockSpec((D_MODEL, D_NA), const),
            pl.BlockSpec((1, D_NA), const),
            pl.BlockSpec((1, D_NA), const),
            pl.BlockSpec((D_NA, D_NA), const),
        ],
        out_specs=[pl.BlockSpec((TOK_TILE, D_NA), row), pl.BlockSpec((TOK_TILE, D_NA), row)],
        out_shape=[jax.ShapeDtypeStruct((T, D_NA), BF16), jax.ShapeDtypeStruct((T, D_NA), BF16)],
        compiler_params=_cparams(("parallel",)),
        name="ctxkv",
    )(ctx2, shift, scale, gain, wk, bk, wv, bv, kg, seg)


NA_BLOCK_ROWS = 4
NA_KEY_ROWS = WIN_ROWS + NA_BLOCK_ROWS
NA_CLASSES = 3


def _natten_kernel(q_ref, k_ref, v_ref, kc_ref, vc_ref, bias_ref, o_ref, *, rows):
    r0 = pl.program_id(1) * NA_BLOCK_ROWS
    key_start = jnp.clip(r0 - WIN_ROWS // 2, 0, rows - NA_KEY_ROWS)
    start = pl.multiple_of(key_start * GRID_W, GRID_W * NA_BLOCK_ROWS)
    n_q = NA_BLOCK_ROWS * GRID_W
    n_loc = NA_KEY_ROWS * GRID_W
    scale = jnp.asarray(HEAD_DIM ** -0.5, BF16)
    pair = 2 * HEAD_DIM
    lane = lax.broadcasted_iota(jnp.int32, (n_q, pair), 1)
    for p in range(NA_HEADS // 2):
        cols = slice(p * pair, (p + 1) * pair)
        qp = q_ref[:, cols] * scale
        kp = k_ref[pl.ds(start, n_loc), cols]
        vp = v_ref[pl.ds(start, n_loc), cols]
        kcp = kc_ref[:, cols]
        vcp = vc_ref[:, cols]
        outs = []
        for hh in range(2):
            qm = jnp.where((lane // HEAD_DIM) == hh, qp, jnp.zeros_like(qp))
            s_loc = _dot_nt(qm, kp) + bias_ref[2 * p + hh]
            s_ctx = _dot_nt(qm, kcp)
            m = jnp.maximum(jnp.max(s_loc, axis=-1, keepdims=True), jnp.max(s_ctx, axis=-1, keepdims=True))
            e_loc = jnp.exp(s_loc - m)
            e_ctx = jnp.exp(s_ctx - m)
            denom = jnp.sum(e_loc, axis=-1, keepdims=True) + jnp.sum(e_ctx, axis=-1, keepdims=True)
            o = _dot(e_loc.astype(BF16), vp) + _dot(e_ctx.astype(BF16), vcp)
            outs.append(o / denom)
        o_ref[:, cols] = jnp.where((lane // HEAD_DIM) == 0, outs[0], outs[1]).astype(o_ref.dtype)


def _natten(q, k, v, kc, vc, bias):
    B, S, _ = q.shape
    rows = S // GRID_W
    n_blk = rows // NA_BLOCK_ROWS
    C = kc.shape[1]
    n_q = NA_BLOCK_ROWS * GRID_W
    single = pl.Buffered(1)
    block_class = lambda b, i: (jnp.where(i == 0, 0, jnp.where(i == n_blk - 1, 2, 1)), 0, 0, 0)
    return pl.pallas_call(
        functools.partial(_natten_kernel, rows=rows),
        grid=(B, n_blk),
        in_specs=[
            pl.BlockSpec((None, n_q, D_NA), lambda b, i: (b, i, 0)),
            pl.BlockSpec((None, S, D_NA), lambda b, i: (b, 0, 0), pipeline_mode=single),
            pl.BlockSpec((None, S, D_NA), lambda b, i: (b, 0, 0), pipeline_mode=single),
            pl.BlockSpec((None, C, D_NA), lambda b, i: (b, 0, 0)),
            pl.BlockSpec((None, C, D_NA), lambda b, i: (b, 0, 0)),
            pl.BlockSpec((None,) + bias.shape[1:], block_class),
        ],
        out_specs=pl.BlockSpec((None, n_q, D_NA), lambda b, i: (b, i, 0)),
        out_shape=jax.ShapeDtypeStruct((B, S, D_NA), BF16),
        compiler_params=_cparams(("parallel", "arbitrary")),
        name="natten",
    )(q, k, v, kc, vc, bias)


def _natten_bias(rpb, rows):
    col = np.arange(GRID_W)
    dc = np.clip(col[None, :] - col[:, None], -(WIN_COLS - 1), WIN_COLS - 1) + (WIN_COLS - 1)
    col_start = np.clip(col - WIN_COLS // 2, 0, GRID_W - WIN_COLS)
    col_mask = (col[None, :] >= col_start[:, None]) & (col[None, :] < col_start[:, None] + WIN_COLS)
    r0 = np.array([0, 2 * NA_BLOCK_ROWS, rows - NA_BLOCK_ROWS])
    r = r0[:, None] + np.arange(NA_BLOCK_ROWS)[None, :]
    key_row = np.clip(r0 - WIN_ROWS // 2, 0, rows - NA_KEY_ROWS)[:, None, None] + np.arange(NA_KEY_ROWS)
    win_start = np.clip(r - WIN_ROWS // 2, 0, rows - WIN_ROWS)[:, :, None]
    row_mask = (key_row >= win_start) & (key_row < win_start + WIN_ROWS)
    dr = np.clip(key_row - r[:, :, None] + (WIN_ROWS - 1), 0, 2 * WIN_ROWS - 2)
    pick_c = jnp.asarray(dc[None] == np.arange(2 * WIN_COLS - 1)[:, None, None], dtype=F32)
    pick_r = jnp.asarray(dr[..., None] == np.arange(2 * WIN_ROWS - 1), dtype=F32)
    b = jnp.einsum('crkd,hde,eqx->chrqkx', pick_r, rpb.astype(F32), pick_c, precision=HI)
    mask = row_mask[:, None, :, None, :, None] & col_mask[None, None, None, :, None, :]
    b = jnp.where(mask, b, NEG_INF)
    return b.reshape(NA_CLASSES, NA_HEADS, NA_BLOCK_ROWS * GRID_W, NA_KEY_ROWS * GRID_W)


def _rope_tables(seq):
    pos = np.arange(seq)
    rows = (pos // GRID_W).astype(np.float64)
    cols = (pos % GRID_W).astype(np.float64)
    nf = HEAD_DIM // 4
    inv = ROPE_THETA ** (-np.arange(nf, dtype=np.float64) / nf)
    ar = rows[:, None] * inv[None, :]
    ac = cols[:, None] * inv[None, :]
    cos = np.concatenate([np.cos(ar), np.cos(ar), np.cos(ac), np.cos(ac)], axis=-1)
    sin = np.concatenate([-np.sin(ar), np.sin(ar), -np.sin(ac), np.sin(ac)], axis=-1)
    return (jnp.asarray(np.tile(cos, (1, 2)), dtype=F32), jnp.asarray(np.tile(sin, (1, 2)), dtype=F32))


def _postmix_kernel(yh_ref, yn_ref, gate_ref, x_ref, g1_ref, sh2_ref, sc2_ref, n2_ref, whyo_ref, wnao_ref,
                    wout_ref, wq_ref, x1_ref, h2_ref, pq_ref):
    a = _dot(yh_ref[...].astype(BF16), whyo_ref[...])
    b = _dot(yn_ref[...], wnao_ref[...])
    gate = gate_ref[...].astype(F32)
    merged = gate[:, :D_MODEL] * a + gate[:, D_MODEL:] * b
    x1 = x_ref[...] + g1_ref[...] * _dot(merged.astype(BF16), wout_ref[...])
    x1_ref[...] = x1
    h2 = _modulated_norm(x1, n2_ref[...], sh2_ref[...], sc2_ref[...]).astype(BF16)
    h2_ref[...] = h2
    for hp in range(2 * PEER_HEADS):
        pq_ref[hp] = _dot(h2, wq_ref[:, hp * PEER_D_HALF:(hp + 1) * PEER_D_HALF]).astype(BF16)


def _postmix(yh, yn, gate, x2, g1, sh2, sc2, n2g, whyo, wnao, wout, wq, seq):
    T = x2.shape[0]
    tiles_per_batch = seq // TOK_TILE
    row = lambda i: (i, 0)
    per_batch = lambda i: (i // tiles_per_batch, 0, 0)
    const = lambda i: (0, 0)
    nq = wq.shape[1]
    return pl.pallas_call(
        _postmix_kernel,
        grid=(T // TOK_TILE,),
        in_specs=[
            pl.BlockSpec((TOK_TILE, D_HY), row),
            pl.BlockSpec((TOK_TILE, D_NA), row),
            pl.BlockSpec((TOK_TILE, 2 * D_MODEL), row),
            pl.BlockSpec((TOK_TILE, D_MODEL), row),
            pl.BlockSpec((None, 1, D_MODEL), per_batch),
            pl.BlockSpec((None, 1, D_MODEL), per_batch),
            pl.BlockSpec((None, 1, D_MODEL), per_batch),
            pl.BlockSpec((1, D_MODEL), const),
            pl.BlockSpec((D_HY, D_MODEL), const),
            pl.BlockSpec((D_NA, D_MODEL), const),
            pl.BlockSpec((D_MODEL, D_MODEL), const),
            pl.BlockSpec((D_MODEL, nq), const),
        ],
        out_specs=[
            pl.BlockSpec((TOK_TILE, D_MODEL), row),
            pl.BlockSpec((TOK_TILE, D_MODEL), row),
            pl.BlockSpec((nq // PEER_D_HALF, TOK_TILE, PEER_D_HALF), lambda i: (0, i, 0)),
        ],
        out_shape=[
            jax.ShapeDtypeStruct((T, D_MODEL), F32),
            jax.ShapeDtypeStruct((T, D_MODEL), BF16),
            jax.ShapeDtypeStruct((nq // PEER_D_HALF, T, PEER_D_HALF), BF16),
        ],
        compiler_params=_cparams(("parallel",)),
        name="postmix",
    )(yh, yn, gate, x2, g1, sh2, sc2, n2g, whyo, wnao, wout, wq)


SUBLANES = 8
LANES = 128


def _argmax_tree(vals, idxs):
    while len(vals) > 1:
        nv, ni = [], []
        for j in range(0, len(vals) - 1, 2):
            right = vals[j + 1] > vals[j]
            nv.append(jnp.where(right, vals[j + 1], vals[j]))
            ni.append(jnp.where(right, idxs[j + 1], idxs[j]))
        if len(vals) % 2:
            nv.append(vals[-1])
            ni.append(idxs[-1])
        vals, idxs = nv, ni
    return vals[0], idxs[0]


W_BLOCK = 128
W_PITCH = 136


SORT_TILE = SUBLANES * LANES


def _batcher_sort_network(n):
    ces = []

    def merge(lo, m, r):
        step = r * 2
        if step < m:
            merge(lo, m, step)
            merge(lo + r, m, step)
            ces.extend((i, i + r) for i in range(lo + r, lo + m - r, step))
        else:
            ces.append((lo, lo + r))

    def sort(lo, m):
        if m > 1:
            sort(lo, m // 2)
            sort(lo + m // 2, m // 2)
            merge(lo, m, 1)

    sort(0, n)
    return ces


def _bitonic_merge_network(n):
    ces, s = [], n // 2
    while s >= 1:
        ces.extend((i, i + s) for i in range(n) if (i & s) == 0)
        s //= 2
    return ces


SORT_NET = _batcher_sort_network(PEER_TOPK)
MERGE_NET = _bitonic_merge_network(PEER_TOPK)


def _compare_exchange(v, x, net):
    for i, j in net:
        c = v[j] > v[i]
        v[i], v[j] = jnp.where(c, v[j], v[i]), jnp.where(c, v[i], v[j])
        x[i], x[j] = jnp.where(c, x[j], x[i]), jnp.where(c, x[i], x[j])


def _merge_top(a, b, dropped):
    (va, xa), (vb, xb) = a, b
    k = len(va)
    v, x = [], []
    for r in range(k):
        c = vb[k - 1 - r] > va[r]
        v.append(jnp.where(c, vb[k - 1 - r], va[r]))
        x.append(jnp.where(c, xb[k - 1 - r], xa[r]))
        dropped = jnp.maximum(dropped, jnp.where(c, va[r], vb[k - 1 - r]))
    _compare_exchange(v, x, MERGE_NET)
    return (v, x), dropped


def _merge_all(lists, dropped):
    while len(lists) > 1:
        nxt = []
        for j in range(0, len(lists) - 1, 2):
            m, dropped = _merge_top(lists[j], lists[j + 1], dropped)
            nxt.append(m)
        if len(lists) % 2:
            nxt.append(lists[-1])
        lists = nxt
    return lists[0], dropped


def _settle_ties(v, x, dropped):
    x = list(x)
    for r in range(len(v) - 1):
        swap = (v[r] == v[r + 1]) & (x[r] > x[r + 1])
        x[r], x[r + 1] = jnp.where(swap, x[r + 1], x[r]), jnp.where(swap, x[r], x[r + 1])
    risk = v[-1] <= dropped
    for r in range(len(v) - 1):
        risk = risk | ((v[r] == v[r + 1]) & (x[r] > x[r + 1]))
    return x, risk.astype(jnp.int32)


def _level2_chains():
    pairs = [(a, b) for a in range(PEER_TOPK) for b in range(PEER_TOPK) if (a + 1) * (b + 1) <= PEER_TOPK]
    n_rows = 4
    chains = [[(a, b) for (a2, b) in pairs if a2 == a] for a in range(n_rows)]
    rest = [p for p in pairs if p[0] >= n_rows]
    for b in sorted({b for _, b in rest}):
        chains.append([(a, b2) for (a, b2) in rest if b2 == b])
    assert sorted(sum(chains, [])) == pairs
    return pairs, chains


def _route_kernel(pq_ref, keys_ref, a_ref, b_ref, g_ref, sx_ref, tv_ref, ti_ref, bs_ref, es_ref,
                  oa_ref, ob_ref, og_ref, risk_ref):
    n = PEER_N_KEYS
    k = PEER_TOPK
    shape = (SUBLANES, LANES)
    neg = jnp.full(shape, -jnp.inf, F32)
    pairs, chains = _level2_chains()

    def scores(hp):
        s = _dot_nt(keys_ref[hp], pq_ref[hp])
        for rg in range(n // SUBLANES):
            for lt in range(SUBLANES):
                sx_ref[pl.ds(rg * SUBLANES * SUBLANES + lt, SUBLANES, stride=SUBLANES), :] = (
                    s[rg * SUBLANES:(rg + 1) * SUBLANES, lt * LANES:(lt + 1) * LANES])
        return [sx_ref[pl.ds(key * SUBLANES, SUBLANES), :] for key in range(n)]

    def store_top(hp, v, x):
        for r in range(k):
            tv_ref[hp, r] = v[r]
            ti_ref[hp, r] = x[r]

    def level1(hp, carry):
        vals = scores(hp)
        lists = []
        for q in range(n // k):
            v = vals[q * k:(q + 1) * k]
            x = [jnp.full(shape, q * k + r, jnp.int32) for r in range(k)]
            _compare_exchange(v, x, SORT_NET)
            lists.append((v, x))
        (v, x), dropped = _merge_all(lists, neg)
        x, risk = _settle_ties(v, x, dropped)
        store_top(hp, v, x)
        risk_ref[hp] = risk
        return carry

    def level1_repair(hp, carry):
        @pl.when(jnp.max(risk_ref[hp]) > 0)
        def _():
            level1_exact(hp)

        return carry

    def level1_exact(hp):
        vals = scores(hp)
        idx = [jnp.full(shape, key, jnp.int32) for key in range(n)]
        v, x = [], []
        for r in range(k):
            best, where_ = _argmax_tree(vals, idx)
            v.append(best)
            x.append(where_)
            vals = [jnp.where(where_ == key, -jnp.inf, vals[key]) for key in range(n)]
        store_top(hp, v, x)

    def candidates(h):
        s1 = [tv_ref[2 * h, r] for r in range(k)]
        s2 = [tv_ref[2 * h + 1, r] for r in range(k)]
        e1 = [ti_ref[2 * h, r] * n for r in range(k)]
        e2 = [ti_ref[2 * h + 1, r] for r in range(k)]
        return (lambda a, b: s1[a] + s2[b]), (lambda a, b: e1[a] + e2[b] + (a * k + b) * PEER_N_EXPERTS)

    def store_best(h, v, x):
        for r in range(k):
            bs_ref[h, r] = v[r]
            es_ref[h, r] = x[r]

    def level2(h, carry):
        val, expert = candidates(h)
        lists = []
        for chain in chains:
            pad = k - len(chain)
            lists.append(([val(a, b) for a, b in chain] + [neg] * pad,
                          [expert(a, b) for a, b in chain] + [jnp.zeros(shape, jnp.int32)] * pad))
        (v, x), dropped = _merge_all(lists, neg)
        x, risk = _settle_ties(v, x, dropped)
        store_best(h, v, x)
        risk_ref[h] = risk
        return carry

    def level2_repair(h, carry):
        @pl.when(jnp.max(risk_ref[h]) > 0)
        def _():
            level2_exact(h)

        return carry

    def level2_exact(h):
        val, expert = candidates(h)
        cand = [val(a, b) for a, b in pairs]
        exps = [expert(a, b) for a, b in pairs]
        pos = [jnp.full(shape, a * k + b, jnp.int32) for a, b in pairs]
        v, x = [], []
        for r in range(k):
            best, where_ = _argmax_tree(cand, pos)
            chosen = jnp.zeros(shape, jnp.int32)
            for m, (a, b) in enumerate(pairs):
                hit = where_ == (a * k + b)
                cand[m] = jnp.where(hit, -jnp.inf, cand[m])
                chosen = jnp.where(hit, exps[m], chosen)
            v.append(best)
            x.append(chosen)
        store_best(h, v, x)

    lax.fori_loop(0, 2 * PEER_HEADS, level1, 0)
    lax.fori_loop(0, 2 * PEER_HEADS, level1_repair, 0)
    lax.fori_loop(0, PEER_HEADS, level2, 0)
    lax.fori_loop(0, PEER_HEADS, level2_repair, 0)

    for h in range(PEER_HEADS):
        best = [bs_ref[h, r] for r in range(k)]
        ex = [jnp.exp(v - best[0]) for v in best]
        denom = functools.reduce(lambda p, q: p + q, ex)
        for r in range(k):
            rows = pl.ds((h * k + r) * SUBLANES, SUBLANES)
            e = jnp.bitwise_and(es_ref[h, r], PEER_N_EXPERTS - 1)
            oa_ref[rows, :] = jnp.right_shift(e, n.bit_length() - 1)
            ob_ref[rows, :] = jnp.bitwise_and(e, n - 1)
            og_ref[rows, :] = ex[r] / denom
    for s in range(SUBLANES):
        rows = pl.ds(s * LANES, LANES)
        plane = pl.ds(s, PEER_HEADS * k, stride=SUBLANES)
        a_ref[rows, :] = oa_ref[plane, :].T
        b_ref[rows, :] = ob_ref[plane, :].T
        g_ref[rows, :] = og_ref[plane, :].T


def _route(pq, keys_bf):
    n_hp, T, _ = pq.shape
    n_sel = PEER_HEADS * PEER_TOPK
    row = lambda i: (i, 0)
    reg = (SUBLANES, LANES)
    return pl.pallas_call(
        _route_kernel,
        grid=(T // SORT_TILE,),
        in_specs=[
            pl.BlockSpec((n_hp, SORT_TILE, PEER_D_HALF), lambda i: (0, i, 0)),
            pl.BlockSpec((n_hp, PEER_N_KEYS, PEER_D_HALF), lambda i: (0, 0, 0)),
        ],
        out_specs=[pl.BlockSpec((SORT_TILE, n_sel), row)] * 3,
        out_shape=[
            jax.ShapeDtypeStruct((T, n_sel), jnp.int32),
            jax.ShapeDtypeStruct((T, n_sel), jnp.int32),
            jax.ShapeDtypeStruct((T, n_sel), F32),
        ],
        scratch_shapes=[
            pltpu.VMEM((PEER_N_KEYS * SUBLANES, LANES), F32),
            pltpu.VMEM((n_hp, PEER_TOPK) + reg, F32),
            pltpu.VMEM((n_hp, PEER_TOPK) + reg, jnp.int32),
            pltpu.VMEM((PEER_HEADS, PEER_TOPK) + reg, F32),
            pltpu.VMEM((PEER_HEADS, PEER_TOPK) + reg, jnp.int32),
            pltpu.VMEM((n_sel * SUBLANES, LANES), jnp.int32),
            pltpu.VMEM((n_sel * SUBLANES, LANES), jnp.int32),
            pltpu.VMEM((n_sel * SUBLANES, LANES), F32),
            pltpu.VMEM((n_hp,) + reg, jnp.int32),
        ],
        compiler_params=_cparams(("parallel",)),
        name="route",
    )(pq, keys_bf)


WBUILD_UNROLL = 32


def _wbuild_kernel(a_ref, b_ref, g_ref, w_ref, s_ref):
    n = PEER_N_KEYS
    tb = a_ref.shape[0]
    key = lax.broadcasted_iota(jnp.int32, (n, a_ref.shape[1]), 0)

    def body(t, carry):
        a = a_ref[pl.ds(t, 1), :]
        b = b_ref[pl.ds(t, 1), :]
        g = g_ref[pl.ds(t, 1), :]
        ga = jnp.where(a == key, g, 0.0).astype(BF16)
        ob = jnp.where(b == key, 1.0, 0.0).astype(BF16)
        s_ref[pl.ds(pl.multiple_of(t * W_PITCH, SUBLANES), n), :] = _dot_nt(ga, ob)
        return carry

    lax.fori_loop(0, tb, body, 0, unroll=WBUILD_UNROLL)

    def emit(i1, carry):
        w_ref[i1] = s_ref[pl.ds(i1, tb, stride=W_PITCH), :].astype(BF16)
        return carry

    lax.fori_loop(0, n, emit, 0, unroll=WBUILD_UNROLL)


def _wbuild(a, b, g):
    T, n_sel = a.shape
    row = lambda i: (i, 0)
    return pl.pallas_call(
        _wbuild_kernel,
        grid=(T // W_BLOCK,),
        in_specs=[pl.BlockSpec((W_BLOCK, n_sel), row)] * 3,
        out_specs=pl.BlockSpec((None, PEER_N_KEYS, W_BLOCK, PEER_N_KEYS), lambda i: (i, 0, 0, 0)),
        out_shape=jax.ShapeDtypeStruct((T // W_BLOCK, PEER_N_KEYS, W_BLOCK, PEER_N_KEYS), BF16),
        scratch_shapes=[pltpu.VMEM((W_BLOCK * W_PITCH, PEER_N_KEYS), F32)],
        compiler_params=_cparams(("parallel",)),
        name="wbuild",
    )(a, b, g)


def _peer_dense_kernel(h2_ref, w_ref, u_ref, v_ref, x1_ref, g2_ref, o_ref, acc_ref):
    e = pl.program_id(1)

    @pl.when(e == 0)
    def _():
        acc_ref[...] = jnp.zeros_like(acc_ref)

    a = _dot_nt(h2_ref[...], u_ref[...])
    act = 0.5 * a * (1.0 + lax.erf(a * (2.0 ** -0.5)))
    w = jnp.concatenate(
        [jnp.concatenate([w_ref[blk, i] for i in range(w_ref.shape[1])], axis=1) for blk in range(w_ref.shape[0])],
        axis=0)
    acc_ref[...] += _dot((w.astype(F32) * act).astype(BF16), v_ref[...])

    @pl.when(e == pl.num_programs(1) - 1)
    def _():
        o_ref[...] = x1_ref[...] + g2_ref[...] * acc_ref[...]


def _peer_dense(h2, w, u_bf, v_bf, x1, g2, seq):
    T = h2.shape[0]
    tiles_per_batch = seq // PEER_TOK_TILE
    n_exp = u_bf.shape[0]
    return pl.pallas_call(
        _peer_dense_kernel,
        grid=(T // PEER_TOK_TILE, n_exp // PEER_EXP_TILE),
        in_specs=[
            pl.BlockSpec((PEER_TOK_TILE, D_MODEL), lambda i, e: (i, 0)),
            pl.BlockSpec((PEER_TOK_TILE // W_BLOCK, PEER_EXP_TILE // PEER_N_KEYS, W_BLOCK, PEER_N_KEYS),
                         lambda i, e: (i, e, 0, 0)),
            pl.BlockSpec((PEER_EXP_TILE, D_MODEL), lambda i, e: (e, 0)),
            pl.BlockSpec((PEER_EXP_TILE, D_MODEL), lambda i, e: (e, 0)),
            pl.BlockSpec((PEER_TOK_TILE, D_MODEL), lambda i, e: (i, 0)),
            pl.BlockSpec((None, 1, D_MODEL), lambda i, e: (i // tiles_per_batch, 0, 0)),
        ],
        out_specs=pl.BlockSpec((PEER_TOK_TILE, D_MODEL), lambda i, e: (i, 0)),
        out_shape=jax.ShapeDtypeStruct((T, D_MODEL), F32),
        scratch_shapes=[pltpu.VMEM((PEER_TOK_TILE, D_MODEL), F32)],
        compiler_params=_cparams(("parallel", "arbitrary")),
        name="peer_dense",
    )(h2, w, u_bf, v_bf, x1, g2)


HI = lax.Precision.HIGHEST


HY_LANES = 128


def _shortconv_kernel(z_ref, w_ref, b_ref, o_ref):
    z = z_ref[...]
    n = z.shape[0]
    row = lax.broadcasted_iota(jnp.int32, z.shape, 0)
    prev = jnp.where(row == 0, 0.0, pltpu.roll(z, 1, 0))
    nxt = jnp.where(row == n - 1, 0.0, pltpu.roll(z, n - 1, 0))
    o_ref[...] = w_ref[0:1] * prev + w_ref[1:2] * z + w_ref[2:3] * nxt + b_ref[...]


def _shortconv(zh, w, b):
    B, L, C = zh.shape
    return pl.pallas_call(
        _shortconv_kernel,
        grid=(B, C // HY_LANES),
        in_specs=[
            pl.BlockSpec((None, L, HY_LANES), lambda i, j: (i, 0, j)),
            pl.BlockSpec((3, HY_LANES), lambda i, j: (0, j)),
            pl.BlockSpec((1, HY_LANES), lambda i, j: (0, j)),
        ],
        out_specs=pl.BlockSpec((None, L, HY_LANES), lambda i, j: (i, 0, j)),
        out_shape=jax.ShapeDtypeStruct((B, L, C), F32),
        compiler_params=_cparams(("parallel", "parallel")),
        name="shortconv",
    )(zh, w, b)


FFT_R = 128
FFT_N = FFT_R * FFT_R
FFT_PITCH = 136
FFT_UNROLL = 8
FFT_K1_USED = FFT_R // 2 + 1
FFT_K1_SLABS = 66
FFT_K1_ROWS = 72
FFT_K1_CHUNK = 22
assert FFT_K1_SLABS % FFT_K1_CHUNK == 0 and FFT_K1_CHUNK % 2 == 0 and FFT_K1_USED <= FFT_K1_SLABS <= FFT_K1_ROWS


def _fft_tables():
    r = np.arange(FFT_R)
    kk = np.arange(FFT_K1_ROWS)
    n2, k1, n1 = r[:, None, None], kk[None, :, None], r[None, None, :]
    ang = (2.0 * math.pi / FFT_N) * ((k1 * (FFT_R * n1 + n2)) % FFT_N)
    used = k1 < FFT_K1_USED
    c, s = np.cos(ang) * used, np.sin(ang) * used
    g_full = np.concatenate([c, -s], axis=1)
    twice = np.where((k1 == 0) | (k1 == FFT_R // 2), 1.0, 2.0) / FFT_N
    h_half = np.concatenate([np.swapaxes(c * twice, 1, 2), -np.swapaxes(s * twice, 1, 2)],
                            axis=2)[:, :FFT_R // 2]
    ang2 = (2.0 * math.pi / FFT_R) * ((r[:, None] * r[None, :]) % FFT_R)
    c2, s2 = np.cos(ang2), np.sin(ang2)
    f_fwd = np.block([[c2, s2], [-s2, c2]])
    f_inv = np.block([[c2, -s2], [s2, c2]])
    return tuple(jnp.asarray(t, dtype=F32).astype(BF16) for t in (g_full, h_half, f_fwd, f_inv))


def _fft_stage1(load_slab, g_ref, bre_ref, bim_ref):
    def body(n2, carry):
        out = _dot(g_ref[n2], load_slab(n2).astype(BF16))
        bre_ref[pl.ds(n2, FFT_K1_ROWS, stride=FFT_PITCH), :] = out[:FFT_K1_ROWS]
        bim_ref[pl.ds(n2, FFT_K1_ROWS, stride=FFT_PITCH), :] = out[FFT_K1_ROWS:]
        return carry

    lax.fori_loop(0, FFT_R, body, 0, unroll=FFT_UNROLL)


def _k1_rows(k1):
    return pl.ds(pl.multiple_of(k1 * FFT_PITCH, SUBLANES), FFT_R)


def _load_k1_pair(bre_ref, bim_ref, k1):
    return jnp.concatenate(
        [jnp.concatenate([bre_ref[_k1_rows(k1 + d), :], bim_ref[_k1_rows(k1 + d), :]], axis=0) for d in range(2)],
        axis=1)


FILT_ROWS = 512
FILT_FEAT = 128


def _filt_kernel(feat_ref, delta_ref, w1_ref, b1_ref, w2_ref, b2_ref, w3_ref, b3_ref, w4_ref, freq_ref,
                 h_ref, sum_ref):
    @pl.when(pl.program_id(0) == 0)
    def _():
        sum_ref[...] = jnp.zeros_like(sum_ref)

    feat = feat_ref[...]
    freq = freq_ref[...]
    dot = lambda a, b: jnp.dot(a, b, precision=HI, preferred_element_type=F32)
    h = jnp.sin(freq * (dot(feat, w1_ref[...]) + b1_ref[...]))
    h = jnp.sin(freq * (dot(h, w2_ref[...]) + b2_ref[...]))
    h = jnp.sin(freq * (dot(h, w3_ref[...]) + b3_ref[...]))
    hi = h.astype(BF16)
    lo = (h - hi.astype(F32)).astype(BF16)
    h = _dot(hi, w4_ref[...]) + _dot(lo, w4_ref[...])
    n_feat = 1 + 2 * HY_BANDS
    t = feat[:, 0:1]
    forward = feat[:, n_feat:n_feat + 1] > 0.5
    keep = feat[:, n_feat + 1:n_feat + 2]
    decay = jnp.exp(-t * delta_ref[...])
    outs = []
    for o in range(HY_ORDER):
        base = o * 2 * D_HY
        outs.append(jnp.where(forward, h[:, base:base + D_HY], h[:, base + D_HY:base + 2 * D_HY]) * decay)
    out = jnp.concatenate(outs, axis=1)
    sum_ref[...] += jnp.sum(jnp.abs(out), axis=0, keepdims=True)
    h_ref[...] = out * keep


def _filt_features(L):
    r = np.arange(FFT_N)
    n = FFT_R * (r % FFT_R) + r // FFT_R
    j = np.where(n < L, n, np.where(n > L, 2 * L - n, 0))
    jj = j.astype(np.float64)[:, None]
    t = jj / (L - 1.0)
    w = (2.0 * math.pi / L) * jj
    bands = np.linspace(1e-4, HY_BANDS - 1.0, HY_BANDS)[None, :]
    flags = np.stack([n < L, n != L], axis=1).astype(np.float64)
    feats = np.concatenate([t, np.cos(bands * w), -np.sin(bands * w), flags], axis=-1)
    return jnp.asarray(np.pad(feats, ((0, 0), (0, FILT_FEAT - feats.shape[1]))), dtype=F32)


def _filters(p, L):
    feats = _filt_features(L)
    deltas = jnp.abs(jnp.linspace(math.log(HY_DECAY_TARGET) / HY_SLOW_PCT,
                                  math.log(HY_DECAY_TARGET) / HY_FAST_PCT, D_HY, dtype=F32))[None, :]
    w1 = jnp.pad(p['hy_f1_w'], ((0, FILT_FEAT - p['hy_f1_w'].shape[0]), (0, 0)))
    n_out = HY_ORDER * D_HY
    const = lambda i: (0, 0)
    full = lambda a: pl.BlockSpec(a.shape, const)
    args = [deltas, w1, p['hy_f1_b'][None], p['hy_f2_w'], p['hy_f2_b'][None], p['hy_f3_w'], p['hy_f3_b'][None],
            p['hy_f4_w'].astype(BF16), p['hy_sin_freq'][None]]
    return pl.pallas_call(
        _filt_kernel,
        grid=(FFT_N // FILT_ROWS,),
        in_specs=[pl.BlockSpec((FILT_ROWS, FILT_FEAT), lambda i: (i, 0))] + [full(a) for a in args],
        out_specs=[pl.BlockSpec((FILT_ROWS, n_out), lambda i: (i, 0)), pl.BlockSpec((1, n_out), const)],
        out_shape=[jax.ShapeDtypeStruct((FFT_N, n_out), F32), jax.ShapeDtypeStruct((1, n_out), F32)],
        compiler_params=_cparams(("arbitrary",)),
        name="hyena_filter",
    )(feats, *args)


def _fspec_kernel(h_ref, sum_ref, g_ref, f_ref, o_ref, bre_ref, bim_ref):
    c = pl.program_id(1)

    @pl.when(c == 0)
    def _():
        _fft_stage1(lambda n2: h_ref[pl.ds(pl.multiple_of(n2 * FFT_R, FFT_R), FFT_R), :], g_ref, bre_ref, bim_ref)

    inv = 1.0 / sum_ref[...]

    lanes = h_ref.shape[1]

    def slab_pair(j, carry):
        x = _dot(f_ref[...], _load_k1_pair(bre_ref, bim_ref, c * FFT_K1_CHUNK + 2 * j).astype(BF16))
        for d in range(2):
            rows = pl.ds(pl.multiple_of((2 * j + d) * FFT_R, FFT_R), FFT_R)
            o_ref[0, rows, :] = x[:FFT_R, d * lanes:(d + 1) * lanes] * inv
            o_ref[1, rows, :] = x[FFT_R:, d * lanes:(d + 1) * lanes] * inv
        return carry

    lax.fori_loop(0, FFT_K1_CHUNK // 2, slab_pair, 0, unroll=True)


def _filter_spectrum(h, hsum, g_full, f_fwd):
    n_ch = h.shape[1]
    single = pl.Buffered(1)
    return pl.pallas_call(
        _fspec_kernel,
        grid=(n_ch // HY_LANES, FFT_K1_SLABS // FFT_K1_CHUNK),
        in_specs=[
            pl.BlockSpec((FFT_N, HY_LANES), lambda j, c: (0, j), pipeline_mode=single),
            pl.BlockSpec((1, HY_LANES), lambda j, c: (0, j)),
            pl.BlockSpec(g_full.shape, lambda j, c: (0, 0, 0), pipeline_mode=single),
            pl.BlockSpec(f_fwd.shape, lambda j, c: (0, 0)),
        ],
        out_specs=pl.BlockSpec((2, FFT_K1_CHUNK * FFT_R, HY_LANES), lambda j, c: (0, c, j)),
        out_shape=jax.ShapeDtypeStruct((2, FFT_K1_SLABS * FFT_R, n_ch), F32),
        scratch_shapes=[pltpu.VMEM((FFT_K1_ROWS * FFT_PITCH, HY_LANES), F32)] * 2,
        compiler_params=_cparams(("parallel", "arbitrary")),
        name="hyena_filter_fft",
    )(h, hsum, g_full, f_fwd)


def _hyconv_kernel(u_ref, gate_ref, kf_ref, skip_ref, g_ref, h_ref, f_ref, fi_ref, o_ref, bre_ref, bim_ref):
    c = pl.program_id(2)
    half = FFT_R // 2

    def rows_of(n2):
        return pl.ds(pl.multiple_of(n2 * half, half), half)

    @pl.when(c == 0)
    def _():
        _fft_stage1(lambda n2: u_ref[rows_of(n2), :], g_ref, bre_ref, bim_ref)

    lanes = u_ref.shape[1]

    def slab_pair(j, carry):
        k1 = c * FFT_K1_CHUNK + 2 * j
        x = _dot(f_ref[...], _load_k1_pair(bre_ref, bim_ref, k1).astype(BF16))
        xr, xi = x[:FFT_R], x[FFT_R:]
        kr, ki = [jnp.concatenate([kf_ref[part, pl.ds(pl.multiple_of((2 * j + d) * FFT_R, FFT_R), FFT_R), :]
                                   for d in range(2)], axis=1) for part in range(2)]
        y = jnp.concatenate([xr * kr - xi * ki, xr * ki + xi * kr], axis=0)
        z = _dot(fi_ref[...], y.astype(BF16))
        for d in range(2):
            bre_ref[_k1_rows(k1 + d), :] = z[:FFT_R, d * lanes:(d + 1) * lanes]
            bim_ref[_k1_rows(k1 + d), :] = z[FFT_R:, d * lanes:(d + 1) * lanes]
        return carry

    lax.fori_loop(0, FFT_K1_CHUNK // 2, slab_pair, 0, unroll=True)

    @pl.when(c == pl.num_programs(2) - 1)
    def _():
        def body(n2, carry):
            z = jnp.concatenate([bre_ref[pl.ds(n2, FFT_K1_ROWS, stride=FFT_PITCH), :],
                                 bim_ref[pl.ds(n2, FFT_K1_ROWS, stride=FFT_PITCH), :]], axis=0)
            conv = _dot(h_ref[n2], z.astype(BF16))
            rows = rows_of(n2)
            o_ref[rows, :] = gate_ref[rows, :] * (conv + skip_ref[...] * u_ref[rows, :])
            return carry

        lax.fori_loop(0, FFT_R, body, 0, unroll=FFT_UNROLL)


def _hyconv(u, u_col, gate, gate_col, kf, kf_col, skip, tables):
    g_half, h_half, f_fwd, f_inv = tables
    B, L, _ = u.shape
    single = pl.Buffered(1)
    return pl.pallas_call(
        _hyconv_kernel,
        grid=(B, D_HY // HY_LANES, FFT_K1_SLABS // FFT_K1_CHUNK),
        in_specs=[
            pl.BlockSpec((None, L, HY_LANES), lambda b, j, c: (b, 0, u_col + j), pipeline_mode=single),
            pl.BlockSpec((None, L, HY_LANES), lambda b, j, c: (b, 0, gate_col + j), pipeline_mode=single),
            pl.BlockSpec((2, FFT_K1_CHUNK * FFT_R, HY_LANES), lambda b, j, c: (0, c, kf_col + j)),
            pl.BlockSpec((1, HY_LANES), lambda b, j, c: (0, j)),
            pl.BlockSpec(g_half.shape, lambda b, j, c: (0, 0, 0), pipeline_mode=single),
            pl.BlockSpec(h_half.shape, lambda b, j, c: (0, 0, 0), pipeline_mode=single),
            pl.BlockSpec(f_fwd.shape, lambda b, j, c: (0, 0)),
            pl.BlockSpec(f_inv.shape, lambda b, j, c: (0, 0)),
        ],
        out_specs=pl.BlockSpec((None, L, HY_LANES), lambda b, j, c: (b, 0, j)),
        out_shape=jax.ShapeDtypeStruct((B, L, D_HY), F32),
        scratch_shapes=[pltpu.VMEM((FFT_K1_ROWS * FFT_PITCH, HY_LANES), F32)] * 2,
        compiler_params=_cparams(("parallel", "parallel", "arbitrary")),
        name="hyena_conv",
    )(u, gate, kf, skip, g_half, h_half, f_fwd, f_inv)


def _slab_major(a, n1):
    B, L, C = a.shape
    return a.reshape(B, n1, L // n1, C).transpose(0, 2, 1, 3).reshape(B, L, C)


def _hyena(zh, p):
    B, L, _ = zh.shape
    assert 2 * L == FFT_N
    g_full, h_half, f_fwd, f_inv = _fft_tables()
    tables = (g_full[:, :, :FFT_R // 2], h_half, f_fwd, f_inv)
    filt, filt_sum = _filters(p, L)
    both_dirs = filt_sum[:, :]
    kf = _filter_spectrum(filt, both_dirs, g_full, f_fwd)
    zc = _slab_major(_shortconv(zh, p['hy_conv_w'], p['hy_conv_b'][None]), FFT_R // 2)
    lanes = D_HY // HY_LANES
    y = _hyconv(zc, 0, zc, lanes, kf, 0, p['hy_skip'][0:1], tables)
    y = _hyconv(y, 0, zc, 2 * lanes, kf, lanes, p['hy_skip'][1:2], tables)
    return _slab_major(y, L // (FFT_R // 2))


def kernel(x, c, ctx, c_ctx, norm1_g, norm2_g, w_ada, b_ada, w_in, b_in, hy_conv_w, hy_conv_b, hy_f1_w, hy_f1_b, hy_f2_w, hy_f2_b, hy_f3_w, hy_f3_b, hy_f4_w, hy_sin_freq, hy_skip, q_norm_g, k_norm_g, na_rpb, w_hy_out, w_na_out, w_out, peer_w_q, peer_keys, peer_u, peer_v):
    assert w_in.shape[0] == 1, "single-layer block"
    B, S, D = x.shape
    C = ctx.shape[1]
    T = B * S

    m_lat = (jax.nn.silu(c) @ w_ada[0] + b_ada[0]).reshape(B, N_MOD, 1, D)
    m_ctx = (jax.nn.silu(c_ctx) @ w_ada[0] + b_ada[0]).reshape(N_MOD, 1, D)
    sh1, sc1, g1, sh2, sc2, g2 = [m_lat[:, i] for i in range(N_MOD)]

    w_in_bf = w_in[0].astype(BF16)
    b_in2 = b_in[0][None, :]
    gain1 = norm1_g[0][None, :]
    qg = jnp.tile(q_norm_g[0], NA_HEADS)[None, :]
    kg = jnp.tile(k_norm_g[0], NA_HEADS)[None, :]
    seg = jnp.asarray(np.kron(np.eye(NA_HEADS), np.full((HEAD_DIM, HEAD_DIM), 1.0 / HEAD_DIM)), dtype=BF16)
    cos, sin = _rope_tables(S)

    x2 = x.reshape(T, D)
    zh, q, k, v, gate = _inproj(x2, sh1, sc1, gain1, w_in_bf, b_in2, cos, sin, qg, kg, seg, S)

    k_ctx, v_ctx = _ctxkv(ctx.reshape(B * C, D), m_ctx[0], m_ctx[1], gain1,
                          w_in_bf[:, COL_K:COL_V], b_in2[:, COL_K:COL_V],
                          w_in_bf[:, COL_V:COL_G_HY], b_in2[:, COL_V:COL_G_HY], kg, seg)

    p = {'hy_conv_w': hy_conv_w[0], 'hy_conv_b': hy_conv_b[0], 'hy_f1_w': hy_f1_w[0], 'hy_f1_b': hy_f1_b[0],
         'hy_f2_w': hy_f2_w[0], 'hy_f2_b': hy_f2_b[0], 'hy_f3_w': hy_f3_w[0], 'hy_f3_b': hy_f3_b[0],
         'hy_f4_w': hy_f4_w[0], 'hy_sin_freq': hy_sin_freq[0], 'hy_skip': hy_skip[0]}
    y_hy = _hyena(zh.reshape(B, S, 3 * D_HY), p).reshape(T, D_HY)

    y_na = _natten(q.reshape(B, S, D_NA), k.reshape(B, S, D_NA), v.reshape(B, S, D_NA),
                   k_ctx.reshape(B, C, D_NA), v_ctx.reshape(B, C, D_NA), _natten_bias(na_rpb[0], S // GRID_W)).reshape(T, D_NA)

    x1, h2, pq = _postmix(y_hy, y_na, gate, x2, g1, sh2, sc2, norm2_g[0][None, :],
                          w_hy_out[0].astype(BF16), w_na_out[0].astype(BF16), w_out[0].astype(BF16),
                          peer_w_q[0].astype(BF16), S)

    keys_bf = peer_keys[0].astype(BF16).reshape(2 * PEER_HEADS, PEER_N_KEYS, PEER_D_HALF)
    sel_a, sel_b, sel_g = _route(pq, keys_bf)
    w = _wbuild(sel_a, sel_b, sel_g)
    out = _peer_dense(h2, w, peer_u[0].astype(BF16), peer_v[0].astype(BF16), x1, g2, S)
    return out.reshape(B, S, D)
```

```python
import functools
import math

import numpy as np
import jax
import jax.numpy as jnp
from jax import lax
from jax.experimental import pallas as pl
from jax.experimental.pallas import tpu as pltpu

F32 = jnp.float32
BF16 = jnp.bfloat16

D_MODEL = 1024
GRID_W = 64
EPS = 1e-6
N_MOD = 6

D_HY = 512
HY_ORDER = 2
HY_BANDS = 16
HY_DECAY_TARGET = 1e-2
HY_FAST_PCT = 0.3
HY_SLOW_PCT = 1.5

NA_HEADS = 8
HEAD_DIM = 64
D_NA = NA_HEADS * HEAD_DIM
WIN_ROWS = 8
WIN_COLS = 16
ROPE_THETA = 10000.0
NEG_INF = -1e30

PEER_HEADS = 8
PEER_N_KEYS = 128
PEER_N_EXPERTS = PEER_N_KEYS * PEER_N_KEYS
PEER_TOPK = 16
PEER_D_KEY = 256
PEER_D_HALF = PEER_D_KEY // 2

COL_HY = 0
COL_Q = COL_HY + 3 * D_HY
COL_K = COL_Q + D_NA
COL_V = COL_K + D_NA
COL_G_HY = COL_V + D_NA
N_PROJ = COL_G_HY + 2 * D_MODEL

VMEM_LIMIT = 56 * 1024 * 1024
TOK_TILE = 256
PEER_TOK_TILE = 512
PEER_EXP_TILE = 2048


def _cparams(sem):
    return pltpu.CompilerParams(dimension_semantics=sem, vmem_limit_bytes=VMEM_LIMIT)


def _dot(a, b):
    return jnp.dot(a, b, preferred_element_type=F32)


def _dot_nt(a, b):
    return lax.dot_general(a, b, (((1,), (1,)), ((), ())), preferred_element_type=F32)


def _modulated_norm(x, gain, shift, scale):
    ms = jnp.mean(x * x, axis=-1, keepdims=True)
    return (x * lax.rsqrt(ms + EPS) * gain) * (1.0 + scale) + shift


def _head_rms(z, seg, gain):
    ms = _dot((z * z).astype(BF16), seg)
    return z * lax.rsqrt(ms + EPS) * gain


def _rope(z, cos, sin_signed):
    n = z.shape[-1]
    lane = lax.broadcasted_iota(jnp.int32, z.shape, 1)
    first = (lane // (HEAD_DIM // 4)) % 2 == 0
    partner = jnp.where(first, pltpu.roll(z, n - HEAD_DIM // 4, 1), pltpu.roll(z, HEAD_DIM // 4, 1))
    return z * cos + partner * sin_signed


def _inproj_kernel(x_ref, shift_ref, scale_ref, g_ref, w_ref, b_ref, cos_ref, sin_ref, qg_ref, kg_ref,
                   seg_ref, zh_ref, q_ref, k_ref, v_ref, gate_ref):
    h = _modulated_norm(x_ref[...], g_ref[...], shift_ref[...], scale_ref[...]).astype(BF16)

    def proj(lo, hi):
        return _dot(h, w_ref[:, lo:hi]) + b_ref[:, lo:hi]

    zh_ref[...] = proj(COL_HY, COL_Q)
    reps = D_NA // cos_ref.shape[1]
    cos = jnp.concatenate([cos_ref[...]] * reps, axis=1)
    sin = jnp.concatenate([sin_ref[...]] * reps, axis=1)
    seg = seg_ref[...]
    q = _rope(_head_rms(proj(COL_Q, COL_K), seg, qg_ref[...]), cos, sin)
    q_ref[...] = q.astype(BF16)
    k = _rope(_head_rms(proj(COL_K, COL_V), seg, kg_ref[...]), cos, sin)
    k_ref[...] = k.astype(BF16)
    v_ref[...] = proj(COL_V, COL_G_HY).astype(BF16)
    gate_ref[...] = jax.nn.sigmoid(proj(COL_G_HY, N_PROJ)).astype(BF16)


def _inproj(x2, shift, scale, gain, w_bf, b_in, cos, sin, qg, kg, seg, seq):
    T = x2.shape[0]
    tiles_per_batch = seq // TOK_TILE
    row = lambda i: (i, 0)
    per_batch = lambda i: (i // tiles_per_batch, 0, 0)
    const = lambda i: (0, 0)
    pos = lambda i: (i % tiles_per_batch, 0)
    return pl.pallas_call(
        _inproj_kernel,
        grid=(T // TOK_TILE,),
        in_specs=[
            pl.BlockSpec((TOK_TILE, D_MODEL), row),
            pl.BlockSpec((None, 1, D_MODEL), per_batch),
            pl.BlockSpec((None, 1, D_MODEL), per_batch),
            pl.BlockSpec((1, D_MODEL), const),
            pl.BlockSpec((D_MODEL, N_PROJ), const),
            pl.BlockSpec((1, N_PROJ), const),
            pl.BlockSpec((TOK_TILE, cos.shape[1]), pos),
            pl.BlockSpec((TOK_TILE, sin.shape[1]), pos),
            pl.BlockSpec((1, D_NA), const),
            pl.BlockSpec((1, D_NA), const),
            pl.BlockSpec((D_NA, D_NA), const),
        ],
        out_specs=[
            pl.BlockSpec((TOK_TILE, 3 * D_HY), row),
            pl.BlockSpec((TOK_TILE, D_NA), row),
            pl.BlockSpec((TOK_TILE, D_NA), row),
            pl.BlockSpec((TOK_TILE, D_NA), row),
            pl.BlockSpec((TOK_TILE, 2 * D_MODEL), row),
        ],
        out_shape=[
            jax.ShapeDtypeStruct((T, 3 * D_HY), F32),
            jax.ShapeDtypeStruct((T, D_NA), BF16),
            jax.ShapeDtypeStruct((T, D_NA), BF16),
            jax.ShapeDtypeStruct((T, D_NA), BF16),
            jax.ShapeDtypeStruct((T, 2 * D_MODEL), BF16),
        ],
        compiler_params=_cparams(("parallel",)),
        name="inproj",
    )(x2, shift, scale, gain, w_bf, b_in, cos, sin, qg, kg, seg)


def _ctxkv_kernel(x_ref, shift_ref, scale_ref, g_ref, wk_ref, bk_ref, wv_ref, bv_ref, kg_ref, seg_ref,
                  k_ref, v_ref):
    h = _modulated_norm(x_ref[...], g_ref[...], shift_ref[...], scale_ref[...]).astype(BF16)
    k = _head_rms(_dot(h, wk_ref[...]) + bk_ref[...], seg_ref[...], kg_ref[...])
    k_ref[...] = k.astype(BF16)
    v_ref[...] = (_dot(h, wv_ref[...]) + bv_ref[...]).astype(BF16)


def _ctxkv(ctx2, shift, scale, gain, wk, bk, wv, bv, kg, seg):
    T = ctx2.shape[0]
    row = lambda i: (i, 0)
    const = lambda i: (0, 0)
    return pl.pallas_call(
        _ctxkv_kernel,
        grid=(T // TOK_TILE,),
        in_specs=[
            pl.BlockSpec((TOK_TILE, D_MODEL), row),
            pl.BlockSpec((1, D_MODEL), const),
            pl.BlockSpec((1, D_MODEL), const),
            pl.BlockSpec((1, D_MODEL), const),
            pl.BlockSpec((D_MODEL, D_NA), const),
            pl.BlockSpec((1, D_NA), const),
            pl.BlockSpec((D_MODEL, D_NA), const),
            pl.BlockSpec((1, D_NA), const),
            pl.BlockSpec((1, D_NA), const),
            pl.BlockSpec((D_NA, D_NA), const),
        ],
        out_specs=[pl.BlockSpec((TOK_TILE, D_NA), row), pl.BlockSpec((TOK_TILE, D_NA), row)],
        out_shape=[jax.ShapeDtypeStruct((T, D_NA), BF16), jax.ShapeDtypeStruct((T, D_NA), BF16)],
        compiler_params=_cparams(("parallel",)),
        name="ctxkv",
    )(ctx2, shift, scale, gain, wk, bk, wv, bv, kg, seg)


NA_BLOCK_ROWS = 4
NA_KEY_ROWS = WIN_ROWS + NA_BLOCK_ROWS
NA_CLASSES = 3


def _natten_kernel(q_ref, k_ref, v_ref, kc_ref, vc_ref, bias_ref, o_ref, *, rows):
    r0 = pl.program_id(1) * NA_BLOCK_ROWS
    key_start = jnp.clip(r0 - WIN_ROWS // 2, 0, rows - NA_KEY_ROWS)
    start = pl.multiple_of(key_start * GRID_W, GRID_W * NA_BLOCK_ROWS)
    n_q = NA_BLOCK_ROWS * GRID_W
    n_loc = NA_KEY_ROWS * GRID_W
    scale = jnp.asarray(HEAD_DIM ** -0.5, BF16)
    pair = 2 * HEAD_DIM
    lane = lax.broadcasted_iota(jnp.int32, (n_q, pair), 1)
    for p in range(NA_HEADS // 2):
        cols = slice(p * pair, (p + 1) * pair)
        qp = q_ref[:, cols] * scale
        kp = k_ref[pl.ds(start, n_loc), cols]
        vp = v_ref[pl.ds(start, n_loc), cols]
        kcp = kc_ref[:, cols]
        vcp = vc_ref[:, cols]
        outs = []
        for hh in range(2):
            qm = jnp.where((lane // HEAD_DIM) == hh, qp, jnp.zeros_like(qp))
            s_loc = _dot_nt(qm, kp) + bias_ref[2 * p + hh]
            s_ctx = _dot_nt(qm, kcp)
            m = jnp.maximum(jnp.max(s_loc, axis=-1, keepdims=True), jnp.max(s_ctx, axis=-1, keepdims=True))
            e_loc = jnp.exp(s_loc - m)
            e_ctx = jnp.exp(s_ctx - m)
            denom = jnp.sum(e_loc, axis=-1, keepdims=True) + jnp.sum(e_ctx, axis=-1, keepdims=True)
            o = _dot(e_loc.astype(BF16), vp) + _dot(e_ctx.astype(BF16), vcp)
            outs.append(o / denom)
        o_ref[:, cols] = jnp.where((lane // HEAD_DIM) == 0, outs[0], outs[1]).astype(o_ref.dtype)


def _natten(q, k, v, kc, vc, bias):
    B, S, _ = q.shape
    rows = S // GRID_W
    n_blk = rows // NA_BLOCK_ROWS
    C = kc.shape[1]
    n_q = NA_BLOCK_ROWS * GRID_W
    single = pl.Buffered(1)
    block_class = lambda b, i: (jnp.where(i == 0, 0, jnp.where(i == n_blk - 1, 2, 1)), 0, 0, 0)
    return pl.pallas_call(
        functools.partial(_natten_kernel, rows=rows),
        grid=(B, n_blk),
        in_specs=[
            pl.BlockSpec((None, n_q, D_NA), lambda b, i: (b, i, 0)),
            pl.BlockSpec((None, S, D_NA), lambda b, i: (b, 0, 0), pipeline_mode=single),
            pl.BlockSpec((None, S, D_NA), lambda b, i: (b, 0, 0), pipeline_mode=single),
            pl.BlockSpec((None, C, D_NA), lambda b, i: (b, 0, 0)),
            pl.BlockSpec((None, C, D_NA), lambda b, i: (b, 0, 0)),
            pl.BlockSpec((None,) + bias.shape[1:], block_class),
        ],
        out_specs=pl.BlockSpec((None, n_q, D_NA), lambda b, i: (b, i, 0)),
        out_shape=jax.ShapeDtypeStruct((B, S, D_NA), BF16),
        compiler_params=_cparams(("parallel", "arbitrary")),
        name="natten",
    )(q, k, v, kc, vc, bias)


def _natten_bias(rpb, rows):
    col = np.arange(GRID_W)
    dc = np.clip(col[None, :] - col[:, None], -(WIN_COLS - 1), WIN_COLS - 1) + (WIN_COLS - 1)
    col_start = np.clip(col - WIN_COLS // 2, 0, GRID_W - WIN_COLS)
    col_mask = (col[None, :] >= col_start[:, None]) & (col[None, :] < col_start[:, None] + WIN_COLS)
    r0 = np.array([0, 2 * NA_BLOCK_ROWS, rows - NA_BLOCK_ROWS])
    r = r0[:, None] + np.arange(NA_BLOCK_ROWS)[None, :]
    key_row = np.clip(r0 - WIN_ROWS // 2, 0, rows - NA_KEY_ROWS)[:, None, None] + np.arange(NA_KEY_ROWS)
    win_start = np.clip(r - WIN_ROWS // 2, 0, rows - WIN_ROWS)[:, :, None]
    row_mask = (key_row >= win_start) & (key_row < win_start + WIN_ROWS)
    dr = key_row - r[:, :, None] + (WIN_ROWS - 1)
    pick_c = jnp.asarray(dc[None] == np.arange(2 * WIN_COLS - 1)[:, None, None], dtype=F32)
    toep = jnp.einsum('hde,eqx->hdqx', rpb.astype(F32), pick_c, precision=HI)
    toep = jnp.where(col_mask[None, None], toep, NEG_INF)
    toep = jnp.pad(toep, ((0, 0), (1, 1), (0, 0), (0, 0)), constant_values=NEG_INF)
    pairs = jnp.concatenate([toep[:, :-1], toep[:, 1:]], axis=-1)

    def kernel(t_ref, o_ref):
        neg = jnp.full((GRID_W, 2 * GRID_W), NEG_INF, F32)
        lane_left = lax.broadcasted_iota(jnp.int32, (GRID_W, 2 * GRID_W), 1) < GRID_W
        for c in range(NA_CLASSES):
            for rq in range(NA_BLOCK_ROWS):
                for kp in range(NA_KEY_ROWS // 2):
                    left, right = bool(row_mask[c, rq, 2 * kp]), bool(row_mask[c, rq, 2 * kp + 1])
                    block = neg
                    if left or right:
                        block = t_ref[int(dr[c, rq, 2 * kp]) + 1]
                        if not (left and right):
                            block = jnp.where(lane_left == left, block, NEG_INF)
                    o_ref[c, rq * GRID_W:(rq + 1) * GRID_W, kp * 2 * GRID_W:(kp + 1) * 2 * GRID_W] = block

    n_q, n_k = NA_BLOCK_ROWS * GRID_W, NA_KEY_ROWS * GRID_W
    return pl.pallas_call(
        kernel,
        grid=(NA_HEADS,),
        in_specs=[pl.BlockSpec((None,) + pairs.shape[1:], lambda h: (h, 0, 0, 0))],
        out_specs=pl.BlockSpec((NA_CLASSES, None, n_q, n_k), lambda h: (0, h, 0, 0)),
        out_shape=jax.ShapeDtypeStruct((NA_CLASSES, NA_HEADS, n_q, n_k), F32),
        compiler_params=_cparams(("parallel",)),
        name="natten_bias",
    )(pairs)


def _rope_tables(seq):
    pos = np.arange(seq)
    rows = (pos // GRID_W).astype(np.float64)
    cols = (pos % GRID_W).astype(np.float64)
    nf = HEAD_DIM // 4
    inv = ROPE_THETA ** (-np.arange(nf, dtype=np.float64) / nf)
    ar = rows[:, None] * inv[None, :]
    ac = cols[:, None] * inv[None, :]
    cos = np.concatenate([np.cos(ar), np.cos(ar), np.cos(ac), np.cos(ac)], axis=-1)
    sin = np.concatenate([-np.sin(ar), np.sin(ar), -np.sin(ac), np.sin(ac)], axis=-1)
    return (jnp.asarray(np.tile(cos, (1, 2)), dtype=F32), jnp.asarray(np.tile(sin, (1, 2)), dtype=F32))


def _postmix_kernel(yh_ref, yn_ref, gate_ref, x_ref, g1_ref, sh2_ref, sc2_ref, n2_ref, whyo_ref, wnao_ref,
                    wout_ref, wq_ref, x1_ref, h2_ref, pq_ref):
    a = _dot(yh_ref[...].astype(BF16), whyo_ref[...])
    b = _dot(yn_ref[...], wnao_ref[...])
    gate = gate_ref[...].astype(F32)
    merged = gate[:, :D_MODEL] * a + gate[:, D_MODEL:] * b
    x1 = x_ref[...] + g1_ref[...] * _dot(merged.astype(BF16), wout_ref[...])
    x1_ref[...] = x1
    h2 = _modulated_norm(x1, n2_ref[...], sh2_ref[...], sc2_ref[...]).astype(BF16)
    h2_ref[...] = h2
    for hp in range(2 * PEER_HEADS):
        pq_ref[hp] = _dot(h2, wq_ref[:, hp * PEER_D_HALF:(hp + 1) * PEER_D_HALF]).astype(BF16)


def _postmix(yh, yn, gate, x2, g1, sh2, sc2, n2g, whyo, wnao, wout, wq, seq):
    T = x2.shape[0]
    tiles_per_batch = seq // TOK_TILE
    row = lambda i: (i, 0)
    per_batch = lambda i: (i // tiles_per_batch, 0, 0)
    const = lambda i: (0, 0)
    nq = wq.shape[1]
    return pl.pallas_call(
        _postmix_kernel,
        grid=(T // TOK_TILE,),
        in_specs=[
            pl.BlockSpec((TOK_TILE, D_HY), row),
            pl.BlockSpec((TOK_TILE, D_NA), row),
            pl.BlockSpec((TOK_TILE, 2 * D_MODEL), row),
            pl.BlockSpec((TOK_TILE, D_MODEL), row),
            pl.BlockSpec((None, 1, D_MODEL), per_batch),
            pl.BlockSpec((None, 1, D_MODEL), per_batch),
            pl.BlockSpec((None, 1, D_MODEL), per_batch),
            pl.BlockSpec((1, D_MODEL), const),
            pl.BlockSpec((D_HY, D_MODEL), const),
            pl.BlockSpec((D_NA, D_MODEL), const),
            pl.BlockSpec((D_MODEL, D_MODEL), const),
            pl.BlockSpec((D_MODEL, nq), const),
        ],
        out_specs=[
            pl.BlockSpec((TOK_TILE, D_MODEL), row),
            pl.BlockSpec((TOK_TILE, D_MODEL), row),
            pl.BlockSpec((nq // PEER_D_HALF, TOK_TILE, PEER_D_HALF), lambda i: (0, i, 0)),
        ],
        out_shape=[
            jax.ShapeDtypeStruct((T, D_MODEL), F32),
            jax.ShapeDtypeStruct((T, D_MODEL), BF16),
            jax.ShapeDtypeStruct((nq // PEER_D_HALF, T, PEER_D_HALF), BF16),
        ],
        compiler_params=_cparams(("parallel",)),
        name="postmix",
    )(yh, yn, gate, x2, g1, sh2, sc2, n2g, whyo, wnao, wout, wq)


SUBLANES = 8
LANES = 128


def _argmax_tree(vals, idxs):
    while len(vals) > 1:
        nv, ni = [], []
        for j in range(0, len(vals) - 1, 2):
            right = vals[j + 1] > vals[j]
            nv.append(jnp.where(right, vals[j + 1], vals[j]))
            ni.append(jnp.where(right, idxs[j + 1], idxs[j]))
        if len(vals) % 2:
            nv.append(vals[-1])
            ni.append(idxs[-1])
        vals, idxs = nv, ni
    return vals[0], idxs[0]


W_BLOCK = 128
W_PITCH = 136


SORT_TILE = SUBLANES * LANES


def _batcher_sort_network(n):
    ces = []

    def merge(lo, m, r):
        step = r * 2
        if step < m:
            merge(lo, m, step)
            merge(lo + r, m, step)
            ces.extend((i, i + r) for i in range(lo + r, lo + m - r, step))
        else:
            ces.append((lo, lo + r))

    def sort(lo, m):
        if m > 1:
            sort(lo, m // 2)
            sort(lo + m // 2, m // 2)
            merge(lo, m, 1)

    sort(0, n)
    return ces


def _bitonic_merge_network(n):
    ces, s = [], n // 2
    while s >= 1:
        ces.extend((i, i + s) for i in range(n) if (i & s) == 0)
        s //= 2
    return ces


SORT_NET = _batcher_sort_network(PEER_TOPK)
MERGE_NET = _bitonic_merge_network(PEER_TOPK)


def _compare_exchange(v, x, net):
    for i, j in net:
        c = v[j] > v[i]
        v[i], v[j] = jnp.where(c, v[j], v[i]), jnp.where(c, v[i], v[j])
        x[i], x[j] = jnp.where(c, x[j], x[i]), jnp.where(c, x[i], x[j])


def _merge_top(a, b, dropped):
    (va, xa), (vb, xb) = a, b
    k = len(va)
    v, x = [], []
    for r in range(k):
        c = vb[k - 1 - r] > va[r]
        v.append(jnp.where(c, vb[k - 1 - r], va[r]))
        x.append(jnp.where(c, xb[k - 1 - r], xa[r]))
        dropped = jnp.maximum(dropped, jnp.where(c, va[r], vb[k - 1 - r]))
    _compare_exchange(v, x, MERGE_NET)
    return (v, x), dropped


def _merge_all(lists, dropped):
    while len(lists) > 1:
        nxt = []
        for j in range(0, len(lists) - 1, 2):
            m, dropped = _merge_top(lists[j], lists[j + 1], dropped)
            nxt.append(m)
        if len(lists) % 2:
            nxt.append(lists[-1])
        lists = nxt
    return lists[0], dropped


def _settle_ties(v, x, dropped):
    x = list(x)
    for r in range(len(v) - 1):
        swap = (v[r] == v[r + 1]) & (x[r] > x[r + 1])
        x[r], x[r + 1] = jnp.where(swap, x[r + 1], x[r]), jnp.where(swap, x[r], x[r + 1])
    risk = v[-1] <= dropped
    for r in range(len(v) - 1):
        risk = risk | ((v[r] == v[r + 1]) & (x[r] > x[r + 1]))
    return x, risk.astype(jnp.int32)


def _level2_chains():
    pairs = [(a, b) for a in range(PEER_TOPK) for b in range(PEER_TOPK) if (a + 1) * (b + 1) <= PEER_TOPK]
    n_rows = 4
    chains = [[(a, b) for (a2, b) in pairs if a2 == a] for a in range(n_rows)]
    rest = [p for p in pairs if p[0] >= n_rows]
    for b in sorted({b for _, b in rest}):
        chains.append([(a, b2) for (a, b2) in rest if b2 == b])
    assert sorted(sum(chains, [])) == pairs
    return pairs, chains


def _route_kernel(pq_ref, keys_ref, a_ref, b_ref, g_ref, sx_ref, tv_ref, ti_ref, bs_ref, es_ref,
                  oa_ref, ob_ref, og_ref, risk_ref):
    n = PEER_N_KEYS
    k = PEER_TOPK
    shape = (SUBLANES, LANES)
    neg = jnp.full(shape, -jnp.inf, F32)
    pairs, chains = _level2_chains()

    def scores(hp):
        s = _dot_nt(keys_ref[hp], pq_ref[hp])
        for rg in range(n // SUBLANES):
            for lt in range(SUBLANES):
                sx_ref[pl.ds(rg * SUBLANES * SUBLANES + lt, SUBLANES, stride=SUBLANES), :] = (
                    s[rg * SUBLANES:(rg + 1) * SUBLANES, lt * LANES:(lt + 1) * LANES])
        return [sx_ref[pl.ds(key * SUBLANES, SUBLANES), :] for key in range(n)]

    def store_top(hp, v, x):
        for r in range(k):
            tv_ref[hp, r] = v[r]
            ti_ref[hp, r] = x[r]

    def level1(hp, carry):
        vals = scores(hp)
        lists = []
        for q in range(n // k):
            v = vals[q * k:(q + 1) * k]
            x = [jnp.full(shape, q * k + r, jnp.int32) for r in range(k)]
            _compare_exchange(v, x, SORT_NET)
            lists.append((v, x))
        (v, x), dropped = _merge_all(lists, neg)
        x, risk = _settle_ties(v, x, dropped)
        store_top(hp, v, x)
        risk_ref[hp] = risk
        return carry

    def level1_repair(hp, carry):
        @pl.when(jnp.max(risk_ref[hp]) > 0)
        def _():
            level1_exact(hp)

        return carry

    def level1_exact(hp):
        vals = scores(hp)
        idx = [jnp.full(shape, key, jnp.int32) for key in range(n)]
        v, x = [], []
        for r in range(k):
            best, where_ = _argmax_tree(vals, idx)
            v.append(best)
            x.append(where_)
            vals = [jnp.where(where_ == key, -jnp.inf, vals[key]) for key in range(n)]
        store_top(hp, v, x)

    def candidates(h):
        s1 = [tv_ref[2 * h, r] for r in range(k)]
        s2 = [tv_ref[2 * h + 1, r] for r in range(k)]
        e1 = [ti_ref[2 * h, r] * n for r in range(k)]
        e2 = [ti_ref[2 * h + 1, r] for r in range(k)]
        return (lambda a, b: s1[a] + s2[b]), (lambda a, b: e1[a] + e2[b] + (a * k + b) * PEER_N_EXPERTS)

    def store_best(h, v, x):
        for r in range(k):
            bs_ref[h, r] = v[r]
            es_ref[h, r] = x[r]

    def level2(h, carry):
        val, expert = candidates(h)
        lists = []
        for chain in chains:
            pad = k - len(chain)
            lists.append(([val(a, b) for a, b in chain] + [neg] * pad,
                          [expert(a, b) for a, b in chain] + [jnp.zeros(shape, jnp.int32)] * pad))
        (v, x), dropped = _merge_all(lists, neg)
        x, risk = _settle_ties(v, x, dropped)
        store_best(h, v, x)
        risk_ref[h] = risk
        return carry

    def level2_repair(h, carry):
        @pl.when(jnp.max(risk_ref[h]) > 0)
        def _():
            level2_exact(h)

        return carry

    def level2_exact(h):
        val, expert = candidates(h)
        cand = [val(a, b) for a, b in pairs]
        exps = [expert(a, b) for a, b in pairs]
        pos = [jnp.full(shape, a * k + b, jnp.int32) for a, b in pairs]
        v, x = [], []
        for r in range(k):
            best, where_ = _argmax_tree(cand, pos)
            chosen = jnp.zeros(shape, jnp.int32)
            for m, (a, b) in enumerate(pairs):
                hit = where_ == (a * k + b)
                cand[m] = jnp.where(hit, -jnp.inf, cand[m])
                chosen = jnp.where(hit, exps[m], chosen)
            v.append(best)
            x.append(chosen)
        store_best(h, v, x)

    lax.fori_loop(0, 2 * PEER_HEADS, level1, 0)
    lax.fori_loop(0, 2 * PEER_HEADS, level1_repair, 0)
    lax.fori_loop(0, PEER_HEADS, level2, 0)
    lax.fori_loop(0, PEER_HEADS, level2_repair, 0)

    for h in range(PEER_HEADS):
        best = [bs_ref[h, r] for r in range(k)]
        ex = [jnp.exp(v - best[0]) for v in best]
        denom = functools.reduce(lambda p, q: p + q, ex)
        for r in range(k):
            rows = pl.ds((h * k + r) * SUBLANES, SUBLANES)
            e = jnp.bitwise_and(es_ref[h, r], PEER_N_EXPERTS - 1)
            oa_ref[rows, :] = jnp.right_shift(e, n.bit_length() - 1)
            ob_ref[rows, :] = jnp.bitwise_and(e, n - 1)
            og_ref[rows, :] = ex[r] / denom
    for s in range(SUBLANES):
        rows = pl.ds(s * LANES, LANES)
        plane = pl.ds(s, PEER_HEADS * k, stride=SUBLANES)
        a_ref[rows, :] = oa_ref[plane, :].T
        b_ref[rows, :] = ob_ref[plane, :].T
        g_ref[rows, :] = og_ref[plane, :].T


def _route(pq, keys_bf):
    n_hp, T, _ = pq.shape
    n_sel = PEER_HEADS * PEER_TOPK
    row = lambda i: (i, 0)
    reg = (SUBLANES, LANES)
    return pl.pallas_call(
        _route_kernel,
        grid=(T // SORT_TILE,),
        in_specs=[
            pl.BlockSpec((n_hp, SORT_TILE, PEER_D_HALF), lambda i: (0, i, 0)),
            pl.BlockSpec((n_hp, PEER_N_KEYS, PEER_D_HALF), lambda i: (0, 0, 0)),
        ],
        out_specs=[pl.BlockSpec((SORT_TILE, n_sel), row)] * 3,
        out_shape=[
            jax.ShapeDtypeStruct((T, n_sel), jnp.int32),
            jax.ShapeDtypeStruct((T, n_sel), jnp.int32),
            jax.ShapeDtypeStruct((T, n_sel), F32),
        ],
        scratch_shapes=[
            pltpu.VMEM((PEER_N_KEYS * SUBLANES, LANES), F32),
            pltpu.VMEM((n_hp, PEER_TOPK) + reg, F32),
            pltpu.VMEM((n_hp, PEER_TOPK) + reg, jnp.int32),
            pltpu.VMEM((PEER_HEADS, PEER_TOPK) + reg, F32),
            pltpu.VMEM((PEER_HEADS, PEER_TOPK) + reg, jnp.int32),
            pltpu.VMEM((n_sel * SUBLANES, LANES), jnp.int32),
            pltpu.VMEM((n_sel * SUBLANES, LANES), jnp.int32),
            pltpu.VMEM((n_sel * SUBLANES, LANES), F32),
            pltpu.VMEM((n_hp,) + reg, jnp.int32),
        ],
        compiler_params=_cparams(("parallel",)),
        name="route",
    )(pq, keys_bf)


WBUILD_UNROLL = 32


def _wbuild_kernel(a_ref, b_ref, g_ref, w_ref, s_ref):
    n = PEER_N_KEYS
    tb = a_ref.shape[0]
    key = lax.broadcasted_iota(jnp.int32, (n, a_ref.shape[1]), 0)

    def body(t, carry):
        a = a_ref[pl.ds(t, 1), :]
        b = b_ref[pl.ds(t, 1), :]
        g = g_ref[pl.ds(t, 1), :]
        ga = jnp.where(a == key, g, 0.0).astype(BF16)
        ob = jnp.where(b == key, 1.0, 0.0).T.astype(BF16)
        s_ref[pl.ds(pl.multiple_of(t * W_PITCH, SUBLANES), n), :] = _dot(ga, ob)
        return carry

    lax.fori_loop(0, tb, body, 0, unroll=WBUILD_UNROLL)

    def emit(i1, carry):
        w_ref[i1] = s_ref[pl.ds(i1, tb, stride=W_PITCH), :].astype(BF16)
        return carry

    lax.fori_loop(0, n, emit, 0, unroll=WBUILD_UNROLL)


def _wbuild(a, b, g):
    T, n_sel = a.shape
    row = lambda i: (i, 0)
    return pl.pallas_call(
        _wbuild_kernel,
        grid=(T // W_BLOCK,),
        in_specs=[pl.BlockSpec((W_BLOCK, n_sel), row)] * 3,
        out_specs=pl.BlockSpec((None, PEER_N_KEYS, W_BLOCK, PEER_N_KEYS), lambda i: (i, 0, 0, 0)),
        out_shape=jax.ShapeDtypeStruct((T // W_BLOCK, PEER_N_KEYS, W_BLOCK, PEER_N_KEYS), BF16),
        scratch_shapes=[pltpu.VMEM((W_BLOCK * W_PITCH, PEER_N_KEYS), F32)],
        compiler_params=_cparams(("parallel",)),
        name="wbuild",
    )(a, b, g)


def _peer_dense_kernel(h2_ref, w_ref, u_ref, v_ref, x1_ref, g2_ref, o_ref, acc_ref):
    e = pl.program_id(1)

    @pl.when(e == 0)
    def _():
        acc_ref[...] = jnp.zeros_like(acc_ref)

    a = _dot_nt(h2_ref[...], u_ref[...])
    act = 0.5 * a * (1.0 + lax.erf(a * (2.0 ** -0.5)))
    w = jnp.concatenate(
        [jnp.concatenate([w_ref[blk, i] for i in range(w_ref.shape[1])], axis=1) for blk in range(w_ref.shape[0])],
        axis=0)
    acc_ref[...] += _dot((w.astype(F32) * act).astype(BF16), v_ref[...])

    @pl.when(e == pl.num_programs(1) - 1)
    def _():
        o_ref[...] = x1_ref[...] + g2_ref[...] * acc_ref[...]


def _peer_dense(h2, w, u_bf, v_bf, x1, g2, seq):
    T = h2.shape[0]
    tiles_per_batch = seq // PEER_TOK_TILE
    n_exp = u_bf.shape[0]
    return pl.pallas_call(
        _peer_dense_kernel,
        grid=(T // PEER_TOK_TILE, n_exp // PEER_EXP_TILE),
        in_specs=[
            pl.BlockSpec((PEER_TOK_TILE, D_MODEL), lambda i, e: (i, 0)),
            pl.BlockSpec((PEER_TOK_TILE // W_BLOCK, PEER_EXP_TILE // PEER_N_KEYS, W_BLOCK, PEER_N_KEYS),
                         lambda i, e: (i, e, 0, 0)),
            pl.BlockSpec((PEER_EXP_TILE, D_MODEL), lambda i, e: (e, 0)),
            pl.BlockSpec((PEER_EXP_TILE, D_MODEL), lambda i, e: (e, 0)),
            pl.BlockSpec((PEER_TOK_TILE, D_MODEL), lambda i, e: (i, 0)),
            pl.BlockSpec((None, 1, D_MODEL), lambda i, e: (i // tiles_per_batch, 0, 0)),
        ],
        out_specs=pl.BlockSpec((PEER_TOK_TILE, D_MODEL), lambda i, e: (i, 0)),
        out_shape=jax.ShapeDtypeStruct((T, D_MODEL), F32),
        scratch_shapes=[pltpu.VMEM((PEER_TOK_TILE, D_MODEL), F32)],
        compiler_params=_cparams(("parallel", "arbitrary")),
        name="peer_dense",
    )(h2, w, u_bf, v_bf, x1, g2)


HI = lax.Precision.HIGHEST


HY_LANES = 128


def _shortconv_kernel(z_ref, w_ref, b_ref, o_ref):
    z = z_ref[...]
    n = z.shape[0]
    row = lax.broadcasted_iota(jnp.int32, z.shape, 0)
    prev = jnp.where(row == 0, 0.0, pltpu.roll(z, 1, 0))
    nxt = jnp.where(row == n - 1, 0.0, pltpu.roll(z, n - 1, 0))
    o_ref[...] = w_ref[0:1] * prev + w_ref[1:2] * z + w_ref[2:3] * nxt + b_ref[...]


def _shortconv(zh, w, b):
    B, L, C = zh.shape
    return pl.pallas_call(
        _shortconv_kernel,
        grid=(B, C // HY_LANES),
        in_specs=[
            pl.BlockSpec((None, L, HY_LANES), lambda i, j: (i, 0, j)),
            pl.BlockSpec((3, HY_LANES), lambda i, j: (0, j)),
            pl.BlockSpec((1, HY_LANES), lambda i, j: (0, j)),
        ],
        out_specs=pl.BlockSpec((None, L, HY_LANES), lambda i, j: (i, 0, j)),
        out_shape=jax.ShapeDtypeStruct((B, L, C), F32),
        compiler_params=_cparams(("parallel", "parallel")),
        name="shortconv",
    )(zh, w, b)


FFT_R = 128
FFT_N = FFT_R * FFT_R
FFT_PITCH = 136
FFT_UNROLL = 8
FFT_K1_USED = FFT_R // 2 + 1
FFT_K1_SLABS = 66
FFT_K1_ROWS = 72
FFT_K1_CHUNK = 22
assert FFT_K1_SLABS % FFT_K1_CHUNK == 0 and FFT_K1_CHUNK % 2 == 0 and FFT_K1_USED <= FFT_K1_SLABS <= FFT_K1_ROWS


def _fft_tables():
    r = np.arange(FFT_R)
    kk = np.arange(FFT_K1_ROWS)
    n2, k1, n1 = r[:, None, None], kk[None, :, None], r[None, None, :]
    ang = (2.0 * math.pi / FFT_N) * ((k1 * (FFT_R * n1 + n2)) % FFT_N)
    used = k1 < FFT_K1_USED
    c, s = np.cos(ang) * used, np.sin(ang) * used
    g_full = np.concatenate([c, -s], axis=1)
    twice = np.where((k1 == 0) | (k1 == FFT_R // 2), 1.0, 2.0) / FFT_N
    h_half = np.concatenate([np.swapaxes(c * twice, 1, 2), -np.swapaxes(s * twice, 1, 2)],
                            axis=2)[:, :FFT_R // 2]
    ang2 = (2.0 * math.pi / FFT_R) * ((r[:, None] * r[None, :]) % FFT_R)
    c2, s2 = np.cos(ang2), np.sin(ang2)
    f_fwd = np.block([[c2, s2], [-s2, c2]])
    f_inv = np.block([[c2, -s2], [s2, c2]])
    return tuple(jnp.asarray(t, dtype=F32).astype(BF16) for t in (g_full, h_half, f_fwd, f_inv))


def _fft_stage1(load_slab, g_ref, bre_ref, bim_ref):
    def body(n2, carry):
        out = _dot(g_ref[n2], load_slab(n2).astype(BF16))
        bre_ref[pl.ds(n2, FFT_K1_ROWS, stride=FFT_PITCH), :] = out[:FFT_K1_ROWS]
        bim_ref[pl.ds(n2, FFT_K1_ROWS, stride=FFT_PITCH), :] = out[FFT_K1_ROWS:]
        return carry

    lax.fori_loop(0, FFT_R, body, 0, unroll=FFT_UNROLL)


def _k1_rows(k1):
    return pl.ds(pl.multiple_of(k1 * FFT_PITCH, SUBLANES), FFT_R)


def _load_k1_pair(bre_ref, bim_ref, k1):
    return jnp.concatenate(
        [jnp.concatenate([bre_ref[_k1_rows(k1 + d), :], bim_ref[_k1_rows(k1 + d), :]], axis=0) for d in range(2)],
        axis=1)


FILT_ROWS = 512
FILT_FEAT = 128


def _filt_kernel(feat_ref, delta_ref, w1_ref, b1_ref, w2_ref, b2_ref, w3_ref, b3_ref, w4_ref, freq_ref,
                 h_ref, sum_ref):
    @pl.when(pl.program_id(0) == 0)
    def _():
        sum_ref[...] = jnp.zeros_like(sum_ref)

    feat = feat_ref[...]
    freq = freq_ref[...]
    dot = lambda a, b: jnp.dot(a, b, precision=HI, preferred_element_type=F32)
    h = jnp.sin(freq * (dot(feat, w1_ref[...]) + b1_ref[...]))
    h = jnp.sin(freq * (dot(h, w2_ref[...]) + b2_ref[...]))
    h = jnp.sin(freq * (dot(h, w3_ref[...]) + b3_ref[...]))
    hi = h.astype(BF16)
    lo = (h - hi.astype(F32)).astype(BF16)
    h = _dot(hi, w4_ref[...]) + _dot(lo, w4_ref[...])
    n_feat = 1 + 2 * HY_BANDS
    t = feat[:, 0:1]
    forward = feat[:, n_feat:n_feat + 1] > 0.5
    keep = feat[:, n_feat + 1:n_feat + 2]
    decay = jnp.exp(-t * delta_ref[...])
    outs = []
    for o in range(HY_ORDER):
        base = o * 2 * D_HY
        outs.append(jnp.where(forward, h[:, base:base + D_HY], h[:, base + D_HY:base + 2 * D_HY]) * decay)
    out = jnp.concatenate(outs, axis=1)
    sum_ref[...] += jnp.sum(jnp.abs(out), axis=0, keepdims=True)
    h_ref[...] = out * keep


def _filt_features(L):
    r = np.arange(FFT_N)
    n = FFT_R * (r % FFT_R) + r // FFT_R
    j = np.where(n < L, n, np.where(n > L, 2 * L - n, 0))
    jj = j.astype(np.float64)[:, None]
    t = jj / (L - 1.0)
    w = (2.0 * math.pi / L) * jj
    bands = np.linspace(1e-4, HY_BANDS - 1.0, HY_BANDS)[None, :]
    flags = np.stack([n < L, n != L], axis=1).astype(np.float64)
    feats = np.concatenate([t, np.cos(bands * w), -np.sin(bands * w), flags], axis=-1)
    return jnp.asarray(np.pad(feats, ((0, 0), (0, FILT_FEAT - feats.shape[1]))), dtype=F32)


def _filters(p, L):
    feats = _filt_features(L)
    deltas = jnp.abs(jnp.linspace(math.log(HY_DECAY_TARGET) / HY_SLOW_PCT,
                                  math.log(HY_DECAY_TARGET) / HY_FAST_PCT, D_HY, dtype=F32))[None, :]
    w1 = jnp.pad(p['hy_f1_w'], ((0, FILT_FEAT - p['hy_f1_w'].shape[0]), (0, 0)))
    n_out = HY_ORDER * D_HY
    const = lambda i: (0, 0)
    full = lambda a: pl.BlockSpec(a.shape, const)
    args = [deltas, w1, p['hy_f1_b'][None], p['hy_f2_w'], p['hy_f2_b'][None], p['hy_f3_w'], p['hy_f3_b'][None],
            p['hy_f4_w'].astype(BF16), p['hy_sin_freq'][None]]
    return pl.pallas_call(
        _filt_kernel,
        grid=(FFT_N // FILT_ROWS,),
        in_specs=[pl.BlockSpec((FILT_ROWS, FILT_FEAT), lambda i: (i, 0))] + [full(a) for a in args],
        out_specs=[pl.BlockSpec((FILT_ROWS, n_out), lambda i: (i, 0)), pl.BlockSpec((1, n_out), const)],
        out_shape=[jax.ShapeDtypeStruct((FFT_N, n_out), F32), jax.ShapeDtypeStruct((1, n_out), F32)],
        compiler_params=_cparams(("arbitrary",)),
        name="hyena_filter",
    )(feats, *args)


def _fspec_kernel(h_ref, sum_ref, g_ref, f_ref, o_ref, bre_ref, bim_ref):
    c = pl.program_id(1)

    @pl.when(c == 0)
    def _():
        _fft_stage1(lambda n2: h_ref[pl.ds(pl.multiple_of(n2 * FFT_R, FFT_R), FFT_R), :], g_ref, bre_ref, bim_ref)

    inv = 1.0 / sum_ref[...]

    lanes = h_ref.shape[1]

    def slab_pair(j, carry):
        x = _dot(f_ref[...], _load_k1_pair(bre_ref, bim_ref, c * FFT_K1_CHUNK + 2 * j).astype(BF16))
        for d in range(2):
            rows = pl.ds(pl.multiple_of((2 * j + d) * FFT_R, FFT_R), FFT_R)
            o_ref[0, rows, :] = x[:FFT_R, d * lanes:(d + 1) * lanes] * inv
            o_ref[1, rows, :] = x[FFT_R:, d * lanes:(d + 1) * lanes] * inv
        return carry

    lax.fori_loop(0, FFT_K1_CHUNK // 2, slab_pair, 0, unroll=True)


def _filter_spectrum(h, hsum, g_full, f_fwd):
    n_ch = h.shape[1]
    single = pl.Buffered(1)
    return pl.pallas_call(
        _fspec_kernel,
        grid=(n_ch // HY_LANES, FFT_K1_SLABS // FFT_K1_CHUNK),
        in_specs=[
            pl.BlockSpec((FFT_N, HY_LANES), lambda j, c: (0, j), pipeline_mode=single),
            pl.BlockSpec((1, HY_LANES), lambda j, c: (0, j)),
            pl.BlockSpec(g_full.shape, lambda j, c: (0, 0, 0), pipeline_mode=single),
            pl.BlockSpec(f_fwd.shape, lambda j, c: (0, 0)),
        ],
        out_specs=pl.BlockSpec((2, FFT_K1_CHUNK * FFT_R, HY_LANES), lambda j, c: (0, c, j)),
        out_shape=jax.ShapeDtypeStruct((2, FFT_K1_SLABS * FFT_R, n_ch), F32),
        scratch_shapes=[pltpu.VMEM((FFT_K1_ROWS * FFT_PITCH, HY_LANES), F32)] * 2,
        compiler_params=_cparams(("parallel", "arbitrary")),
        name="hyena_filter_fft",
    )(h, hsum, g_full, f_fwd)


def _hyconv_kernel(u_ref, gate_ref, kf_ref, skip_ref, g_ref, h_ref, f_ref, fi_ref, o_ref, bre_ref, bim_ref):
    c = pl.program_id(2)
    half = FFT_R // 2

    def rows_of(n2):
        return pl.ds(pl.multiple_of(n2 * half, half), half)

    @pl.when(c == 0)
    def _():
        _fft_stage1(lambda n2: u_ref[rows_of(n2), :], g_ref, bre_ref, bim_ref)

    lanes = u_ref.shape[1]

    def slab_pair(j, carry):
        k1 = c * FFT_K1_CHUNK + 2 * j
        x = _dot(f_ref[...], _load_k1_pair(bre_ref, bim_ref, k1).astype(BF16))
        xr, xi = x[:FFT_R], x[FFT_R:]
        kr, ki = [jnp.concatenate([kf_ref[part, pl.ds(pl.multiple_of((2 * j + d) * FFT_R, FFT_R), FFT_R), :]
                                   for d in range(2)], axis=1) for part in range(2)]
        y = jnp.concatenate([xr * kr - xi * ki, xr * ki + xi * kr], axis=0)
        z = _dot(fi_ref[...], y.astype(BF16))
        for d in range(2):
            bre_ref[_k1_rows(k1 + d), :] = z[:FFT_R, d * lanes:(d + 1) * lanes]
            bim_ref[_k1_rows(k1 + d), :] = z[FFT_R:, d * lanes:(d + 1) * lanes]
        return carry

    lax.fori_loop(0, FFT_K1_CHUNK // 2, slab_pair, 0, unroll=True)

    @pl.when(c == pl.num_programs(2) - 1)
    def _():
        def body(n2, carry):
            z = jnp.concatenate([bre_ref[pl.ds(n2, FFT_K1_ROWS, stride=FFT_PITCH), :],
                                 bim_ref[pl.ds(n2, FFT_K1_ROWS, stride=FFT_PITCH), :]], axis=0)
            conv = _dot(h_ref[n2], z.astype(BF16))
            rows = rows_of(n2)
            o_ref[rows, :] = gate_ref[rows, :] * (conv + skip_ref[...] * u_ref[rows, :])
            return carry

        lax.fori_loop(0, FFT_R, body, 0, unroll=FFT_UNROLL)


def _hyconv(u, u_col, gate, gate_col, kf, kf_col, skip, tables):
    g_half, h_half, f_fwd, f_inv = tables
    B, L, _ = u.shape
    single = pl.Buffered(1)
    return pl.pallas_call(
        _hyconv_kernel,
        grid=(B, D_HY // HY_LANES, FFT_K1_SLABS // FFT_K1_CHUNK),
        in_specs=[
            pl.BlockSpec((None, L, HY_LANES), lambda b, j, c: (b, 0, u_col + j), pipeline_mode=single),
            pl.BlockSpec((None, L, HY_LANES), lambda b, j, c: (b, 0, gate_col + j), pipeline_mode=single),
            pl.BlockSpec((2, FFT_K1_CHUNK * FFT_R, HY_LANES), lambda b, j, c: (0, c, kf_col + j)),
            pl.BlockSpec((1, HY_LANES), lambda b, j, c: (0, j)),
            pl.BlockSpec(g_half.shape, lambda b, j, c: (0, 0, 0), pipeline_mode=single),
            pl.BlockSpec(h_half.shape, lambda b, j, c: (0, 0, 0), pipeline_mode=single),
            pl.BlockSpec(f_fwd.shape, lambda b, j, c: (0, 0)),
            pl.BlockSpec(f_inv.shape, lambda b, j, c: (0, 0)),
        ],
        out_specs=pl.BlockSpec((None, L, HY_LANES), lambda b, j, c: (b, 0, j)),
        out_shape=jax.ShapeDtypeStruct((B, L, D_HY), F32),
        scratch_shapes=[pltpu.VMEM((FFT_K1_ROWS * FFT_PITCH, HY_LANES), F32)] * 2,
        compiler_params=_cparams(("parallel", "parallel", "arbitrary")),
        name="hyena_conv",
    )(u, gate, kf, skip, g_half, h_half, f_fwd, f_inv)


def _slab_major(a, n1):
    B, L, C = a.shape
    return a.reshape(B, n1, L // n1, C).transpose(0, 2, 1, 3).reshape(B, L, C)


def _hyena(zh, p):
    B, L, _ = zh.shape
    assert 2 * L == FFT_N
    g_full, h_half, f_fwd, f_inv = _fft_tables()
    tables = (g_full[:, :, :FFT_R // 2], h_half, f_fwd, f_inv)
    filt, filt_sum = _filters(p, L)
    both_dirs = filt_sum[:, :]
    kf = _filter_spectrum(filt, both_dirs, g_full, f_fwd)
    zc = _slab_major(_shortconv(zh, p['hy_conv_w'], p['hy_conv_b'][None]), FFT_R // 2)
    lanes = D_HY // HY_LANES
    y = _hyconv(zc, 0, zc, lanes, kf, 0, p['hy_skip'][0:1], tables)
    y = _hyconv(y, 0, zc, 2 * lanes, kf, lanes, p['hy_skip'][1:2], tables)
    return _slab_major(y, L // (FFT_R // 2))


def kernel(x, c, ctx, c_ctx, norm1_g, norm2_g, w_ada, b_ada, w_in, b_in, hy_conv_w, hy_conv_b, hy_f1_w, hy_f1_b, hy_f2_w, hy_f2_b, hy_f3_w, hy_f3_b, hy_f4_w, hy_sin_freq, hy_skip, q_norm_g, k_norm_g, na_rpb, w_hy_out, w_na_out, w_out, peer_w_q, peer_keys, peer_u, peer_v):
    assert w_in.shape[0] == 1, "single-layer block"
    B, S, D = x.shape
    C = ctx.shape[1]
    T = B * S

    m_lat = (jax.nn.silu(c) @ w_ada[0] + b_ada[0]).reshape(B, N_MOD, 1, D)
    m_ctx = (jax.nn.silu(c_ctx) @ w_ada[0] + b_ada[0]).reshape(N_MOD, 1, D)
    sh1, sc1, g1, sh2, sc2, g2 = [m_lat[:, i] for i in range(N_MOD)]

    w_in_bf = w_in[0].astype(BF16)
    b_in2 = b_in[0][None, :]
    gain1 = norm1_g[0][None, :]
    qg = jnp.tile(q_norm_g[0], NA_HEADS)[None, :]
    kg = jnp.tile(k_norm_g[0], NA_HEADS)[None, :]
    seg = jnp.asarray(np.kron(np.eye(NA_HEADS), np.full((HEAD_DIM, HEAD_DIM), 1.0 / HEAD_DIM)), dtype=BF16)
    cos, sin = _rope_tables(S)

    x2 = x.reshape(T, D)
    zh, q, k, v, gate = _inproj(x2, sh1, sc1, gain1, w_in_bf, b_in2, cos, sin, qg, kg, seg, S)

    k_ctx, v_ctx = _ctxkv(ctx.reshape(B * C, D), m_ctx[0], m_ctx[1], gain1,
                          w_in_bf[:, COL_K:COL_V], b_in2[:, COL_K:COL_V],
                          w_in_bf[:, COL_V:COL_G_HY], b_in2[:, COL_V:COL_G_HY], kg, seg)

    p = {'hy_conv_w': hy_conv_w[0], 'hy_conv_b': hy_conv_b[0], 'hy_f1_w': hy_f1_w[0], 'hy_f1_b': hy_f1_b[0],
         'hy_f2_w': hy_f2_w[0], 'hy_f2_b': hy_f2_b[0], 'hy_f3_w': hy_f3_w[0], 'hy_f3_b': hy_f3_b[0],
         'hy_f4_w': hy_f4_w[0], 'hy_sin_freq': hy_sin_freq[0], 'hy_skip': hy_skip[0]}
    y_hy = _hyena(zh.reshape(B, S, 3 * D_HY), p).reshape(T, D_HY)

    y_na = _natten(q.reshape(B, S, D_NA), k.reshape(B, S, D_NA), v.reshape(B, S, D_NA),
                   k_ctx.reshape(B, C, D_NA), v_ctx.reshape(B, C, D_NA), _natten_bias(na_rpb[0], S // GRID_W)).reshape(T, D_NA)

    x1, h2, pq = _postmix(y_hy, y_na, gate, x2, g1, sh2, sc2, norm2_g[0][None, :],
                          w_hy_out[0].astype(BF16), w_na_out[0].astype(BF16), w_out[0].astype(BF16),
                          peer_w_q[0].astype(BF16), S)

    keys_bf = peer_keys[0].astype(BF16).reshape(2 * PEER_HEADS, PEER_N_KEYS, PEER_D_HALF)
    sel_a, sel_b, sel_g = _route(pq, keys_bf)
    w = _wbuild(sel_a, sel_b, sel_g)
    out = _peer_dense(h2, w, peer_u[0].astype(BF16), peer_v[0].astype(BF16), x1, g2, S)
    return out.reshape(B, S, D)
```

```python
import functools
import math

import numpy as np
import jax
import jax.numpy as jnp
from jax import lax
from jax.experimental import pallas as pl
from jax.experimental.pallas import tpu as pltpu

F32 = jnp.float32
BF16 = jnp.bfloat16

D_MODEL = 1024
GRID_W = 64
EPS = 1e-6
N_MOD = 6

D_HY = 512
HY_ORDER = 2
HY_BANDS = 16
HY_DECAY_TARGET = 1e-2
HY_FAST_PCT = 0.3
HY_SLOW_PCT = 1.5

NA_HEADS = 8
HEAD_DIM = 64
D_NA = NA_HEADS * HEAD_DIM
WIN_ROWS = 8
WIN_COLS = 16
ROPE_THETA = 10000.0
NEG_INF = -1e30

PEER_HEADS = 8
PEER_N_KEYS = 128
PEER_N_EXPERTS = PEER_N_KEYS * PEER_N_KEYS
PEER_TOPK = 16
PEER_D_KEY = 256
PEER_D_HALF = PEER_D_KEY // 2

COL_HY = 0
COL_Q = COL_HY + 3 * D_HY
COL_K = COL_Q + D_NA
COL_V = COL_K + D_NA
COL_G_HY = COL_V + D_NA
N_PROJ = COL_G_HY + 2 * D_MODEL

VMEM_LIMIT = 56 * 1024 * 1024
TOK_TILE = 512
PEER_TOK_TILE = 512
PEER_EXP_TILE = 2048


def _cparams(sem):
    return pltpu.CompilerParams(dimension_semantics=sem, vmem_limit_bytes=VMEM_LIMIT)


def _dot(a, b):
    return jnp.dot(a, b, preferred_element_type=F32)


def _dot_nt(a, b):
    return lax.dot_general(a, b, (((1,), (1,)), ((), ())), preferred_element_type=F32)


def _modulated_norm(x, gain, shift, scale):
    ms = jnp.mean(x * x, axis=-1, keepdims=True)
    return (x * lax.rsqrt(ms + EPS) * gain) * (1.0 + scale) + shift


def _head_rms(z, seg, gain):
    ms = _dot((z * z).astype(BF16), seg)
    return z * lax.rsqrt(ms + EPS) * gain


def _rope(z, cos, sin_signed):
    n = z.shape[-1]
    lane = lax.broadcasted_iota(jnp.int32, z.shape, 1)
    first = (lane // (HEAD_DIM // 4)) % 2 == 0
    partner = jnp.where(first, pltpu.roll(z, n - HEAD_DIM // 4, 1), pltpu.roll(z, HEAD_DIM // 4, 1))
    return z * cos + partner * sin_signed


def _inproj_kernel(x_ref, shift_ref, scale_ref, g_ref, w_ref, b_ref, cos_ref, sin_ref, qg_ref, kg_ref,
                   seg_ref, zh_ref, q_ref, k_ref, v_ref, gate_ref):
    h = _modulated_norm(x_ref[...], g_ref[...], shift_ref[...], scale_ref[...]).astype(BF16)

    def proj(lo, hi):
        return _dot(h, w_ref[:, lo:hi]) + b_ref[:, lo:hi]

    zh_ref[...] = proj(COL_HY, COL_Q)
    reps = D_NA // cos_ref.shape[1]
    cos = jnp.concatenate([cos_ref[...]] * reps, axis=1)
    sin = jnp.concatenate([sin_ref[...]] * reps, axis=1)
    seg = seg_ref[...]
    q = _rope(_head_rms(proj(COL_Q, COL_K), seg, qg_ref[...]), cos, sin)
    q_ref[...] = q.astype(BF16)
    k = _rope(_head_rms(proj(COL_K, COL_V), seg, kg_ref[...]), cos, sin)
    k_ref[...] = k.astype(BF16)
    v_ref[...] = proj(COL_V, COL_G_HY).astype(BF16)
    gate_ref[...] = jax.nn.sigmoid(proj(COL_G_HY, N_PROJ)).astype(BF16)


def _inproj(x2, shift, scale, gain, w_bf, b_in, cos, sin, qg, kg, seg, seq):
    T = x2.shape[0]
    tiles_per_batch = seq // TOK_TILE
    row = lambda i: (i, 0)
    per_batch = lambda i: (i // tiles_per_batch, 0, 0)
    const = lambda i: (0, 0)
    pos = lambda i: (i % tiles_per_batch, 0)
    return pl.pallas_call(
        _inproj_kernel,
        grid=(T // TOK_TILE,),
        in_specs=[
            pl.BlockSpec((TOK_TILE, D_MODEL), row),
            pl.BlockSpec((None, 1, D_MODEL), per_batch),
            pl.BlockSpec((None, 1, D_MODEL), per_batch),
            pl.BlockSpec((1, D_MODEL), const),
            pl.BlockSpec((D_MODEL, N_PROJ), const),
            pl.BlockSpec((1, N_PROJ), const),
            pl.BlockSpec((TOK_TILE, cos.shape[1]), pos),
            pl.BlockSpec((TOK_TILE, sin.shape[1]), pos),
            pl.BlockSpec((1, D_NA), const),
            pl.BlockSpec((1, D_NA), const),
            pl.BlockSpec((D_NA, D_NA), const),
        ],
        out_specs=[
            pl.BlockSpec((TOK_TILE, 3 * D_HY), row),
            pl.BlockSpec((TOK_TILE, D_NA), row),
            pl.BlockSpec((TOK_TILE, D_NA), row),
            pl.BlockSpec((TOK_TILE, D_NA), row),
            pl.BlockSpec((TOK_TILE, 2 * D_MODEL), row),
        ],
        out_shape=[
            jax.ShapeDtypeStruct((T, 3 * D_HY), F32),
            jax.ShapeDtypeStruct((T, D_NA), BF16),
            jax.ShapeDtypeStruct((T, D_NA), BF16),
            jax.ShapeDtypeStruct((T, D_NA), BF16),
            jax.ShapeDtypeStruct((T, 2 * D_MODEL), BF16),
        ],
        compiler_params=_cparams(("parallel",)),
        name="inproj",
    )(x2, shift, scale, gain, w_bf, b_in, cos, sin, qg, kg, seg)


def _ctxkv_kernel(x_ref, shift_ref, scale_ref, g_ref, wk_ref, bk_ref, wv_ref, bv_ref, kg_ref, seg_ref,
                  k_ref, v_ref):
    h = _modulated_norm(x_ref[...], g_ref[...], shift_ref[...], scale_ref[...]).astype(BF16)
    k = _head_rms(_dot(h, wk_ref[...]) + bk_ref[...], seg_ref[...], kg_ref[...])
    k_ref[...] = k.astype(BF16)
    v_ref[...] = (_dot(h, wv_ref[...]) + bv_ref[...]).astype(BF16)


def _ctxkv(ctx2, shift, scale, gain, wk, bk, wv, bv, kg, seg):
    T = ctx2.shape[0]
    row = lambda i: (i, 0)
    const = lambda i: (0, 0)
    return pl.pallas_call(
        _ctxkv_kernel,
        grid=(T // TOK_TILE,),
        in_specs=[
            pl.BlockSpec((TOK_TILE, D_MODEL), row),
            pl.BlockSpec((1, D_MODEL), const),
            pl.BlockSpec((1, D_MODEL), const),
            pl.BlockSpec((1, D_MODEL), const),
            pl.BlockSpec((D_MODEL, D_NA), const),
            pl.BlockSpec((1, D_NA), const),
            pl.BlockSpec((D_MODEL, D_NA), const),
            pl.BlockSpec((1, D_NA), const),
            pl.BlockSpec((1, D_NA), const),
            pl.BlockSpec((D_NA, D_NA), const),
        ],
        out_specs=[pl.BlockSpec((TOK_TILE, D_NA), row), pl.BlockSpec((TOK_TILE, D_NA), row)],
        out_shape=[jax.ShapeDtypeStruct((T, D_NA), BF16), jax.ShapeDtypeStruct((T, D_NA), BF16)],
        compiler_params=_cparams(("parallel",)),
        name="ctxkv",
    )(ctx2, shift, scale, gain, wk, bk, wv, bv, kg, seg)


NA_BLOCK_ROWS = 4
NA_KEY_ROWS = WIN_ROWS + NA_BLOCK_ROWS
NA_CLASSES = 3


def _natten_kernel(q_ref, k_ref, v_ref, kc_ref, vc_ref, bias_ref, o_ref, *, rows):
    r0 = pl.program_id(1) * NA_BLOCK_ROWS
    key_start = jnp.clip(r0 - WIN_ROWS // 2, 0, rows - NA_KEY_ROWS)
    start = pl.multiple_of(key_start * GRID_W, GRID_W * NA_BLOCK_ROWS)
    n_q = NA_BLOCK_ROWS * GRID_W
    n_loc = NA_KEY_ROWS * GRID_W
    scale = jnp.asarray(HEAD_DIM ** -0.5, BF16)
    pair = 2 * HEAD_DIM
    lane = lax.broadcasted_iota(jnp.int32, (n_q, pair), 1)
    for p in range(NA_HEADS // 2):
        cols = slice(p * pair, (p + 1) * pair)
        qp = q_ref[:, cols] * scale
        kp = k_ref[pl.ds(start, n_loc), cols]
        vp = v_ref[pl.ds(start, n_loc), cols]
        kcp = kc_ref[:, cols]
        vcp = vc_ref[:, cols]
        outs = []
        for hh in range(2):
            qm = jnp.where((lane // HEAD_DIM) == hh, qp, jnp.zeros_like(qp))
            s_loc = _dot_nt(qm, kp) + bias_ref[2 * p + hh]
            s_ctx = _dot_nt(qm, kcp)
            m = jnp.maximum(jnp.max(s_loc, axis=-1, keepdims=True), jnp.max(s_ctx, axis=-1, keepdims=True))
            e_loc = jnp.exp(s_loc - m)
            e_ctx = jnp.exp(s_ctx - m)
            denom = jnp.sum(e_loc, axis=-1, keepdims=True) + jnp.sum(e_ctx, axis=-1, keepdims=True)
            o = _dot(e_loc.astype(BF16), vp) + _dot(e_ctx.astype(BF16), vcp)
            outs.append(o / denom)
        o_ref[:, cols] = jnp.where((lane // HEAD_DIM) == 0, outs[0], outs[1]).astype(o_ref.dtype)


def _natten(q, k, v, kc, vc, bias):
    B, S, _ = q.shape
    rows = S // GRID_W
    n_blk = rows // NA_BLOCK_ROWS
    C = kc.shape[1]
    n_q = NA_BLOCK_ROWS * GRID_W
    single = pl.Buffered(1)
    block_class = lambda b, i: (jnp.where(i == 0, 0, jnp.where(i == n_blk - 1, 2, 1)), 0, 0, 0)
    return pl.pallas_call(
        functools.partial(_natten_kernel, rows=rows),
        grid=(B, n_blk),
        in_specs=[
            pl.BlockSpec((None, n_q, D_NA), lambda b, i: (b, i, 0)),
            pl.BlockSpec((None, S, D_NA), lambda b, i: (b, 0, 0), pipeline_mode=single),
            pl.BlockSpec((None, S, D_NA), lambda b, i: (b, 0, 0), pipeline_mode=single),
            pl.BlockSpec((None, C, D_NA), lambda b, i: (b, 0, 0)),
            pl.BlockSpec((None, C, D_NA), lambda b, i: (b, 0, 0)),
            pl.BlockSpec((None,) + bias.shape[1:], block_class),
        ],
        out_specs=pl.BlockSpec((None, n_q, D_NA), lambda b, i: (b, i, 0)),
        out_shape=jax.ShapeDtypeStruct((B, S, D_NA), BF16),
        compiler_params=_cparams(("parallel", "arbitrary")),
        name="natten",
    )(q, k, v, kc, vc, bias)


def _natten_bias(rpb, rows):
    col = np.arange(GRID_W)
    dc = np.clip(col[None, :] - col[:, None], -(WIN_COLS - 1), WIN_COLS - 1) + (WIN_COLS - 1)
    col_start = np.clip(col - WIN_COLS // 2, 0, GRID_W - WIN_COLS)
    col_mask = (col[None, :] >= col_start[:, None]) & (col[None, :] < col_start[:, None] + WIN_COLS)
    r0 = np.array([0, 2 * NA_BLOCK_ROWS, rows - NA_BLOCK_ROWS])
    r = r0[:, None] + np.arange(NA_BLOCK_ROWS)[None, :]
    key_row = np.clip(r0 - WIN_ROWS // 2, 0, rows - NA_KEY_ROWS)[:, None, None] + np.arange(NA_KEY_ROWS)
    win_start = np.clip(r - WIN_ROWS // 2, 0, rows - WIN_ROWS)[:, :, None]
    row_mask = (key_row >= win_start) & (key_row < win_start + WIN_ROWS)
    dr = key_row - r[:, :, None] + (WIN_ROWS - 1)
    pick_c = jnp.asarray(dc[None] == np.arange(2 * WIN_COLS - 1)[:, None, None], dtype=F32)
    toep = jnp.einsum('hde,eqx->hdqx', rpb.astype(F32), pick_c, precision=HI)
    toep = jnp.where(col_mask[None, None], toep, NEG_INF)
    toep = jnp.pad(toep, ((0, 0), (1, 1), (0, 0), (0, 0)), constant_values=NEG_INF)
    pairs = jnp.concatenate([toep[:, :-1], toep[:, 1:]], axis=-1)

    def kernel(t_ref, o_ref):
        neg = jnp.full((GRID_W, 2 * GRID_W), NEG_INF, F32)
        lane_left = lax.broadcasted_iota(jnp.int32, (GRID_W, 2 * GRID_W), 1) < GRID_W
        for c in range(NA_CLASSES):
            for rq in range(NA_BLOCK_ROWS):
                for kp in range(NA_KEY_ROWS // 2):
                    left, right = bool(row_mask[c, rq, 2 * kp]), bool(row_mask[c, rq, 2 * kp + 1])
                    block = neg
                    if left or right:
                        block = t_ref[int(dr[c, rq, 2 * kp]) + 1]
                        if not (left and right):
                            block = jnp.where(lane_left == left, block, NEG_INF)
                    o_ref[c, rq * GRID_W:(rq + 1) * GRID_W, kp * 2 * GRID_W:(kp + 1) * 2 * GRID_W] = block

    n_q, n_k = NA_BLOCK_ROWS * GRID_W, NA_KEY_ROWS * GRID_W
    return pl.pallas_call(
        kernel,
        grid=(NA_HEADS,),
        in_specs=[pl.BlockSpec((None,) + pairs.shape[1:], lambda h: (h, 0, 0, 0))],
        out_specs=pl.BlockSpec((NA_CLASSES, None, n_q, n_k), lambda h: (0, h, 0, 0)),
        out_shape=jax.ShapeDtypeStruct((NA_CLASSES, NA_HEADS, n_q, n_k), F32),
        compiler_params=_cparams(("parallel",)),
        name="natten_bias",
    )(pairs)


def _rope_tables(seq):
    pos = np.arange(seq)
    rows = (pos // GRID_W).astype(np.float64)
    cols = (pos % GRID_W).astype(np.float64)
    nf = HEAD_DIM // 4
    inv = ROPE_THETA ** (-np.arange(nf, dtype=np.float64) / nf)
    ar = rows[:, None] * inv[None, :]
    ac = cols[:, None] * inv[None, :]
    cos = np.concatenate([np.cos(ar), np.cos(ar), np.cos(ac), np.cos(ac)], axis=-1)
    sin = np.concatenate([-np.sin(ar), np.sin(ar), -np.sin(ac), np.sin(ac)], axis=-1)
    return (jnp.asarray(np.tile(cos, (1, 2)), dtype=F32), jnp.asarray(np.tile(sin, (1, 2)), dtype=F32))


def _postmix_kernel(yh_ref, yn_ref, gate_ref, x_ref, g1_ref, sh2_ref, sc2_ref, n2_ref, whyo_ref, wnao_ref,
                    wout_ref, wq_ref, x1_ref, h2_ref, pq_ref):
    a = _dot(yh_ref[...].astype(BF16), whyo_ref[...])
    b = _dot(yn_ref[...], wnao_ref[...])
    gate = gate_ref[...].astype(F32)
    merged = gate[:, :D_MODEL] * a + gate[:, D_MODEL:] * b
    x1 = x_ref[...] + g1_ref[...] * _dot(merged.astype(BF16), wout_ref[...])
    x1_ref[...] = x1
    h2 = _modulated_norm(x1, n2_ref[...], sh2_ref[...], sc2_ref[...]).astype(BF16)
    h2_ref[...] = h2
    for hp in range(2 * PEER_HEADS):
        pq_ref[hp] = _dot(h2, wq_ref[:, hp * PEER_D_HALF:(hp + 1) * PEER_D_HALF]).astype(BF16)


def _postmix(yh, yn, gate, x2, g1, sh2, sc2, n2g, whyo, wnao, wout, wq, seq):
    T = x2.shape[0]
    tiles_per_batch = seq // TOK_TILE
    row = lambda i: (i, 0)
    per_batch = lambda i: (i // tiles_per_batch, 0, 0)
    const = lambda i: (0, 0)
    nq = wq.shape[1]
    return pl.pallas_call(
        _postmix_kernel,
        grid=(T // TOK_TILE,),
        in_specs=[
            pl.BlockSpec((TOK_TILE, D_HY), row),
            pl.BlockSpec((TOK_TILE, D_NA), row),
            pl.BlockSpec((TOK_TILE, 2 * D_MODEL), row),
            pl.BlockSpec((TOK_TILE, D_MODEL), row),
            pl.BlockSpec((None, 1, D_MODEL), per_batch),
            pl.BlockSpec((None, 1, D_MODEL), per_batch),
            pl.BlockSpec((None, 1, D_MODEL), per_batch),
            pl.BlockSpec((1, D_MODEL), const),
            pl.BlockSpec((D_HY, D_MODEL), const),
            pl.BlockSpec((D_NA, D_MODEL), const),
            pl.BlockSpec((D_MODEL, D_MODEL), const),
            pl.BlockSpec((D_MODEL, nq), const),
        ],
        out_specs=[
            pl.BlockSpec((TOK_TILE, D_MODEL), row),
            pl.BlockSpec((TOK_TILE, D_MODEL), row),
            pl.BlockSpec((nq // PEER_D_HALF, TOK_TILE, PEER_D_HALF), lambda i: (0, i, 0)),
        ],
        out_shape=[
            jax.ShapeDtypeStruct((T, D_MODEL), F32),
            jax.ShapeDtypeStruct((T, D_MODEL), BF16),
            jax.ShapeDtypeStruct((nq // PEER_D_HALF, T, PEER_D_HALF), BF16),
        ],
        compiler_params=_cparams(("parallel",)),
        name="postmix",
    )(yh, yn, gate, x2, g1, sh2, sc2, n2g, whyo, wnao, wout, wq)


SUBLANES = 8
LANES = 128


def _argmax_tree(vals, idxs):
    while len(vals) > 1:
        nv, ni = [], []
        for j in range(0, len(vals) - 1, 2):
            right = vals[j + 1] > vals[j]
            nv.append(jnp.where(right, vals[j + 1], vals[j]))
            ni.append(jnp.where(right, idxs[j + 1], idxs[j]))
        if len(vals) % 2:
            nv.append(vals[-1])
            ni.append(idxs[-1])
        vals, idxs = nv, ni
    return vals[0], idxs[0]


W_BLOCK = 128
W_PITCH = 136


SORT_TILE = SUBLANES * LANES


def _batcher_sort_network(n):
    ces = []

    def merge(lo, m, r):
        step = r * 2
        if step < m:
            merge(lo, m, step)
            merge(lo + r, m, step)
            ces.extend((i, i + r) for i in range(lo + r, lo + m - r, step))
        else:
            ces.append((lo, lo + r))

    def sort(lo, m):
        if m > 1:
            sort(lo, m // 2)
            sort(lo + m // 2, m // 2)
            merge(lo, m, 1)

    sort(0, n)
    return ces


def _bitonic_merge_network(n):
    ces, s = [], n // 2
    while s >= 1:
        ces.extend((i, i + s) for i in range(n) if (i & s) == 0)
        s //= 2
    return ces


SORT_NET = _batcher_sort_network(PEER_TOPK)
MERGE_NET = _bitonic_merge_network(PEER_TOPK)


def _compare_exchange(v, x, net):
    for i, j in net:
        c = v[j] > v[i]
        v[i], v[j] = jnp.where(c, v[j], v[i]), jnp.where(c, v[i], v[j])
        x[i], x[j] = jnp.where(c, x[j], x[i]), jnp.where(c, x[i], x[j])


def _merge_top(a, b, dropped):
    (va, xa), (vb, xb) = a, b
    k = len(va)
    v, x = [], []
    for r in range(k):
        c = vb[k - 1 - r] > va[r]
        v.append(jnp.where(c, vb[k - 1 - r], va[r]))
        x.append(jnp.where(c, xb[k - 1 - r], xa[r]))
        dropped = jnp.maximum(dropped, jnp.where(c, va[r], vb[k - 1 - r]))
    _compare_exchange(v, x, MERGE_NET)
    return (v, x), dropped


def _merge_all(lists, dropped):
    while len(lists) > 1:
        nxt = []
        for j in range(0, len(lists) - 1, 2):
            m, dropped = _merge_top(lists[j], lists[j + 1], dropped)
            nxt.append(m)
        if len(lists) % 2:
            nxt.append(lists[-1])
        lists = nxt
    return lists[0], dropped


def _settle_ties(v, x, dropped):
    x = list(x)
    for r in range(len(v) - 1):
        swap = (v[r] == v[r + 1]) & (x[r] > x[r + 1])
        x[r], x[r + 1] = jnp.where(swap, x[r + 1], x[r]), jnp.where(swap, x[r], x[r + 1])
    risk = v[-1] <= dropped
    for r in range(len(v) - 1):
        risk = risk | ((v[r] == v[r + 1]) & (x[r] > x[r + 1]))
    return x, risk.astype(jnp.int32)


def _level2_chains():
    pairs = [(a, b) for a in range(PEER_TOPK) for b in range(PEER_TOPK) if (a + 1) * (b + 1) <= PEER_TOPK]
    n_rows = 4
    chains = [[(a, b) for (a2, b) in pairs if a2 == a] for a in range(n_rows)]
    rest = [p for p in pairs if p[0] >= n_rows]
    for b in sorted({b for _, b in rest}):
        chains.append([(a, b2) for (a, b2) in rest if b2 == b])
    assert sorted(sum(chains, [])) == pairs
    return pairs, chains


def _route_kernel(pq_ref, keys_ref, a_ref, b_ref, g_ref, sx_ref, tv_ref, ti_ref, bs_ref, es_ref,
                  oa_ref, ob_ref, og_ref, risk_ref):
    n = PEER_N_KEYS
    k = PEER_TOPK
    shape = (SUBLANES, LANES)
    neg = jnp.full(shape, -jnp.inf, F32)
    pairs, chains = _level2_chains()

    def scores(hp):
        s = _dot_nt(keys_ref[hp], pq_ref[hp])
        for rg in range(n // SUBLANES):
            for lt in range(SUBLANES):
                sx_ref[pl.ds(rg * SUBLANES * SUBLANES + lt, SUBLANES, stride=SUBLANES), :] = (
                    s[rg * SUBLANES:(rg + 1) * SUBLANES, lt * LANES:(lt + 1) * LANES])
        return [sx_ref[pl.ds(key * SUBLANES, SUBLANES), :] for key in range(n)]

    def store_top(hp, v, x):
        for r in range(k):
            tv_ref[hp, r] = v[r]
            ti_ref[hp, r] = x[r]

    def level1(hp, carry):
        vals = scores(hp)
        lists = []
        for q in range(n // k):
            v = vals[q * k:(q + 1) * k]
            x = [jnp.full(shape, q * k + r, jnp.int32) for r in range(k)]
            _compare_exchange(v, x, SORT_NET)
            lists.append((v, x))
        (v, x), dropped = _merge_all(lists, neg)
        x, risk = _settle_ties(v, x, dropped)
        store_top(hp, v, x)
        risk_ref[hp] = risk
        return carry

    def level1_repair(hp, carry):
        @pl.when(jnp.max(risk_ref[hp]) > 0)
        def _():
            level1_exact(hp)

        return carry

    def level1_exact(hp):
        vals = scores(hp)
        idx = [jnp.full(shape, key, jnp.int32) for key in range(n)]
        v, x = [], []
        for r in range(k):
            best, where_ = _argmax_tree(vals, idx)
            v.append(best)
            x.append(where_)
            vals = [jnp.where(where_ == key, -jnp.inf, vals[key]) for key in range(n)]
        store_top(hp, v, x)

    def candidates(h):
        s1 = [tv_ref[2 * h, r] for r in range(k)]
        s2 = [tv_ref[2 * h + 1, r] for r in range(k)]
        e1 = [ti_ref[2 * h, r] * n for r in range(k)]
        e2 = [ti_ref[2 * h + 1, r] for r in range(k)]
        return (lambda a, b: s1[a] + s2[b]), (lambda a, b: e1[a] + e2[b] + (a * k + b) * PEER_N_EXPERTS)

    def store_best(h, v, x):
        for r in range(k):
            bs_ref[h, r] = v[r]
            es_ref[h, r] = x[r]

    def level2(h, carry):
        val, expert = candidates(h)
        lists = []
        for chain in chains:
            pad = k - len(chain)
            lists.append(([val(a, b) for a, b in chain] + [neg] * pad,
                          [expert(a, b) for a, b in chain] + [jnp.zeros(shape, jnp.int32)] * pad))
        (v, x), dropped = _merge_all(lists, neg)
        x, risk = _settle_ties(v, x, dropped)
        store_best(h, v, x)
        risk_ref[h] = risk
        return carry

    def level2_repair(h, carry):
        @pl.when(jnp.max(risk_ref[h]) > 0)
        def _():
            level2_exact(h)

        return carry

    def level2_exact(h):
        val, expert = candidates(h)
        cand = [val(a, b) for a, b in pairs]
        exps = [expert(a, b) for a, b in pairs]
        pos = [jnp.full(shape, a * k + b, jnp.int32) for a, b in pairs]
        v, x = [], []
        for r in range(k):
            best, where_ = _argmax_tree(cand, pos)
            chosen = jnp.zeros(shape, jnp.int32)
            for m, (a, b) in enumerate(pairs):
                hit = where_ == (a * k + b)
                cand[m] = jnp.where(hit, -jnp.inf, cand[m])
                chosen = jnp.where(hit, exps[m], chosen)
            v.append(best)
            x.append(chosen)
        store_best(h, v, x)

    lax.fori_loop(0, 2 * PEER_HEADS, level1, 0)
    lax.fori_loop(0, 2 * PEER_HEADS, level1_repair, 0)
    lax.fori_loop(0, PEER_HEADS, level2, 0)
    lax.fori_loop(0, PEER_HEADS, level2_repair, 0)

    for h in range(PEER_HEADS):
        best = [bs_ref[h, r] for r in range(k)]
        ex = [jnp.exp(v - best[0]) for v in best]
        denom = functools.reduce(lambda p, q: p + q, ex)
        for r in range(k):
            rows = pl.ds((h * k + r) * SUBLANES, SUBLANES)
            e = jnp.bitwise_and(es_ref[h, r], PEER_N_EXPERTS - 1)
            oa_ref[rows, :] = jnp.right_shift(e, n.bit_length() - 1)
            ob_ref[rows, :] = jnp.bitwise_and(e, n - 1)
            og_ref[rows, :] = ex[r] / denom
    for s in range(SUBLANES):
        rows = pl.ds(s * LANES, LANES)
        plane = pl.ds(s, PEER_HEADS * k, stride=SUBLANES)
        a_ref[rows, :] = oa_ref[plane, :].T
        b_ref[rows, :] = ob_ref[plane, :].T
        g_ref[rows, :] = og_ref[plane, :].T


def _route(pq, keys_bf):
    n_hp, T, _ = pq.shape
    n_sel = PEER_HEADS * PEER_TOPK
    row = lambda i: (i, 0)
    reg = (SUBLANES, LANES)
    return pl.pallas_call(
        _route_kernel,
        grid=(T // SORT_TILE,),
        in_specs=[
            pl.BlockSpec((n_hp, SORT_TILE, PEER_D_HALF), lambda i: (0, i, 0)),
            pl.BlockSpec((n_hp, PEER_N_KEYS, PEER_D_HALF), lambda i: (0, 0, 0)),
        ],
        out_specs=[pl.BlockSpec((SORT_TILE, n_sel), row)] * 3,
        out_shape=[
            jax.ShapeDtypeStruct((T, n_sel), jnp.int32),
            jax.ShapeDtypeStruct((T, n_sel), jnp.int32),
            jax.ShapeDtypeStruct((T, n_sel), F32),
        ],
        scratch_shapes=[
            pltpu.VMEM((PEER_N_KEYS * SUBLANES, LANES), F32),
            pltpu.VMEM((n_hp, PEER_TOPK) + reg, F32),
            pltpu.VMEM((n_hp, PEER_TOPK) + reg, jnp.int32),
            pltpu.VMEM((PEER_HEADS, PEER_TOPK) + reg, F32),
            pltpu.VMEM((PEER_HEADS, PEER_TOPK) + reg, jnp.int32),
            pltpu.VMEM((n_sel * SUBLANES, LANES), jnp.int32),
            pltpu.VMEM((n_sel * SUBLANES, LANES), jnp.int32),
            pltpu.VMEM((n_sel * SUBLANES, LANES), F32),
            pltpu.VMEM((n_hp,) + reg, jnp.int32),
        ],
        compiler_params=_cparams(("parallel",)),
        name="route",
    )(pq, keys_bf)


WBUILD_UNROLL = 32


def _wbuild_kernel(a_ref, b_ref, g_ref, w_ref, s_ref):
    n = PEER_N_KEYS
    tb = a_ref.shape[0]
    key = lax.broadcasted_iota(jnp.int32, (n, a_ref.shape[1]), 0)

    def body(t, carry):
        a = a_ref[pl.ds(t, 1), :]
        b = b_ref[pl.ds(t, 1), :]
        g = g_ref[pl.ds(t, 1), :]
        ga = jnp.where(a == key, g, 0.0).astype(BF16)
        ob = jnp.where(b == key, 1.0, 0.0).T.astype(BF16)
        s_ref[pl.ds(pl.multiple_of(t * W_PITCH, SUBLANES), n), :] = _dot(ga, ob)
        return carry

    lax.fori_loop(0, tb, body, 0, unroll=WBUILD_UNROLL)

    def emit(i1, carry):
        w_ref[i1] = s_ref[pl.ds(i1, tb, stride=W_PITCH), :].astype(BF16)
        return carry

    lax.fori_loop(0, n, emit, 0, unroll=WBUILD_UNROLL)


def _wbuild(a, b, g):
    T, n_sel = a.shape
    row = lambda i: (i, 0)
    return pl.pallas_call(
        _wbuild_kernel,
        grid=(T // W_BLOCK,),
        in_specs=[pl.BlockSpec((W_BLOCK, n_sel), row)] * 3,
        out_specs=pl.BlockSpec((None, PEER_N_KEYS, W_BLOCK, PEER_N_KEYS), lambda i: (i, 0, 0, 0)),
        out_shape=jax.ShapeDtypeStruct((T // W_BLOCK, PEER_N_KEYS, W_BLOCK, PEER_N_KEYS), BF16),
        scratch_shapes=[pltpu.VMEM((W_BLOCK * W_PITCH, PEER_N_KEYS), F32)],
        compiler_params=_cparams(("parallel",)),
        name="wbuild",
    )(a, b, g)


def _peer_dense_kernel(h2_ref, w_ref, u_ref, v_ref, x1_ref, g2_ref, o_ref, acc_ref):
    e = pl.program_id(1)

    @pl.when(e == 0)
    def _():
        acc_ref[...] = jnp.zeros_like(acc_ref)

    a = _dot_nt(h2_ref[...], u_ref[...])
    act = 0.5 * a * (1.0 + lax.erf(a * (2.0 ** -0.5)))
    w = jnp.concatenate(
        [jnp.concatenate([w_ref[blk, i] for i in range(w_ref.shape[1])], axis=1) for blk in range(w_ref.shape[0])],
        axis=0)
    acc_ref[...] += _dot((w.astype(F32) * act).astype(BF16), v_ref[...])

    @pl.when(e == pl.num_programs(1) - 1)
    def _():
        o_ref[...] = x1_ref[...] + g2_ref[...] * acc_ref[...]


def _peer_dense(h2, w, u_bf, v_bf, x1, g2, seq):
    T = h2.shape[0]
    tiles_per_batch = seq // PEER_TOK_TILE
    n_exp = u_bf.shape[0]
    return pl.pallas_call(
        _peer_dense_kernel,
        grid=(T // PEER_TOK_TILE, n_exp // PEER_EXP_TILE),
        in_specs=[
            pl.BlockSpec((PEER_TOK_TILE, D_MODEL), lambda i, e: (i, 0)),
            pl.BlockSpec((PEER_TOK_TILE // W_BLOCK, PEER_EXP_TILE // PEER_N_KEYS, W_BLOCK, PEER_N_KEYS),
                         lambda i, e: (i, e, 0, 0)),
            pl.BlockSpec((PEER_EXP_TILE, D_MODEL), lambda i, e: (e, 0)),
            pl.BlockSpec((PEER_EXP_TILE, D_MODEL), lambda i, e: (e, 0)),
            pl.BlockSpec((PEER_TOK_TILE, D_MODEL), lambda i, e: (i, 0)),
            pl.BlockSpec((None, 1, D_MODEL), lambda i, e: (i // tiles_per_batch, 0, 0)),
        ],
        out_specs=pl.BlockSpec((PEER_TOK_TILE, D_MODEL), lambda i, e: (i, 0)),
        out_shape=jax.ShapeDtypeStruct((T, D_MODEL), F32),
        scratch_shapes=[pltpu.VMEM((PEER_TOK_TILE, D_MODEL), F32)],
        compiler_params=_cparams(("parallel", "arbitrary")),
        name="peer_dense",
    )(h2, w, u_bf, v_bf, x1, g2)


HI = lax.Precision.HIGHEST


HY_LANES = 128
FFT_R = 128
FFT_N = FFT_R * FFT_R
FFT_PITCH = 136
FFT_UNROLL = 8
FFT_K1_USED = FFT_R // 2 + 1
FFT_K1_SLABS = 66
FFT_K1_ROWS = 72
FFT_K1_CHUNK = 22
assert FFT_K1_SLABS % FFT_K1_CHUNK == 0 and FFT_K1_CHUNK % 2 == 0 and FFT_K1_USED <= FFT_K1_SLABS <= FFT_K1_ROWS


def _fft_tables():
    r = np.arange(FFT_R)
    kk = np.arange(FFT_K1_ROWS)
    n2, k1, n1 = r[:, None, None], kk[None, :, None], r[None, None, :]
    ang = (2.0 * math.pi / FFT_N) * ((k1 * (FFT_R * n1 + n2)) % FFT_N)
    used = k1 < FFT_K1_USED
    c, s = np.cos(ang) * used, np.sin(ang) * used
    g_full = np.concatenate([c, -s], axis=1)
    twice = np.where((k1 == 0) | (k1 == FFT_R // 2), 1.0, 2.0) / FFT_N
    h_half = np.concatenate([np.swapaxes(c * twice, 1, 2), -np.swapaxes(s * twice, 1, 2)],
                            axis=2)[:, :FFT_R // 2]
    ang2 = (2.0 * math.pi / FFT_R) * ((r[:, None] * r[None, :]) % FFT_R)
    c2, s2 = np.cos(ang2), np.sin(ang2)
    f_fwd = np.block([[c2, s2], [-s2, c2]])
    f_inv = np.block([[c2, -s2], [s2, c2]])
    return tuple(jnp.asarray(t, dtype=F32).astype(BF16) for t in (g_full, h_half, f_fwd, f_inv))


def _fft_stage1(load_slab, g_ref, bre_ref, bim_ref):
    def body(n2, carry):
        out = _dot(g_ref[n2], load_slab(n2).astype(BF16))
        bre_ref[pl.ds(n2, FFT_K1_ROWS, stride=FFT_PITCH), :] = out[:FFT_K1_ROWS]
        bim_ref[pl.ds(n2, FFT_K1_ROWS, stride=FFT_PITCH), :] = out[FFT_K1_ROWS:]
        return carry

    lax.fori_loop(0, FFT_R, body, 0, unroll=FFT_UNROLL)


def _k1_rows(k1):
    return pl.ds(pl.multiple_of(k1 * FFT_PITCH, SUBLANES), FFT_R)


def _load_k1_pair(bre_ref, bim_ref, k1):
    return jnp.concatenate(
        [jnp.concatenate([bre_ref[_k1_rows(k1 + d), :], bim_ref[_k1_rows(k1 + d), :]], axis=0) for d in range(2)],
        axis=1)


FILT_ROWS = 512
FILT_FEAT = 128


def _filt_kernel(feat_ref, delta_ref, w1_ref, b1_ref, w2_ref, b2_ref, w3_ref, b3_ref, w4_ref, freq_ref,
                 h_ref, sum_ref):
    @pl.when(pl.program_id(0) == 0)
    def _():
        sum_ref[...] = jnp.zeros_like(sum_ref)

    feat = feat_ref[...]
    freq = freq_ref[...]
    dot = lambda a, b: jnp.dot(a, b, precision=HI, preferred_element_type=F32)
    h = jnp.sin(freq * (dot(feat, w1_ref[...]) + b1_ref[...]))
    h = jnp.sin(freq * (dot(h, w2_ref[...]) + b2_ref[...]))
    h = jnp.sin(freq * (dot(h, w3_ref[...]) + b3_ref[...]))
    hi = h.astype(BF16)
    lo = (h - hi.astype(F32)).astype(BF16)
    h = _dot(hi, w4_ref[...]) + _dot(lo, w4_ref[...])
    n_feat = 1 + 2 * HY_BANDS
    t = feat[:, 0:1]
    forward = feat[:, n_feat:n_feat + 1] > 0.5
    keep = feat[:, n_feat + 1:n_feat + 2]
    decay = jnp.exp(-t * delta_ref[...])
    outs = []
    for o in range(HY_ORDER):
        base = o * 2 * D_HY
        outs.append(jnp.where(forward, h[:, base:base + D_HY], h[:, base + D_HY:base + 2 * D_HY]) * decay)
    out = jnp.concatenate(outs, axis=1)
    sum_ref[...] += jnp.sum(jnp.abs(out), axis=0, keepdims=True)
    h_ref[...] = out * keep


def _filt_features(L):
    r = np.arange(FFT_N)
    n = FFT_R * (r % FFT_R) + r // FFT_R
    j = np.where(n < L, n, np.where(n > L, 2 * L - n, 0))
    jj = j.astype(np.float64)[:, None]
    t = jj / (L - 1.0)
    w = (2.0 * math.pi / L) * jj
    bands = np.linspace(1e-4, HY_BANDS - 1.0, HY_BANDS)[None, :]
    flags = np.stack([n < L, n != L], axis=1).astype(np.float64)
    feats = np.concatenate([t, np.cos(bands * w), -np.sin(bands * w), flags], axis=-1)
    return jnp.asarray(np.pad(feats, ((0, 0), (0, FILT_FEAT - feats.shape[1]))), dtype=F32)


def _filters(p, L):
    feats = _filt_features(L)
    deltas = jnp.abs(jnp.linspace(math.log(HY_DECAY_TARGET) / HY_SLOW_PCT,
                                  math.log(HY_DECAY_TARGET) / HY_FAST_PCT, D_HY, dtype=F32))[None, :]
    w1 = jnp.pad(p['hy_f1_w'], ((0, FILT_FEAT - p['hy_f1_w'].shape[0]), (0, 0)))
    n_out = HY_ORDER * D_HY
    const = lambda i: (0, 0)
    full = lambda a: pl.BlockSpec(a.shape, const)
    args = [deltas, w1, p['hy_f1_b'][None], p['hy_f2_w'], p['hy_f2_b'][None], p['hy_f3_w'], p['hy_f3_b'][None],
            p['hy_f4_w'].astype(BF16), p['hy_sin_freq'][None]]
    return pl.pallas_call(
        _filt_kernel,
        grid=(FFT_N // FILT_ROWS,),
        in_specs=[pl.BlockSpec((FILT_ROWS, FILT_FEAT), lambda i: (i, 0))] + [full(a) for a in args],
        out_specs=[pl.BlockSpec((FILT_ROWS, n_out), lambda i: (i, 0)), pl.BlockSpec((1, n_out), const)],
        out_shape=[jax.ShapeDtypeStruct((FFT_N, n_out), F32), jax.ShapeDtypeStruct((1, n_out), F32)],
        compiler_params=_cparams(("arbitrary",)),
        name="hyena_filter",
    )(feats, *args)


def _fspec_kernel(h_ref, sum_ref, g_ref, f_ref, o_ref, bre_ref, bim_ref):
    c = pl.program_id(1)

    @pl.when(c == 0)
    def _():
        _fft_stage1(lambda n2: h_ref[pl.ds(pl.multiple_of(n2 * FFT_R, FFT_R), FFT_R), :], g_ref, bre_ref, bim_ref)

    inv = 1.0 / sum_ref[...]

    lanes = h_ref.shape[1]

    def slab_pair(j, carry):
        x = _dot(f_ref[...], _load_k1_pair(bre_ref, bim_ref, c * FFT_K1_CHUNK + 2 * j).astype(BF16))
        for d in range(2):
            rows = pl.ds(pl.multiple_of((2 * j + d) * FFT_R, FFT_R), FFT_R)
            o_ref[0, rows, :] = x[:FFT_R, d * lanes:(d + 1) * lanes] * inv
            o_ref[1, rows, :] = x[FFT_R:, d * lanes:(d + 1) * lanes] * inv
        return carry

    lax.fori_loop(0, FFT_K1_CHUNK // 2, slab_pair, 0, unroll=True)


def _filter_spectrum(h, hsum, g_full, f_fwd):
    n_ch = h.shape[1]
    single = pl.Buffered(1)
    return pl.pallas_call(
        _fspec_kernel,
        grid=(n_ch // HY_LANES, FFT_K1_SLABS // FFT_K1_CHUNK),
        in_specs=[
            pl.BlockSpec((FFT_N, HY_LANES), lambda j, c: (0, j), pipeline_mode=single),
            pl.BlockSpec((1, HY_LANES), lambda j, c: (0, j)),
            pl.BlockSpec(g_full.shape, lambda j, c: (0, 0, 0), pipeline_mode=single),
            pl.BlockSpec(f_fwd.shape, lambda j, c: (0, 0)),
        ],
        out_specs=pl.BlockSpec((2, FFT_K1_CHUNK * FFT_R, HY_LANES), lambda j, c: (0, c, j)),
        out_shape=jax.ShapeDtypeStruct((2, FFT_K1_SLABS * FFT_R, n_ch), F32),
        scratch_shapes=[pltpu.VMEM((FFT_K1_ROWS * FFT_PITCH, HY_LANES), F32)] * 2,
        compiler_params=_cparams(("parallel", "arbitrary")),
        name="hyena_filter_fft",
    )(h, hsum, g_full, f_fwd)


def _hyconv_kernel(u_ref, gate_ref, uw_ref, ub_ref, gw_ref, gb_ref, kf_ref, skip_ref, g_ref, h_ref, f_ref, fi_ref,
                   o_ref, bre_ref, bim_ref, *, conv_u):
    c = pl.program_id(2)
    half = FFT_R // 2
    last = FFT_R - 1

    def rows_of(n2):
        return pl.ds(pl.multiple_of(n2 * half, half), half)

    row = lax.broadcasted_iota(jnp.int32, (half, u_ref.shape[1]), 0)

    def short_conv(ref, w_ref, b_ref):
        wrap_prev = jnp.where(row == 0, 0.0, pltpu.roll(ref[rows_of(last), :], 1, 0))
        wrap_next = jnp.where(row == half - 1, 0.0, pltpu.roll(ref[rows_of(0), :], half - 1, 0))

        def slab(n2):
            prev = jnp.where(n2 == 0, wrap_prev, ref[rows_of(jnp.maximum(n2 - 1, 0)), :])
            nxt = jnp.where(n2 == last, wrap_next, ref[rows_of(jnp.minimum(n2 + 1, last)), :])
            return w_ref[0:1] * prev + w_ref[1:2] * ref[rows_of(n2), :] + w_ref[2:3] * nxt + b_ref[...]

        return slab

    u_slab = short_conv(u_ref, uw_ref, ub_ref) if conv_u else (lambda n2: u_ref[rows_of(n2), :])
    gate_slab = short_conv(gate_ref, gw_ref, gb_ref)

    @pl.when(c == 0)
    def _():
        _fft_stage1(u_slab, g_ref, bre_ref, bim_ref)

    lanes = u_ref.shape[1]

    def slab_pair(j, carry):
        k1 = c * FFT_K1_CHUNK + 2 * j
        x = _dot(f_ref[...], _load_k1_pair(bre_ref, bim_ref, k1).astype(BF16))
        xr, xi = x[:FFT_R], x[FFT_R:]
        kr, ki = [jnp.concatenate([kf_ref[part, pl.ds(pl.multiple_of((2 * j + d) * FFT_R, FFT_R), FFT_R), :]
                                   for d in range(2)], axis=1) for part in range(2)]
        y = jnp.concatenate([xr * kr - xi * ki, xr * ki + xi * kr], axis=0)
        z = _dot(fi_ref[...], y.astype(BF16))
        for d in range(2):
            bre_ref[_k1_rows(k1 + d), :] = z[:FFT_R, d * lanes:(d + 1) * lanes]
            bim_ref[_k1_rows(k1 + d), :] = z[FFT_R:, d * lanes:(d + 1) * lanes]
        return carry

    lax.fori_loop(0, FFT_K1_CHUNK // 2, slab_pair, 0, unroll=True)

    @pl.when(c == pl.num_programs(2) - 1)
    def _():
        def body(n2, carry):
            z = jnp.concatenate([bre_ref[pl.ds(n2, FFT_K1_ROWS, stride=FFT_PITCH), :],
                                 bim_ref[pl.ds(n2, FFT_K1_ROWS, stride=FFT_PITCH), :]], axis=0)
            conv = _dot(h_ref[n2], z.astype(BF16))
            o_ref[rows_of(n2), :] = gate_slab(n2) * (conv + skip_ref[...] * u_slab(n2))
            return carry

        lax.fori_loop(0, FFT_R, body, 0, unroll=FFT_UNROLL)


def _hyconv(u, u_col, gate, gate_col, conv_w, conv_b, conv_u, kf, kf_col, skip, tables):
    g_half, h_half, f_fwd, f_inv = tables
    B, L, _ = u.shape
    single = pl.Buffered(1)
    uw_col = u_col if conv_u else gate_col
    return pl.pallas_call(
        functools.partial(_hyconv_kernel, conv_u=conv_u),
        grid=(B, D_HY // HY_LANES, FFT_K1_SLABS // FFT_K1_CHUNK),
        in_specs=[
            pl.BlockSpec((None, L, HY_LANES), lambda b, j, c: (b, 0, u_col + j), pipeline_mode=single),
            pl.BlockSpec((None, L, HY_LANES), lambda b, j, c: (b, 0, gate_col + j), pipeline_mode=single),
            pl.BlockSpec((3, HY_LANES), lambda b, j, c: (0, uw_col + j)),
            pl.BlockSpec((1, HY_LANES), lambda b, j, c: (0, uw_col + j)),
            pl.BlockSpec((3, HY_LANES), lambda b, j, c: (0, gate_col + j)),
            pl.BlockSpec((1, HY_LANES), lambda b, j, c: (0, gate_col + j)),
            pl.BlockSpec((2, FFT_K1_CHUNK * FFT_R, HY_LANES), lambda b, j, c: (0, c, kf_col + j)),
            pl.BlockSpec((1, HY_LANES), lambda b, j, c: (0, j)),
            pl.BlockSpec(g_half.shape, lambda b, j, c: (0, 0, 0), pipeline_mode=single),
            pl.BlockSpec(h_half.shape, lambda b, j, c: (0, 0, 0), pipeline_mode=single),
            pl.BlockSpec(f_fwd.shape, lambda b, j, c: (0, 0)),
            pl.BlockSpec(f_inv.shape, lambda b, j, c: (0, 0)),
        ],
        out_specs=pl.BlockSpec((None, L, HY_LANES), lambda b, j, c: (b, 0, j)),
        out_shape=jax.ShapeDtypeStruct((B, L, D_HY), F32),
        scratch_shapes=[pltpu.VMEM((FFT_K1_ROWS * FFT_PITCH, HY_LANES), F32)] * 2,
        compiler_params=_cparams(("parallel", "parallel", "arbitrary")),
        name="hyena_conv",
    )(u, gate, conv_w, conv_b, conv_w, conv_b, kf, skip, g_half, h_half, f_fwd, f_inv)


def _slab_major(a, n1):
    B, L, C = a.shape
    return a.reshape(B, n1, L // n1, C).transpose(0, 2, 1, 3).reshape(B, L, C)


def _hyena(zh, p):
    B, L, _ = zh.shape
    assert 2 * L == FFT_N
    g_full, h_half, f_fwd, f_inv = _fft_tables()
    tables = (g_full[:, :, :FFT_R // 2], h_half, f_fwd, f_inv)
    filt, filt_sum = _filters(p, L)
    both_dirs = filt_sum[:, :]
    kf = _filter_spectrum(filt, both_dirs, g_full, f_fwd)
    zs = _slab_major(zh, FFT_R // 2)
    cw, cb = p['hy_conv_w'], p['hy_conv_b'][None]
    lanes = D_HY // HY_LANES
    y = _hyconv(zs, 0, zs, lanes, cw, cb, True, kf, 0, p['hy_skip'][0:1], tables)
    y = _hyconv(y, 0, zs, 2 * lanes, cw, cb, False, kf, lanes, p['hy_skip'][1:2], tables)
    return _slab_major(y, L // (FFT_R // 2))


def kernel(x, c, ctx, c_ctx, norm1_g, norm2_g, w_ada, b_ada, w_in, b_in, hy_conv_w, hy_conv_b, hy_f1_w, hy_f1_b, hy_f2_w, hy_f2_b, hy_f3_w, hy_f3_b, hy_f4_w, hy_sin_freq, hy_skip, q_norm_g, k_norm_g, na_rpb, w_hy_out, w_na_out, w_out, peer_w_q, peer_keys, peer_u, peer_v):
    assert w_in.shape[0] == 1, "single-layer block"
    B, S, D = x.shape
    C = ctx.shape[1]
    T = B * S

    m_lat = (jax.nn.silu(c) @ w_ada[0] + b_ada[0]).reshape(B, N_MOD, 1, D)
    m_ctx = (jax.nn.silu(c_ctx) @ w_ada[0] + b_ada[0]).reshape(N_MOD, 1, D)
    sh1, sc1, g1, sh2, sc2, g2 = [m_lat[:, i] for i in range(N_MOD)]

    w_in_bf = w_in[0].astype(BF16)
    b_in2 = b_in[0][None, :]
    gain1 = norm1_g[0][None, :]
    qg = jnp.tile(q_norm_g[0], NA_HEADS)[None, :]
    kg = jnp.tile(k_norm_g[0], NA_HEADS)[None, :]
    seg = jnp.asarray(np.kron(np.eye(NA_HEADS), np.full((HEAD_DIM, HEAD_DIM), 1.0 / HEAD_DIM)), dtype=BF16)
    cos, sin = _rope_tables(S)

    x2 = x.reshape(T, D)
    zh, q, k, v, gate = _inproj(x2, sh1, sc1, gain1, w_in_bf, b_in2, cos, sin, qg, kg, seg, S)

    k_ctx, v_ctx = _ctxkv(ctx.reshape(B * C, D), m_ctx[0], m_ctx[1], gain1,
                          w_in_bf[:, COL_K:COL_V], b_in2[:, COL_K:COL_V],
                          w_in_bf[:, COL_V:COL_G_HY], b_in2[:, COL_V:COL_G_HY], kg, seg)

    p = {'hy_conv_w': hy_conv_w[0], 'hy_conv_b': hy_conv_b[0], 'hy_f1_w': hy_f1_w[0], 'hy_f1_b': hy_f1_b[0],
         'hy_f2_w': hy_f2_w[0], 'hy_f2_b': hy_f2_b[0], 'hy_f3_w': hy_f3_w[0], 'hy_f3_b': hy_f3_b[0],
         'hy_f4_w': hy_f4_w[0], 'hy_sin_freq': hy_sin_freq[0], 'hy_skip': hy_skip[0]}
    y_hy = _hyena(zh.reshape(B, S, 3 * D_HY), p).reshape(T, D_HY)

    y_na = _natten(q.reshape(B, S, D_NA), k.reshape(B, S, D_NA), v.reshape(B, S, D_NA),
                   k_ctx.reshape(B, C, D_NA), v_ctx.reshape(B, C, D_NA), _natten_bias(na_rpb[0], S // GRID_W)).reshape(T, D_NA)

    x1, h2, pq = _postmix(y_hy, y_na, gate, x2, g1, sh2, sc2, norm2_g[0][None, :],
                          w_hy_out[0].astype(BF16), w_na_out[0].astype(BF16), w_out[0].astype(BF16),
                          peer_w_q[0].astype(BF16), S)

    keys_bf = peer_keys[0].astype(BF16).reshape(2 * PEER_HEADS, PEER_N_KEYS, PEER_D_HALF)
    sel_a, sel_b, sel_g = _route(pq, keys_bf)
    w = _wbuild(sel_a, sel_b, sel_g)
    out = _peer_dense(h2, w, peer_u[0].astype(BF16), peer_v[0].astype(BF16), x1, g2, S)
    return out.reshape(B, S, D)
```

```python
import functools
import math

import numpy as np
import jax
import jax.numpy as jnp
from jax import lax
from jax.experimental import pallas as pl
from jax.experimental.pallas import tpu as pltpu

F32 = jnp.float32
BF16 = jnp.bfloat16

D_MODEL = 1024
GRID_W = 64
EPS = 1e-6
N_MOD = 6

D_HY = 512
HY_ORDER = 2
HY_BANDS = 16
HY_DECAY_TARGET = 1e-2
HY_FAST_PCT = 0.3
HY_SLOW_PCT = 1.5

NA_HEADS = 8
HEAD_DIM = 64
D_NA = NA_HEADS * HEAD_DIM
WIN_ROWS = 8
WIN_COLS = 16
ROPE_THETA = 10000.0
NEG_INF = -1e30

PEER_HEADS = 8
PEER_N_KEYS = 128
PEER_N_EXPERTS = PEER_N_KEYS * PEER_N_KEYS
PEER_TOPK = 16
PEER_D_KEY = 256
PEER_D_HALF = PEER_D_KEY // 2

COL_HY = 0
COL_Q = COL_HY + 3 * D_HY
COL_K = COL_Q + D_NA
COL_V = COL_K + D_NA
COL_G_HY = COL_V + D_NA
N_PROJ = COL_G_HY + 2 * D_MODEL

VMEM_LIMIT = 56 * 1024 * 1024
TOK_TILE = 512
PEER_TOK_TILE = 512
PEER_EXP_TILE = 2048


def _cparams(sem):
    return pltpu.CompilerParams(dimension_semantics=sem, vmem_limit_bytes=VMEM_LIMIT)


def _dot(a, b):
    return jnp.dot(a, b, preferred_element_type=F32)


def _dot_nt(a, b):
    return lax.dot_general(a, b, (((1,), (1,)), ((), ())), preferred_element_type=F32)


def _modulated_norm(x, gain, shift, scale):
    ms = jnp.mean(x * x, axis=-1, keepdims=True)
    return (x * lax.rsqrt(ms + EPS) * gain) * (1.0 + scale) + shift


def _head_rms(z, seg, gain):
    ms = _dot((z * z).astype(BF16), seg)
    return z * lax.rsqrt(ms + EPS) * gain


def _rope(z, cos, sin_signed):
    n = z.shape[-1]
    lane = lax.broadcasted_iota(jnp.int32, z.shape, 1)
    first = (lane // (HEAD_DIM // 4)) % 2 == 0
    partner = jnp.where(first, pltpu.roll(z, n - HEAD_DIM // 4, 1), pltpu.roll(z, HEAD_DIM // 4, 1))
    return z * cos + partner * sin_signed


def _inproj_kernel(x_ref, shift_ref, scale_ref, g_ref, w_ref, b_ref, cos_ref, sin_ref, qg_ref, kg_ref,
                   seg_ref, zh_ref, q_ref, k_ref, v_ref, gate_ref):
    h = _modulated_norm(x_ref[...], g_ref[...], shift_ref[...], scale_ref[...]).astype(BF16)

    def proj(lo, hi):
        return _dot(h, w_ref[:, lo:hi]) + b_ref[:, lo:hi]

    zh_ref[...] = proj(COL_HY, COL_Q)
    reps = D_NA // cos_ref.shape[1]
    cos = jnp.concatenate([cos_ref[...]] * reps, axis=1)
    sin = jnp.concatenate([sin_ref[...]] * reps, axis=1)
    seg = seg_ref[...]
    q = _rope(_head_rms(proj(COL_Q, COL_K), seg, qg_ref[...]), cos, sin)
    q_ref[...] = q.astype(BF16)
    k = _rope(_head_rms(proj(COL_K, COL_V), seg, kg_ref[...]), cos, sin)
    k_ref[...] = k.astype(BF16)
    v_ref[...] = proj(COL_V, COL_G_HY).astype(BF16)
    gate_ref[...] = jax.nn.sigmoid(proj(COL_G_HY, N_PROJ)).astype(BF16)


def _inproj(x2, shift, scale, gain, w_bf, b_in, cos, sin, qg, kg, seg, seq):
    T = x2.shape[0]
    tiles_per_batch = seq // TOK_TILE
    row = lambda i: (i, 0)
    per_batch = lambda i: (i // tiles_per_batch, 0, 0)
    const = lambda i: (0, 0)
    pos = lambda i: (i % tiles_per_batch, 0)
    return pl.pallas_call(
        _inproj_kernel,
        grid=(T // TOK_TILE,),
        in_specs=[
            pl.BlockSpec((TOK_TILE, D_MODEL), row),
            pl.BlockSpec((None, 1, D_MODEL), per_batch),
            pl.BlockSpec((None, 1, D_MODEL), per_batch),
            pl.BlockSpec((1, D_MODEL), const),
            pl.BlockSpec((D_MODEL, N_PROJ), const),
            pl.BlockSpec((1, N_PROJ), const),
            pl.BlockSpec((TOK_TILE, cos.shape[1]), pos),
            pl.BlockSpec((TOK_TILE, sin.shape[1]), pos),
            pl.BlockSpec((1, D_NA), const),
            pl.BlockSpec((1, D_NA), const),
            pl.BlockSpec((D_NA, D_NA), const),
        ],
        out_specs=[
            pl.BlockSpec((TOK_TILE, 3 * D_HY), row),
            pl.BlockSpec((TOK_TILE, D_NA), row),
            pl.BlockSpec((TOK_TILE, D_NA), row),
            pl.BlockSpec((TOK_TILE, D_NA), row),
            pl.BlockSpec((TOK_TILE, 2 * D_MODEL), row),
        ],
        out_shape=[
            jax.ShapeDtypeStruct((T, 3 * D_HY), F32),
            jax.ShapeDtypeStruct((T, D_NA), BF16),
            jax.ShapeDtypeStruct((T, D_NA), BF16),
            jax.ShapeDtypeStruct((T, D_NA), BF16),
            jax.ShapeDtypeStruct((T, 2 * D_MODEL), BF16),
        ],
        compiler_params=_cparams(("parallel",)),
        name="inproj",
    )(x2, shift, scale, gain, w_bf, b_in, cos, sin, qg, kg, seg)


def _ctxkv_kernel(x_ref, shift_ref, scale_ref, g_ref, wk_ref, bk_ref, wv_ref, bv_ref, kg_ref, seg_ref,
                  k_ref, v_ref):
    h = _modulated_norm(x_ref[...], g_ref[...], shift_ref[...], scale_ref[...]).astype(BF16)
    k = _head_rms(_dot(h, wk_ref[...]) + bk_ref[...], seg_ref[...], kg_ref[...])
    k_ref[...] = k.astype(BF16)
    v_ref[...] = (_dot(h, wv_ref[...]) + bv_ref[...]).astype(BF16)


def _ctxkv(ctx2, shift, scale, gain, wk, bk, wv, bv, kg, seg):
    T = ctx2.shape[0]
    row = lambda i: (i, 0)
    const = lambda i: (0, 0)
    return pl.pallas_call(
        _ctxkv_kernel,
        grid=(T // TOK_TILE,),
        in_specs=[
            pl.BlockSpec((TOK_TILE, D_MODEL), row),
            pl.BlockSpec((1, D_MODEL), const),
            pl.BlockSpec((1, D_MODEL), const),
            pl.BlockSpec((1, D_MODEL), const),
            pl.BlockSpec((D_MODEL, D_NA), const),
            pl.BlockSpec((1, D_NA), const),
            pl.BlockSpec((D_MODEL, D_NA), const),
            pl.BlockSpec((1, D_NA), const),
            pl.BlockSpec((1, D_NA), const),
            pl.BlockSpec((D_NA, D_NA), const),
        ],
        out_specs=[pl.BlockSpec((TOK_TILE, D_NA), row), pl.BlockSpec((TOK_TILE, D_NA), row)],
        out_shape=[jax.ShapeDtypeStruct((T, D_NA), BF16), jax.ShapeDtypeStruct((T, D_NA), BF16)],
        compiler_params=_cparams(("parallel",)),
        name="ctxkv",
    )(ctx2, shift, scale, gain, wk, bk, wv, bv, kg, seg)


NA_BLOCK_ROWS = 4
NA_KEY_ROWS = WIN_ROWS + NA_BLOCK_ROWS
NA_CLASSES = 3


def _natten_kernel(q_ref, k_ref, v_ref, kc_ref, vc_ref, bias_ref, o_ref, *, rows):
    r0 = pl.program_id(1) * NA_BLOCK_ROWS
    key_start = jnp.clip(r0 - WIN_ROWS // 2, 0, rows - NA_KEY_ROWS)
    start = pl.multiple_of(key_start * GRID_W, GRID_W * NA_BLOCK_ROWS)
    n_q = NA_BLOCK_ROWS * GRID_W
    n_loc = NA_KEY_ROWS * GRID_W
    scale = jnp.asarray(HEAD_DIM ** -0.5, BF16)
    pair = 2 * HEAD_DIM
    lane = lax.broadcasted_iota(jnp.int32, (n_q, pair), 1)
    for p in range(NA_HEADS // 2):
        cols = slice(p * pair, (p + 1) * pair)
        qp = q_ref[:, cols] * scale
        kp = k_ref[pl.ds(start, n_loc), cols]
        vp = v_ref[pl.ds(start, n_loc), cols]
        kcp = kc_ref[:, cols]
        vcp = vc_ref[:, cols]
        outs = []
        for hh in range(2):
            qm = jnp.where((lane // HEAD_DIM) == hh, qp, jnp.zeros_like(qp))
            s_loc = _dot_nt(qm, kp) + bias_ref[2 * p + hh]
            s_ctx = _dot_nt(qm, kcp)
            m = jnp.maximum(jnp.max(s_loc, axis=-1, keepdims=True), jnp.max(s_ctx, axis=-1, keepdims=True))
            e_loc = jnp.exp(s_loc - m)
            e_ctx = jnp.exp(s_ctx - m)
            denom = jnp.sum(e_loc, axis=-1, keepdims=True) + jnp.sum(e_ctx, axis=-1, keepdims=True)
            o = _dot(e_loc.astype(BF16), vp) + _dot(e_ctx.astype(BF16), vcp)
            outs.append(o / denom)
        o_ref[:, cols] = jnp.where((lane // HEAD_DIM) == 0, outs[0], outs[1]).astype(o_ref.dtype)


def _natten(q, k, v, kc, vc, bias):
    B, S, _ = q.shape
    rows = S // GRID_W
    n_blk = rows // NA_BLOCK_ROWS
    C = kc.shape[1]
    n_q = NA_BLOCK_ROWS * GRID_W
    block_class = lambda b, i: (jnp.where(i == 0, 0, jnp.where(i == n_blk - 1, 2, 1)), 0, 0, 0)
    return pl.pallas_call(
        functools.partial(_natten_kernel, rows=rows),
        grid=(B, n_blk),
        in_specs=[
            pl.BlockSpec((None, n_q, D_NA), lambda b, i: (b, i, 0)),
            pl.BlockSpec((None, S, D_NA), lambda b, i: (b, 0, 0)),
            pl.BlockSpec((None, S, D_NA), lambda b, i: (b, 0, 0)),
            pl.BlockSpec((None, C, D_NA), lambda b, i: (b, 0, 0)),
            pl.BlockSpec((None, C, D_NA), lambda b, i: (b, 0, 0)),
            pl.BlockSpec((None,) + bias.shape[1:], block_class),
        ],
        out_specs=pl.BlockSpec((None, n_q, D_NA), lambda b, i: (b, i, 0)),
        out_shape=jax.ShapeDtypeStruct((B, S, D_NA), BF16),
        compiler_params=_cparams(("parallel", "arbitrary")),
        name="natten",
    )(q, k, v, kc, vc, bias)


def _natten_bias(rpb, rows):
    col = np.arange(GRID_W)
    dc = np.clip(col[None, :] - col[:, None], -(WIN_COLS - 1), WIN_COLS - 1) + (WIN_COLS - 1)
    col_start = np.clip(col - WIN_COLS // 2, 0, GRID_W - WIN_COLS)
    col_mask = (col[None, :] >= col_start[:, None]) & (col[None, :] < col_start[:, None] + WIN_COLS)
    r0 = np.array([0, 2 * NA_BLOCK_ROWS, rows - NA_BLOCK_ROWS])
    r = r0[:, None] + np.arange(NA_BLOCK_ROWS)[None, :]
    key_row = np.clip(r0 - WIN_ROWS // 2, 0, rows - NA_KEY_ROWS)[:, None, None] + np.arange(NA_KEY_ROWS)
    win_start = np.clip(r - WIN_ROWS // 2, 0, rows - WIN_ROWS)[:, :, None]
    row_mask = (key_row >= win_start) & (key_row < win_start + WIN_ROWS)
    dr = key_row - r[:, :, None] + (WIN_ROWS - 1)
    pick_c = jnp.asarray(dc[None] == np.arange(2 * WIN_COLS - 1)[:, None, None], dtype=F32)
    toep = jnp.einsum('hde,eqx->hdqx', rpb.astype(F32), pick_c, precision=HI)
    toep = jnp.where(col_mask[None, None], toep, NEG_INF)
    toep = jnp.pad(toep, ((0, 0), (1, 1), (0, 0), (0, 0)), constant_values=NEG_INF)
    pairs = jnp.concatenate([toep[:, :-1], toep[:, 1:]], axis=-1)

    def kernel(t_ref, o_ref):
        neg = jnp.full((GRID_W, 2 * GRID_W), NEG_INF, F32)
        lane_left = lax.broadcasted_iota(jnp.int32, (GRID_W, 2 * GRID_W), 1) < GRID_W
        for c in range(NA_CLASSES):
            for rq in range(NA_BLOCK_ROWS):
                for kp in range(NA_KEY_ROWS // 2):
                    left, right = bool(row_mask[c, rq, 2 * kp]), bool(row_mask[c, rq, 2 * kp + 1])
                    block = neg
                    if left or right:
                        block = t_ref[int(dr[c, rq, 2 * kp]) + 1]
                        if not (left and right):
                            block = jnp.where(lane_left == left, block, NEG_INF)
                    o_ref[c, rq * GRID_W:(rq + 1) * GRID_W, kp * 2 * GRID_W:(kp + 1) * 2 * GRID_W] = block

    n_q, n_k = NA_BLOCK_ROWS * GRID_W, NA_KEY_ROWS * GRID_W
    return pl.pallas_call(
        kernel,
        grid=(NA_HEADS,),
        in_specs=[pl.BlockSpec((None,) + pairs.shape[1:], lambda h: (h, 0, 0, 0))],
        out_specs=pl.BlockSpec((NA_CLASSES, None, n_q, n_k), lambda h: (0, h, 0, 0)),
        out_shape=jax.ShapeDtypeStruct((NA_CLASSES, NA_HEADS, n_q, n_k), F32),
        compiler_params=_cparams(("parallel",)),
        name="natten_bias",
    )(pairs)


def _rope_tables(seq):
    pos = np.arange(seq)
    rows = (pos // GRID_W).astype(np.float64)
    cols = (pos % GRID_W).astype(np.float64)
    nf = HEAD_DIM // 4
    inv = ROPE_THETA ** (-np.arange(nf, dtype=np.float64) / nf)
    ar = rows[:, None] * inv[None, :]
    ac = cols[:, None] * inv[None, :]
    cos = np.concatenate([np.cos(ar), np.cos(ar), np.cos(ac), np.cos(ac)], axis=-1)
    sin = np.concatenate([-np.sin(ar), np.sin(ar), -np.sin(ac), np.sin(ac)], axis=-1)
    return (jnp.asarray(np.tile(cos, (1, 2)), dtype=F32), jnp.asarray(np.tile(sin, (1, 2)), dtype=F32))


def _postmix_kernel(yh_ref, yn_ref, gate_ref, x_ref, g1_ref, sh2_ref, sc2_ref, n2_ref, whyo_ref, wnao_ref,
                    wout_ref, wq_ref, x1_ref, h2_ref, pq_ref):
    a = _dot(yh_ref[...].astype(BF16), whyo_ref[...])
    b = _dot(yn_ref[...], wnao_ref[...])
    gate = gate_ref[...].astype(F32)
    merged = gate[:, :D_MODEL] * a + gate[:, D_MODEL:] * b
    x1 = x_ref[...] + g1_ref[...] * _dot(merged.astype(BF16), wout_ref[...])
    x1_ref[...] = x1
    h2 = _modulated_norm(x1, n2_ref[...], sh2_ref[...], sc2_ref[...]).astype(BF16)
    h2_ref[...] = h2
    for hp in range(2 * PEER_HEADS):
        pq_ref[hp] = _dot(h2, wq_ref[:, hp * PEER_D_HALF:(hp + 1) * PEER_D_HALF]).astype(BF16)


def _postmix(yh, yn, gate, x2, g1, sh2, sc2, n2g, whyo, wnao, wout, wq, seq):
    T = x2.shape[0]
    tiles_per_batch = seq // TOK_TILE
    row = lambda i: (i, 0)
    per_batch = lambda i: (i // tiles_per_batch, 0, 0)
    const = lambda i: (0, 0)
    nq = wq.shape[1]
    return pl.pallas_call(
        _postmix_kernel,
        grid=(T // TOK_TILE,),
        in_specs=[
            pl.BlockSpec((TOK_TILE, D_HY), row),
            pl.BlockSpec((TOK_TILE, D_NA), row),
            pl.BlockSpec((TOK_TILE, 2 * D_MODEL), row),
            pl.BlockSpec((TOK_TILE, D_MODEL), row),
            pl.BlockSpec((None, 1, D_MODEL), per_batch),
            pl.BlockSpec((None, 1, D_MODEL), per_batch),
            pl.BlockSpec((None, 1, D_MODEL), per_batch),
            pl.BlockSpec((1, D_MODEL), const),
            pl.BlockSpec((D_HY, D_MODEL), const),
            pl.BlockSpec((D_NA, D_MODEL), const),
            pl.BlockSpec((D_MODEL, D_MODEL), const),
            pl.BlockSpec((D_MODEL, nq), const),
        ],
        out_specs=[
            pl.BlockSpec((TOK_TILE, D_MODEL), row),
            pl.BlockSpec((TOK_TILE, D_MODEL), row),
            pl.BlockSpec((nq // PEER_D_HALF, TOK_TILE, PEER_D_HALF), lambda i: (0, i, 0)),
        ],
        out_shape=[
            jax.ShapeDtypeStruct((T, D_MODEL), F32),
            jax.ShapeDtypeStruct((T, D_MODEL), BF16),
            jax.ShapeDtypeStruct((nq // PEER_D_HALF, T, PEER_D_HALF), BF16),
        ],
        compiler_params=_cparams(("parallel",)),
        name="postmix",
    )(yh, yn, gate, x2, g1, sh2, sc2, n2g, whyo, wnao, wout, wq)


SUBLANES = 8
LANES = 128


def _argmax_tree(vals, idxs):
    while len(vals) > 1:
        nv, ni = [], []
        for j in range(0, len(vals) - 1, 2):
            right = vals[j + 1] > vals[j]
            nv.append(jnp.where(right, vals[j + 1], vals[j]))
            ni.append(jnp.where(right, idxs[j + 1], idxs[j]))
        if len(vals) % 2:
            nv.append(vals[-1])
            ni.append(idxs[-1])
        vals, idxs = nv, ni
    return vals[0], idxs[0]


W_BLOCK = 128
W_PITCH = 136


SORT_TILE = SUBLANES * LANES


def _batcher_sort_network(n):
    ces = []

    def merge(lo, m, r):
        step = r * 2
        if step < m:
            merge(lo, m, step)
            merge(lo + r, m, step)
            ces.extend((i, i + r) for i in range(lo + r, lo + m - r, step))
        else:
            ces.append((lo, lo + r))

    def sort(lo, m):
        if m > 1:
            sort(lo, m // 2)
            sort(lo + m // 2, m // 2)
            merge(lo, m, 1)

    sort(0, n)
    return ces


def _bitonic_merge_network(n):
    ces, s = [], n // 2
    while s >= 1:
        ces.extend((i, i + s) for i in range(n) if (i & s) == 0)
        s //= 2
    return ces


SORT_NET = _batcher_sort_network(PEER_TOPK)
MERGE_NET = _bitonic_merge_network(PEER_TOPK)


def _compare_exchange(v, x, net):
    for i, j in net:
        c = v[j] > v[i]
        v[i], v[j] = jnp.where(c, v[j], v[i]), jnp.where(c, v[i], v[j])
        x[i], x[j] = jnp.where(c, x[j], x[i]), jnp.where(c, x[i], x[j])


def _merge_top(a, b, dropped):
    (va, xa), (vb, xb) = a, b
    k = len(va)
    v, x = [], []
    for r in range(k):
        c = vb[k - 1 - r] > va[r]
        v.append(jnp.where(c, vb[k - 1 - r], va[r]))
        x.append(jnp.where(c, xb[k - 1 - r], xa[r]))
        dropped = jnp.maximum(dropped, jnp.where(c, va[r], vb[k - 1 - r]))
    _compare_exchange(v, x, MERGE_NET)
    return (v, x), dropped


def _merge_all(lists, dropped):
    while len(lists) > 1:
        nxt = []
        for j in range(0, len(lists) - 1, 2):
            m, dropped = _merge_top(lists[j], lists[j + 1], dropped)
            nxt.append(m)
        if len(lists) % 2:
            nxt.append(lists[-1])
        lists = nxt
    return lists[0], dropped


def _settle_ties(v, x, dropped):
    x = list(x)
    for r in range(len(v) - 1):
        swap = (v[r] == v[r + 1]) & (x[r] > x[r + 1])
        x[r], x[r + 1] = jnp.where(swap, x[r + 1], x[r]), jnp.where(swap, x[r], x[r + 1])
    risk = v[-1] <= dropped
    for r in range(len(v) - 1):
        risk = risk | ((v[r] == v[r + 1]) & (x[r] > x[r + 1]))
    return x, risk.astype(jnp.int32)


def _level2_chains():
    pairs = [(a, b) for a in range(PEER_TOPK) for b in range(PEER_TOPK) if (a + 1) * (b + 1) <= PEER_TOPK]
    n_rows = 4
    chains = [[(a, b) for (a2, b) in pairs if a2 == a] for a in range(n_rows)]
    rest = [p for p in pairs if p[0] >= n_rows]
    for b in sorted({b for _, b in rest}):
        chains.append([(a, b2) for (a, b2) in rest if b2 == b])
    assert sorted(sum(chains, [])) == pairs
    return pairs, chains


def _route_kernel(pq_ref, keys_ref, a_ref, b_ref, g_ref, sx_ref, tv_ref, ti_ref, bs_ref, es_ref,
                  oa_ref, ob_ref, og_ref, risk_ref):
    n = PEER_N_KEYS
    k = PEER_TOPK
    shape = (SUBLANES, LANES)
    neg = jnp.full(shape, -jnp.inf, F32)
    pairs, chains = _level2_chains()

    def scores(hp):
        s = _dot_nt(keys_ref[hp], pq_ref[hp])
        for rg in range(n // SUBLANES):
            for lt in range(SUBLANES):
                sx_ref[pl.ds(rg * SUBLANES * SUBLANES + lt, SUBLANES, stride=SUBLANES), :] = (
                    s[rg * SUBLANES:(rg + 1) * SUBLANES, lt * LANES:(lt + 1) * LANES])
        return [sx_ref[pl.ds(key * SUBLANES, SUBLANES), :] for key in range(n)]

    def store_top(hp, v, x):
        for r in range(k):
            tv_ref[hp, r] = v[r]
            ti_ref[hp, r] = x[r]

    def level1(hp, carry):
        vals = scores(hp)
        lists = []
        for q in range(n // k):
            v = vals[q * k:(q + 1) * k]
            x = [jnp.full(shape, q * k + r, jnp.int32) for r in range(k)]
            _compare_exchange(v, x, SORT_NET)
            lists.append((v, x))
        (v, x), dropped = _merge_all(lists, neg)
        x, risk = _settle_ties(v, x, dropped)
        store_top(hp, v, x)
        risk_ref[hp] = risk
        return carry

    def level1_repair(hp, carry):
        @pl.when(jnp.max(risk_ref[hp]) > 0)
        def _():
            level1_exact(hp)

        return carry

    def level1_exact(hp):
        vals = scores(hp)
        idx = [jnp.full(shape, key, jnp.int32) for key in range(n)]
        v, x = [], []
        for r in range(k):
            best, where_ = _argmax_tree(vals, idx)
            v.append(best)
            x.append(where_)
            vals = [jnp.where(where_ == key, -jnp.inf, vals[key]) for key in range(n)]
        store_top(hp, v, x)

    def candidates(h):
        s1 = [tv_ref[2 * h, r] for r in range(k)]
        s2 = [tv_ref[2 * h + 1, r] for r in range(k)]
        e1 = [ti_ref[2 * h, r] * n for r in range(k)]
        e2 = [ti_ref[2 * h + 1, r] for r in range(k)]
        return (lambda a, b: s1[a] + s2[b]), (lambda a, b: e1[a] + e2[b] + (a * k + b) * PEER_N_EXPERTS)

    def store_best(h, v, x):
        for r in range(k):
            bs_ref[h, r] = v[r]
            es_ref[h, r] = x[r]

    def level2(h, carry):
        val, expert = candidates(h)
        lists = []
        for chain in chains:
            pad = k - len(chain)
            lists.append(([val(a, b) for a, b in chain] + [neg] * pad,
                          [expert(a, b) for a, b in chain] + [jnp.zeros(shape, jnp.int32)] * pad))
        (v, x), dropped = _merge_all(lists, neg)
        x, risk = _settle_ties(v, x, dropped)
        store_best(h, v, x)
        risk_ref[h] = risk
        return carry

    def level2_repair(h, carry):
        @pl.when(jnp.max(risk_ref[h]) > 0)
        def _():
            level2_exact(h)

        return carry

    def level2_exact(h):
        val, expert = candidates(h)
        cand = [val(a, b) for a, b in pairs]
        exps = [expert(a, b) for a, b in pairs]
        pos = [jnp.full(shape, a * k + b, jnp.int32) for a, b in pairs]
        v, x = [], []
        for r in range(k):
            best, where_ = _argmax_tree(cand, pos)
            chosen = jnp.zeros(shape, jnp.int32)
            for m, (a, b) in enumerate(pairs):
                hit = where_ == (a * k + b)
                cand[m] = jnp.where(hit, -jnp.inf, cand[m])
                chosen = jnp.where(hit, exps[m], chosen)
            v.append(best)
            x.append(chosen)
        store_best(h, v, x)

    lax.fori_loop(0, 2 * PEER_HEADS, level1, 0, unroll=2)
    lax.fori_loop(0, 2 * PEER_HEADS, level1_repair, 0)
    lax.fori_loop(0, PEER_HEADS, level2, 0)
    lax.fori_loop(0, PEER_HEADS, level2_repair, 0)

    for h in range(PEER_HEADS):
        best = [bs_ref[h, r] for r in range(k)]
        ex = [jnp.exp(v - best[0]) for v in best]
        denom = functools.reduce(lambda p, q: p + q, ex)
        for r in range(k):
            rows = pl.ds((h * k + r) * SUBLANES, SUBLANES)
            e = jnp.bitwise_and(es_ref[h, r], PEER_N_EXPERTS - 1)
            oa_ref[rows, :] = jnp.right_shift(e, n.bit_length() - 1)
            ob_ref[rows, :] = jnp.bitwise_and(e, n - 1)
            og_ref[rows, :] = ex[r] / denom
    for s in range(SUBLANES):
        rows = pl.ds(s * LANES, LANES)
        plane = pl.ds(s, PEER_HEADS * k, stride=SUBLANES)
        a_ref[rows, :] = oa_ref[plane, :].T
        b_ref[rows, :] = ob_ref[plane, :].T
        g_ref[rows, :] = og_ref[plane, :].T


def _route(pq, keys_bf):
    n_hp, T, _ = pq.shape
    n_sel = PEER_HEADS * PEER_TOPK
    row = lambda i: (i, 0)
    reg = (SUBLANES, LANES)
    return pl.pallas_call(
        _route_kernel,
        grid=(T // SORT_TILE,),
        in_specs=[
            pl.BlockSpec((n_hp, SORT_TILE, PEER_D_HALF), lambda i: (0, i, 0)),
            pl.BlockSpec((n_hp, PEER_N_KEYS, PEER_D_HALF), lambda i: (0, 0, 0)),
        ],
        out_specs=[pl.BlockSpec((SORT_TILE, n_sel), row)] * 3,
        out_shape=[
            jax.ShapeDtypeStruct((T, n_sel), jnp.int32),
            jax.ShapeDtypeStruct((T, n_sel), jnp.int32),
            jax.ShapeDtypeStruct((T, n_sel), F32),
        ],
        scratch_shapes=[
            pltpu.VMEM((PEER_N_KEYS * SUBLANES, LANES), F32),
            pltpu.VMEM((n_hp, PEER_TOPK) + reg, F32),
            pltpu.VMEM((n_hp, PEER_TOPK) + reg, jnp.int32),
            pltpu.VMEM((PEER_HEADS, PEER_TOPK) + reg, F32),
            pltpu.VMEM((PEER_HEADS, PEER_TOPK) + reg, jnp.int32),
            pltpu.VMEM((n_sel * SUBLANES, LANES), jnp.int32),
            pltpu.VMEM((n_sel * SUBLANES, LANES), jnp.int32),
            pltpu.VMEM((n_sel * SUBLANES, LANES), F32),
            pltpu.VMEM((n_hp,) + reg, jnp.int32),
        ],
        compiler_params=_cparams(("parallel",)),
        name="route",
    )(pq, keys_bf)


WBUILD_UNROLL = 32


def _wbuild_kernel(a_ref, b_ref, g_ref, w_ref, s_ref):
    n = PEER_N_KEYS
    tb = a_ref.shape[0]
    key = lax.broadcasted_iota(jnp.int32, (n, a_ref.shape[1]), 0)

    def body(t, carry):
        a = a_ref[pl.ds(t, 1), :]
        b = b_ref[pl.ds(t, 1), :]
        g = g_ref[pl.ds(t, 1), :]
        ga = jnp.where(a == key, g, 0.0).astype(BF16)
        ob = jnp.where(b == key, 1.0, 0.0).T.astype(BF16)
        s_ref[pl.ds(pl.multiple_of(t * W_PITCH, SUBLANES), n), :] = _dot(ga, ob)
        return carry

    lax.fori_loop(0, tb, body, 0, unroll=WBUILD_UNROLL)

    def emit(i1, carry):
        w_ref[i1] = s_ref[pl.ds(i1, tb, stride=W_PITCH), :].astype(BF16)
        return carry

    lax.fori_loop(0, n, emit, 0, unroll=WBUILD_UNROLL)


def _wbuild(a, b, g):
    T, n_sel = a.shape
    row = lambda i: (i, 0)
    return pl.pallas_call(
        _wbuild_kernel,
        grid=(T // W_BLOCK,),
        in_specs=[pl.BlockSpec((W_BLOCK, n_sel), row)] * 3,
        out_specs=pl.BlockSpec((None, PEER_N_KEYS, W_BLOCK, PEER_N_KEYS), lambda i: (i, 0, 0, 0)),
        out_shape=jax.ShapeDtypeStruct((T // W_BLOCK, PEER_N_KEYS, W_BLOCK, PEER_N_KEYS), BF16),
        scratch_shapes=[pltpu.VMEM((W_BLOCK * W_PITCH, PEER_N_KEYS), F32)],
        compiler_params=_cparams(("parallel",)),
        name="wbuild",
    )(a, b, g)


def _peer_dense_kernel(h2_ref, w_ref, u_ref, v_ref, x1_ref, g2_ref, o_ref, acc_ref):
    e = pl.program_id(1)

    @pl.when(e == 0)
    def _():
        acc_ref[...] = jnp.zeros_like(acc_ref)

    a = _dot_nt(h2_ref[...], u_ref[...])
    act = 0.5 * a * (1.0 + lax.erf(a * (2.0 ** -0.5)))
    w = jnp.concatenate(
        [jnp.concatenate([w_ref[blk, i] for i in range(w_ref.shape[1])], axis=1) for blk in range(w_ref.shape[0])],
        axis=0)
    acc_ref[...] += _dot((w.astype(F32) * act).astype(BF16), v_ref[...])

    @pl.when(e == pl.num_programs(1) - 1)
    def _():
        o_ref[...] = x1_ref[...] + g2_ref[...] * acc_ref[...]


def _peer_dense(h2, w, u_bf, v_bf, x1, g2, seq):
    T = h2.shape[0]
    tiles_per_batch = seq // PEER_TOK_TILE
    n_exp = u_bf.shape[0]
    return pl.pallas_call(
        _peer_dense_kernel,
        grid=(T // PEER_TOK_TILE, n_exp // PEER_EXP_TILE),
        in_specs=[
            pl.BlockSpec((PEER_TOK_TILE, D_MODEL), lambda i, e: (i, 0)),
            pl.BlockSpec((PEER_TOK_TILE // W_BLOCK, PEER_EXP_TILE // PEER_N_KEYS, W_BLOCK, PEER_N_KEYS),
                         lambda i, e: (i, e, 0, 0)),
            pl.BlockSpec((PEER_EXP_TILE, D_MODEL), lambda i, e: (e, 0)),
            pl.BlockSpec((PEER_EXP_TILE, D_MODEL), lambda i, e: (e, 0)),
            pl.BlockSpec((PEER_TOK_TILE, D_MODEL), lambda i, e: (i, 0)),
            pl.BlockSpec((None, 1, D_MODEL), lambda i, e: (i // tiles_per_batch, 0, 0)),
        ],
        out_specs=pl.BlockSpec((PEER_TOK_TILE, D_MODEL), lambda i, e: (i, 0)),
        out_shape=jax.ShapeDtypeStruct((T, D_MODEL), F32),
        scratch_shapes=[pltpu.VMEM((PEER_TOK_TILE, D_MODEL), F32)],
        compiler_params=_cparams(("parallel", "arbitrary")),
        name="peer_dense",
    )(h2, w, u_bf, v_bf, x1, g2)


HI = lax.Precision.HIGHEST


HY_LANES = 128
FFT_R = 128
FFT_N = FFT_R * FFT_R
FFT_PITCH = 136
FFT_UNROLL = 8
FFT_K1_USED = FFT_R // 2 + 1
FFT_K1_SLABS = 66
FFT_K1_ROWS = 72
FFT_K1_CHUNK = 22
assert FFT_K1_SLABS % FFT_K1_CHUNK == 0 and FFT_K1_CHUNK % 2 == 0 and FFT_K1_USED <= FFT_K1_SLABS <= FFT_K1_ROWS


def _fft_tables():
    r = np.arange(FFT_R)
    kk = np.arange(FFT_K1_ROWS)
    n2, k1, n1 = r[:, None, None], kk[None, :, None], r[None, None, :]
    ang = (2.0 * math.pi / FFT_N) * ((k1 * (FFT_R * n1 + n2)) % FFT_N)
    used = k1 < FFT_K1_USED
    c, s = np.cos(ang) * used, np.sin(ang) * used
    g_full = np.concatenate([c, -s], axis=1)
    twice = np.where((k1 == 0) | (k1 == FFT_R // 2), 1.0, 2.0) / FFT_N
    h_half = np.concatenate([np.swapaxes(c * twice, 1, 2), -np.swapaxes(s * twice, 1, 2)],
                            axis=2)[:, :FFT_R // 2]
    ang2 = (2.0 * math.pi / FFT_R) * ((r[:, None] * r[None, :]) % FFT_R)
    c2, s2 = np.cos(ang2), np.sin(ang2)
    f_fwd = np.block([[c2, s2], [-s2, c2]])
    f_inv = np.block([[c2, -s2], [s2, c2]])
    return tuple(jnp.asarray(t, dtype=F32).astype(BF16) for t in (g_full, h_half, f_fwd, f_inv))


def _fft_stage1(load_slab, g_ref, bre_ref, bim_ref):
    def body(n2, carry):
        out = _dot(g_ref[n2], load_slab(n2).astype(BF16))
        bre_ref[pl.ds(n2, FFT_K1_ROWS, stride=FFT_PITCH), :] = out[:FFT_K1_ROWS]
        bim_ref[pl.ds(n2, FFT_K1_ROWS, stride=FFT_PITCH), :] = out[FFT_K1_ROWS:]
        return carry

    lax.fori_loop(0, FFT_R, body, 0, unroll=FFT_UNROLL)


def _k1_rows(k1):
    return pl.ds(pl.multiple_of(k1 * FFT_PITCH, SUBLANES), FFT_R)


def _load_k1_pair(bre_ref, bim_ref, k1):
    return jnp.concatenate(
        [jnp.concatenate([bre_ref[_k1_rows(k1 + d), :], bim_ref[_k1_rows(k1 + d), :]], axis=0) for d in range(2)],
        axis=1)


FILT_ROWS = 512
FILT_FEAT = 128


def _filt_kernel(feat_ref, delta_ref, w1_ref, b1_ref, w2_ref, b2_ref, w3_ref, b3_ref, w4_ref, freq_ref,
                 h_ref, sum_ref):
    @pl.when(pl.program_id(0) == 0)
    def _():
        sum_ref[...] = jnp.zeros_like(sum_ref)

    feat = feat_ref[...]
    freq = freq_ref[...]
    dot = lambda a, b: jnp.dot(a, b, precision=HI, preferred_element_type=F32)
    h = jnp.sin(freq * (dot(feat, w1_ref[...]) + b1_ref[...]))
    h = jnp.sin(freq * (dot(h, w2_ref[...]) + b2_ref[...]))
    h = jnp.sin(freq * (dot(h, w3_ref[...]) + b3_ref[...]))
    hi = h.astype(BF16)
    lo = (h - hi.astype(F32)).astype(BF16)
    h = _dot(hi, w4_ref[...]) + _dot(lo, w4_ref[...])
    n_feat = 1 + 2 * HY_BANDS
    t = feat[:, 0:1]
    forward = feat[:, n_feat:n_feat + 1] > 0.5
    keep = feat[:, n_feat + 1:n_feat + 2]
    decay = jnp.exp(-t * delta_ref[...])
    outs = []
    for o in range(HY_ORDER):
        base = o * 2 * D_HY
        outs.append(jnp.where(forward, h[:, base:base + D_HY], h[:, base + D_HY:base + 2 * D_HY]) * decay)
    out = jnp.concatenate(outs, axis=1)
    sum_ref[...] += jnp.sum(jnp.abs(out), axis=0, keepdims=True)
    h_ref[...] = out * keep


def _filt_features(L):
    r = np.arange(FFT_N)
    n = FFT_R * (r % FFT_R) + r // FFT_R
    j = np.where(n < L, n, np.where(n > L, 2 * L - n, 0))
    jj = j.astype(np.float64)[:, None]
    t = jj / (L - 1.0)
    w = (2.0 * math.pi / L) * jj
    bands = np.linspace(1e-4, HY_BANDS - 1.0, HY_BANDS)[None, :]
    flags = np.stack([n < L, n != L], axis=1).astype(np.float64)
    feats = np.concatenate([t, np.cos(bands * w), -np.sin(bands * w), flags], axis=-1)
    return jnp.asarray(np.pad(feats, ((0, 0), (0, FILT_FEAT - feats.shape[1]))), dtype=F32)


def _filters(p, L):
    feats = _filt_features(L)
    deltas = jnp.abs(jnp.linspace(math.log(HY_DECAY_TARGET) / HY_SLOW_PCT,
                                  math.log(HY_DECAY_TARGET) / HY_FAST_PCT, D_HY, dtype=F32))[None, :]
    w1 = jnp.pad(p['hy_f1_w'], ((0, FILT_FEAT - p['hy_f1_w'].shape[0]), (0, 0)))
    n_out = HY_ORDER * D_HY
    const = lambda i: (0, 0)
    full = lambda a: pl.BlockSpec(a.shape, const)
    args = [deltas, w1, p['hy_f1_b'][None], p['hy_f2_w'], p['hy_f2_b'][None], p['hy_f3_w'], p['hy_f3_b'][None],
            p['hy_f4_w'].astype(BF16), p['hy_sin_freq'][None]]
    return pl.pallas_call(
        _filt_kernel,
        grid=(FFT_N // FILT_ROWS,),
        in_specs=[pl.BlockSpec((FILT_ROWS, FILT_FEAT), lambda i: (i, 0))] + [full(a) for a in args],
        out_specs=[pl.BlockSpec((FILT_ROWS, n_out), lambda i: (i, 0)), pl.BlockSpec((1, n_out), const)],
        out_shape=[jax.ShapeDtypeStruct((FFT_N, n_out), F32), jax.ShapeDtypeStruct((1, n_out), F32)],
        compiler_params=_cparams(("arbitrary",)),
        name="hyena_filter",
    )(feats, *args)


def _fspec_kernel(h_ref, sum_ref, g_ref, f_ref, o_ref, bre_ref, bim_ref):
    c = pl.program_id(1)

    @pl.when(c == 0)
    def _():
        _fft_stage1(lambda n2: h_ref[pl.ds(pl.multiple_of(n2 * FFT_R, FFT_R), FFT_R), :], g_ref, bre_ref, bim_ref)

    inv = 1.0 / sum_ref[...]

    lanes = h_ref.shape[1]

    def slab_pair(j, carry):
        x = _dot(f_ref[...], _load_k1_pair(bre_ref, bim_ref, c * FFT_K1_CHUNK + 2 * j).astype(BF16))
        for d in range(2):
            rows = pl.ds(pl.multiple_of((2 * j + d) * FFT_R, FFT_R), FFT_R)
            o_ref[0, rows, :] = x[:FFT_R, d * lanes:(d + 1) * lanes] * inv
            o_ref[1, rows, :] = x[FFT_R:, d * lanes:(d + 1) * lanes] * inv
        return carry

    lax.fori_loop(0, FFT_K1_CHUNK // 2, slab_pair, 0, unroll=True)


def _filter_spectrum(h, hsum, g_full, f_fwd):
    n_ch = h.shape[1]
    single = pl.Buffered(1)
    return pl.pallas_call(
        _fspec_kernel,
        grid=(n_ch // HY_LANES, FFT_K1_SLABS // FFT_K1_CHUNK),
        in_specs=[
            pl.BlockSpec((FFT_N, HY_LANES), lambda j, c: (0, j)),
            pl.BlockSpec((1, HY_LANES), lambda j, c: (0, j)),
            pl.BlockSpec(g_full.shape, lambda j, c: (0, 0, 0), pipeline_mode=single),
            pl.BlockSpec(f_fwd.shape, lambda j, c: (0, 0)),
        ],
        out_specs=pl.BlockSpec((2, FFT_K1_CHUNK * FFT_R, HY_LANES), lambda j, c: (0, c, j)),
        out_shape=jax.ShapeDtypeStruct((2, FFT_K1_SLABS * FFT_R, n_ch), F32),
        scratch_shapes=[pltpu.VMEM((FFT_K1_ROWS * FFT_PITCH, HY_LANES), F32)] * 2,
        compiler_params=_cparams(("parallel", "arbitrary")),
        name="hyena_filter_fft",
    )(h, hsum, g_full, f_fwd)


def _hyconv_kernel(u_ref, gate_ref, uw_ref, ub_ref, gw_ref, gb_ref, kf_ref, skip_ref, g_ref, h_ref, f_ref, fi_ref,
                   o_ref, bre_ref, bim_ref, *, conv_u):
    c = pl.program_id(2)
    half = FFT_R // 2
    last = FFT_R - 1

    def rows_of(n2):
        return pl.ds(pl.multiple_of(n2 * half, half), half)

    row = lax.broadcasted_iota(jnp.int32, (half, u_ref.shape[1]), 0)

    def short_conv(ref, w_ref, b_ref):
        wrap_prev = jnp.where(row == 0, 0.0, pltpu.roll(ref[rows_of(last), :], 1, 0))
        wrap_next = jnp.where(row == half - 1, 0.0, pltpu.roll(ref[rows_of(0), :], half - 1, 0))

        def slab(n2):
            prev = jnp.where(n2 == 0, wrap_prev, ref[rows_of(jnp.maximum(n2 - 1, 0)), :])
            nxt = jnp.where(n2 == last, wrap_next, ref[rows_of(jnp.minimum(n2 + 1, last)), :])
            return w_ref[0:1] * prev + w_ref[1:2] * ref[rows_of(n2), :] + w_ref[2:3] * nxt + b_ref[...]

        return slab

    u_slab = short_conv(u_ref, uw_ref, ub_ref) if conv_u else (lambda n2: u_ref[rows_of(n2), :])
    gate_slab = short_conv(gate_ref, gw_ref, gb_ref)

    @pl.when(c == 0)
    def _():
        _fft_stage1(u_slab, g_ref, bre_ref, bim_ref)

    lanes = u_ref.shape[1]

    def slab_pair(j, carry):
        k1 = c * FFT_K1_CHUNK + 2 * j
        x = _dot(f_ref[...], _load_k1_pair(bre_ref, bim_ref, k1).astype(BF16))
        xr, xi = x[:FFT_R], x[FFT_R:]
        kr, ki = [jnp.concatenate([kf_ref[part, pl.ds(pl.multiple_of((2 * j + d) * FFT_R, FFT_R), FFT_R), :]
                                   for d in range(2)], axis=1) for part in range(2)]
        y = jnp.concatenate([xr * kr - xi * ki, xr * ki + xi * kr], axis=0)
        z = _dot(fi_ref[...], y.astype(BF16))
        for d in range(2):
            bre_ref[_k1_rows(k1 + d), :] = z[:FFT_R, d * lanes:(d + 1) * lanes]
            bim_ref[_k1_rows(k1 + d), :] = z[FFT_R:, d * lanes:(d + 1) * lanes]
        return carry

    lax.fori_loop(0, FFT_K1_CHUNK // 2, slab_pair, 0, unroll=True)

    @pl.when(c == pl.num_programs(2) - 1)
    def _():
        def body(n2, carry):
            z = jnp.concatenate([bre_ref[pl.ds(n2, FFT_K1_ROWS, stride=FFT_PITCH), :],
                                 bim_ref[pl.ds(n2, FFT_K1_ROWS, stride=FFT_PITCH), :]], axis=0)
            conv = _dot(h_ref[n2], z.astype(BF16))
            o_ref[rows_of(n2), :] = gate_slab(n2) * (conv + skip_ref[...] * u_slab(n2))
            return carry

        lax.fori_loop(0, FFT_R, body, 0, unroll=FFT_UNROLL)


def _hyconv(u, u_col, gate, gate_col, conv_w, conv_b, conv_u, kf, kf_col, skip, tables):
    g_half, h_half, f_fwd, f_inv = tables
    B, L, _ = u.shape
    single = pl.Buffered(1)
    uw_col = u_col if conv_u else gate_col
    return pl.pallas_call(
        functools.partial(_hyconv_kernel, conv_u=conv_u),
        grid=(B, D_HY // HY_LANES, FFT_K1_SLABS // FFT_K1_CHUNK),
        in_specs=[
            pl.BlockSpec((None, L, HY_LANES), lambda b, j, c: (b, 0, u_col + j)),
            pl.BlockSpec((None, L, HY_LANES), lambda b, j, c: (b, 0, gate_col + j)),
            pl.BlockSpec((3, HY_LANES), lambda b, j, c: (0, uw_col + j)),
            pl.BlockSpec((1, HY_LANES), lambda b, j, c: (0, uw_col + j)),
            pl.BlockSpec((3, HY_LANES), lambda b, j, c: (0, gate_col + j)),
            pl.BlockSpec((1, HY_LANES), lambda b, j, c: (0, gate_col + j)),
            pl.BlockSpec((2, FFT_K1_CHUNK * FFT_R, HY_LANES), lambda b, j, c: (0, c, kf_col + j)),
            pl.BlockSpec((1, HY_LANES), lambda b, j, c: (0, j)),
            pl.BlockSpec(g_half.shape, lambda b, j, c: (0, 0, 0), pipeline_mode=single),
            pl.BlockSpec(h_half.shape, lambda b, j, c: (0, 0, 0), pipeline_mode=single),
            pl.BlockSpec(f_fwd.shape, lambda b, j, c: (0, 0)),
            pl.BlockSpec(f_inv.shape, lambda b, j, c: (0, 0)),
        ],
        out_specs=pl.BlockSpec((None, L, HY_LANES), lambda b, j, c: (b, 0, j)),
        out_shape=jax.ShapeDtypeStruct((B, L, D_HY), F32),
        scratch_shapes=[pltpu.VMEM((FFT_K1_ROWS * FFT_PITCH, HY_LANES), F32)] * 2,
        compiler_params=_cparams(("parallel", "parallel", "arbitrary")),
        name="hyena_conv",
    )(u, gate, conv_w, conv_b, conv_w, conv_b, kf, skip, g_half, h_half, f_fwd, f_inv)


def _slab_major(a, n1):
    B, L, C = a.shape
    return a.reshape(B, n1, L // n1, C).transpose(0, 2, 1, 3).reshape(B, L, C)


def _hyena(zh, p):
    B, L, _ = zh.shape
    assert 2 * L == FFT_N
    g_full, h_half, f_fwd, f_inv = _fft_tables()
    tables = (g_full[:, :, :FFT_R // 2], h_half, f_fwd, f_inv)
    filt, filt_sum = _filters(p, L)
    both_dirs = filt_sum[:, :]
    kf = _filter_spectrum(filt, both_dirs, g_full, f_fwd)
    zs = _slab_major(zh, FFT_R // 2)
    cw, cb = p['hy_conv_w'], p['hy_conv_b'][None]
    lanes = D_HY // HY_LANES
    y = _hyconv(zs, 0, zs, lanes, cw, cb, True, kf, 0, p['hy_skip'][0:1], tables)
    y = _hyconv(y, 0, zs, 2 * lanes, cw, cb, False, kf, lanes, p['hy_skip'][1:2], tables)
    return _slab_major(y, L // (FFT_R // 2))


def kernel(x, c, ctx, c_ctx, norm1_g, norm2_g, w_ada, b_ada, w_in, b_in, hy_conv_w, hy_conv_b, hy_f1_w, hy_f1_b, hy_f2_w, hy_f2_b, hy_f3_w, hy_f3_b, hy_f4_w, hy_sin_freq, hy_skip, q_norm_g, k_norm_g, na_rpb, w_hy_out, w_na_out, w_out, peer_w_q, peer_keys, peer_u, peer_v):
    assert w_in.shape[0] == 1, "single-layer block"
    B, S, D = x.shape
    C = ctx.shape[1]
    T = B * S

    m_lat = (jax.nn.silu(c) @ w_ada[0] + b_ada[0]).reshape(B, N_MOD, 1, D)
    m_ctx = (jax.nn.silu(c_ctx) @ w_ada[0] + b_ada[0]).reshape(N_MOD, 1, D)
    sh1, sc1, g1, sh2, sc2, g2 = [m_lat[:, i] for i in range(N_MOD)]

    w_in_bf = w_in[0].astype(BF16)
    b_in2 = b_in[0][None, :]
    gain1 = norm1_g[0][None, :]
    qg = jnp.tile(q_norm_g[0], NA_HEADS)[None, :]
    kg = jnp.tile(k_norm_g[0], NA_HEADS)[None, :]
    seg = jnp.asarray(np.kron(np.eye(NA_HEADS), np.full((HEAD_DIM, HEAD_DIM), 1.0 / HEAD_DIM)), dtype=BF16)
    cos, sin = _rope_tables(S)

    x2 = x.reshape(T, D)
    zh, q, k, v, gate = _inproj(x2, sh1, sc1, gain1, w_in_bf, b_in2, cos, sin, qg, kg, seg, S)

    k_ctx, v_ctx = _ctxkv(ctx.reshape(B * C, D), m_ctx[0], m_ctx[1], gain1,
                          w_in_bf[:, COL_K:COL_V], b_in2[:, COL_K:COL_V],
                          w_in_bf[:, COL_V:COL_G_HY], b_in2[:, COL_V:COL_G_HY], kg, seg)

    p = {'hy_conv_w': hy_conv_w[0], 'hy_conv_b': hy_conv_b[0], 'hy_f1_w': hy_f1_w[0], 'hy_f1_b': hy_f1_b[0],
         'hy_f2_w': hy_f2_w[0], 'hy_f2_b': hy_f2_b[0], 'hy_f3_w': hy_f3_w[0], 'hy_f3_b': hy_f3_b[0],
         'hy_f4_w': hy_f4_w[0], 'hy_sin_freq': hy_sin_freq[0], 'hy_skip': hy_skip[0]}
    y_hy = _hyena(zh.reshape(B, S, 3 * D_HY), p).reshape(T, D_HY)

    y_na = _natten(q.reshape(B, S, D_NA), k.reshape(B, S, D_NA), v.reshape(B, S, D_NA),
                   k_ctx.reshape(B, C, D_NA), v_ctx.reshape(B, C, D_NA), _natten_bias(na_rpb[0], S // GRID_W)).reshape(T, D_NA)

    x1, h2, pq = _postmix(y_hy, y_na, gate, x2, g1, sh2, sc2, norm2_g[0][None, :],
                          w_hy_out[0].astype(BF16), w_na_out[0].astype(BF16), w_out[0].astype(BF16),
                          peer_w_q[0].astype(BF16), S)

    keys_bf = peer_keys[0].astype(BF16).reshape(2 * PEER_HEADS, PEER_N_KEYS, PEER_D_HALF)
    sel_a, sel_b, sel_g = _route(pq, keys_bf)
    w = _wbuild(sel_a, sel_b, sel_g)
    out = _peer_dense(h2, w, peer_u[0].astype(BF16), peer_v[0].astype(BF16), x1, g2, S)
    return out.reshape(B, S, D)
```

```python
import functools
import math

import numpy as np
import jax
import jax.numpy as jnp
from jax import lax
from jax.experimental import pallas as pl
from jax.experimental.pallas import tpu as pltpu

F32 = jnp.float32
BF16 = jnp.bfloat16

D_MODEL = 1024
GRID_W = 64
EPS = 1e-6
N_MOD = 6

D_HY = 512
HY_ORDER = 2
HY_BANDS = 16
HY_DECAY_TARGET = 1e-2
HY_FAST_PCT = 0.3
HY_SLOW_PCT = 1.5

NA_HEADS = 8
HEAD_DIM = 64
D_NA = NA_HEADS * HEAD_DIM
WIN_ROWS = 8
WIN_COLS = 16
ROPE_THETA = 10000.0
NEG_INF = -1e30

PEER_HEADS = 8
PEER_N_KEYS = 128
PEER_N_EXPERTS = PEER_N_KEYS * PEER_N_KEYS
PEER_TOPK = 16
PEER_D_KEY = 256
PEER_D_HALF = PEER_D_KEY // 2

COL_HY = 0
COL_Q = COL_HY + 3 * D_HY
COL_K = COL_Q + D_NA
COL_V = COL_K + D_NA
COL_G_HY = COL_V + D_NA
N_PROJ = COL_G_HY + 2 * D_MODEL

VMEM_LIMIT = 56 * 1024 * 1024
TOK_TILE = 512
PEER_TOK_TILE = 512
PEER_EXP_TILE = 2048


def _cparams(sem):
    return pltpu.CompilerParams(dimension_semantics=sem, vmem_limit_bytes=VMEM_LIMIT)


def _dot(a, b):
    return jnp.dot(a, b, preferred_element_type=F32)


def _dot_nt(a, b):
    return lax.dot_general(a, b, (((1,), (1,)), ((), ())), preferred_element_type=F32)


def _modulated_norm(x, gain, shift, scale):
    ms = jnp.mean(x * x, axis=-1, keepdims=True)
    return (x * lax.rsqrt(ms + EPS) * gain) * (1.0 + scale) + shift


def _head_rms(z, seg, gain):
    ms = _dot((z * z).astype(BF16), seg)
    return z * lax.rsqrt(ms + EPS) * gain


def _rope(z, cos, sin_signed):
    n = z.shape[-1]
    lane = lax.broadcasted_iota(jnp.int32, z.shape, 1)
    first = (lane // (HEAD_DIM // 4)) % 2 == 0
    partner = jnp.where(first, pltpu.roll(z, n - HEAD_DIM // 4, 1), pltpu.roll(z, HEAD_DIM // 4, 1))
    return z * cos + partner * sin_signed


ADALN_COLS = 1024


def _adaln_kernel(c_ref, w_ref, b_ref, o_ref):
    cond = c_ref[...]
    act = cond * jax.nn.sigmoid(cond)
    o_ref[...] = jnp.dot(act, w_ref[...], precision=lax.Precision.HIGHEST, preferred_element_type=F32) + b_ref[...]


def _adaln(cond, w, b):
    n, d = cond.shape
    rows = -(-n // SUBLANES) * SUBLANES
    cond = jnp.pad(cond, ((0, rows - n), (0, 0)))
    n_out = w.shape[1]
    return pl.pallas_call(
        _adaln_kernel,
        grid=(n_out // ADALN_COLS,),
        in_specs=[
            pl.BlockSpec((rows, d), lambda j: (0, 0)),
            pl.BlockSpec((d, ADALN_COLS), lambda j: (0, j)),
            pl.BlockSpec((1, ADALN_COLS), lambda j: (0, j)),
        ],
        out_specs=pl.BlockSpec((rows, ADALN_COLS), lambda j: (0, j)),
        out_shape=jax.ShapeDtypeStruct((rows, n_out), F32),
        compiler_params=_cparams(("parallel",)),
        name="adaln",
    )(cond, w, b)[:n]


def _inproj_kernel(x_ref, shift_ref, scale_ref, g_ref, w_ref, b_ref, cos_ref, sin_ref, qg_ref, kg_ref,
                   seg_ref, zh_ref, q_ref, k_ref, v_ref, gate_ref):
    h = _modulated_norm(x_ref[...], g_ref[...], shift_ref[...], scale_ref[...]).astype(BF16)

    def proj(lo, hi):
        return _dot(h, w_ref[:, lo:hi]) + b_ref[:, lo:hi]

    zh_ref[...] = proj(COL_HY, COL_Q)
    reps = D_NA // cos_ref.shape[1]
    cos = jnp.concatenate([cos_ref[...]] * reps, axis=1)
    sin = jnp.concatenate([sin_ref[...]] * reps, axis=1)
    seg = seg_ref[...]
    q = _rope(_head_rms(proj(COL_Q, COL_K), seg, qg_ref[...]), cos, sin)
    q_ref[...] = q.astype(BF16)
    k = _rope(_head_rms(proj(COL_K, COL_V), seg, kg_ref[...]), cos, sin)
    k_ref[...] = k.astype(BF16)
    v_ref[...] = proj(COL_V, COL_G_HY).astype(BF16)
    gate_ref[...] = jax.nn.sigmoid(proj(COL_G_HY, N_PROJ)).astype(BF16)


def _inproj(x2, shift, scale, gain, w_bf, b_in, cos, sin, qg, kg, seg, seq):
    T = x2.shape[0]
    tiles_per_batch = seq // TOK_TILE
    row = lambda i: (i, 0)
    per_batch = lambda i: (i // tiles_per_batch, 0, 0)
    const = lambda i: (0, 0)
    pos = lambda i: (i % tiles_per_batch, 0)
    return pl.pallas_call(
        _inproj_kernel,
        grid=(T // TOK_TILE,),
        in_specs=[
            pl.BlockSpec((TOK_TILE, D_MODEL), row),
            pl.BlockSpec((None, 1, D_MODEL), per_batch),
            pl.BlockSpec((None, 1, D_MODEL), per_batch),
            pl.BlockSpec((1, D_MODEL), const),
            pl.BlockSpec((D_MODEL, N_PROJ), const),
            pl.BlockSpec((1, N_PROJ), const),
            pl.BlockSpec((TOK_TILE, cos.shape[1]), pos),
            pl.BlockSpec((TOK_TILE, sin.shape[1]), pos),
            pl.BlockSpec((1, D_NA), const),
            pl.BlockSpec((1, D_NA), const),
            pl.BlockSpec((D_NA, D_NA), const),
        ],
        out_specs=[
            pl.BlockSpec((TOK_TILE, 3 * D_HY), row),
            pl.BlockSpec((TOK_TILE, D_NA), row),
            pl.BlockSpec((TOK_TILE, D_NA), row),
            pl.BlockSpec((TOK_TILE, D_NA), row),
            pl.BlockSpec((TOK_TILE, 2 * D_MODEL), row),
        ],
        out_shape=[
            jax.ShapeDtypeStruct((T, 3 * D_HY), F32),
            jax.ShapeDtypeStruct((T, D_NA), BF16),
            jax.ShapeDtypeStruct((T, D_NA), BF16),
            jax.ShapeDtypeStruct((T, D_NA), BF16),
            jax.ShapeDtypeStruct((T, 2 * D_MODEL), BF16),
        ],
        compiler_params=_cparams(("parallel",)),
        name="inproj",
    )(x2, shift, scale, gain, w_bf, b_in, cos, sin, qg, kg, seg)


def _ctxkv_kernel(x_ref, shift_ref, scale_ref, g_ref, wk_ref, bk_ref, wv_ref, bv_ref, kg_ref, seg_ref,
                  k_ref, v_ref):
    h = _modulated_norm(x_ref[...], g_ref[...], shift_ref[...], scale_ref[...]).astype(BF16)
    k = _head_rms(_dot(h, wk_ref[...]) + bk_ref[...], seg_ref[...], kg_ref[...])
    k_ref[...] = k.astype(BF16)
    v_ref[...] = (_dot(h, wv_ref[...]) + bv_ref[...]).astype(BF16)


def _ctxkv(ctx2, shift, scale, gain, wk, bk, wv, bv, kg, seg):
    T = ctx2.shape[0]
    row = lambda i: (i, 0)
    const = lambda i: (0, 0)
    return pl.pallas_call(
        _ctxkv_kernel,
        grid=(T // TOK_TILE,),
        in_specs=[
            pl.BlockSpec((TOK_TILE, D_MODEL), row),
            pl.BlockSpec((1, D_MODEL), const),
            pl.BlockSpec((1, D_MODEL), const),
            pl.BlockSpec((1, D_MODEL), const),
            pl.BlockSpec((D_MODEL, D_NA), const),
            pl.BlockSpec((1, D_NA), const),
            pl.BlockSpec((D_MODEL, D_NA), const),
            pl.BlockSpec((1, D_NA), const),
            pl.BlockSpec((1, D_NA), const),
            pl.BlockSpec((D_NA, D_NA), const),
        ],
        out_specs=[pl.BlockSpec((TOK_TILE, D_NA), row), pl.BlockSpec((TOK_TILE, D_NA), row)],
        out_shape=[jax.ShapeDtypeStruct((T, D_NA), BF16), jax.ShapeDtypeStruct((T, D_NA), BF16)],
        compiler_params=_cparams(("parallel",)),
        name="ctxkv",
    )(ctx2, shift, scale, gain, wk, bk, wv, bv, kg, seg)


NA_BLOCK_ROWS = 4
NA_KEY_ROWS = WIN_ROWS + NA_BLOCK_ROWS
NA_CLASSES = 3


def _natten_kernel(q_ref, k_ref, v_ref, kc_ref, vc_ref, bias_ref, o_ref, *, rows):
    r0 = pl.program_id(1) * NA_BLOCK_ROWS
    key_start = jnp.clip(r0 - WIN_ROWS // 2, 0, rows - NA_KEY_ROWS)
    start = pl.multiple_of(key_start * GRID_W, GRID_W * NA_BLOCK_ROWS)
    n_q = NA_BLOCK_ROWS * GRID_W
    n_loc = NA_KEY_ROWS * GRID_W
    scale = jnp.asarray(HEAD_DIM ** -0.5, BF16)
    pair = 2 * HEAD_DIM
    lane = lax.broadcasted_iota(jnp.int32, (n_q, pair), 1)
    for p in range(NA_HEADS // 2):
        cols = slice(p * pair, (p + 1) * pair)
        qp = q_ref[:, cols] * scale
        kp = k_ref[pl.ds(start, n_loc), cols]
        vp = v_ref[pl.ds(start, n_loc), cols]
        kcp = kc_ref[:, cols]
        vcp = vc_ref[:, cols]
        outs = []
        for hh in range(2):
            qm = jnp.where((lane // HEAD_DIM) == hh, qp, jnp.zeros_like(qp))
            s_loc = _dot_nt(qm, kp) + bias_ref[2 * p + hh]
            s_ctx = _dot_nt(qm, kcp)
            m = jnp.maximum(jnp.max(s_loc, axis=-1, keepdims=True), jnp.max(s_ctx, axis=-1, keepdims=True))
            e_loc = jnp.exp(s_loc - m)
            e_ctx = jnp.exp(s_ctx - m)
            denom = jnp.sum(e_loc, axis=-1, keepdims=True) + jnp.sum(e_ctx, axis=-1, keepdims=True)
            o = _dot(e_loc.astype(BF16), vp) + _dot(e_ctx.astype(BF16), vcp)
            outs.append(o / denom)
        o_ref[:, cols] = jnp.where((lane // HEAD_DIM) == 0, outs[0], outs[1]).astype(o_ref.dtype)


def _natten(q, k, v, kc, vc, bias):
    B, S, _ = q.shape
    rows = S // GRID_W
    n_blk = rows // NA_BLOCK_ROWS
    C = kc.shape[1]
    n_q = NA_BLOCK_ROWS * GRID_W
    block_class = lambda b, i: (jnp.where(i == 0, 0, jnp.where(i == n_blk - 1, 2, 1)), 0, 0, 0)
    return pl.pallas_call(
        functools.partial(_natten_kernel, rows=rows),
        grid=(B, n_blk),
        in_specs=[
            pl.BlockSpec((None, n_q, D_NA), lambda b, i: (b, i, 0)),
            pl.BlockSpec((None, S, D_NA), lambda b, i: (b, 0, 0)),
            pl.BlockSpec((None, S, D_NA), lambda b, i: (b, 0, 0)),
            pl.BlockSpec((None, C, D_NA), lambda b, i: (b, 0, 0)),
            pl.BlockSpec((None, C, D_NA), lambda b, i: (b, 0, 0)),
            pl.BlockSpec((None,) + bias.shape[1:], block_class),
        ],
        out_specs=pl.BlockSpec((None, n_q, D_NA), lambda b, i: (b, i, 0)),
        out_shape=jax.ShapeDtypeStruct((B, S, D_NA), BF16),
        compiler_params=_cparams(("parallel", "arbitrary")),
        name="natten",
    )(q, k, v, kc, vc, bias)


def _natten_bias(rpb, rows):
    col = np.arange(GRID_W)
    dc = np.clip(col[None, :] - col[:, None], -(WIN_COLS - 1), WIN_COLS - 1) + (WIN_COLS - 1)
    col_start = np.clip(col - WIN_COLS // 2, 0, GRID_W - WIN_COLS)
    col_mask = (col[None, :] >= col_start[:, None]) & (col[None, :] < col_start[:, None] + WIN_COLS)
    r0 = np.array([0, 2 * NA_BLOCK_ROWS, rows - NA_BLOCK_ROWS])
    r = r0[:, None] + np.arange(NA_BLOCK_ROWS)[None, :]
    key_row = np.clip(r0 - WIN_ROWS // 2, 0, rows - NA_KEY_ROWS)[:, None, None] + np.arange(NA_KEY_ROWS)
    win_start = np.clip(r - WIN_ROWS // 2, 0, rows - WIN_ROWS)[:, :, None]
    row_mask = (key_row >= win_start) & (key_row < win_start + WIN_ROWS)
    dr = key_row - r[:, :, None] + (WIN_ROWS - 1)
    pick_c = jnp.asarray(dc[None] == np.arange(2 * WIN_COLS - 1)[:, None, None], dtype=F32)
    toep = jnp.einsum('hde,eqx->hdqx', rpb.astype(F32), pick_c, precision=HI)
    toep = jnp.where(col_mask[None, None], toep, NEG_INF)
    toep = jnp.pad(toep, ((0, 0), (1, 1), (0, 0), (0, 0)), constant_values=NEG_INF)
    pairs = jnp.concatenate([toep[:, :-1], toep[:, 1:]], axis=-1)

    def kernel(t_ref, o_ref):
        neg = jnp.full((GRID_W, 2 * GRID_W), NEG_INF, F32)
        lane_left = lax.broadcasted_iota(jnp.int32, (GRID_W, 2 * GRID_W), 1) < GRID_W
        for c in range(NA_CLASSES):
            for rq in range(NA_BLOCK_ROWS):
                for kp in range(NA_KEY_ROWS // 2):
                    left, right = bool(row_mask[c, rq, 2 * kp]), bool(row_mask[c, rq, 2 * kp + 1])
                    block = neg
                    if left or right:
                        block = t_ref[int(dr[c, rq, 2 * kp]) + 1]
                        if not (left and right):
                            block = jnp.where(lane_left == left, block, NEG_INF)
                    o_ref[c, rq * GRID_W:(rq + 1) * GRID_W, kp * 2 * GRID_W:(kp + 1) * 2 * GRID_W] = block

    n_q, n_k = NA_BLOCK_ROWS * GRID_W, NA_KEY_ROWS * GRID_W
    return pl.pallas_call(
        kernel,
        grid=(NA_HEADS,),
        in_specs=[pl.BlockSpec((None,) + pairs.shape[1:], lambda h: (h, 0, 0, 0))],
        out_specs=pl.BlockSpec((NA_CLASSES, None, n_q, n_k), lambda h: (0, h, 0, 0)),
        out_shape=jax.ShapeDtypeStruct((NA_CLASSES, NA_HEADS, n_q, n_k), F32),
        compiler_params=_cparams(("parallel",)),
        name="natten_bias",
    )(pairs)


def _rope_tables(seq):
    pos = np.arange(seq)
    rows = (pos // GRID_W).astype(np.float64)
    cols = (pos % GRID_W).astype(np.float64)
    nf = HEAD_DIM // 4
    inv = ROPE_THETA ** (-np.arange(nf, dtype=np.float64) / nf)
    ar = rows[:, None] * inv[None, :]
    ac = cols[:, None] * inv[None, :]
    cos = np.concatenate([np.cos(ar), np.cos(ar), np.cos(ac), np.cos(ac)], axis=-1)
    sin = np.concatenate([-np.sin(ar), np.sin(ar), -np.sin(ac), np.sin(ac)], axis=-1)
    return (jnp.asarray(np.tile(cos, (1, 2)), dtype=F32), jnp.asarray(np.tile(sin, (1, 2)), dtype=F32))


def _postmix_kernel(yh_ref, yn_ref, gate_ref, x_ref, g1_ref, sh2_ref, sc2_ref, n2_ref, whyo_ref, wnao_ref,
                    wout_ref, wq_ref, x1_ref, h2_ref, pq_ref):
    a = _dot(yh_ref[...].astype(BF16), whyo_ref[...])
    b = _dot(yn_ref[...], wnao_ref[...])
    gate = gate_ref[...].astype(F32)
    merged = gate[:, :D_MODEL] * a + gate[:, D_MODEL:] * b
    x1 = x_ref[...] + g1_ref[...] * _dot(merged.astype(BF16), wout_ref[...])
    x1_ref[...] = x1
    h2 = _modulated_norm(x1, n2_ref[...], sh2_ref[...], sc2_ref[...]).astype(BF16)
    h2_ref[...] = h2
    for hp in range(2 * PEER_HEADS):
        pq_ref[hp] = _dot(h2, wq_ref[:, hp * PEER_D_HALF:(hp + 1) * PEER_D_HALF]).astype(BF16)


def _postmix(yh, yn, gate, x2, g1, sh2, sc2, n2g, whyo, wnao, wout, wq, seq):
    T = x2.shape[0]
    tiles_per_batch = seq // TOK_TILE
    row = lambda i: (i, 0)
    per_batch = lambda i: (i // tiles_per_batch, 0, 0)
    const = lambda i: (0, 0)
    nq = wq.shape[1]
    return pl.pallas_call(
        _postmix_kernel,
        grid=(T // TOK_TILE,),
        in_specs=[
            pl.BlockSpec((TOK_TILE, D_HY), row),
            pl.BlockSpec((TOK_TILE, D_NA), row),
            pl.BlockSpec((TOK_TILE, 2 * D_MODEL), row),
            pl.BlockSpec((TOK_TILE, D_MODEL), row),
            pl.BlockSpec((None, 1, D_MODEL), per_batch),
            pl.BlockSpec((None, 1, D_MODEL), per_batch),
            pl.BlockSpec((None, 1, D_MODEL), per_batch),
            pl.BlockSpec((1, D_MODEL), const),
            pl.BlockSpec((D_HY, D_MODEL), const),
            pl.BlockSpec((D_NA, D_MODEL), const),
            pl.BlockSpec((D_MODEL, D_MODEL), const),
            pl.BlockSpec((D_MODEL, nq), const),
        ],
        out_specs=[
            pl.BlockSpec((TOK_TILE, D_MODEL), row),
            pl.BlockSpec((TOK_TILE, D_MODEL), row),
            pl.BlockSpec((nq // PEER_D_HALF, TOK_TILE, PEER_D_HALF), lambda i: (0, i, 0)),
        ],
        out_shape=[
            jax.ShapeDtypeStruct((T, D_MODEL), F32),
            jax.ShapeDtypeStruct((T, D_MODEL), BF16),
            jax.ShapeDtypeStruct((nq // PEER_D_HALF, T, PEER_D_HALF), BF16),
        ],
        compiler_params=_cparams(("parallel",)),
        name="postmix",
    )(yh, yn, gate, x2, g1, sh2, sc2, n2g, whyo, wnao, wout, wq)


SUBLANES = 8
LANES = 128


def _argmax_tree(vals, idxs):
    while len(vals) > 1:
        nv, ni = [], []
        for j in range(0, len(vals) - 1, 2):
            right = vals[j + 1] > vals[j]
            nv.append(jnp.where(right, vals[j + 1], vals[j]))
            ni.append(jnp.where(right, idxs[j + 1], idxs[j]))
        if len(vals) % 2:
            nv.append(vals[-1])
            ni.append(idxs[-1])
        vals, idxs = nv, ni
    return vals[0], idxs[0]


W_BLOCK = 128
W_PITCH = 136


SORT_TILE = SUBLANES * LANES


def _batcher_sort_network(n):
    ces = []

    def merge(lo, m, r):
        step = r * 2
        if step < m:
            merge(lo, m, step)
            merge(lo + r, m, step)
            ces.extend((i, i + r) for i in range(lo + r, lo + m - r, step))
        else:
            ces.append((lo, lo + r))

    def sort(lo, m):
        if m > 1:
            sort(lo, m // 2)
            sort(lo + m // 2, m // 2)
            merge(lo, m, 1)

    sort(0, n)
    return ces


def _bitonic_merge_network(n):
    ces, s = [], n // 2
    while s >= 1:
        ces.extend((i, i + s) for i in range(n) if (i & s) == 0)
        s //= 2
    return ces


SORT_NET = _batcher_sort_network(PEER_TOPK)
MERGE_NET = _bitonic_merge_network(PEER_TOPK)


def _compare_exchange(v, x, net):
    for i, j in net:
        c = v[j] > v[i]
        v[i], v[j] = jnp.where(c, v[j], v[i]), jnp.where(c, v[i], v[j])
        x[i], x[j] = jnp.where(c, x[j], x[i]), jnp.where(c, x[i], x[j])


def _merge_top(a, b, dropped):
    (va, xa), (vb, xb) = a, b
    k = len(va)
    v, x = [], []
    for r in range(k):
        c = vb[k - 1 - r] > va[r]
        v.append(jnp.where(c, vb[k - 1 - r], va[r]))
        x.append(jnp.where(c, xb[k - 1 - r], xa[r]))
        dropped = jnp.maximum(dropped, jnp.where(c, va[r], vb[k - 1 - r]))
    _compare_exchange(v, x, MERGE_NET)
    return (v, x), dropped


def _merge_all(lists, dropped):
    while len(lists) > 1:
        nxt = []
        for j in range(0, len(lists) - 1, 2):
            m, dropped = _merge_top(lists[j], lists[j + 1], dropped)
            nxt.append(m)
        if len(lists) % 2:
            nxt.append(lists[-1])
        lists = nxt
    return lists[0], dropped


def _settle_ties(v, x, dropped):
    x = list(x)
    for r in range(len(v) - 1):
        swap = (v[r] == v[r + 1]) & (x[r] > x[r + 1])
        x[r], x[r + 1] = jnp.where(swap, x[r + 1], x[r]), jnp.where(swap, x[r], x[r + 1])
    risk = v[-1] <= dropped
    for r in range(len(v) - 1):
        risk = risk | ((v[r] == v[r + 1]) & (x[r] > x[r + 1]))
    return x, risk.astype(jnp.int32)


def _level2_chains():
    pairs = [(a, b) for a in range(PEER_TOPK) for b in range(PEER_TOPK) if (a + 1) * (b + 1) <= PEER_TOPK]
    n_rows = 4
    chains = [[(a, b) for (a2, b) in pairs if a2 == a] for a in range(n_rows)]
    rest = [p for p in pairs if p[0] >= n_rows]
    for b in sorted({b for _, b in rest}):
        chains.append([(a, b2) for (a, b2) in rest if b2 == b])
    assert sorted(sum(chains, [])) == pairs
    return pairs, chains


def _route_kernel(pq_ref, keys_ref, a_ref, b_ref, g_ref, sx_ref, tv_ref, ti_ref, bs_ref, es_ref,
                  oa_ref, ob_ref, og_ref, risk_ref):
    n = PEER_N_KEYS
    k = PEER_TOPK
    shape = (SUBLANES, LANES)
    neg = jnp.full(shape, -jnp.inf, F32)
    pairs, chains = _level2_chains()

    def scores(hp):
        s = _dot_nt(keys_ref[hp], pq_ref[hp])
        for rg in range(n // SUBLANES):
            for lt in range(SUBLANES):
                sx_ref[pl.ds(rg * SUBLANES * SUBLANES + lt, SUBLANES, stride=SUBLANES), :] = (
                    s[rg * SUBLANES:(rg + 1) * SUBLANES, lt * LANES:(lt + 1) * LANES])
        return [sx_ref[pl.ds(key * SUBLANES, SUBLANES), :] for key in range(n)]

    def store_top(hp, v, x):
        for r in range(k):
            tv_ref[hp, r] = v[r]
            ti_ref[hp, r] = x[r]

    def level1(hp, carry):
        vals = scores(hp)
        lists = []
        for q in range(n // k):
            v = vals[q * k:(q + 1) * k]
            x = [jnp.full(shape, q * k + r, jnp.int32) for r in range(k)]
            _compare_exchange(v, x, SORT_NET)
            lists.append((v, x))
        (v, x), dropped = _merge_all(lists, neg)
        x, risk = _settle_ties(v, x, dropped)
        store_top(hp, v, x)
        risk_ref[hp] = risk
        return carry

    def level1_repair(hp, carry):
        @pl.when(jnp.max(risk_ref[hp]) > 0)
        def _():
            level1_exact(hp)

        return carry

    def level1_exact(hp):
        vals = scores(hp)
        idx = [jnp.full(shape, key, jnp.int32) for key in range(n)]
        v, x = [], []
        for r in range(k):
            best, where_ = _argmax_tree(vals, idx)
            v.append(best)
            x.append(where_)
            vals = [jnp.where(where_ == key, -jnp.inf, vals[key]) for key in range(n)]
        store_top(hp, v, x)

    def candidates(h):
        s1 = [tv_ref[2 * h, r] for r in range(k)]
        s2 = [tv_ref[2 * h + 1, r] for r in range(k)]
        e1 = [ti_ref[2 * h, r] * n for r in range(k)]
        e2 = [ti_ref[2 * h + 1, r] for r in range(k)]
        return (lambda a, b: s1[a] + s2[b]), (lambda a, b: e1[a] + e2[b] + (a * k + b) * PEER_N_EXPERTS)

    def store_best(h, v, x):
        for r in range(k):
            bs_ref[h, r] = v[r]
            es_ref[h, r] = x[r]

    def level2(h, carry):
        val, expert = candidates(h)
        lists = []
        for chain in chains:
            pad = k - len(chain)
            lists.append(([val(a, b) for a, b in chain] + [neg] * pad,
                          [expert(a, b) for a, b in chain] + [jnp.zeros(shape, jnp.int32)] * pad))
        (v, x), dropped = _merge_all(lists, neg)
        x, risk = _settle_ties(v, x, dropped)
        store_best(h, v, x)
        risk_ref[h] = risk
        return carry

    def level2_repair(h, carry):
        @pl.when(jnp.max(risk_ref[h]) > 0)
        def _():
            level2_exact(h)

        return carry

    def level2_exact(h):
        val, expert = candidates(h)
        cand = [val(a, b) for a, b in pairs]
        exps = [expert(a, b) for a, b in pairs]
        pos = [jnp.full(shape, a * k + b, jnp.int32) for a, b in pairs]
        v, x = [], []
        for r in range(k):
            best, where_ = _argmax_tree(cand, pos)
            chosen = jnp.zeros(shape, jnp.int32)
            for m, (a, b) in enumerate(pairs):
                hit = where_ == (a * k + b)
                cand[m] = jnp.where(hit, -jnp.inf, cand[m])
                chosen = jnp.where(hit, exps[m], chosen)
            v.append(best)
            x.append(chosen)
        store_best(h, v, x)

    lax.fori_loop(0, 2 * PEER_HEADS, level1, 0, unroll=2)
    lax.fori_loop(0, 2 * PEER_HEADS, level1_repair, 0)
    lax.fori_loop(0, PEER_HEADS, level2, 0)
    lax.fori_loop(0, PEER_HEADS, level2_repair, 0)

    for h in range(PEER_HEADS):
        best = [bs_ref[h, r] for r in range(k)]
        ex = [jnp.exp(v - best[0]) for v in best]
        denom = functools.reduce(lambda p, q: p + q, ex)
        for r in range(k):
            rows = pl.ds((h * k + r) * SUBLANES, SUBLANES)
            e = jnp.bitwise_and(es_ref[h, r], PEER_N_EXPERTS - 1)
            oa_ref[rows, :] = jnp.right_shift(e, n.bit_length() - 1)
            ob_ref[rows, :] = jnp.bitwise_and(e, n - 1)
            og_ref[rows, :] = ex[r] / denom
    for s in range(SUBLANES):
        rows = pl.ds(s * LANES, LANES)
        plane = pl.ds(s, PEER_HEADS * k, stride=SUBLANES)
        a_ref[rows, :] = oa_ref[plane, :].T
        b_ref[rows, :] = ob_ref[plane, :].T
        g_ref[rows, :] = og_ref[plane, :].T


def _route(pq, keys_bf):
    n_hp, T, _ = pq.shape
    n_sel = PEER_HEADS * PEER_TOPK
    row = lambda i: (i, 0)
    reg = (SUBLANES, LANES)
    return pl.pallas_call(
        _route_kernel,
        grid=(T // SORT_TILE,),
        in_specs=[
            pl.BlockSpec((n_hp, SORT_TILE, PEER_D_HALF), lambda i: (0, i, 0)),
            pl.BlockSpec((n_hp, PEER_N_KEYS, PEER_D_HALF), lambda i: (0, 0, 0)),
        ],
        out_specs=[pl.BlockSpec((SORT_TILE, n_sel), row)] * 3,
        out_shape=[
            jax.ShapeDtypeStruct((T, n_sel), jnp.int32),
            jax.ShapeDtypeStruct((T, n_sel), jnp.int32),
            jax.ShapeDtypeStruct((T, n_sel), F32),
        ],
        scratch_shapes=[
            pltpu.VMEM((PEER_N_KEYS * SUBLANES, LANES), F32),
            pltpu.VMEM((n_hp, PEER_TOPK) + reg, F32),
            pltpu.VMEM((n_hp, PEER_TOPK) + reg, jnp.int32),
            pltpu.VMEM((PEER_HEADS, PEER_TOPK) + reg, F32),
            pltpu.VMEM((PEER_HEADS, PEER_TOPK) + reg, jnp.int32),
            pltpu.VMEM((n_sel * SUBLANES, LANES), jnp.int32),
            pltpu.VMEM((n_sel * SUBLANES, LANES), jnp.int32),
            pltpu.VMEM((n_sel * SUBLANES, LANES), F32),
            pltpu.VMEM((n_hp,) + reg, jnp.int32),
        ],
        compiler_params=_cparams(("parallel",)),
        name="route",
    )(pq, keys_bf)


WBUILD_UNROLL = 32


def _wbuild_kernel(a_ref, b_ref, g_ref, w_ref, s_ref):
    n = PEER_N_KEYS
    tb = a_ref.shape[0]
    key = lax.broadcasted_iota(jnp.int32, (n, a_ref.shape[1]), 0)

    def body(t, carry):
        a = a_ref[pl.ds(t, 1), :]
        b = b_ref[pl.ds(t, 1), :]
        g = g_ref[pl.ds(t, 1), :]
        ga = jnp.where(a == key, g, 0.0).astype(BF16)
        ob = jnp.where(b == key, 1.0, 0.0).T.astype(BF16)
        s_ref[pl.ds(pl.multiple_of(t * W_PITCH, SUBLANES), n), :] = _dot(ga, ob)
        return carry

    lax.fori_loop(0, tb, body, 0, unroll=WBUILD_UNROLL)

    def emit(i1, carry):
        w_ref[i1] = s_ref[pl.ds(i1, tb, stride=W_PITCH), :].astype(BF16)
        return carry

    lax.fori_loop(0, n, emit, 0, unroll=WBUILD_UNROLL)


def _wbuild(a, b, g):
    T, n_sel = a.shape
    row = lambda i: (i, 0)
    return pl.pallas_call(
        _wbuild_kernel,
        grid=(T // W_BLOCK,),
        in_specs=[pl.BlockSpec((W_BLOCK, n_sel), row)] * 3,
        out_specs=pl.BlockSpec((None, PEER_N_KEYS, W_BLOCK, PEER_N_KEYS), lambda i: (i, 0, 0, 0)),
        out_shape=jax.ShapeDtypeStruct((T // W_BLOCK, PEER_N_KEYS, W_BLOCK, PEER_N_KEYS), BF16),
        scratch_shapes=[pltpu.VMEM((W_BLOCK * W_PITCH, PEER_N_KEYS), F32)],
        compiler_params=_cparams(("parallel",)),
        name="wbuild",
    )(a, b, g)


def _peer_dense_kernel(h2_ref, w_ref, u_ref, v_ref, x1_ref, g2_ref, o_ref, acc_ref):
    e = pl.program_id(1)

    @pl.when(e == 0)
    def _():
        acc_ref[...] = jnp.zeros_like(acc_ref)

    a = _dot_nt(h2_ref[...], u_ref[...])
    act = 0.5 * a * (1.0 + lax.erf(a * (2.0 ** -0.5)))
    w = jnp.concatenate(
        [jnp.concatenate([w_ref[blk, i] for i in range(w_ref.shape[1])], axis=1) for blk in range(w_ref.shape[0])],
        axis=0)
    acc_ref[...] += _dot((w.astype(F32) * act).astype(BF16), v_ref[...])

    @pl.when(e == pl.num_programs(1) - 1)
    def _():
        o_ref[...] = x1_ref[...] + g2_ref[...] * acc_ref[...]


def _peer_dense(h2, w, u_bf, v_bf, x1, g2, seq):
    T = h2.shape[0]
    tiles_per_batch = seq // PEER_TOK_TILE
    n_exp = u_bf.shape[0]
    return pl.pallas_call(
        _peer_dense_kernel,
        grid=(T // PEER_TOK_TILE, n_exp // PEER_EXP_TILE),
        in_specs=[
            pl.BlockSpec((PEER_TOK_TILE, D_MODEL), lambda i, e: (i, 0)),
            pl.BlockSpec((PEER_TOK_TILE // W_BLOCK, PEER_EXP_TILE // PEER_N_KEYS, W_BLOCK, PEER_N_KEYS),
                         lambda i, e: (i, e, 0, 0)),
            pl.BlockSpec((PEER_EXP_TILE, D_MODEL), lambda i, e: (e, 0)),
            pl.BlockSpec((PEER_EXP_TILE, D_MODEL), lambda i, e: (e, 0)),
            pl.BlockSpec((PEER_TOK_TILE, D_MODEL), lambda i, e: (i, 0)),
            pl.BlockSpec((None, 1, D_MODEL), lambda i, e: (i // tiles_per_batch, 0, 0)),
        ],
        out_specs=pl.BlockSpec((PEER_TOK_TILE, D_MODEL), lambda i, e: (i, 0)),
        out_shape=jax.ShapeDtypeStruct((T, D_MODEL), F32),
        scratch_shapes=[pltpu.VMEM((PEER_TOK_TILE, D_MODEL), F32)],
        compiler_params=_cparams(("parallel", "arbitrary")),
        name="peer_dense",
    )(h2, w, u_bf, v_bf, x1, g2)


HI = lax.Precision.HIGHEST


HY_LANES = 128
FFT_R = 128
FFT_N = FFT_R * FFT_R
FFT_PITCH = 136
FFT_UNROLL = 8
FFT_K1_USED = FFT_R // 2 + 1
FFT_K1_SLABS = 66
FFT_K1_ROWS = 72
FFT_K1_CHUNK = 22
assert FFT_K1_SLABS % FFT_K1_CHUNK == 0 and FFT_K1_CHUNK % 2 == 0 and FFT_K1_USED <= FFT_K1_SLABS <= FFT_K1_ROWS


def _fft_tables():
    r = np.arange(FFT_R)
    kk = np.arange(FFT_K1_ROWS)
    n2, k1, n1 = r[:, None, None], kk[None, :, None], r[None, None, :]
    ang = (2.0 * math.pi / FFT_N) * ((k1 * (FFT_R * n1 + n2)) % FFT_N)
    used = k1 < FFT_K1_USED
    c, s = np.cos(ang) * used, np.sin(ang) * used
    g_full = np.concatenate([c, -s], axis=1)
    twice = np.where((k1 == 0) | (k1 == FFT_R // 2), 1.0, 2.0) / FFT_N
    h_half = np.concatenate([np.swapaxes(c * twice, 1, 2), -np.swapaxes(s * twice, 1, 2)],
                            axis=2)[:, :FFT_R // 2]
    ang2 = (2.0 * math.pi / FFT_R) * ((r[:, None] * r[None, :]) % FFT_R)
    c2, s2 = np.cos(ang2), np.sin(ang2)
    f_fwd = np.block([[c2, s2], [-s2, c2]])
    f_inv = np.block([[c2, -s2], [s2, c2]])
    return tuple(jnp.asarray(t, dtype=F32).astype(BF16) for t in (g_full, h_half, f_fwd, f_inv))


def _fft_stage1(load_slab, g_ref, bre_ref, bim_ref):
    def body(n2, carry):
        out = _dot(g_ref[n2], load_slab(n2).astype(BF16))
        bre_ref[pl.ds(n2, FFT_K1_ROWS, stride=FFT_PITCH), :] = out[:FFT_K1_ROWS]
        bim_ref[pl.ds(n2, FFT_K1_ROWS, stride=FFT_PITCH), :] = out[FFT_K1_ROWS:]
        return carry

    lax.fori_loop(0, FFT_R, body, 0, unroll=FFT_UNROLL)


def _k1_rows(k1):
    return pl.ds(pl.multiple_of(k1 * FFT_PITCH, SUBLANES), FFT_R)


def _load_k1_pair(bre_ref, bim_ref, k1):
    return jnp.concatenate(
        [jnp.concatenate([bre_ref[_k1_rows(k1 + d), :], bim_ref[_k1_rows(k1 + d), :]], axis=0) for d in range(2)],
        axis=1)


FILT_ROWS = 512
FILT_FEAT = 128


def _filt_kernel(feat_ref, delta_ref, w1_ref, b1_ref, w2_ref, b2_ref, w3_ref, b3_ref, w4_ref, freq_ref,
                 h_ref, sum_ref):
    @pl.when(pl.program_id(0) == 0)
    def _():
        sum_ref[...] = jnp.zeros_like(sum_ref)

    feat = feat_ref[...]
    freq = freq_ref[...]
    dot = lambda a, b: jnp.dot(a, b, precision=HI, preferred_element_type=F32)
    h = jnp.sin(freq * (dot(feat, w1_ref[...]) + b1_ref[...]))
    h = jnp.sin(freq * (dot(h, w2_ref[...]) + b2_ref[...]))
    h = jnp.sin(freq * (dot(h, w3_ref[...]) + b3_ref[...]))
    hi = h.astype(BF16)
    lo = (h - hi.astype(F32)).astype(BF16)
    h = _dot(hi, w4_ref[...]) + _dot(lo, w4_ref[...])
    n_feat = 1 + 2 * HY_BANDS
    t = feat[:, 0:1]
    forward = feat[:, n_feat:n_feat + 1] > 0.5
    keep = feat[:, n_feat + 1:n_feat + 2]
    decay = jnp.exp(-t * delta_ref[...])
    outs = []
    for o in range(HY_ORDER):
        base = o * 2 * D_HY
        outs.append(jnp.where(forward, h[:, base:base + D_HY], h[:, base + D_HY:base + 2 * D_HY]) * decay)
    out = jnp.concatenate(outs, axis=1)
    sum_ref[...] += jnp.sum(jnp.abs(out), axis=0, keepdims=True)
    out = out * keep
    for j in range(h_ref.shape[0]):
        h_ref[j] = out[:, j * HY_LANES:(j + 1) * HY_LANES]


def _filt_features(L):
    r = np.arange(FFT_N)
    n = FFT_R * (r % FFT_R) + r // FFT_R
    j = np.where(n < L, n, np.where(n > L, 2 * L - n, 0))
    jj = j.astype(np.float64)[:, None]
    t = jj / (L - 1.0)
    w = (2.0 * math.pi / L) * jj
    bands = np.linspace(1e-4, HY_BANDS - 1.0, HY_BANDS)[None, :]
    flags = np.stack([n < L, n != L], axis=1).astype(np.float64)
    feats = np.concatenate([t, np.cos(bands * w), -np.sin(bands * w), flags], axis=-1)
    return jnp.asarray(np.pad(feats, ((0, 0), (0, FILT_FEAT - feats.shape[1]))), dtype=F32)


def _filters(p, L):
    feats = _filt_features(L)
    deltas = jnp.abs(jnp.linspace(math.log(HY_DECAY_TARGET) / HY_SLOW_PCT,
                                  math.log(HY_DECAY_TARGET) / HY_FAST_PCT, D_HY, dtype=F32))[None, :]
    w1 = jnp.pad(p['hy_f1_w'], ((0, FILT_FEAT - p['hy_f1_w'].shape[0]), (0, 0)))
    n_out = HY_ORDER * D_HY
    const = lambda i: (0, 0)
    full = lambda a: pl.BlockSpec(a.shape, const)
    args = [deltas, w1, p['hy_f1_b'][None], p['hy_f2_w'], p['hy_f2_b'][None], p['hy_f3_w'], p['hy_f3_b'][None],
            p['hy_f4_w'].astype(BF16), p['hy_sin_freq'][None]]
    return pl.pallas_call(
        _filt_kernel,
        grid=(FFT_N // FILT_ROWS,),
        in_specs=[pl.BlockSpec((FILT_ROWS, FILT_FEAT), lambda i: (i, 0))] + [full(a) for a in args],
        out_specs=[pl.BlockSpec((n_out // HY_LANES, FILT_ROWS, HY_LANES), lambda i: (0, i, 0)),
                   pl.BlockSpec((1, n_out), const)],
        out_shape=[jax.ShapeDtypeStruct((n_out // HY_LANES, FFT_N, HY_LANES), F32),
                   jax.ShapeDtypeStruct((1, n_out), F32)],
        compiler_params=_cparams(("arbitrary",)),
        name="hyena_filter",
    )(feats, *args)


def _fspec_kernel(h_ref, sum_ref, g_ref, f_ref, o_ref, bre_ref, bim_ref):
    c = pl.program_id(1)

    @pl.when(c == 0)
    def _():
        _fft_stage1(lambda n2: h_ref[pl.ds(pl.multiple_of(n2 * FFT_R, FFT_R), FFT_R), :], g_ref, bre_ref, bim_ref)

    inv = 1.0 / sum_ref[...]

    lanes = h_ref.shape[1]

    def slab_pair(j, carry):
        x = _dot(f_ref[...], _load_k1_pair(bre_ref, bim_ref, c * FFT_K1_CHUNK + 2 * j).astype(BF16))
        for d in range(2):
            rows = pl.ds(pl.multiple_of((2 * j + d) * FFT_R, FFT_R), FFT_R)
            o_ref[0, rows, :] = x[:FFT_R, d * lanes:(d + 1) * lanes] * inv
            o_ref[1, rows, :] = x[FFT_R:, d * lanes:(d + 1) * lanes] * inv
        return carry

    lax.fori_loop(0, FFT_K1_CHUNK // 2, slab_pair, 0, unroll=True)


def _filter_spectrum(h, hsum, g_full, f_fwd):
    n_ch = h.shape[0] * h.shape[2]
    single = pl.Buffered(1)
    return pl.pallas_call(
        _fspec_kernel,
        grid=(n_ch // HY_LANES, FFT_K1_SLABS // FFT_K1_CHUNK),
        in_specs=[
            pl.BlockSpec((None, FFT_N, HY_LANES), lambda j, c: (j, 0, 0)),
            pl.BlockSpec((1, HY_LANES), lambda j, c: (0, j)),
            pl.BlockSpec(g_full.shape, lambda j, c: (0, 0, 0), pipeline_mode=single),
            pl.BlockSpec(f_fwd.shape, lambda j, c: (0, 0)),
        ],
        out_specs=pl.BlockSpec((None, 2, FFT_K1_CHUNK * FFT_R, HY_LANES), lambda j, c: (j, 0, c, 0)),
        out_shape=jax.ShapeDtypeStruct((n_ch // HY_LANES, 2, FFT_K1_SLABS * FFT_R, HY_LANES), F32),
        scratch_shapes=[pltpu.VMEM((FFT_K1_ROWS * FFT_PITCH, HY_LANES), F32)] * 2,
        compiler_params=_cparams(("parallel", "arbitrary")),
        name="hyena_filter_fft",
    )(h, hsum, g_full, f_fwd)


def _hyconv_kernel(u_ref, gate_ref, uw_ref, ub_ref, gw_ref, gb_ref, kf_ref, skip_ref, g_ref, h_ref, f_ref, fi_ref,
                   o_ref, bre_ref, bim_ref, *, conv_u):
    c = pl.program_id(2)
    half = FFT_R // 2
    last = FFT_R - 1

    def rows_of(n2):
        return pl.ds(pl.multiple_of(n2 * half, half), half)

    row = lax.broadcasted_iota(jnp.int32, (half, u_ref.shape[1]), 0)

    def short_conv(ref, w_ref, b_ref):
        wrap_prev = jnp.where(row == 0, 0.0, pltpu.roll(ref[rows_of(last), :], 1, 0))
        wrap_next = jnp.where(row == half - 1, 0.0, pltpu.roll(ref[rows_of(0), :], half - 1, 0))

        def slab(n2):
            prev = jnp.where(n2 == 0, wrap_prev, ref[rows_of(jnp.maximum(n2 - 1, 0)), :])
            nxt = jnp.where(n2 == last, wrap_next, ref[rows_of(jnp.minimum(n2 + 1, last)), :])
            return w_ref[0:1] * prev + w_ref[1:2] * ref[rows_of(n2), :] + w_ref[2:3] * nxt + b_ref[...]

        return slab

    u_slab = short_conv(u_ref, uw_ref, ub_ref) if conv_u else (lambda n2: u_ref[rows_of(n2), :])
    gate_slab = short_conv(gate_ref, gw_ref, gb_ref)

    @pl.when(c == 0)
    def _():
        _fft_stage1(u_slab, g_ref, bre_ref, bim_ref)

    lanes = u_ref.shape[1]

    def slab_pair(j, carry):
        k1 = c * FFT_K1_CHUNK + 2 * j
        x = _dot(f_ref[...], _load_k1_pair(bre_ref, bim_ref, k1).astype(BF16))
        xr, xi = x[:FFT_R], x[FFT_R:]
        kr, ki = [jnp.concatenate([kf_ref[part, pl.ds(pl.multiple_of((2 * j + d) * FFT_R, FFT_R), FFT_R), :]
                                   for d in range(2)], axis=1) for part in range(2)]
        y = jnp.concatenate([xr * kr - xi * ki, xr * ki + xi * kr], axis=0)
        z = _dot(fi_ref[...], y.astype(BF16))
        for d in range(2):
            bre_ref[_k1_rows(k1 + d), :] = z[:FFT_R, d * lanes:(d + 1) * lanes]
            bim_ref[_k1_rows(k1 + d), :] = z[FFT_R:, d * lanes:(d + 1) * lanes]
        return carry

    lax.fori_loop(0, FFT_K1_CHUNK // 2, slab_pair, 0, unroll=True)

    @pl.when(c == pl.num_programs(2) - 1)
    def _():
        def body(n2, carry):
            z = jnp.concatenate([bre_ref[pl.ds(n2, FFT_K1_ROWS, stride=FFT_PITCH), :],
                                 bim_ref[pl.ds(n2, FFT_K1_ROWS, stride=FFT_PITCH), :]], axis=0)
            conv = _dot(h_ref[n2], z.astype(BF16))
            o_ref[rows_of(n2), :] = gate_slab(n2) * (conv + skip_ref[...] * u_slab(n2))
            return carry

        lax.fori_loop(0, FFT_R, body, 0, unroll=FFT_UNROLL)


def _hyconv(u, u_col, gate, gate_col, conv_w, conv_b, conv_u, kf, kf_col, skip, tables):
    g_half, h_half, f_fwd, f_inv = tables
    B, L, _ = u.shape
    single = pl.Buffered(1)
    uw_col = u_col if conv_u else gate_col
    return pl.pallas_call(
        functools.partial(_hyconv_kernel, conv_u=conv_u),
        grid=(B, D_HY // HY_LANES, FFT_K1_SLABS // FFT_K1_CHUNK),
        in_specs=[
            pl.BlockSpec((None, L, HY_LANES), lambda b, j, c: (b, 0, u_col + j)),
            pl.BlockSpec((None, L, HY_LANES), lambda b, j, c: (b, 0, gate_col + j)),
            pl.BlockSpec((3, HY_LANES), lambda b, j, c: (0, uw_col + j)),
            pl.BlockSpec((1, HY_LANES), lambda b, j, c: (0, uw_col + j)),
            pl.BlockSpec((3, HY_LANES), lambda b, j, c: (0, gate_col + j)),
            pl.BlockSpec((1, HY_LANES), lambda b, j, c: (0, gate_col + j)),
            pl.BlockSpec((None, 2, FFT_K1_CHUNK * FFT_R, HY_LANES), lambda b, j, c: (kf_col + j, 0, c, 0)),
            pl.BlockSpec((1, HY_LANES), lambda b, j, c: (0, j)),
            pl.BlockSpec(g_half.shape, lambda b, j, c: (0, 0, 0), pipeline_mode=single),
            pl.BlockSpec(h_half.shape, lambda b, j, c: (0, 0, 0), pipeline_mode=single),
            pl.BlockSpec(f_fwd.shape, lambda b, j, c: (0, 0)),
            pl.BlockSpec(f_inv.shape, lambda b, j, c: (0, 0)),
        ],
        out_specs=pl.BlockSpec((None, L, HY_LANES), lambda b, j, c: (b, 0, j)),
        out_shape=jax.ShapeDtypeStruct((B, L, D_HY), F32),
        scratch_shapes=[pltpu.VMEM((FFT_K1_ROWS * FFT_PITCH, HY_LANES), F32)] * 2,
        compiler_params=_cparams(("parallel", "parallel", "arbitrary")),
        name="hyena_conv",
    )(u, gate, conv_w, conv_b, conv_w, conv_b, kf, skip, g_half, h_half, f_fwd, f_inv)


def _slab_major(a, n1):
    B, L, C = a.shape
    return a.reshape(B, n1, L // n1, C).transpose(0, 2, 1, 3).reshape(B, L, C)


def _hyena(zh, p):
    B, L, _ = zh.shape
    assert 2 * L == FFT_N
    g_full, h_half, f_fwd, f_inv = _fft_tables()
    tables = (g_full[:, :, :FFT_R // 2], h_half, f_fwd, f_inv)
    filt, filt_sum = _filters(p, L)
    both_dirs = filt_sum[:, :]
    kf = _filter_spectrum(filt, both_dirs, g_full, f_fwd)
    zs = _slab_major(zh, FFT_R // 2)
    cw, cb = p['hy_conv_w'], p['hy_conv_b'][None]
    lanes = D_HY // HY_LANES
    y = _hyconv(zs, 0, zs, lanes, cw, cb, True, kf, 0, p['hy_skip'][0:1], tables)
    y = _hyconv(y, 0, zs, 2 * lanes, cw, cb, False, kf, lanes, p['hy_skip'][1:2], tables)
    return _slab_major(y, L // (FFT_R // 2))


def kernel(x, c, ctx, c_ctx, norm1_g, norm2_g, w_ada, b_ada, w_in, b_in, hy_conv_w, hy_conv_b, hy_f1_w, hy_f1_b, hy_f2_w, hy_f2_b, hy_f3_w, hy_f3_b, hy_f4_w, hy_sin_freq, hy_skip, q_norm_g, k_norm_g, na_rpb, w_hy_out, w_na_out, w_out, peer_w_q, peer_keys, peer_u, peer_v):
    assert w_in.shape[0] == 1, "single-layer block"
    B, S, D = x.shape
    C = ctx.shape[1]
    T = B * S

    mod = _adaln(jnp.concatenate([c, c_ctx[None]], axis=0), w_ada[0], b_ada[0][None])
    m_lat = mod[:B].reshape(B, N_MOD, 1, D)
    m_ctx = mod[B].reshape(N_MOD, 1, D)
    sh1, sc1, g1, sh2, sc2, g2 = [m_lat[:, i] for i in range(N_MOD)]

    w_in_bf = w_in[0].astype(BF16)
    b_in2 = b_in[0][None, :]
    gain1 = norm1_g[0][None, :]
    qg = jnp.tile(q_norm_g[0], NA_HEADS)[None, :]
    kg = jnp.tile(k_norm_g[0], NA_HEADS)[None, :]
    seg = jnp.asarray(np.kron(np.eye(NA_HEADS), np.full((HEAD_DIM, HEAD_DIM), 1.0 / HEAD_DIM)), dtype=BF16)
    cos, sin = _rope_tables(S)

    x2 = x.reshape(T, D)
    zh, q, k, v, gate = _inproj(x2, sh1, sc1, gain1, w_in_bf, b_in2, cos, sin, qg, kg, seg, S)

    k_ctx, v_ctx = _ctxkv(ctx.reshape(B * C, D), m_ctx[0], m_ctx[1], gain1,
                          w_in_bf[:, COL_K:COL_V], b_in2[:, COL_K:COL_V],
                          w_in_bf[:, COL_V:COL_G_HY], b_in2[:, COL_V:COL_G_HY], kg, seg)

    p = {'hy_conv_w': hy_conv_w[0], 'hy_conv_b': hy_conv_b[0], 'hy_f1_w': hy_f1_w[0], 'hy_f1_b': hy_f1_b[0],
         'hy_f2_w': hy_f2_w[0], 'hy_f2_b': hy_f2_b[0], 'hy_f3_w': hy_f3_w[0], 'hy_f3_b': hy_f3_b[0],
         'hy_f4_w': hy_f4_w[0], 'hy_sin_freq': hy_sin_freq[0], 'hy_skip': hy_skip[0]}
    y_hy = _hyena(zh.reshape(B, S, 3 * D_HY), p).reshape(T, D_HY)

    y_na = _natten(q.reshape(B, S, D_NA), k.reshape(B, S, D_NA), v.reshape(B, S, D_NA),
                   k_ctx.reshape(B, C, D_NA), v_ctx.reshape(B, C, D_NA), _natten_bias(na_rpb[0], S // GRID_W)).reshape(T, D_NA)

    x1, h2, pq = _postmix(y_hy, y_na, gate, x2, g1, sh2, sc2, norm2_g[0][None, :],
                          w_hy_out[0].astype(BF16), w_na_out[0].astype(BF16), w_out[0].astype(BF16),
                          peer_w_q[0].astype(BF16), S)

    keys_bf = peer_keys[0].astype(BF16).reshape(2 * PEER_HEADS, PEER_N_KEYS, PEER_D_HALF)
    sel_a, sel_b, sel_g = _route(pq, keys_bf)
    w = _wbuild(sel_a, sel_b, sel_g)
    out = _peer_dense(h2, w, peer_u[0].astype(BF16), peer_v[0].astype(BF16), x1, g2, S)
    return out.reshape(B, S, D)
```

```python
import functools
import math

import numpy as np
import jax
import jax.numpy as jnp
from jax import lax
from jax.experimental import pallas as pl
from jax.experimental.pallas import tpu as pltpu

F32 = jnp.float32
BF16 = jnp.bfloat16

D_MODEL = 1024
GRID_W = 64
EPS = 1e-6
N_MOD = 6

D_HY = 512
HY_ORDER = 2
HY_BANDS = 16
HY_DECAY_TARGET = 1e-2
HY_FAST_PCT = 0.3
HY_SLOW_PCT = 1.5

NA_HEADS = 8
HEAD_DIM = 64
D_NA = NA_HEADS * HEAD_DIM
WIN_ROWS = 8
WIN_COLS = 16
ROPE_THETA = 10000.0
NEG_INF = -1e30

PEER_HEADS = 8
PEER_N_KEYS = 128
PEER_N_EXPERTS = PEER_N_KEYS * PEER_N_KEYS
PEER_TOPK = 16
PEER_D_KEY = 256
PEER_D_HALF = PEER_D_KEY // 2

COL_HY = 0
COL_Q = COL_HY + 3 * D_HY
COL_K = COL_Q + D_NA
COL_V = COL_K + D_NA
COL_G_HY = COL_V + D_NA
N_PROJ = COL_G_HY + 2 * D_MODEL

VMEM_LIMIT = 56 * 1024 * 1024
TOK_TILE = 512
PEER_TOK_TILE = 512
PEER_EXP_TILE = 2048


def _cparams(sem):
    return pltpu.CompilerParams(dimension_semantics=sem, vmem_limit_bytes=VMEM_LIMIT)


def _dot(a, b):
    return jnp.dot(a, b, preferred_element_type=F32)


def _dot_nt(a, b):
    return lax.dot_general(a, b, (((1,), (1,)), ((), ())), preferred_element_type=F32)


def _modulated_norm(x, gain, shift, scale):
    ms = jnp.mean(x * x, axis=-1, keepdims=True)
    return (x * lax.rsqrt(ms + EPS) * gain) * (1.0 + scale) + shift


def _head_rms(z, seg, gain):
    ms = _dot((z * z).astype(BF16), seg)
    return z * lax.rsqrt(ms + EPS) * gain


def _rope(z, cos, sin_signed):
    n = z.shape[-1]
    lane = lax.broadcasted_iota(jnp.int32, z.shape, 1)
    first = (lane // (HEAD_DIM // 4)) % 2 == 0
    partner = jnp.where(first, pltpu.roll(z, n - HEAD_DIM // 4, 1), pltpu.roll(z, HEAD_DIM // 4, 1))
    return z * cos + partner * sin_signed


ADALN_COLS = 1024


def _adaln_kernel(c_ref, w_ref, b_ref, o_ref):
    cond = c_ref[...]
    act = cond * jax.nn.sigmoid(cond)
    o_ref[...] = jnp.dot(act, w_ref[...], precision=lax.Precision.HIGHEST, preferred_element_type=F32) + b_ref[...]


def _adaln(cond, w, b):
    n, d = cond.shape
    rows = -(-n // SUBLANES) * SUBLANES
    cond = jnp.pad(cond, ((0, rows - n), (0, 0)))
    n_out = w.shape[1]
    return pl.pallas_call(
        _adaln_kernel,
        grid=(n_out // ADALN_COLS,),
        in_specs=[
            pl.BlockSpec((rows, d), lambda j: (0, 0)),
            pl.BlockSpec((d, ADALN_COLS), lambda j: (0, j)),
            pl.BlockSpec((1, ADALN_COLS), lambda j: (0, j)),
        ],
        out_specs=pl.BlockSpec((rows, ADALN_COLS), lambda j: (0, j)),
        out_shape=jax.ShapeDtypeStruct((rows, n_out), F32),
        compiler_params=_cparams(("parallel",)),
        name="adaln",
    )(cond, w, b)[:n]


def _inproj_kernel(x_ref, shift_ref, scale_ref, g_ref, w_ref, b_ref, cos_ref, sin_ref, qg_ref, kg_ref,
                   seg_ref, zh_ref, q_ref, k_ref, v_ref, gate_ref):
    h = _modulated_norm(x_ref[...], g_ref[...], shift_ref[...], scale_ref[...]).astype(BF16)

    def proj(lo, hi):
        return _dot(h, w_ref[:, lo:hi]) + b_ref[:, lo:hi]

    zh_ref[...] = proj(COL_HY, COL_Q)
    reps = D_NA // cos_ref.shape[1]
    cos = jnp.concatenate([cos_ref[...]] * reps, axis=1)
    sin = jnp.concatenate([sin_ref[...]] * reps, axis=1)
    seg = seg_ref[...]
    q = _rope(_head_rms(proj(COL_Q, COL_K), seg, qg_ref[...]), cos, sin)
    q_ref[...] = q.astype(BF16)
    k = _rope(_head_rms(proj(COL_K, COL_V), seg, kg_ref[...]), cos, sin)
    k_ref[...] = k.astype(BF16)
    v_ref[...] = proj(COL_V, COL_G_HY).astype(BF16)
    gate_ref[...] = jax.nn.sigmoid(proj(COL_G_HY, N_PROJ)).astype(BF16)


def _inproj(x2, shift, scale, gain, w_bf, b_in, cos, sin, qg, kg, seg, seq):
    T = x2.shape[0]
    tiles_per_batch = seq // TOK_TILE
    row = lambda i: (i, 0)
    per_batch = lambda i: (i // tiles_per_batch, 0, 0)
    const = lambda i: (0, 0)
    pos = lambda i: (i % tiles_per_batch, 0)
    return pl.pallas_call(
        _inproj_kernel,
        grid=(T // TOK_TILE,),
        in_specs=[
            pl.BlockSpec((TOK_TILE, D_MODEL), row),
            pl.BlockSpec((None, 1, D_MODEL), per_batch),
            pl.BlockSpec((None, 1, D_MODEL), per_batch),
            pl.BlockSpec((1, D_MODEL), const),
            pl.BlockSpec((D_MODEL, N_PROJ), const),
            pl.BlockSpec((1, N_PROJ), const),
            pl.BlockSpec((TOK_TILE, cos.shape[1]), pos),
            pl.BlockSpec((TOK_TILE, sin.shape[1]), pos),
            pl.BlockSpec((1, D_NA), const),
            pl.BlockSpec((1, D_NA), const),
            pl.BlockSpec((D_NA, D_NA), const),
        ],
        out_specs=[
            pl.BlockSpec((TOK_TILE, 3 * D_HY), row),
            pl.BlockSpec((TOK_TILE, D_NA), row),
            pl.BlockSpec((TOK_TILE, D_NA), row),
            pl.BlockSpec((TOK_TILE, D_NA), row),
            pl.BlockSpec((TOK_TILE, 2 * D_MODEL), row),
        ],
        out_shape=[
            jax.ShapeDtypeStruct((T, 3 * D_HY), F32),
            jax.ShapeDtypeStruct((T, D_NA), BF16),
            jax.ShapeDtypeStruct((T, D_NA), BF16),
            jax.ShapeDtypeStruct((T, D_NA), BF16),
            jax.ShapeDtypeStruct((T, 2 * D_MODEL), BF16),
        ],
        compiler_params=_cparams(("parallel",)),
        name="inproj",
    )(x2, shift, scale, gain, w_bf, b_in, cos, sin, qg, kg, seg)


def _ctxkv_kernel(x_ref, shift_ref, scale_ref, g_ref, wk_ref, bk_ref, wv_ref, bv_ref, kg_ref, seg_ref,
                  k_ref, v_ref):
    h = _modulated_norm(x_ref[...], g_ref[...], shift_ref[...], scale_ref[...]).astype(BF16)
    k = _head_rms(_dot(h, wk_ref[...]) + bk_ref[...], seg_ref[...], kg_ref[...])
    k_ref[...] = k.astype(BF16)
    v_ref[...] = (_dot(h, wv_ref[...]) + bv_ref[...]).astype(BF16)


def _ctxkv(ctx2, shift, scale, gain, wk, bk, wv, bv, kg, seg):
    T = ctx2.shape[0]
    row = lambda i: (i, 0)
    const = lambda i: (0, 0)
    return pl.pallas_call(
        _ctxkv_kernel,
        grid=(T // TOK_TILE,),
        in_specs=[
            pl.BlockSpec((TOK_TILE, D_MODEL), row),
            pl.BlockSpec((1, D_MODEL), const),
            pl.BlockSpec((1, D_MODEL), const),
            pl.BlockSpec((1, D_MODEL), const),
            pl.BlockSpec((D_MODEL, D_NA), const),
            pl.BlockSpec((1, D_NA), const),
            pl.BlockSpec((D_MODEL, D_NA), const),
            pl.BlockSpec((1, D_NA), const),
            pl.BlockSpec((1, D_NA), const),
            pl.BlockSpec((D_NA, D_NA), const),
        ],
        out_specs=[pl.BlockSpec((TOK_TILE, D_NA), row), pl.BlockSpec((TOK_TILE, D_NA), row)],
        out_shape=[jax.ShapeDtypeStruct((T, D_NA), BF16), jax.ShapeDtypeStruct((T, D_NA), BF16)],
        compiler_params=_cparams(("parallel",)),
        name="ctxkv",
    )(ctx2, shift, scale, gain, wk, bk, wv, bv, kg, seg)


NA_BLOCK_ROWS = 4
NA_KEY_ROWS = WIN_ROWS + NA_BLOCK_ROWS
NA_CLASSES = 3


def _natten_kernel(q_ref, k_ref, v_ref, kc_ref, vc_ref, bias_ref, o_ref, *, rows):
    r0 = pl.program_id(1) * NA_BLOCK_ROWS
    key_start = jnp.clip(r0 - WIN_ROWS // 2, 0, rows - NA_KEY_ROWS)
    start = pl.multiple_of(key_start * GRID_W, GRID_W * NA_BLOCK_ROWS)
    n_q = NA_BLOCK_ROWS * GRID_W
    n_loc = NA_KEY_ROWS * GRID_W
    scale = jnp.asarray(HEAD_DIM ** -0.5, BF16)
    pair = 2 * HEAD_DIM
    lane = lax.broadcasted_iota(jnp.int32, (n_q, pair), 1)
    for p in range(NA_HEADS // 2):
        cols = slice(p * pair, (p + 1) * pair)
        qp = q_ref[:, cols] * scale
        kp = k_ref[pl.ds(start, n_loc), cols]
        vp = v_ref[pl.ds(start, n_loc), cols]
        kcp = kc_ref[:, cols]
        vcp = vc_ref[:, cols]
        qm = jnp.concatenate([jnp.where((lane // HEAD_DIM) == hh, qp, jnp.zeros_like(qp)) for hh in range(2)], axis=0)
        s_loc = _dot_nt(qm, kp) + jnp.concatenate([bias_ref[2 * p], bias_ref[2 * p + 1]], axis=0)
        s_ctx = _dot_nt(qm, kcp)
        m = jnp.maximum(jnp.max(s_loc, axis=-1, keepdims=True), jnp.max(s_ctx, axis=-1, keepdims=True))
        e_loc = jnp.exp(s_loc - m)
        e_ctx = jnp.exp(s_ctx - m)
        denom = jnp.sum(e_loc, axis=-1, keepdims=True) + jnp.sum(e_ctx, axis=-1, keepdims=True)
        o = (_dot(e_loc.astype(BF16), vp) + _dot(e_ctx.astype(BF16), vcp)) / denom
        o_ref[:, cols] = jnp.where((lane // HEAD_DIM) == 0, o[:n_q], o[n_q:]).astype(o_ref.dtype)


def _natten(q, k, v, kc, vc, bias):
    B, S, _ = q.shape
    rows = S // GRID_W
    n_blk = rows // NA_BLOCK_ROWS
    C = kc.shape[1]
    n_q = NA_BLOCK_ROWS * GRID_W
    block_class = lambda b, i: (jnp.where(i == 0, 0, jnp.where(i == n_blk - 1, 2, 1)), 0, 0, 0)
    return pl.pallas_call(
        functools.partial(_natten_kernel, rows=rows),
        grid=(B, n_blk),
        in_specs=[
            pl.BlockSpec((None, n_q, D_NA), lambda b, i: (b, i, 0)),
            pl.BlockSpec((None, S, D_NA), lambda b, i: (b, 0, 0)),
            pl.BlockSpec((None, S, D_NA), lambda b, i: (b, 0, 0)),
            pl.BlockSpec((None, C, D_NA), lambda b, i: (b, 0, 0)),
            pl.BlockSpec((None, C, D_NA), lambda b, i: (b, 0, 0)),
            pl.BlockSpec((None,) + bias.shape[1:], block_class),
        ],
        out_specs=pl.BlockSpec((None, n_q, D_NA), lambda b, i: (b, i, 0)),
        out_shape=jax.ShapeDtypeStruct((B, S, D_NA), BF16),
        compiler_params=_cparams(("parallel", "arbitrary")),
        name="natten",
    )(q, k, v, kc, vc, bias)


def _natten_bias(rpb, rows):
    col = np.arange(GRID_W)
    dc = np.clip(col[None, :] - col[:, None], -(WIN_COLS - 1), WIN_COLS - 1) + (WIN_COLS - 1)
    col_start = np.clip(col - WIN_COLS // 2, 0, GRID_W - WIN_COLS)
    col_mask = (col[None, :] >= col_start[:, None]) & (col[None, :] < col_start[:, None] + WIN_COLS)
    r0 = np.array([0, 2 * NA_BLOCK_ROWS, rows - NA_BLOCK_ROWS])
    r = r0[:, None] + np.arange(NA_BLOCK_ROWS)[None, :]
    key_row = np.clip(r0 - WIN_ROWS // 2, 0, rows - NA_KEY_ROWS)[:, None, None] + np.arange(NA_KEY_ROWS)
    win_start = np.clip(r - WIN_ROWS // 2, 0, rows - WIN_ROWS)[:, :, None]
    row_mask = (key_row >= win_start) & (key_row < win_start + WIN_ROWS)
    dr = key_row - r[:, :, None] + (WIN_ROWS - 1)
    pick_c = jnp.asarray(dc[None] == np.arange(2 * WIN_COLS - 1)[:, None, None], dtype=F32)
    toep = jnp.einsum('hde,eqx->hdqx', rpb.astype(F32), pick_c, precision=HI)
    toep = jnp.where(col_mask[None, None], toep, NEG_INF)
    toep = jnp.pad(toep, ((0, 0), (1, 1), (0, 0), (0, 0)), constant_values=NEG_INF)
    pairs = jnp.concatenate([toep[:, :-1], toep[:, 1:]], axis=-1)

    def kernel(t_ref, o_ref):
        neg = jnp.full((GRID_W, 2 * GRID_W), NEG_INF, F32)
        lane_left = lax.broadcasted_iota(jnp.int32, (GRID_W, 2 * GRID_W), 1) < GRID_W
        for c in range(NA_CLASSES):
            for rq in range(NA_BLOCK_ROWS):
                for kp in range(NA_KEY_ROWS // 2):
                    left, right = bool(row_mask[c, rq, 2 * kp]), bool(row_mask[c, rq, 2 * kp + 1])
                    block = neg
                    if left or right:
                        block = t_ref[int(dr[c, rq, 2 * kp]) + 1]
                        if not (left and right):
                            block = jnp.where(lane_left == left, block, NEG_INF)
                    o_ref[c, rq * GRID_W:(rq + 1) * GRID_W, kp * 2 * GRID_W:(kp + 1) * 2 * GRID_W] = block

    n_q, n_k = NA_BLOCK_ROWS * GRID_W, NA_KEY_ROWS * GRID_W
    return pl.pallas_call(
        kernel,
        grid=(NA_HEADS,),
        in_specs=[pl.BlockSpec((None,) + pairs.shape[1:], lambda h: (h, 0, 0, 0))],
        out_specs=pl.BlockSpec((NA_CLASSES, None, n_q, n_k), lambda h: (0, h, 0, 0)),
        out_shape=jax.ShapeDtypeStruct((NA_CLASSES, NA_HEADS, n_q, n_k), F32),
        compiler_params=_cparams(("parallel",)),
        name="natten_bias",
    )(pairs)


def _rope_tables(seq):
    pos = np.arange(seq)
    rows = (pos // GRID_W).astype(np.float64)
    cols = (pos % GRID_W).astype(np.float64)
    nf = HEAD_DIM // 4
    inv = ROPE_THETA ** (-np.arange(nf, dtype=np.float64) / nf)
    ar = rows[:, None] * inv[None, :]
    ac = cols[:, None] * inv[None, :]
    cos = np.concatenate([np.cos(ar), np.cos(ar), np.cos(ac), np.cos(ac)], axis=-1)
    sin = np.concatenate([-np.sin(ar), np.sin(ar), -np.sin(ac), np.sin(ac)], axis=-1)
    return (jnp.asarray(np.tile(cos, (1, 2)), dtype=F32), jnp.asarray(np.tile(sin, (1, 2)), dtype=F32))


def _postmix_kernel(yh_ref, yn_ref, gate_ref, x_ref, g1_ref, sh2_ref, sc2_ref, n2_ref, whyo_ref, wnao_ref,
                    wout_ref, wq_ref, x1_ref, h2_ref, pq_ref):
    a = _dot(yh_ref[...].astype(BF16), whyo_ref[...])
    b = _dot(yn_ref[...], wnao_ref[...])
    gate = gate_ref[...].astype(F32)
    merged = gate[:, :D_MODEL] * a + gate[:, D_MODEL:] * b
    x1 = x_ref[...] + g1_ref[...] * _dot(merged.astype(BF16), wout_ref[...])
    x1_ref[...] = x1
    h2 = _modulated_norm(x1, n2_ref[...], sh2_ref[...], sc2_ref[...]).astype(BF16)
    h2_ref[...] = h2
    for hp in range(2 * PEER_HEADS):
        pq_ref[hp] = _dot(h2, wq_ref[:, hp * PEER_D_HALF:(hp + 1) * PEER_D_HALF]).astype(BF16)


def _postmix(yh, yn, gate, x2, g1, sh2, sc2, n2g, whyo, wnao, wout, wq, seq):
    T = x2.shape[0]
    tiles_per_batch = seq // TOK_TILE
    row = lambda i: (i, 0)
    per_batch = lambda i: (i // tiles_per_batch, 0, 0)
    const = lambda i: (0, 0)
    nq = wq.shape[1]
    return pl.pallas_call(
        _postmix_kernel,
        grid=(T // TOK_TILE,),
        in_specs=[
            pl.BlockSpec((TOK_TILE, D_HY), row),
            pl.BlockSpec((TOK_TILE, D_NA), row),
            pl.BlockSpec((TOK_TILE, 2 * D_MODEL), row),
            pl.BlockSpec((TOK_TILE, D_MODEL), row),
            pl.BlockSpec((None, 1, D_MODEL), per_batch),
            pl.BlockSpec((None, 1, D_MODEL), per_batch),
            pl.BlockSpec((None, 1, D_MODEL), per_batch),
            pl.BlockSpec((1, D_MODEL), const),
            pl.BlockSpec((D_HY, D_MODEL), const),
            pl.BlockSpec((D_NA, D_MODEL), const),
            pl.BlockSpec((D_MODEL, D_MODEL), const),
            pl.BlockSpec((D_MODEL, nq), const),
        ],
        out_specs=[
            pl.BlockSpec((TOK_TILE, D_MODEL), row),
            pl.BlockSpec((TOK_TILE, D_MODEL), row),
            pl.BlockSpec((nq // PEER_D_HALF, TOK_TILE, PEER_D_HALF), lambda i: (0, i, 0)),
        ],
        out_shape=[
            jax.ShapeDtypeStruct((T, D_MODEL), F32),
            jax.ShapeDtypeStruct((T, D_MODEL), BF16),
            jax.ShapeDtypeStruct((nq // PEER_D_HALF, T, PEER_D_HALF), BF16),
        ],
        compiler_params=_cparams(("parallel",)),
        name="postmix",
    )(yh, yn, gate, x2, g1, sh2, sc2, n2g, whyo, wnao, wout, wq)


SUBLANES = 8
LANES = 128


def _argmax_tree(vals, idxs):
    while len(vals) > 1:
        nv, ni = [], []
        for j in range(0, len(vals) - 1, 2):
            right = vals[j + 1] > vals[j]
            nv.append(jnp.where(right, vals[j + 1], vals[j]))
            ni.append(jnp.where(right, idxs[j + 1], idxs[j]))
        if len(vals) % 2:
            nv.append(vals[-1])
            ni.append(idxs[-1])
        vals, idxs = nv, ni
    return vals[0], idxs[0]


W_BLOCK = 128
W_PITCH = 136


SORT_TILE = SUBLANES * LANES


def _batcher_sort_network(n):
    ces = []

    def merge(lo, m, r):
        step = r * 2
        if step < m:
            merge(lo, m, step)
            merge(lo + r, m, step)
            ces.extend((i, i + r) for i in range(lo + r, lo + m - r, step))
        else:
            ces.append((lo, lo + r))

    def sort(lo, m):
        if m > 1:
            sort(lo, m // 2)
            sort(lo + m // 2, m // 2)
            merge(lo, m, 1)

    sort(0, n)
    return ces


def _bitonic_merge_network(n):
    ces, s = [], n // 2
    while s >= 1:
        ces.extend((i, i + s) for i in range(n) if (i & s) == 0)
        s //= 2
    return ces


SORT_NET = _batcher_sort_network(PEER_TOPK)
MERGE_NET = _bitonic_merge_network(PEER_TOPK)


def _compare_exchange(v, x, net):
    for i, j in net:
        c = v[j] > v[i]
        v[i], v[j] = jnp.where(c, v[j], v[i]), jnp.where(c, v[i], v[j])
        x[i], x[j] = jnp.where(c, x[j], x[i]), jnp.where(c, x[i], x[j])


def _merge_top(a, b, dropped):
    (va, xa), (vb, xb) = a, b
    k = len(va)
    v, x = [], []
    for r in range(k):
        c = vb[k - 1 - r] > va[r]
        v.append(jnp.where(c, vb[k - 1 - r], va[r]))
        x.append(jnp.where(c, xb[k - 1 - r], xa[r]))
        dropped = jnp.maximum(dropped, jnp.where(c, va[r], vb[k - 1 - r]))
    _compare_exchange(v, x, MERGE_NET)
    return (v, x), dropped


def _merge_all(lists, dropped):
    while len(lists) > 1:
        nxt = []
        for j in range(0, len(lists) - 1, 2):
            m, dropped = _merge_top(lists[j], lists[j + 1], dropped)
            nxt.append(m)
        if len(lists) % 2:
            nxt.append(lists[-1])
        lists = nxt
    return lists[0], dropped


def _settle_ties(v, x, dropped):
    x = list(x)
    for r in range(len(v) - 1):
        swap = (v[r] == v[r + 1]) & (x[r] > x[r + 1])
        x[r], x[r + 1] = jnp.where(swap, x[r + 1], x[r]), jnp.where(swap, x[r], x[r + 1])
    risk = v[-1] <= dropped
    for r in range(len(v) - 1):
        risk = risk | ((v[r] == v[r + 1]) & (x[r] > x[r + 1]))
    return x, risk.astype(jnp.int32)


def _level2_chains():
    pairs = [(a, b) for a in range(PEER_TOPK) for b in range(PEER_TOPK) if (a + 1) * (b + 1) <= PEER_TOPK]
    n_rows = 4
    chains = [[(a, b) for (a2, b) in pairs if a2 == a] for a in range(n_rows)]
    rest = [p for p in pairs if p[0] >= n_rows]
    for b in sorted({b for _, b in rest}):
        chains.append([(a, b2) for (a, b2) in rest if b2 == b])
    assert sorted(sum(chains, [])) == pairs
    return pairs, chains


def _route_kernel(pq_ref, keys_ref, a_ref, b_ref, g_ref, sx_ref, tv_ref, ti_ref, bs_ref, es_ref,
                  oa_ref, ob_ref, og_ref, risk_ref):
    n = PEER_N_KEYS
    k = PEER_TOPK
    shape = (SUBLANES, LANES)
    neg = jnp.full(shape, -jnp.inf, F32)
    pairs, chains = _level2_chains()

    def scores(hp):
        s = _dot_nt(keys_ref[hp], pq_ref[hp])
        for rg in range(n // SUBLANES):
            for lt in range(SUBLANES):
                sx_ref[pl.ds(rg * SUBLANES * SUBLANES + lt, SUBLANES, stride=SUBLANES), :] = (
                    s[rg * SUBLANES:(rg + 1) * SUBLANES, lt * LANES:(lt + 1) * LANES])
        return [sx_ref[pl.ds(key * SUBLANES, SUBLANES), :] for key in range(n)]

    def store_top(hp, v, x):
        for r in range(k):
            tv_ref[hp, r] = v[r]
            ti_ref[hp, r] = x[r]

    def level1(hp, carry):
        vals = scores(hp)
        lists = []
        for q in range(n // k):
            v = vals[q * k:(q + 1) * k]
            x = [jnp.full(shape, q * k + r, jnp.int32) for r in range(k)]
            _compare_exchange(v, x, SORT_NET)
            lists.append((v, x))
        (v, x), dropped = _merge_all(lists, neg)
        x, risk = _settle_ties(v, x, dropped)
        store_top(hp, v, x)
        risk_ref[hp] = risk
        return carry

    def level1_repair(hp, carry):
        @pl.when(jnp.max(risk_ref[hp]) > 0)
        def _():
            level1_exact(hp)

        return carry

    def level1_exact(hp):
        vals = scores(hp)
        idx = [jnp.full(shape, key, jnp.int32) for key in range(n)]
        v, x = [], []
        for r in range(k):
            best, where_ = _argmax_tree(vals, idx)
            v.append(best)
            x.append(where_)
            vals = [jnp.where(where_ == key, -jnp.inf, vals[key]) for key in range(n)]
        store_top(hp, v, x)

    def candidates(h):
        s1 = [tv_ref[2 * h, r] for r in range(k)]
        s2 = [tv_ref[2 * h + 1, r] for r in range(k)]
        e1 = [ti_ref[2 * h, r] * n for r in range(k)]
        e2 = [ti_ref[2 * h + 1, r] for r in range(k)]
        return (lambda a, b: s1[a] + s2[b]), (lambda a, b: e1[a] + e2[b] + (a * k + b) * PEER_N_EXPERTS)

    def store_best(h, v, x):
        for r in range(k):
            bs_ref[h, r] = v[r]
            es_ref[h, r] = x[r]

    def level2(h, carry):
        val, expert = candidates(h)
        lists = []
        for chain in chains:
            pad = k - len(chain)
            lists.append(([val(a, b) for a, b in chain] + [neg] * pad,
                          [expert(a, b) for a, b in chain] + [jnp.zeros(shape, jnp.int32)] * pad))
        (v, x), dropped = _merge_all(lists, neg)
        x, risk = _settle_ties(v, x, dropped)
        store_best(h, v, x)
        risk_ref[h] = risk
        return carry

    def level2_repair(h, carry):
        @pl.when(jnp.max(risk_ref[h]) > 0)
        def _():
            level2_exact(h)

        return carry

    def level2_exact(h):
        val, expert = candidates(h)
        cand = [val(a, b) for a, b in pairs]
        exps = [expert(a, b) for a, b in pairs]
        pos = [jnp.full(shape, a * k + b, jnp.int32) for a, b in pairs]
        v, x = [], []
        for r in range(k):
            best, where_ = _argmax_tree(cand, pos)
            chosen = jnp.zeros(shape, jnp.int32)
            for m, (a, b) in enumerate(pairs):
                hit = where_ == (a * k + b)
                cand[m] = jnp.where(hit, -jnp.inf, cand[m])
                chosen = jnp.where(hit, exps[m], chosen)
            v.append(best)
            x.append(chosen)
        store_best(h, v, x)

    lax.fori_loop(0, 2 * PEER_HEADS, level1, 0, unroll=2)
    lax.fori_loop(0, 2 * PEER_HEADS, level1_repair, 0)
    lax.fori_loop(0, PEER_HEADS, level2, 0)
    lax.fori_loop(0, PEER_HEADS, level2_repair, 0)

    for h in range(PEER_HEADS):
        best = [bs_ref[h, r] for r in range(k)]
        ex = [jnp.exp(v - best[0]) for v in best]
        denom = functools.reduce(lambda p, q: p + q, ex)
        for r in range(k):
            rows = pl.ds((h * k + r) * SUBLANES, SUBLANES)
            e = jnp.bitwise_and(es_ref[h, r], PEER_N_EXPERTS - 1)
            oa_ref[rows, :] = jnp.right_shift(e, n.bit_length() - 1)
            ob_ref[rows, :] = jnp.bitwise_and(e, n - 1)
            og_ref[rows, :] = ex[r] / denom
    for s in range(SUBLANES):
        rows = pl.ds(s * LANES, LANES)
        plane = pl.ds(s, PEER_HEADS * k, stride=SUBLANES)
        a_ref[rows, :] = oa_ref[plane, :].T
        b_ref[rows, :] = ob_ref[plane, :].T
        g_ref[rows, :] = og_ref[plane, :].T


def _route(pq, keys_bf):
    n_hp, T, _ = pq.shape
    n_sel = PEER_HEADS * PEER_TOPK
    row = lambda i: (i, 0)
    reg = (SUBLANES, LANES)
    return pl.pallas_call(
        _route_kernel,
        grid=(T // SORT_TILE,),
        in_specs=[
            pl.BlockSpec((n_hp, SORT_TILE, PEER_D_HALF), lambda i: (0, i, 0)),
            pl.BlockSpec((n_hp, PEER_N_KEYS, PEER_D_HALF), lambda i: (0, 0, 0)),
        ],
        out_specs=[pl.BlockSpec((SORT_TILE, n_sel), row)] * 3,
        out_shape=[
            jax.ShapeDtypeStruct((T, n_sel), jnp.int32),
            jax.ShapeDtypeStruct((T, n_sel), jnp.int32),
            jax.ShapeDtypeStruct((T, n_sel), F32),
        ],
        scratch_shapes=[
            pltpu.VMEM((PEER_N_KEYS * SUBLANES, LANES), F32),
            pltpu.VMEM((n_hp, PEER_TOPK) + reg, F32),
            pltpu.VMEM((n_hp, PEER_TOPK) + reg, jnp.int32),
            pltpu.VMEM((PEER_HEADS, PEER_TOPK) + reg, F32),
            pltpu.VMEM((PEER_HEADS, PEER_TOPK) + reg, jnp.int32),
            pltpu.VMEM((n_sel * SUBLANES, LANES), jnp.int32),
            pltpu.VMEM((n_sel * SUBLANES, LANES), jnp.int32),
            pltpu.VMEM((n_sel * SUBLANES, LANES), F32),
            pltpu.VMEM((n_hp,) + reg, jnp.int32),
        ],
        compiler_params=_cparams(("parallel",)),
        name="route",
    )(pq, keys_bf)


WBUILD_UNROLL = 32


def _wbuild_kernel(a_ref, b_ref, g_ref, w_ref, s_ref):
    n = PEER_N_KEYS
    tb = a_ref.shape[0]
    key = lax.broadcasted_iota(jnp.int32, (n, a_ref.shape[1]), 0)

    def body(t, carry):
        a = a_ref[pl.ds(t, 1), :]
        b = b_ref[pl.ds(t, 1), :]
        g = g_ref[pl.ds(t, 1), :]
        ga = jnp.where(a == key, g, 0.0).astype(BF16)
        ob = jnp.where(b == key, 1.0, 0.0).T.astype(BF16)
        s_ref[pl.ds(pl.multiple_of(t * W_PITCH, SUBLANES), n), :] = _dot(ga, ob)
        return carry

    lax.fori_loop(0, tb, body, 0, unroll=WBUILD_UNROLL)

    def emit(i1, carry):
        w_ref[i1] = s_ref[pl.ds(i1, tb, stride=W_PITCH), :].astype(BF16)
        return carry

    lax.fori_loop(0, n, emit, 0, unroll=WBUILD_UNROLL)


def _wbuild(a, b, g):
    T, n_sel = a.shape
    row = lambda i: (i, 0)
    return pl.pallas_call(
        _wbuild_kernel,
        grid=(T // W_BLOCK,),
        in_specs=[pl.BlockSpec((W_BLOCK, n_sel), row)] * 3,
        out_specs=pl.BlockSpec((None, PEER_N_KEYS, W_BLOCK, PEER_N_KEYS), lambda i: (i, 0, 0, 0)),
        out_shape=jax.ShapeDtypeStruct((T // W_BLOCK, PEER_N_KEYS, W_BLOCK, PEER_N_KEYS), BF16),
        scratch_shapes=[pltpu.VMEM((W_BLOCK * W_PITCH, PEER_N_KEYS), F32)],
        compiler_params=_cparams(("parallel",)),
        name="wbuild",
    )(a, b, g)


def _peer_dense_kernel(h2_ref, w_ref, u_ref, v_ref, x1_ref, g2_ref, o_ref, acc_ref):
    e = pl.program_id(1)

    @pl.when(e == 0)
    def _():
        acc_ref[...] = jnp.zeros_like(acc_ref)

    a = _dot_nt(h2_ref[...], u_ref[...])
    act = 0.5 * a * (1.0 + lax.erf(a * (2.0 ** -0.5)))
    w = jnp.concatenate(
        [jnp.concatenate([w_ref[blk, i] for i in range(w_ref.shape[1])], axis=1) for blk in range(w_ref.shape[0])],
        axis=0)
    acc_ref[...] += _dot((w.astype(F32) * act).astype(BF16), v_ref[...])

    @pl.when(e == pl.num_programs(1) - 1)
    def _():
        o_ref[...] = x1_ref[...] + g2_ref[...] * acc_ref[...]


def _peer_dense(h2, w, u_bf, v_bf, x1, g2, seq):
    T = h2.shape[0]
    tiles_per_batch = seq // PEER_TOK_TILE
    n_exp = u_bf.shape[0]
    return pl.pallas_call(
        _peer_dense_kernel,
        grid=(T // PEER_TOK_TILE, n_exp // PEER_EXP_TILE),
        in_specs=[
            pl.BlockSpec((PEER_TOK_TILE, D_MODEL), lambda i, e: (i, 0)),
            pl.BlockSpec((PEER_TOK_TILE // W_BLOCK, PEER_EXP_TILE // PEER_N_KEYS, W_BLOCK, PEER_N_KEYS),
                         lambda i, e: (i, e, 0, 0)),
            pl.BlockSpec((PEER_EXP_TILE, D_MODEL), lambda i, e: (e, 0)),
            pl.BlockSpec((PEER_EXP_TILE, D_MODEL), lambda i, e: (e, 0)),
            pl.BlockSpec((PEER_TOK_TILE, D_MODEL), lambda i, e: (i, 0)),
            pl.BlockSpec((None, 1, D_MODEL), lambda i, e: (i // tiles_per_batch, 0, 0)),
        ],
        out_specs=pl.BlockSpec((PEER_TOK_TILE, D_MODEL), lambda i, e: (i, 0)),
        out_shape=jax.ShapeDtypeStruct((T, D_MODEL), F32),
        scratch_shapes=[pltpu.VMEM((PEER_TOK_TILE, D_MODEL), F32)],
        compiler_params=_cparams(("parallel", "arbitrary")),
        name="peer_dense",
    )(h2, w, u_bf, v_bf, x1, g2)


HI = lax.Precision.HIGHEST


HY_LANES = 128
FFT_R = 128
FFT_N = FFT_R * FFT_R
FFT_PITCH = 136
FFT_UNROLL = 8
FFT_K1_USED = FFT_R // 2 + 1
FFT_K1_SLABS = 66
FFT_K1_ROWS = 72
FFT_K1_CHUNK = 22
assert FFT_K1_SLABS % FFT_K1_CHUNK == 0 and FFT_K1_CHUNK % 2 == 0 and FFT_K1_USED <= FFT_K1_SLABS <= FFT_K1_ROWS


def _fft_tables():
    r = np.arange(FFT_R)
    kk = np.arange(FFT_K1_ROWS)
    n2, k1, n1 = r[:, None, None], kk[None, :, None], r[None, None, :]
    ang = (2.0 * math.pi / FFT_N) * ((k1 * (FFT_R * n1 + n2)) % FFT_N)
    used = k1 < FFT_K1_USED
    c, s = np.cos(ang) * used, np.sin(ang) * used
    g_full = np.concatenate([c, -s], axis=1)
    twice = np.where((k1 == 0) | (k1 == FFT_R // 2), 1.0, 2.0) / FFT_N
    h_half = np.concatenate([np.swapaxes(c * twice, 1, 2), -np.swapaxes(s * twice, 1, 2)],
                            axis=2)[:, :FFT_R // 2]
    ang2 = (2.0 * math.pi / FFT_R) * ((r[:, None] * r[None, :]) % FFT_R)
    c2, s2 = np.cos(ang2), np.sin(ang2)
    f_fwd = np.block([[c2, s2], [-s2, c2]])
    f_inv = np.block([[c2, -s2], [s2, c2]])
    return tuple(jnp.asarray(t, dtype=F32).astype(BF16) for t in (g_full, h_half, f_fwd, f_inv))


def _fft_stage1(load_slab, g_ref, bre_ref, bim_ref):
    def body(n2, carry):
        out = _dot(g_ref[n2], load_slab(n2).astype(BF16))
        bre_ref[pl.ds(n2, FFT_K1_ROWS, stride=FFT_PITCH), :] = out[:FFT_K1_ROWS]
        bim_ref[pl.ds(n2, FFT_K1_ROWS, stride=FFT_PITCH), :] = out[FFT_K1_ROWS:]
        return carry

    lax.fori_loop(0, FFT_R, body, 0, unroll=FFT_UNROLL)


def _k1_rows(k1):
    return pl.ds(pl.multiple_of(k1 * FFT_PITCH, SUBLANES), FFT_R)


def _load_k1_pair(bre_ref, bim_ref, k1):
    return jnp.concatenate(
        [jnp.concatenate([bre_ref[_k1_rows(k1 + d), :], bim_ref[_k1_rows(k1 + d), :]], axis=0) for d in range(2)],
        axis=1)


FILT_ROWS = 512
FILT_FEAT = 128


def _filt_kernel(feat_ref, delta_ref, w1_ref, b1_ref, w2_ref, b2_ref, w3_ref, b3_ref, w4_ref, freq_ref,
                 h_ref, sum_ref):
    @pl.when(pl.program_id(0) == 0)
    def _():
        sum_ref[...] = jnp.zeros_like(sum_ref)

    feat = feat_ref[...]
    freq = freq_ref[...]
    dot = lambda a, b: jnp.dot(a, b, precision=HI, preferred_element_type=F32)
    h = jnp.sin(freq * (dot(feat, w1_ref[...]) + b1_ref[...]))
    h = jnp.sin(freq * (dot(h, w2_ref[...]) + b2_ref[...]))
    h = jnp.sin(freq * (dot(h, w3_ref[...]) + b3_ref[...]))
    hi = h.astype(BF16)
    lo = (h - hi.astype(F32)).astype(BF16)
    h = _dot(hi, w4_ref[...]) + _dot(lo, w4_ref[...])
    n_feat = 1 + 2 * HY_BANDS
    t = feat[:, 0:1]
    forward = feat[:, n_feat:n_feat + 1] > 0.5
    keep = feat[:, n_feat + 1:n_feat + 2]
    decay = jnp.exp(-t * delta_ref[...])
    outs = []
    for o in range(HY_ORDER):
        base = o * 2 * D_HY
        outs.append(jnp.where(forward, h[:, base:base + D_HY], h[:, base + D_HY:base + 2 * D_HY]) * decay)
    out = jnp.concatenate(outs, axis=1)
    sum_ref[...] += jnp.sum(jnp.abs(out), axis=0, keepdims=True)
    out = out * keep
    for j in range(h_ref.shape[0]):
        h_ref[j] = out[:, j * HY_LANES:(j + 1) * HY_LANES]


def _filt_features(L):
    r = np.arange(FFT_N)
    n = FFT_R * (r % FFT_R) + r // FFT_R
    j = np.where(n < L, n, np.where(n > L, 2 * L - n, 0))
    jj = j.astype(np.float64)[:, None]
    t = jj / (L - 1.0)
    w = (2.0 * math.pi / L) * jj
    bands = np.linspace(1e-4, HY_BANDS - 1.0, HY_BANDS)[None, :]
    flags = np.stack([n < L, n != L], axis=1).astype(np.float64)
    feats = np.concatenate([t, np.cos(bands * w), -np.sin(bands * w), flags], axis=-1)
    return jnp.asarray(np.pad(feats, ((0, 0), (0, FILT_FEAT - feats.shape[1]))), dtype=F32)


def _filters(p, L):
    feats = _filt_features(L)
    deltas = jnp.abs(jnp.linspace(math.log(HY_DECAY_TARGET) / HY_SLOW_PCT,
                                  math.log(HY_DECAY_TARGET) / HY_FAST_PCT, D_HY, dtype=F32))[None, :]
    w1 = jnp.pad(p['hy_f1_w'], ((0, FILT_FEAT - p['hy_f1_w'].shape[0]), (0, 0)))
    n_out = HY_ORDER * D_HY
    const = lambda i: (0, 0)
    full = lambda a: pl.BlockSpec(a.shape, const)
    args = [deltas, w1, p['hy_f1_b'][None], p['hy_f2_w'], p['hy_f2_b'][None], p['hy_f3_w'], p['hy_f3_b'][None],
            p['hy_f4_w'].astype(BF16), p['hy_sin_freq'][None]]
    return pl.pallas_call(
        _filt_kernel,
        grid=(FFT_N // FILT_ROWS,),
        in_specs=[pl.BlockSpec((FILT_ROWS, FILT_FEAT), lambda i: (i, 0))] + [full(a) for a in args],
        out_specs=[pl.BlockSpec((n_out // HY_LANES, FILT_ROWS, HY_LANES), lambda i: (0, i, 0)),
                   pl.BlockSpec((1, n_out), const)],
        out_shape=[jax.ShapeDtypeStruct((n_out // HY_LANES, FFT_N, HY_LANES), F32),
                   jax.ShapeDtypeStruct((1, n_out), F32)],
        compiler_params=_cparams(("arbitrary",)),
        name="hyena_filter",
    )(feats, *args)


def _fspec_kernel(h_ref, sum_ref, g_ref, f_ref, o_ref, bre_ref, bim_ref):
    c = pl.program_id(1)

    @pl.when(c == 0)
    def _():
        _fft_stage1(lambda n2: h_ref[pl.ds(pl.multiple_of(n2 * FFT_R, FFT_R), FFT_R), :], g_ref, bre_ref, bim_ref)

    inv = 1.0 / sum_ref[...]

    lanes = h_ref.shape[1]

    def slab_pair(j, carry):
        x = _dot(f_ref[...], _load_k1_pair(bre_ref, bim_ref, c * FFT_K1_CHUNK + 2 * j).astype(BF16))
        for d in range(2):
            rows = pl.ds(pl.multiple_of((2 * j + d) * FFT_R, FFT_R), FFT_R)
            o_ref[0, rows, :] = x[:FFT_R, d * lanes:(d + 1) * lanes] * inv
            o_ref[1, rows, :] = x[FFT_R:, d * lanes:(d + 1) * lanes] * inv
        return carry

    lax.fori_loop(0, FFT_K1_CHUNK // 2, slab_pair, 0, unroll=True)


def _filter_spectrum(h, hsum, g_full, f_fwd):
    n_ch = h.shape[0] * h.shape[2]
    single = pl.Buffered(1)
    return pl.pallas_call(
        _fspec_kernel,
        grid=(n_ch // HY_LANES, FFT_K1_SLABS // FFT_K1_CHUNK),
        in_specs=[
            pl.BlockSpec((None, FFT_N, HY_LANES), lambda j, c: (j, 0, 0)),
            pl.BlockSpec((1, HY_LANES), lambda j, c: (0, j)),
            pl.BlockSpec(g_full.shape, lambda j, c: (0, 0, 0), pipeline_mode=single),
            pl.BlockSpec(f_fwd.shape, lambda j, c: (0, 0)),
        ],
        out_specs=pl.BlockSpec((None, 2, FFT_K1_CHUNK * FFT_R, HY_LANES), lambda j, c: (j, 0, c, 0)),
        out_shape=jax.ShapeDtypeStruct((n_ch // HY_LANES, 2, FFT_K1_SLABS * FFT_R, HY_LANES), F32),
        scratch_shapes=[pltpu.VMEM((FFT_K1_ROWS * FFT_PITCH, HY_LANES), F32)] * 2,
        compiler_params=_cparams(("parallel", "arbitrary")),
        name="hyena_filter_fft",
    )(h, hsum, g_full, f_fwd)


def _hyconv_kernel(u_ref, gate_ref, uw_ref, ub_ref, gw_ref, gb_ref, kf_ref, skip_ref, g_ref, h_ref, f_ref, fi_ref,
                   o_ref, bre_ref, bim_ref, *, conv_u):
    c = pl.program_id(2)
    half = FFT_R // 2
    last = FFT_R - 1

    def rows_of(n2):
        return pl.ds(pl.multiple_of(n2 * half, half), half)

    row = lax.broadcasted_iota(jnp.int32, (half, u_ref.shape[1]), 0)

    def short_conv(ref, w_ref, b_ref):
        wrap_prev = jnp.where(row == 0, 0.0, pltpu.roll(ref[rows_of(last), :], 1, 0))
        wrap_next = jnp.where(row == half - 1, 0.0, pltpu.roll(ref[rows_of(0), :], half - 1, 0))

        def slab(n2):
            prev = jnp.where(n2 == 0, wrap_prev, ref[rows_of(jnp.maximum(n2 - 1, 0)), :])
            nxt = jnp.where(n2 == last, wrap_next, ref[rows_of(jnp.minimum(n2 + 1, last)), :])
            return w_ref[0:1] * prev + w_ref[1:2] * ref[rows_of(n2), :] + w_ref[2:3] * nxt + b_ref[...]

        return slab

    u_slab = short_conv(u_ref, uw_ref, ub_ref) if conv_u else (lambda n2: u_ref[rows_of(n2), :])
    gate_slab = short_conv(gate_ref, gw_ref, gb_ref)

    @pl.when(c == 0)
    def _():
        _fft_stage1(u_slab, g_ref, bre_ref, bim_ref)

    lanes = u_ref.shape[1]

    def slab_pair(j, carry):
        k1 = c * FFT_K1_CHUNK + 2 * j
        x = _dot(f_ref[...], _load_k1_pair(bre_ref, bim_ref, k1).astype(BF16))
        xr, xi = x[:FFT_R], x[FFT_R:]
        kr, ki = [jnp.concatenate([kf_ref[part, pl.ds(pl.multiple_of((2 * j + d) * FFT_R, FFT_R), FFT_R), :]
                                   for d in range(2)], axis=1) for part in range(2)]
        y = jnp.concatenate([xr * kr - xi * ki, xr * ki + xi * kr], axis=0)
        z = _dot(fi_ref[...], y.astype(BF16))
        for d in range(2):
            bre_ref[_k1_rows(k1 + d), :] = z[:FFT_R, d * lanes:(d + 1) * lanes]
            bim_ref[_k1_rows(k1 + d), :] = z[FFT_R:, d * lanes:(d + 1) * lanes]
        return carry

    lax.fori_loop(0, FFT_K1_CHUNK // 2, slab_pair, 0, unroll=True)

    @pl.when(c == pl.num_programs(2) - 1)
    def _():
        def body(n2, carry):
            z = jnp.concatenate([bre_ref[pl.ds(n2, FFT_K1_ROWS, stride=FFT_PITCH), :],
                                 bim_ref[pl.ds(n2, FFT_K1_ROWS, stride=FFT_PITCH), :]], axis=0)
            conv = _dot(h_ref[n2], z.astype(BF16))
            o_ref[rows_of(n2), :] = gate_slab(n2) * (conv + skip_ref[...] * u_slab(n2))
            return carry

        lax.fori_loop(0, FFT_R, body, 0, unroll=FFT_UNROLL)


def _hyconv(u, u_col, gate, gate_col, conv_w, conv_b, conv_u, kf, kf_col, skip, tables):
    g_half, h_half, f_fwd, f_inv = tables
    B, L, _ = u.shape
    single = pl.Buffered(1)
    uw_col = u_col if conv_u else gate_col
    return pl.pallas_call(
        functools.partial(_hyconv_kernel, conv_u=conv_u),
        grid=(B, D_HY // HY_LANES, FFT_K1_SLABS // FFT_K1_CHUNK),
        in_specs=[
            pl.BlockSpec((None, L, HY_LANES), lambda b, j, c: (b, 0, u_col + j)),
            pl.BlockSpec((None, L, HY_LANES), lambda b, j, c: (b, 0, gate_col + j)),
            pl.BlockSpec((3, HY_LANES), lambda b, j, c: (0, uw_col + j)),
            pl.BlockSpec((1, HY_LANES), lambda b, j, c: (0, uw_col + j)),
            pl.BlockSpec((3, HY_LANES), lambda b, j, c: (0, gate_col + j)),
            pl.BlockSpec((1, HY_LANES), lambda b, j, c: (0, gate_col + j)),
            pl.BlockSpec((None, 2, FFT_K1_CHUNK * FFT_R, HY_LANES), lambda b, j, c: (kf_col + j, 0, c, 0)),
            pl.BlockSpec((1, HY_LANES), lambda b, j, c: (0, j)),
            pl.BlockSpec(g_half.shape, lambda b, j, c: (0, 0, 0), pipeline_mode=single),
            pl.BlockSpec(h_half.shape, lambda b, j, c: (0, 0, 0), pipeline_mode=single),
            pl.BlockSpec(f_fwd.shape, lambda b, j, c: (0, 0)),
            pl.BlockSpec(f_inv.shape, lambda b, j, c: (0, 0)),
        ],
        out_specs=pl.BlockSpec((None, L, HY_LANES), lambda b, j, c: (b, 0, j)),
        out_shape=jax.ShapeDtypeStruct((B, L, D_HY), F32),
        scratch_shapes=[pltpu.VMEM((FFT_K1_ROWS * FFT_PITCH, HY_LANES), F32)] * 2,
        compiler_params=_cparams(("parallel", "parallel", "arbitrary")),
        name="hyena_conv",
    )(u, gate, conv_w, conv_b, conv_w, conv_b, kf, skip, g_half, h_half, f_fwd, f_inv)


def _slab_major(a, n1):
    B, L, C = a.shape
    return a.reshape(B, n1, L // n1, C).transpose(0, 2, 1, 3).reshape(B, L, C)


def _hyena(zh, p):
    B, L, _ = zh.shape
    assert 2 * L == FFT_N
    g_full, h_half, f_fwd, f_inv = _fft_tables()
    tables = (g_full[:, :, :FFT_R // 2], h_half, f_fwd, f_inv)
    filt, filt_sum = _filters(p, L)
    both_dirs = filt_sum[:, :]
    kf = _filter_spectrum(filt, both_dirs, g_full, f_fwd)
    zs = _slab_major(zh, FFT_R // 2)
    cw, cb = p['hy_conv_w'], p['hy_conv_b'][None]
    lanes = D_HY // HY_LANES
    y = _hyconv(zs, 0, zs, lanes, cw, cb, True, kf, 0, p['hy_skip'][0:1], tables)
    y = _hyconv(y, 0, zs, 2 * lanes, cw, cb, False, kf, lanes, p['hy_skip'][1:2], tables)
    return _slab_major(y, L // (FFT_R // 2))


def kernel(x, c, ctx, c_ctx, norm1_g, norm2_g, w_ada, b_ada, w_in, b_in, hy_conv_w, hy_conv_b, hy_f1_w, hy_f1_b, hy_f2_w, hy_f2_b, hy_f3_w, hy_f3_b, hy_f4_w, hy_sin_freq, hy_skip, q_norm_g, k_norm_g, na_rpb, w_hy_out, w_na_out, w_out, peer_w_q, peer_keys, peer_u, peer_v):
    assert w_in.shape[0] == 1, "single-layer block"
    B, S, D = x.shape
    C = ctx.shape[1]
    T = B * S

    mod = _adaln(jnp.concatenate([c, c_ctx[None]], axis=0), w_ada[0], b_ada[0][None])
    m_lat = mod[:B].reshape(B, N_MOD, 1, D)
    m_ctx = mod[B].reshape(N_MOD, 1, D)
    sh1, sc1, g1, sh2, sc2, g2 = [m_lat[:, i] for i in range(N_MOD)]

    w_in_bf = w_in[0].astype(BF16)
    b_in2 = b_in[0][None, :]
    gain1 = norm1_g[0][None, :]
    qg = jnp.tile(q_norm_g[0], NA_HEADS)[None, :]
    kg = jnp.tile(k_norm_g[0], NA_HEADS)[None, :]
    seg = jnp.asarray(np.kron(np.eye(NA_HEADS), np.full((HEAD_DIM, HEAD_DIM), 1.0 / HEAD_DIM)), dtype=BF16)
    cos, sin = _rope_tables(S)

    x2 = x.reshape(T, D)
    zh, q, k, v, gate = _inproj(x2, sh1, sc1, gain1, w_in_bf, b_in2, cos, sin, qg, kg, seg, S)

    k_ctx, v_ctx = _ctxkv(ctx.reshape(B * C, D), m_ctx[0], m_ctx[1], gain1,
                          w_in_bf[:, COL_K:COL_V], b_in2[:, COL_K:COL_V],
                          w_in_bf[:, COL_V:COL_G_HY], b_in2[:, COL_V:COL_G_HY], kg, seg)

    p = {'hy_conv_w': hy_conv_w[0], 'hy_conv_b': hy_conv_b[0], 'hy_f1_w': hy_f1_w[0], 'hy_f1_b': hy_f1_b[0],
         'hy_f2_w': hy_f2_w[0], 'hy_f2_b': hy_f2_b[0], 'hy_f3_w': hy_f3_w[0], 'hy_f3_b': hy_f3_b[0],
         'hy_f4_w': hy_f4_w[0], 'hy_sin_freq': hy_sin_freq[0], 'hy_skip': hy_skip[0]}
    y_hy = _hyena(zh.reshape(B, S, 3 * D_HY), p).reshape(T, D_HY)

    y_na = _natten(q.reshape(B, S, D_NA), k.reshape(B, S, D_NA), v.reshape(B, S, D_NA),
                   k_ctx.reshape(B, C, D_NA), v_ctx.reshape(B, C, D_NA), _natten_bias(na_rpb[0], S // GRID_W)).reshape(T, D_NA)

    x1, h2, pq = _postmix(y_hy, y_na, gate, x2, g1, sh2, sc2, norm2_g[0][None, :],
                          w_hy_out[0].astype(BF16), w_na_out[0].astype(BF16), w_out[0].astype(BF16),
                          peer_w_q[0].astype(BF16), S)

    keys_bf = peer_keys[0].astype(BF16).reshape(2 * PEER_HEADS, PEER_N_KEYS, PEER_D_HALF)
    sel_a, sel_b, sel_g = _route(pq, keys_bf)
    w = _wbuild(sel_a, sel_b, sel_g)
    out = _peer_dense(h2, w, peer_u[0].astype(BF16), peer_v[0].astype(BF16), x1, g2, S)
    return out.reshape(B, S, D)
```
